```python
import math
import jax, jax.numpy as jnp
from jax import lax
import numpy as np

D_MODEL = 1024
BATCH = 32
SEQ = 2048
DEPTH = 1

RET_HEADS = 4
RET_QK_DIM = 128
RET_V_DIM = 256
RET_CHUNK = 128
RET_QK_W = RET_HEADS * RET_QK_DIM
RET_V_W = RET_HEADS * RET_V_DIM
NA_HEADS = 8
NA_HEAD_DIM = 64
NA_W = NA_HEADS * NA_HEAD_DIM
GRID_W = 64
NA_WIN_ROWS = 8
NA_WIN_COLS = 16
NA_Q_BLOCK_COLS = 16
NA_K_BLOCK_COLS = 32
NA_REL_ROWS = 2 * NA_WIN_ROWS - 1
NA_REL_COLS = 2 * NA_WIN_COLS - 1
D_FF = 2816
RMS_EPS = 1e-6
ROPE_BASE = 10000.0
NEG_INF = -1e30
MIX_SPLITS = (RET_QK_W, RET_QK_W, RET_V_W, RET_V_W, NA_W, NA_W, NA_W, D_MODEL, D_MODEL)
MIX_IN_W = sum(MIX_SPLITS)

kernel_name = "hybrid_retention_natten_macaron_block"


def _rms_norm(x, gain):
    xf = x.astype(jnp.float32)
    y = xf * lax.rsqrt(jnp.mean(xf * xf, axis=-1, keepdims=True) + RMS_EPS)
    return (y * gain.astype(jnp.float32)).astype(x.dtype)


def _swiglu(x, w_in, w_out):
    g, u = jnp.split(x @ w_in, 2, axis=-1)
    return (jax.nn.silu(g) * u) @ w_out


def _rotary(x, pos):
    half = x.shape[-1] // 2
    inv = 1.0 / (ROPE_BASE ** jnp.linspace(0.0, 1.0, half, dtype=jnp.float32))
    ang = pos[:, None] * inv[None, :]
    cos, sin = jnp.cos(ang), jnp.sin(ang)
    x1, x2 = x[..., :half], x[..., half:]
    return jnp.concatenate([x1 * cos - x2 * sin, x1 * sin + x2 * cos], axis=-1)


def _retention_one_dir(q, k, v, log_gamma, strict):
    B, H, S, DK = q.shape
    DV = v.shape[-1]
    C = RET_CHUNK
    N = S // C
    qc = q.reshape(B, H, N, C, DK)
    kc = k.reshape(B, H, N, C, DK)
    vc = v.reshape(B, H, N, C, DV)
    idx = jnp.arange(C, dtype=jnp.float32)
    diff = idx[:, None] - idx[None, :]
    lower = (diff > 0) if strict else (diff >= 0)
    intra = jnp.where(lower[None], jnp.exp(log_gamma[:, None, None] * jnp.where(lower, diff, 0.0)[None]), 0.0)
    scores = jnp.einsum('bhnqd,bhnkd->bhnqk', qc, kc) * intra[None, :, None]
    inner = jnp.einsum('bhnqk,bhnkv->bhnqv', scores, vc)
    k_dec = jnp.exp(log_gamma[:, None] * (C - 1 - idx)[None, :])
    upd = jnp.einsum('bhnkd,bhnkv->nbhdv', kc * k_dec[None, :, None, :, None], vc)
    chunk_dec = jnp.exp(log_gamma * C)[None, :, None, None]

    def step(state, u):
        return chunk_dec * state + u, state

    _, prev = lax.scan(step, jnp.zeros((B, H, DK, DV), jnp.float32), upd)
    q_dec = jnp.exp(log_gamma[:, None] * (idx + 1.0)[None, :])
    cross = jnp.einsum('bhnqd,nbhdv->bhnqv', qc * q_dec[None, :, None, :, None], prev)
    return (inner + cross).reshape(B, H, S, DV)


def _bidirectional_retention(q, k, v, decay_fwd_logit, decay_bwd_logit):
    lg_f = jax.nn.log_sigmoid(decay_fwd_logit.astype(jnp.float32))
    lg_b = jax.nn.log_sigmoid(decay_bwd_logit.astype(jnp.float32))
    y_f = _retention_one_dir(q, k, v, lg_f, strict=False)
    flip = lambda a: jnp.flip(a, axis=2)
    y_b = flip(_retention_one_dir(flip(q), flip(k), flip(v), lg_b, strict=True))
    return y_f + y_b


def _neighborhood_attention(q, k, v, rel_bias):
    B, S, NH, DH = q.shape
    rows = S // GRID_W
    kr = min(NA_WIN_ROWS, rows)
    ncb = GRID_W // NA_Q_BLOCK_COLS
    qg = (q * (DH ** -0.5)).reshape(B, rows, GRID_W, NH, DH)
    kg = k.reshape(B, rows, GRID_W, NH, DH)
    vg = v.reshape(B, rows, GRID_W, NH, DH)
    c0 = jnp.arange(ncb, dtype=jnp.int32) * NA_Q_BLOCK_COLS
    q_cols = c0[:, None] + jnp.arange(NA_Q_BLOCK_COLS, dtype=jnp.int32)[None, :]
    key_cols = (jnp.clip(c0 - NA_WIN_COLS // 2, 0, GRID_W - NA_K_BLOCK_COLS)[:, None]
                + jnp.arange(NA_K_BLOCK_COLS, dtype=jnp.int32)[None, :])
    win_start = jnp.clip(q_cols - NA_WIN_COLS // 2, 0, GRID_W - NA_WIN_COLS)
    kcb = key_cols[:, None, :]
    col_mask = (kcb >= win_start[:, :, None]) & (kcb < win_start[:, :, None] + NA_WIN_COLS)
    col_idx = jnp.clip(kcb - q_cols[:, :, None] + NA_WIN_COLS - 1, 0, NA_REL_COLS - 1)

    def row_block(r):
        rs = jnp.clip(r - kr // 2, 0, rows - kr)
        q_r = lax.dynamic_index_in_dim(qg, r, axis=1, keepdims=False).reshape(B, ncb, NA_Q_BLOCK_COLS, NH, DH)
        k_blk = lax.dynamic_slice_in_dim(kg, rs, kr, axis=1)[:, :, key_cols]
        v_blk = lax.dynamic_slice_in_dim(vg, rs, kr, axis=1)[:, :, key_cols]
        s = jnp.einsum('bnqhd,brnkhd->bhnqrk', q_r, k_blk).astype(jnp.float32)
        row_idx = rs + jnp.arange(kr, dtype=jnp.int32) - r + NA_WIN_ROWS - 1
        bias = rel_bias[:, row_idx[None, None, :, None], col_idx[:, :, None, :]].astype(jnp.float32)
        s = jnp.where(col_mask[:, :, None, :], s + bias, NEG_INF)
        p = jax.nn.softmax(s.reshape(B, NH, ncb, NA_Q_BLOCK_COLS, kr * NA_K_BLOCK_COLS), axis=-1)
        p = p.reshape(s.shape).astype(v.dtype)
        o = jnp.einsum('bhnqrk,brnkhd->bnqhd', p, v_blk)
        return o.reshape(B, GRID_W, NH, DH)

    out = lax.map(row_block, jnp.arange(rows, dtype=jnp.int32))
    return out.transpose(1, 0, 2, 3, 4).reshape(B, S, NH * DH)


def _token_mixing(u, w_in, decay_fwd, decay_bwd, rel_bias, w_ret_out, w_na_out, w_out, pos):
    B, S, _ = u.shape
    points = [sum(MIX_SPLITS[:i + 1]) for i in range(len(MIX_SPLITS) - 1)]
    rq, rk, rv, rg, nq, nk, nv, g_ret, g_na = jnp.split(u @ w_in, points, axis=-1)
    heads = lambda a, d: a.reshape(B, S, RET_HEADS, d).transpose(0, 2, 1, 3).astype(jnp.float32)
    q = _rotary(heads(rq, RET_QK_DIM), pos)
    k = _rotary(heads(rk, RET_QK_DIM), pos) * (RET_QK_DIM ** -0.5)
    y = _bidirectional_retention(q, k, heads(rv, RET_V_DIM), decay_fwd, decay_bwd)
    y = y * lax.rsqrt(jnp.mean(y * y, axis=-1, keepdims=True) + RMS_EPS)
    y = y.transpose(0, 2, 1, 3).reshape(B, S, RET_V_W).astype(u.dtype)
    y_ret = (jax.nn.silu(rg) * y) @ w_ret_out
    na_heads = lambda a: a.reshape(B, S, NA_HEADS, NA_HEAD_DIM)
    y_na = _neighborhood_attention(na_heads(nq), na_heads(nk), na_heads(nv), rel_bias) @ w_na_out
    merged = jax.nn.sigmoid(g_ret) * y_ret + jax.nn.sigmoid(g_na) * y_na
    return merged @ w_out


def setup_inputs(seed: int = 0) -> dict:
    key = jax.random.key(seed)
    ks = jax.random.split(key, 20)
    f32 = jnp.float32

    def w(k, shape, fan_in):
        return jax.random.normal(k, (DEPTH,) + shape, f32) * (fan_in ** -0.5)

    def gain(k):
        return 1.0 + 0.05 * jax.random.normal(k, (DEPTH, D_MODEL), f32)

    gamma0 = 1.0 - 2.0 ** (-5.0 - np.arange(RET_HEADS, dtype=np.float32))
    logit0 = jnp.asarray(np.log(gamma0 / (1.0 - gamma0)), f32)
    return {
        "x": jax.random.normal(ks[0], (BATCH, SEQ, D_MODEL), f32),
        "ffn1_pre_norm": gain(ks[1]),
        "ffn1_w_in": w(ks[2], (D_MODEL, 2 * D_FF), D_MODEL),
        "ffn1_w_out": w(ks[3], (D_FF, D_MODEL), D_FF),
        "ffn1_post_norm": gain(ks[4]),
        "mix_pre_norm": gain(ks[5]),
        "w_mix_in": w(ks[6], (D_MODEL, MIX_IN_W), D_MODEL),
        "ret_decay_fwd": logit0[None] + 0.1 * jax.random.normal(ks[7], (DEPTH, RET_HEADS), f32),
        "ret_decay_bwd": logit0[None] + 0.1 * jax.random.normal(ks[8], (DEPTH, RET_HEADS), f32),
        "na_rel_bias": 0.02 * jax.random.normal(ks[9], (DEPTH, NA_HEADS, NA_REL_ROWS, NA_REL_COLS), f32),
        "w_ret_out": w(ks[10], (RET_V_W, D_MODEL), RET_V_W),
        "w_na_out": w(ks[11], (NA_W, D_MODEL), NA_W),
        "w_mix_out": w(ks[12], (D_MODEL, D_MODEL), D_MODEL),
        "mix_post_norm": gain(ks[13]),
        "ffn2_pre_norm": gain(ks[14]),
        "ffn2_w_in": w(ks[15], (D_MODEL, 2 * D_FF), D_MODEL),
        "ffn2_w_out": w(ks[16], (D_FF, D_MODEL), D_FF),
        "ffn2_post_norm": gain(ks[17]),
    }


def reference(x, ffn1_pre_norm, ffn1_w_in, ffn1_w_out, ffn1_post_norm, mix_pre_norm, w_mix_in,
              ret_decay_fwd, ret_decay_bwd, na_rel_bias, w_ret_out, w_na_out, w_mix_out, mix_post_norm,
              ffn2_pre_norm, ffn2_w_in, ffn2_w_out, ffn2_post_norm):
    S = x.shape[1]
    pos = jnp.arange(S, dtype=jnp.float32)
    for l in range(DEPTH):
        h = _swiglu(_rms_norm(x, ffn1_pre_norm[l]), ffn1_w_in[l], ffn1_w_out[l])
        x = x + 0.5 * _rms_norm(h, ffn1_post_norm[l])
        m = _token_mixing(_rms_norm(x, mix_pre_norm[l]), w_mix_in[l], ret_decay_fwd[l], ret_decay_bwd[l],
                          na_rel_bias[l], w_ret_out[l], w_na_out[l], w_mix_out[l], pos)
        x = x + _rms_norm(m, mix_post_norm[l])
        h = _swiglu(_rms_norm(x, ffn2_pre_norm[l]), ffn2_w_in[l], ffn2_w_out[l])
        x = x + 0.5 * _rms_norm(h, ffn2_post_norm[l])
    return x
```

```python
import functools

import numpy as np
import jax
import jax.numpy as jnp
from jax import lax
from jax.experimental import pallas as pl
from jax.experimental.pallas import tpu as pltpu

F32 = jnp.float32
BF16 = jnp.bfloat16

D_MODEL = 1024
D_FF = 2816
RMS_EPS = 1e-6
ROPE_BASE = 10000.0
NEG_INF = -1e30

RET_HEADS = 4
RET_QK_DIM = 128
RET_V_DIM = 256
RET_QK_W = RET_HEADS * RET_QK_DIM
RET_V_W = RET_HEADS * RET_V_DIM
RET_BLOCK = 256

NA_HEADS = 8
NA_HEAD_DIM = 64
NA_W = NA_HEADS * NA_HEAD_DIM
GRID_W = 64
NA_WIN_ROWS = 8
NA_WIN_COLS = 16
NA_REL_ROWS = 2 * NA_WIN_ROWS - 1
NA_REL_COLS = 2 * NA_WIN_COLS - 1
NA_Q_ROWS = 8
NA_Q_COLS = 16
NA_K_ROWS = 16
NA_K_COLS = 32
NA_LANES = 128
NA_SLABS = NA_W // NA_LANES

MIX_SPLITS = (RET_QK_W, RET_QK_W, RET_V_W, RET_V_W, NA_W, NA_W, NA_W, D_MODEL, D_MODEL)
MIX_OFFS = tuple(int(v) for v in np.cumsum((0,) + MIX_SPLITS[:-1]))
MIX_IN_W = sum(MIX_SPLITS)

TOKEN_TILE = 512
FF_CHUNK = 256
VMEM_LIMIT = 56 * 1024 * 1024


def _rms(x, gain):
    ms = jnp.mean(x * x, axis=-1, keepdims=True)
    return x * lax.rsqrt(ms + RMS_EPS) * gain


def _silu(x):
    return x * jax.nn.sigmoid(x)


def _const_spec(shape):
    nd = len(shape)
    return pl.BlockSpec(shape, lambda *_: (0,) * nd, pipeline_mode=pl.Buffered(1))


def _ffn_kernel(x_ref, gpre_ref, win_ref, wout_ref, gpost_ref, o_ref, act_ref):
    x = x_ref[...]
    xn = _rms(x, gpre_ref[...]).astype(BF16)
    for c in range(D_FF // FF_CHUNK):
        lo = c * FF_CHUNK
        g = jnp.dot(xn, win_ref[:, lo:lo + FF_CHUNK], preferred_element_type=F32)
        u = jnp.dot(xn, win_ref[:, D_FF + lo:D_FF + lo + FF_CHUNK], preferred_element_type=F32)
        act_ref[:, lo:lo + FF_CHUNK] = (_silu(g) * u).astype(BF16)
    h = jnp.dot(act_ref[...], wout_ref[...], preferred_element_type=F32)
    o_ref[...] = x + 0.5 * _rms(h, gpost_ref[...])


def _ffn(x, gpre, w_in, w_out, gpost):
    t = x.shape[0]
    tm = TOKEN_TILE
    row = pl.BlockSpec((tm, D_MODEL), lambda i: (i, 0))
    return pl.pallas_call(
        _ffn_kernel,
        grid=(t // tm,),
        in_specs=[row, _const_spec((1, D_MODEL)), _const_spec((D_MODEL, 2 * D_FF)),
                  _const_spec((D_FF, D_MODEL)), _const_spec((1, D_MODEL))],
        out_specs=row,
        out_shape=jax.ShapeDtypeStruct((t, D_MODEL), F32),
        scratch_shapes=[pltpu.VMEM((tm, D_FF), BF16)],
        compiler_params=pltpu.CompilerParams(
            dimension_semantics=("parallel",), vmem_limit_bytes=VMEM_LIMIT),
        name="ffn",
    )(x, gpre, w_in, w_out, gpost)


def _mixin_kernel(x_ref, gain_ref, w_ref, rope_ref,
                  rq_ref, rk_ref, rv_ref, rg_ref, nq_ref, nk_ref, nv_ref, gr_ref, gn_ref):
    u = _rms(x_ref[...], gain_ref[...]).astype(BF16)

    def proj(idx):
        lo = MIX_OFFS[idx]
        return jnp.dot(u, w_ref[:, lo:lo + MIX_SPLITS[idx]], preferred_element_type=F32)

    def rotary(y, cos, sin, out_ref):
        for h in range(RET_HEADS):
            sl = slice(h * RET_QK_DIM, (h + 1) * RET_QK_DIM)
            yh = y[:, sl]
            out_ref[:, sl] = (yh * cos + pltpu.roll(yh, RET_QK_DIM // 2, 1) * sin).astype(out_ref.dtype)

    rotary(proj(0), rope_ref[0], rope_ref[1], rq_ref)
    rotary(proj(1), rope_ref[2], rope_ref[3], rk_ref)
    rv_ref[...] = proj(2).astype(rv_ref.dtype)
    rg_ref[...] = proj(3).astype(rg_ref.dtype)
    nq_ref[...] = (proj(4) * (NA_HEAD_DIM ** -0.5)).astype(nq_ref.dtype)
    nk_ref[...] = proj(5).astype(nk_ref.dtype)
    nv_ref[...] = proj(6).astype(nv_ref.dtype)
    gr_ref[...] = proj(7).astype(gr_ref.dtype)
    gn_ref[...] = proj(8).astype(gn_ref.dtype)


def _rope_tables(seq):
    half = RET_QK_DIM // 2
    pos = jnp.arange(seq, dtype=F32)
    inv = 1.0 / (ROPE_BASE ** jnp.linspace(0.0, 1.0, half, dtype=F32))
    ang = pos[:, None] * inv[None, :]
    cos, sin = jnp.cos(ang), jnp.sin(ang)
    cos2 = jnp.concatenate([cos, cos], axis=-1)
    sin2 = jnp.concatenate([-sin, sin], axis=-1)
    ks = RET_QK_DIM ** -0.5
    return jnp.stack([cos2, sin2, cos2 * ks, sin2 * ks])


def _mix_in(x, gain, w, seq):
    t = x.shape[0]
    tm = TOKEN_TILE
    tiles_per_seq = seq // tm
    out_dtypes = (BF16, BF16, BF16, BF16, BF16, F32, F32, BF16, BF16)
    return pl.pallas_call(
        _mixin_kernel,
        grid=(t // tm,),
        in_specs=[pl.BlockSpec((tm, D_MODEL), lambda i: (i, 0)),
                  _const_spec((1, D_MODEL)), _const_spec((D_MODEL, MIX_IN_W)),
                  pl.BlockSpec((4, tm, RET_QK_DIM), lambda i: (0, i % tiles_per_seq, 0))],
        out_specs=[pl.BlockSpec((tm, wd), lambda i: (i, 0)) for wd in MIX_SPLITS],
        out_shape=[jax.ShapeDtypeStruct((t, wd), dt) for wd, dt in zip(MIX_SPLITS, out_dtypes)],
        compiler_params=pltpu.CompilerParams(
            dimension_semantics=("parallel",), vmem_limit_bytes=VMEM_LIMIT),
        name="mix_in",
    )(x, gain, w, _rope_tables(seq))


def _log_sigmoid(z):
    return jnp.minimum(z, 0.0) - jnp.log1p(jnp.exp(-jnp.abs(z)))


def _ret_kernel(dec_ref, q_ref, k_ref, v_ref, g_ref, o_ref, sb_ref, dm_ref, vec_ref):
    c = RET_BLOCK
    dk = RET_QK_DIM
    dv = RET_V_DIM
    n_blocks = q_ref.shape[0] // c
    head = pl.program_id(1)
    zf = dec_ref[0, head]
    zb = dec_ref[1, head]

    def log_gamma(z, shape):
        return _log_sigmoid(jnp.full(shape, z, F32))

    ri = lax.broadcasted_iota(jnp.int32, (c, c), 0).astype(F32)
    ci = lax.broadcasted_iota(jnp.int32, (c, c), 1).astype(F32)
    diff = ri - ci
    dm_ref[...] = jnp.where(diff >= 0, jnp.exp(log_gamma(zf, (c, c)) * jnp.maximum(diff, 0.0)),
                            jnp.exp(log_gamma(zb, (c, c)) * jnp.maximum(-diff, 0.0)))
    r = lax.broadcasted_iota(jnp.int32, (c, dk), 0).astype(F32)
    lgf = log_gamma(zf, (c, dk))
    lgb = log_gamma(zb, (c, dk))
    vec_ref[0] = jnp.exp(lgf * (r + 1.0))
    vec_ref[1] = jnp.exp(lgb * (c - r))
    vec_ref[2] = jnp.exp(lgf * (c - 1.0 - r))
    vec_ref[3] = jnp.exp(lgb * r)
    blk_f = jnp.exp(log_gamma(zf, (dk, dv)) * c)
    blk_b = jnp.exp(log_gamma(zb, (dk, dv)) * c)
    tn = (((0,), (0,)), ((), ()))
    nt = (((1,), (1,)), ((), ()))

    def rows_of(n):
        return pl.ds(pl.multiple_of(n * c, c), c)

    def state_rows(n):
        return pl.ds(pl.multiple_of(n * dk, dk), dk)

    def reverse_step(i, sb):
        n = n_blocks - 1 - i
        sb_ref[state_rows(n), :] = sb.astype(BF16)
        kb = (k_ref[rows_of(n), :].astype(F32) * vec_ref[3]).astype(BF16)
        upd = lax.dot_general(kb, v_ref[rows_of(n), :], tn, preferred_element_type=F32)
        return blk_b * sb + upd

    lax.fori_loop(0, n_blocks, reverse_step, jnp.zeros((dk, dv), F32))

    def forward_step(n, sf):
        rows = rows_of(n)
        q = q_ref[rows, :]
        k = k_ref[rows, :]
        v = v_ref[rows, :]
        s = lax.dot_general(q, k, nt, preferred_element_type=F32)
        p = (s * dm_ref[...]).astype(BF16)
        qf = q.astype(F32)
        q_cat = jnp.concatenate([(qf * vec_ref[0]).astype(BF16), (qf * vec_ref[1]).astype(BF16)], axis=1)
        s_cat = jnp.concatenate([sf.astype(BF16), sb_ref[state_rows(n), :]], axis=0)
        y = (jnp.dot(p, v, preferred_element_type=F32)
             + jnp.dot(q_cat, s_cat, preferred_element_type=F32))
        yn = y * lax.rsqrt(jnp.mean(y * y, axis=-1, keepdims=True) + RMS_EPS)
        o_ref[rows, :] = (_silu(g_ref[rows, :].astype(F32)) * yn).astype(o_ref.dtype)
        kf = (k.astype(F32) * vec_ref[2]).astype(BF16)
        upd = lax.dot_general(kf, v, tn, preferred_element_type=F32)
        return blk_f * sf + upd

    lax.fori_loop(0, n_blocks, forward_step, jnp.zeros((dk, dv), F32))


def _retention(decay_logits, rq, rk, rv, rg):
    b, s, _ = rq.shape
    c = RET_BLOCK
    qk_spec = pl.BlockSpec((None, s, RET_QK_DIM), lambda i, h, dec: (i, 0, h))
    v_spec = pl.BlockSpec((None, s, RET_V_DIM), lambda i, h, dec: (i, 0, h))
    return pl.pallas_call(
        _ret_kernel,
        grid_spec=pltpu.PrefetchScalarGridSpec(
            num_scalar_prefetch=1,
            grid=(b, RET_HEADS),
            in_specs=[qk_spec, qk_spec, v_spec, v_spec],
            out_specs=v_spec,
            scratch_shapes=[pltpu.VMEM((s // c * RET_QK_DIM, RET_V_DIM), BF16),
                            pltpu.VMEM((c, c), F32),
                            pltpu.VMEM((4, c, RET_QK_DIM), F32)],
        ),
        out_shape=jax.ShapeDtypeStruct((b, s, RET_V_W), BF16),
        compiler_params=pltpu.CompilerParams(
            dimension_semantics=("parallel", "parallel"), vmem_limit_bytes=VMEM_LIMIT),
        name="retention",
    )(decay_logits, rq, rk, rv, rg)


_NA_ROW_BLOCKS = tuple((8 * i, (0, 4, 12, 16)[i], (0, 1, 1, 2)[i]) for i in range(4))
_NA_COL_BLOCKS = tuple((16 * i, (0, 8, 24, 32)[i], (0, 1, 1, 2)[i]) for i in range(4))
_NA_CONFIGS = 9


def _na_bias_tables(rel_bias, rows):
    kr = min(NA_WIN_ROWS, rows)
    tabs_r, tabs_c, mask_r, mask_c = [], [], [], []
    for r0, ks, _ in (_NA_ROW_BLOCKS[0], _NA_ROW_BLOCKS[1], _NA_ROW_BLOCKS[3]):
        qr = r0 + np.arange(NA_Q_ROWS)[:, None]
        kr_abs = ks + np.arange(NA_K_ROWS)[None, :]
        rs = np.clip(qr - kr // 2, 0, rows - kr)
        mask_r.append((kr_abs >= rs) & (kr_abs < rs + kr))
        tabs_r.append(np.clip(kr_abs - qr + NA_WIN_ROWS - 1, 0, NA_REL_ROWS - 1))
    for c0, kc0, _ in (_NA_COL_BLOCKS[0], _NA_COL_BLOCKS[1], _NA_COL_BLOCKS[3]):
        qc = c0 + np.arange(NA_Q_COLS)[:, None]
        kc_abs = kc0 + np.arange(NA_K_COLS)[None, :]
        ws = np.clip(qc - NA_WIN_COLS // 2, 0, GRID_W - NA_WIN_COLS)
        mask_c.append((kc_abs >= ws) & (kc_abs < ws + NA_WIN_COLS))
        tabs_c.append(np.clip(kc_abs - qc + NA_WIN_COLS - 1, 0, NA_REL_COLS - 1))
    row_idx = np.stack(tabs_r)[:, None, :, None, :, None]
    col_idx = np.stack(tabs_c)[None, :, None, :, None, :]
    mask = np.stack(mask_r)[:, None, :, None, :, None] & np.stack(mask_c)[None, :, None, :, None, :]
    bias = rel_bias.astype(F32)[:, row_idx, col_idx]
    bias = jnp.where(mask[None], bias, NEG_INF)
    return bias.reshape(NA_HEADS, _NA_CONFIGS, NA_Q_ROWS * NA_Q_COLS, NA_K_ROWS * NA_K_COLS)


def _na_kernel(q_ref, k_ref, v_ref, bias_ref, o_ref):
    nq = NA_Q_ROWS * NA_Q_COLS
    nk = NA_K_ROWS * NA_K_COLS
    first = lax.broadcasted_iota(jnp.int32, (nq, NA_LANES), 1) < NA_HEAD_DIM
    nt = (((1,), (1,)), ((), ()))
    for r0, ks, rcfg in _NA_ROW_BLOCKS:
        for c0, kc0, ccfg in _NA_COL_BLOCKS:
            cfg = rcfg * 3 + ccfg
            q = q_ref[r0:r0 + NA_Q_ROWS, c0:c0 + NA_Q_COLS, :].reshape(nq, NA_LANES)
            k = k_ref[ks:ks + NA_K_ROWS, kc0:kc0 + NA_K_COLS, :].reshape(nk, NA_LANES).astype(BF16)
            v = v_ref[ks:ks + NA_K_ROWS, kc0:kc0 + NA_K_COLS, :].reshape(nk, NA_LANES).astype(BF16)
            outs = []
            for hh in range(2):
                qm = jnp.where(first if hh == 0 else jnp.logical_not(first), q, jnp.zeros_like(q))
                s = lax.dot_general(qm, k, nt, preferred_element_type=F32) + bias_ref[hh, cfg]
                m = jnp.max(s, axis=-1, keepdims=True)
                p = jnp.exp(s - m)
                l = jnp.sum(p, axis=-1, keepdims=True)
                outs.append(jnp.dot(p.astype(BF16), v, preferred_element_type=F32) / l)
            o = jnp.where(first, outs[0], outs[1])
            o_ref[r0:r0 + NA_Q_ROWS, c0:c0 + NA_Q_COLS, :] = (
                o.reshape(NA_Q_ROWS, NA_Q_COLS, NA_LANES).astype(o_ref.dtype))


def _natten(nq, nk, nv, bias_tabs):
    b, rows = nq.shape[0], nq.shape[1]
    slab = pl.BlockSpec((None, rows, GRID_W, NA_LANES), lambda j, i: (i, 0, 0, j))
    nqk = NA_Q_ROWS * NA_Q_COLS
    nkk = NA_K_ROWS * NA_K_COLS
    return pl.pallas_call(
        _na_kernel,
        grid=(NA_SLABS, b),
        in_specs=[slab, slab, slab,
                  pl.BlockSpec((2, _NA_CONFIGS, nqk, nkk), lambda j, i: (j, 0, 0, 0))],
        out_specs=slab,
        out_shape=jax.ShapeDtypeStruct((b, rows, GRID_W, NA_W), BF16),
        compiler_params=pltpu.CompilerParams(
            dimension_semantics=("parallel", "parallel"), vmem_limit_bytes=VMEM_LIMIT),
        name="natten",
    )(nq, nk, nv, bias_tabs)


def _mixout_kernel(x_ref, yr_ref, na_ref, gr_ref, gn_ref, wr_ref, wn_ref, wo_ref, gain_ref, o_ref):
    y_ret = jnp.dot(yr_ref[...], wr_ref[...], preferred_element_type=F32)
    y_na = jnp.dot(na_ref[...], wn_ref[...], preferred_element_type=F32)
    merged = (jax.nn.sigmoid(gr_ref[...].astype(F32)) * y_ret
              + jax.nn.sigmoid(gn_ref[...].astype(F32)) * y_na)
    m = jnp.dot(merged.astype(BF16), wo_ref[...], preferred_element_type=F32)
    o_ref[...] = x_ref[...] + _rms(m, gain_ref[...])


def _mix_out(x, y_ret, y_na, g_ret, g_na, w_ret, w_na, w_out, gain):
    t = x.shape[0]
    tm = TOKEN_TILE

    def row(width):
        return pl.BlockSpec((tm, width), lambda i: (i, 0))

    return pl.pallas_call(
        _mixout_kernel,
        grid=(t // tm,),
        in_specs=[row(D_MODEL), row(RET_V_W), row(NA_W), row(D_MODEL), row(D_MODEL),
                  _const_spec((RET_V_W, D_MODEL)), _const_spec((NA_W, D_MODEL)),
                  _const_spec((D_MODEL, D_MODEL)), _const_spec((1, D_MODEL))],
        out_specs=row(D_MODEL),
        out_shape=jax.ShapeDtypeStruct((t, D_MODEL), F32),
        compiler_params=pltpu.CompilerParams(
            dimension_semantics=("parallel",), vmem_limit_bytes=VMEM_LIMIT),
        name="mix_out",
    )(x, y_ret, y_na, g_ret, g_na, w_ret, w_na, w_out, gain)


def kernel(x, ffn1_pre_norm, ffn1_w_in, ffn1_w_out, ffn1_post_norm, mix_pre_norm, w_mix_in,
           ret_decay_fwd, ret_decay_bwd, na_rel_bias, w_ret_out, w_na_out, w_mix_out, mix_post_norm,
           ffn2_pre_norm, ffn2_w_in, ffn2_w_out, ffn2_post_norm):
    b, s, d = x.shape
    assert d == D_MODEL and s % GRID_W == 0 and s % RET_BLOCK == 0 and (b * s) % TOKEN_TILE == 0
    assert s % TOKEN_TILE == 0
    rows = s // GRID_W
    assert rows == 32, "the neighbourhood-attention tiling below is laid out for a 32 x 64 token grid"
    t = b * s
    xt = x.reshape(t, d)
    for l in range(ffn1_w_in.shape[0]):
        gain = lambda g: g[l].reshape(1, D_MODEL).astype(F32)
        wt = lambda w: w[l].astype(BF16)
        xt = _ffn(xt, gain(ffn1_pre_norm), wt(ffn1_w_in), wt(ffn1_w_out), gain(ffn1_post_norm))
        rq, rk, rv, rg, nq, nk, nv, g_ret, g_na = _mix_in(xt, gain(mix_pre_norm), wt(w_mix_in), s)
        decay = jnp.stack([ret_decay_fwd[l], ret_decay_bwd[l]]).astype(F32)
        seq = lambda a: a.reshape(b, s, a.shape[-1])
        y_ret = _retention(decay, seq(rq), seq(rk), seq(rv), seq(rg)).reshape(t, RET_V_W)
        grid4 = lambda a: a.reshape(b, rows, GRID_W, NA_W)
        y_na = _natten(grid4(nq), grid4(nk), grid4(nv), _na_bias_tables(na_rel_bias[l], rows))
        xt = _mix_out(xt, y_ret, y_na.reshape(t, NA_W), g_ret, g_na,
                      wt(w_ret_out), wt(w_na_out), wt(w_mix_out), gain(mix_post_norm))
        xt = _ffn(xt, gain(ffn2_pre_norm), wt(ffn2_w_in), wt(ffn2_w_out), gain(ffn2_post_norm))
    return xt.reshape(b, s, d)
```

```python
import functools

import numpy as np
import jax
import jax.numpy as jnp
from jax import lax
from jax.experimental import pallas as pl
from jax.experimental.pallas import tpu as pltpu

F32 = jnp.float32
BF16 = jnp.bfloat16

D_MODEL = 1024
D_FF = 2816
RMS_EPS = 1e-6
ROPE_BASE = 10000.0
NEG_INF = -1e30

RET_HEADS = 4
RET_QK_DIM = 128
RET_V_DIM = 256
RET_QK_W = RET_HEADS * RET_QK_DIM
RET_V_W = RET_HEADS * RET_V_DIM
RET_BLOCK = 256

NA_HEADS = 8
NA_HEAD_DIM = 64
NA_W = NA_HEADS * NA_HEAD_DIM
GRID_W = 64
NA_WIN_ROWS = 8
NA_WIN_COLS = 16
NA_REL_ROWS = 2 * NA_WIN_ROWS - 1
NA_REL_COLS = 2 * NA_WIN_COLS - 1
NA_Q_ROWS = 8
NA_Q_COLS = 16
NA_K_ROWS = 16
NA_K_COLS = 32
NA_LANES = 128
NA_SLABS = NA_W // NA_LANES

MIX_SPLITS = (RET_QK_W, RET_QK_W, RET_V_W, RET_V_W, NA_W, NA_W, NA_W, D_MODEL, D_MODEL)
MIX_OFFS = tuple(int(v) for v in np.cumsum((0,) + MIX_SPLITS[:-1]))
MIX_IN_W = sum(MIX_SPLITS)

TOKEN_TILE = 512
FF_CHUNK = 256
VMEM_LIMIT = 56 * 1024 * 1024


def _rms(x, gain):
    ms = jnp.mean(x * x, axis=-1, keepdims=True)
    return x * lax.rsqrt(ms + RMS_EPS) * gain


def _silu(x):
    return x * jax.nn.sigmoid(x)


def _const_spec(shape):
    nd = len(shape)
    return pl.BlockSpec(shape, lambda *_: (0,) * nd, pipeline_mode=pl.Buffered(1))


def _ffn_kernel(x_ref, gpre_ref, win_ref, wout_ref, gpost_ref, o_ref, act_ref):
    x = x_ref[...]
    xn = _rms(x, gpre_ref[...]).astype(BF16)
    for c in range(D_FF // FF_CHUNK):
        lo = c * FF_CHUNK
        g = jnp.dot(xn, win_ref[:, lo:lo + FF_CHUNK], preferred_element_type=F32)
        u = jnp.dot(xn, win_ref[:, D_FF + lo:D_FF + lo + FF_CHUNK], preferred_element_type=F32)
        act_ref[:, lo:lo + FF_CHUNK] = (_silu(g) * u).astype(BF16)
    h = jnp.dot(act_ref[...], wout_ref[...], preferred_element_type=F32)
    o_ref[...] = x + 0.5 * _rms(h, gpost_ref[...])


def _ffn(x, gpre, w_in, w_out, gpost):
    t = x.shape[0]
    tm = TOKEN_TILE
    row = pl.BlockSpec((tm, D_MODEL), lambda i: (i, 0))
    return pl.pallas_call(
        _ffn_kernel,
        grid=(t // tm,),
        in_specs=[row, _const_spec((1, D_MODEL)), _const_spec((D_MODEL, 2 * D_FF)),
                  _const_spec((D_FF, D_MODEL)), _const_spec((1, D_MODEL))],
        out_specs=row,
        out_shape=jax.ShapeDtypeStruct((t, D_MODEL), F32),
        scratch_shapes=[pltpu.VMEM((tm, D_FF), BF16)],
        compiler_params=pltpu.CompilerParams(
            dimension_semantics=("parallel",), vmem_limit_bytes=VMEM_LIMIT),
        name="ffn",
    )(x, gpre, w_in, w_out, gpost)


def _mixin_kernel(x_ref, gain_ref, w_ref, rope_ref,
                  rq_ref, rk_ref, rv_ref, rg_ref, nq_ref, nk_ref, nv_ref, gr_ref, gn_ref):
    u = _rms(x_ref[...], gain_ref[...]).astype(BF16)

    def proj(idx):
        lo = MIX_OFFS[idx]
        return jnp.dot(u, w_ref[:, lo:lo + MIX_SPLITS[idx]], preferred_element_type=F32)

    def rotary(y, cos, sin, out_ref):
        for h in range(RET_HEADS):
            sl = slice(h * RET_QK_DIM, (h + 1) * RET_QK_DIM)
            yh = y[:, sl]
            out_ref[:, sl] = (yh * cos + pltpu.roll(yh, RET_QK_DIM // 2, 1) * sin).astype(out_ref.dtype)

    rotary(proj(0), rope_ref[0], rope_ref[1], rq_ref)
    rotary(proj(1), rope_ref[2], rope_ref[3], rk_ref)
    rv_ref[...] = proj(2).astype(rv_ref.dtype)
    rg_ref[...] = proj(3).astype(rg_ref.dtype)
    nq_ref[...] = (proj(4) * (NA_HEAD_DIM ** -0.5)).astype(nq_ref.dtype)
    nk_ref[...] = proj(5).astype(nk_ref.dtype)
    nv_ref[...] = proj(6).astype(nv_ref.dtype)
    gr_ref[...] = proj(7).astype(gr_ref.dtype)
    gn_ref[...] = proj(8).astype(gn_ref.dtype)


def _rope_tables(seq):
    half = RET_QK_DIM // 2
    pos = jnp.arange(seq, dtype=F32)
    inv = 1.0 / (ROPE_BASE ** jnp.linspace(0.0, 1.0, half, dtype=F32))
    ang = pos[:, None] * inv[None, :]
    cos, sin = jnp.cos(ang), jnp.sin(ang)
    cos2 = jnp.concatenate([cos, cos], axis=-1)
    sin2 = jnp.concatenate([-sin, sin], axis=-1)
    ks = RET_QK_DIM ** -0.5
    return jnp.stack([cos2, sin2, cos2 * ks, sin2 * ks])


def _mix_in(x, gain, w, seq):
    t = x.shape[0]
    tm = TOKEN_TILE
    tiles_per_seq = seq // tm
    out_dtypes = (BF16, BF16, BF16, BF16, BF16, F32, F32, BF16, BF16)
    return pl.pallas_call(
        _mixin_kernel,
        grid=(t // tm,),
        in_specs=[pl.BlockSpec((tm, D_MODEL), lambda i: (i, 0)),
                  _const_spec((1, D_MODEL)), _const_spec((D_MODEL, MIX_IN_W)),
                  pl.BlockSpec((4, tm, RET_QK_DIM), lambda i: (0, i % tiles_per_seq, 0))],
        out_specs=[pl.BlockSpec((tm, wd), lambda i: (i, 0)) for wd in MIX_SPLITS],
        out_shape=[jax.ShapeDtypeStruct((t, wd), dt) for wd, dt in zip(MIX_SPLITS, out_dtypes)],
        compiler_params=pltpu.CompilerParams(
            dimension_semantics=("parallel",), vmem_limit_bytes=VMEM_LIMIT),
        name="mix_in",
    )(x, gain, w, _rope_tables(seq))


def _log_sigmoid(z):
    return jnp.minimum(z, 0.0) - jnp.log1p(jnp.exp(-jnp.abs(z)))


def _ret_kernel(dec_ref, q_ref, k_ref, v_ref, g_ref, o_ref, sb_ref, dm_ref, vec_ref):
    c = RET_BLOCK
    dk = RET_QK_DIM
    dv = RET_V_DIM
    n_blocks = q_ref.shape[0] // c
    head = pl.program_id(1)
    zf = dec_ref[0, head]
    zb = dec_ref[1, head]

    def log_gamma(z, shape):
        return _log_sigmoid(jnp.full(shape, z, F32))

    ri = lax.broadcasted_iota(jnp.int32, (c, c), 0).astype(F32)
    ci = lax.broadcasted_iota(jnp.int32, (c, c), 1).astype(F32)
    diff = ri - ci
    dm_ref[...] = jnp.where(diff >= 0, jnp.exp(log_gamma(zf, (c, c)) * jnp.maximum(diff, 0.0)),
                            jnp.exp(log_gamma(zb, (c, c)) * jnp.maximum(-diff, 0.0)))
    r = lax.broadcasted_iota(jnp.int32, (c, dk), 0).astype(F32)
    lgf = log_gamma(zf, (c, dk))
    lgb = log_gamma(zb, (c, dk))
    vec_ref[0] = jnp.exp(lgf * (r + 1.0))
    vec_ref[1] = jnp.exp(lgb * (c - r))
    vec_ref[2] = jnp.exp(lgf * (c - 1.0 - r))
    vec_ref[3] = jnp.exp(lgb * r)
    blk_f = jnp.exp(log_gamma(zf, (dk, dv)) * c)
    blk_b = jnp.exp(log_gamma(zb, (dk, dv)) * c)
    tn = (((0,), (0,)), ((), ()))
    nt = (((1,), (1,)), ((), ()))

    def rows_of(n):
        return pl.ds(pl.multiple_of(n * c, c), c)

    def state_rows(n):
        return pl.ds(pl.multiple_of(n * dk, dk), dk)

    def reverse_step(i, sb):
        n = n_blocks - 1 - i
        sb_ref[state_rows(n), :] = sb.astype(BF16)
        kb = (k_ref[rows_of(n), :].astype(F32) * vec_ref[3]).astype(BF16)
        upd = lax.dot_general(kb, v_ref[rows_of(n), :], tn, preferred_element_type=F32)
        return blk_b * sb + upd

    lax.fori_loop(0, n_blocks, reverse_step, jnp.zeros((dk, dv), F32))

    def forward_step(n, sf):
        rows = rows_of(n)
        q = q_ref[rows, :]
        k = k_ref[rows, :]
        v = v_ref[rows, :]
        s = lax.dot_general(q, k, nt, preferred_element_type=F32)
        p = (s * dm_ref[...]).astype(BF16)
        qf = q.astype(F32)
        q_cat = jnp.concatenate([(qf * vec_ref[0]).astype(BF16), (qf * vec_ref[1]).astype(BF16)], axis=1)
        s_cat = jnp.concatenate([sf.astype(BF16), sb_ref[state_rows(n), :]], axis=0)
        y = (jnp.dot(p, v, preferred_element_type=F32)
             + jnp.dot(q_cat, s_cat, preferred_element_type=F32))
        yn = y * lax.rsqrt(jnp.mean(y * y, axis=-1, keepdims=True) + RMS_EPS)
        o_ref[rows, :] = (_silu(g_ref[rows, :].astype(F32)) * yn).astype(o_ref.dtype)
        kf = (k.astype(F32) * vec_ref[2]).astype(BF16)
        upd = lax.dot_general(kf, v, tn, preferred_element_type=F32)
        return blk_f * sf + upd

    lax.fori_loop(0, n_blocks, forward_step, jnp.zeros((dk, dv), F32))


def _retention(decay_logits, rq, rk, rv, rg):
    b, s, _ = rq.shape
    c = RET_BLOCK
    qk_spec = pl.BlockSpec((None, s, RET_QK_DIM), lambda i, h, dec: (i, 0, h))
    v_spec = pl.BlockSpec((None, s, RET_V_DIM), lambda i, h, dec: (i, 0, h))
    return pl.pallas_call(
        _ret_kernel,
        grid_spec=pltpu.PrefetchScalarGridSpec(
            num_scalar_prefetch=1,
            grid=(b, RET_HEADS),
            in_specs=[qk_spec, qk_spec, v_spec, v_spec],
            out_specs=v_spec,
            scratch_shapes=[pltpu.VMEM((s // c * RET_QK_DIM, RET_V_DIM), BF16),
                            pltpu.VMEM((c, c), F32),
                            pltpu.VMEM((4, c, RET_QK_DIM), F32)],
        ),
        out_shape=jax.ShapeDtypeStruct((b, s, RET_V_W), BF16),
        compiler_params=pltpu.CompilerParams(
            dimension_semantics=("parallel", "parallel"), vmem_limit_bytes=VMEM_LIMIT),
        name="retention",
    )(decay_logits, rq, rk, rv, rg)


def _na_blocks(extent, q_size, k_size, window):
    n = extent // q_size
    out = []
    for i in range(n):
        q0 = i * q_size
        k0 = min(max(q0 - window // 2, 0), extent - k_size)
        out.append((q0, k0, 0 if i == 0 else (2 if i == n - 1 else 1)))
    return tuple(out)


NA_ROWS = 32
_NA_ROW_BLOCKS = _na_blocks(NA_ROWS, NA_Q_ROWS, NA_K_ROWS, NA_WIN_ROWS)
_NA_COL_BLOCKS = _na_blocks(GRID_W, NA_Q_COLS, NA_K_COLS, NA_WIN_COLS)
_NA_CONFIGS = 9


def _na_bias_kernel(rel_ref, out_ref, col_ref):
    nk = NA_K_ROWS * NA_K_COLS
    base = pl.program_id(0) * (NA_REL_ROWS * NA_REL_COLS)
    lane = lax.broadcasted_iota(jnp.int32, (NA_Q_COLS, nk), 1)
    qc = lax.broadcasted_iota(jnp.int32, (NA_Q_COLS, nk), 0)
    kc = lane % NA_K_COLS
    kr = lane // NA_K_COLS
    col_cfgs = (_NA_COL_BLOCKS[0], _NA_COL_BLOCKS[1], _NA_COL_BLOCKS[3])
    row_cfgs = (_NA_ROW_BLOCKS[0], _NA_ROW_BLOCKS[1], _NA_ROW_BLOCKS[3])
    for b, (c0, kc0, _) in enumerate(col_cfgs):
        kc_abs = kc + kc0
        qc_abs = qc + c0
        rel_col = jnp.clip(kc_abs - qc_abs + NA_WIN_COLS - 1, 0, NA_REL_COLS - 1)
        win = jnp.clip(qc_abs - NA_WIN_COLS // 2, 0, GRID_W - NA_WIN_COLS)
        in_win = (kc_abs >= win) & (kc_abs < win + NA_WIN_COLS)

        def rel_row_body(dr, carry, rel_col=rel_col, in_win=in_win, b=b):
            def rel_col_body(dc, acc):
                return jnp.where(rel_col == dc, rel_ref[base + dr * NA_REL_COLS + dc], acc)
            vals = lax.fori_loop(0, NA_REL_COLS, rel_col_body, jnp.zeros((NA_Q_COLS, nk), F32))
            col_ref[b * NA_REL_ROWS + dr] = jnp.where(in_win, vals, NEG_INF)
            return carry

        lax.fori_loop(0, NA_REL_ROWS, rel_row_body, 0)

    for a, (r0, ks, _) in enumerate(row_cfgs):
        kr_abs = kr + ks
        for qr in range(NA_Q_ROWS):
            r_abs = r0 + qr
            rs = min(max(r_abs - NA_WIN_ROWS // 2, 0), NA_ROWS - NA_WIN_ROWS)
            hits = [kr_abs == rs + j for j in range(NA_WIN_ROWS)]
            for b in range(3):
                tile = jnp.full((NA_Q_COLS, nk), NEG_INF, F32)
                for j in range(NA_WIN_ROWS):
                    dr = rs + j - r_abs + NA_WIN_ROWS - 1
                    tile = jnp.where(hits[j], col_ref[b * NA_REL_ROWS + dr], tile)
                out_ref[0, a * 3 + b, qr * NA_Q_COLS:(qr + 1) * NA_Q_COLS, :] = tile


def _na_bias_tables(rel_bias):
    nq = NA_Q_ROWS * NA_Q_COLS
    nk = NA_K_ROWS * NA_K_COLS
    return pl.pallas_call(
        _na_bias_kernel,
        grid_spec=pltpu.PrefetchScalarGridSpec(
            num_scalar_prefetch=1,
            grid=(NA_HEADS,),
            in_specs=[],
            out_specs=pl.BlockSpec((1, _NA_CONFIGS, nq, nk), lambda h, rel: (h, 0, 0, 0)),
            scratch_shapes=[pltpu.VMEM((3 * NA_REL_ROWS, NA_Q_COLS, nk), F32)],
        ),
        out_shape=jax.ShapeDtypeStruct((NA_HEADS, _NA_CONFIGS, nq, nk), F32),
        compiler_params=pltpu.CompilerParams(dimension_semantics=("parallel",)),
        name="na_bias",
    )(rel_bias.astype(F32).reshape(-1))


def _na_kernel(q_ref, k_ref, v_ref, bias_ref, o_ref):
    nq = NA_Q_ROWS * NA_Q_COLS
    nk = NA_K_ROWS * NA_K_COLS
    first = lax.broadcasted_iota(jnp.int32, (nq, NA_LANES), 1) < NA_HEAD_DIM
    nt = (((1,), (1,)), ((), ()))
    for r0, ks, rcfg in _NA_ROW_BLOCKS:
        for c0, kc0, ccfg in _NA_COL_BLOCKS:
            cfg = rcfg * 3 + ccfg
            q = q_ref[r0:r0 + NA_Q_ROWS, c0:c0 + NA_Q_COLS, :].reshape(nq, NA_LANES)
            k = k_ref[ks:ks + NA_K_ROWS, kc0:kc0 + NA_K_COLS, :].reshape(nk, NA_LANES).astype(BF16)
            v = v_ref[ks:ks + NA_K_ROWS, kc0:kc0 + NA_K_COLS, :].reshape(nk, NA_LANES).astype(BF16)
            outs = []
            for hh in range(2):
                qm = jnp.where(first if hh == 0 else jnp.logical_not(first), q, jnp.zeros_like(q))
                s = lax.dot_general(qm, k, nt, preferred_element_type=F32) + bias_ref[hh, cfg]
                m = jnp.max(s, axis=-1, keepdims=True)
                p = jnp.exp(s - m)
                l = jnp.sum(p, axis=-1, keepdims=True)
                outs.append(jnp.dot(p.astype(BF16), v, preferred_element_type=F32) / l)
            o = jnp.where(first, outs[0], outs[1])
            o_ref[r0:r0 + NA_Q_ROWS, c0:c0 + NA_Q_COLS, :] = (
                o.reshape(NA_Q_ROWS, NA_Q_COLS, NA_LANES).astype(o_ref.dtype))


def _natten(nq, nk, nv, bias_tabs):
    b, rows = nq.shape[0], nq.shape[1]
    slab = pl.BlockSpec((None, rows, GRID_W, NA_LANES), lambda j, i: (i, 0, 0, j))
    nqk = NA_Q_ROWS * NA_Q_COLS
    nkk = NA_K_ROWS * NA_K_COLS
    return pl.pallas_call(
        _na_kernel,
        grid=(NA_SLABS, b),
        in_specs=[slab, slab, slab,
                  pl.BlockSpec((2, _NA_CONFIGS, nqk, nkk), lambda j, i: (j, 0, 0, 0))],
        out_specs=slab,
        out_shape=jax.ShapeDtypeStruct((b, rows, GRID_W, NA_W), BF16),
        compiler_params=pltpu.CompilerParams(
            dimension_semantics=("parallel", "parallel"), vmem_limit_bytes=VMEM_LIMIT),
        name="natten",
    )(nq, nk, nv, bias_tabs)


def _mixout_kernel(x_ref, yr_ref, na_ref, gr_ref, gn_ref, wr_ref, wn_ref, wo_ref, gain_ref, o_ref):
    y_ret = jnp.dot(yr_ref[...], wr_ref[...], preferred_element_type=F32)
    y_na = jnp.dot(na_ref[...], wn_ref[...], preferred_element_type=F32)
    merged = (jax.nn.sigmoid(gr_ref[...].astype(F32)) * y_ret
              + jax.nn.sigmoid(gn_ref[...].astype(F32)) * y_na)
    m = jnp.dot(merged.astype(BF16), wo_ref[...], preferred_element_type=F32)
    o_ref[...] = x_ref[...] + _rms(m, gain_ref[...])


def _mix_out(x, y_ret, y_na, g_ret, g_na, w_ret, w_na, w_out, gain):
    t = x.shape[0]
    tm = TOKEN_TILE

    def row(width):
        return pl.BlockSpec((tm, width), lambda i: (i, 0))

    return pl.pallas_call(
        _mixout_kernel,
        grid=(t // tm,),
        in_specs=[row(D_MODEL), row(RET_V_W), row(NA_W), row(D_MODEL), row(D_MODEL),
                  _const_spec((RET_V_W, D_MODEL)), _const_spec((NA_W, D_MODEL)),
                  _const_spec((D_MODEL, D_MODEL)), _const_spec((1, D_MODEL))],
        out_specs=row(D_MODEL),
        out_shape=jax.ShapeDtypeStruct((t, D_MODEL), F32),
        compiler_params=pltpu.CompilerParams(
            dimension_semantics=("parallel",), vmem_limit_bytes=VMEM_LIMIT),
        name="mix_out",
    )(x, y_ret, y_na, g_ret, g_na, w_ret, w_na, w_out, gain)


def kernel(x, ffn1_pre_norm, ffn1_w_in, ffn1_w_out, ffn1_post_norm, mix_pre_norm, w_mix_in,
           ret_decay_fwd, ret_decay_bwd, na_rel_bias, w_ret_out, w_na_out, w_mix_out, mix_post_norm,
           ffn2_pre_norm, ffn2_w_in, ffn2_w_out, ffn2_post_norm):
    b, s, d = x.shape
    assert d == D_MODEL and s % GRID_W == 0 and s % RET_BLOCK == 0 and (b * s) % TOKEN_TILE == 0
    assert s % TOKEN_TILE == 0
    rows = s // GRID_W
    assert rows == NA_ROWS, "the neighbourhood-attention tiling is laid out for a 32 x 64 token grid"
    t = b * s
    xt = x.reshape(t, d)
    for l in range(ffn1_w_in.shape[0]):
        gain = lambda g: g[l].reshape(1, D_MODEL).astype(F32)
        wt = lambda w: w[l].astype(BF16)
        xt = _ffn(xt, gain(ffn1_pre_norm), wt(ffn1_w_in), wt(ffn1_w_out), gain(ffn1_post_norm))
        rq, rk, rv, rg, nq, nk, nv, g_ret, g_na = _mix_in(xt, gain(mix_pre_norm), wt(w_mix_in), s)
        decay = jnp.stack([ret_decay_fwd[l], ret_decay_bwd[l]]).astype(F32)
        seq = lambda a: a.reshape(b, s, a.shape[-1])
        y_ret = _retention(decay, seq(rq), seq(rk), seq(rv), seq(rg)).reshape(t, RET_V_W)
        grid4 = lambda a: a.reshape(b, rows, GRID_W, NA_W)
        y_na = _natten(grid4(nq), grid4(nk), grid4(nv), _na_bias_tables(na_rel_bias[l]))
        xt = _mix_out(xt, y_ret, y_na.reshape(t, NA_W), g_ret, g_na,
                      wt(w_ret_out), wt(w_na_out), wt(w_mix_out), gain(mix_post_norm))
        xt = _ffn(xt, gain(ffn2_pre_norm), wt(ffn2_w_in), wt(ffn2_w_out), gain(ffn2_post_norm))
    return xt.reshape(b, s, d)
```

```python
import functools

import numpy as np
import jax
import jax.numpy as jnp
from jax import lax
from jax.experimental import pallas as pl
from jax.experimental.pallas import tpu as pltpu

F32 = jnp.float32
BF16 = jnp.bfloat16

D_MODEL = 1024
D_FF = 2816
RMS_EPS = 1e-6
ROPE_BASE = 10000.0
NEG_INF = -1e30

RET_HEADS = 4
RET_QK_DIM = 128
RET_V_DIM = 256
RET_QK_W = RET_HEADS * RET_QK_DIM
RET_V_W = RET_HEADS * RET_V_DIM
RET_BLOCK = 256

NA_HEADS = 8
NA_HEAD_DIM = 64
NA_W = NA_HEADS * NA_HEAD_DIM
GRID_W = 64
NA_WIN_ROWS = 8
NA_WIN_COLS = 16
NA_REL_ROWS = 2 * NA_WIN_ROWS - 1
NA_REL_COLS = 2 * NA_WIN_COLS - 1
NA_Q_ROWS = 8
NA_Q_COLS = 16
NA_K_ROWS = 16
NA_K_COLS = 32
NA_LANES = 128
NA_SLABS = NA_W // NA_LANES

MIX_SPLITS = (RET_QK_W, RET_QK_W, RET_V_W, RET_V_W, NA_W, NA_W, NA_W, D_MODEL, D_MODEL)
MIX_OFFS = tuple(int(v) for v in np.cumsum((0,) + MIX_SPLITS[:-1]))
MIX_IN_W = sum(MIX_SPLITS)

TOKEN_TILE = 512
FF_CHUNK = 256
VMEM_LIMIT = 56 * 1024 * 1024


def _rms(x, gain):
    ms = jnp.mean(x * x, axis=-1, keepdims=True)
    return x * lax.rsqrt(ms + RMS_EPS) * gain


def _silu(x):
    return x * jax.nn.sigmoid(x)


def _const_spec(shape):
    nd = len(shape)
    return pl.BlockSpec(shape, lambda *_: (0,) * nd, pipeline_mode=pl.Buffered(1))


def _ffn_kernel(x_ref, gpre_ref, win_ref, wout_ref, gpost_ref, o_ref, act_ref):
    x = x_ref[...]
    xn = _rms(x, gpre_ref[...]).astype(BF16)
    for c in range(D_FF // FF_CHUNK):
        lo = c * FF_CHUNK
        g = jnp.dot(xn, win_ref[:, lo:lo + FF_CHUNK], preferred_element_type=F32)
        u = jnp.dot(xn, win_ref[:, D_FF + lo:D_FF + lo + FF_CHUNK], preferred_element_type=F32)
        act_ref[:, lo:lo + FF_CHUNK] = (_silu(g) * u).astype(BF16)
    h = jnp.dot(act_ref[...], wout_ref[...], preferred_element_type=F32)
    o_ref[...] = x + 0.5 * _rms(h, gpost_ref[...])


def _ffn(x, gpre, w_in, w_out, gpost):
    t = x.shape[0]
    tm = TOKEN_TILE
    row = pl.BlockSpec((tm, D_MODEL), lambda i: (i, 0))
    return pl.pallas_call(
        _ffn_kernel,
        grid=(t // tm,),
        in_specs=[row, _const_spec((1, D_MODEL)), _const_spec((D_MODEL, 2 * D_FF)),
                  _const_spec((D_FF, D_MODEL)), _const_spec((1, D_MODEL))],
        out_specs=row,
        out_shape=jax.ShapeDtypeStruct((t, D_MODEL), F32),
        scratch_shapes=[pltpu.VMEM((tm, D_FF), BF16)],
        compiler_params=pltpu.CompilerParams(
            dimension_semantics=("parallel",), vmem_limit_bytes=VMEM_LIMIT),
        name="ffn",
    )(x, gpre, w_in, w_out, gpost)


def _mixin_kernel(x_ref, gain_ref, w_ref, rope_ref,
                  rq_ref, rk_ref, rv_ref, rg_ref, nq_ref, nk_ref, nv_ref, gr_ref, gn_ref):
    u = _rms(x_ref[...], gain_ref[...]).astype(BF16)

    def proj(idx):
        lo = MIX_OFFS[idx]
        return jnp.dot(u, w_ref[:, lo:lo + MIX_SPLITS[idx]], preferred_element_type=F32)

    def rotary(y, cos, sin, out_ref):
        for h in range(RET_HEADS):
            sl = slice(h * RET_QK_DIM, (h + 1) * RET_QK_DIM)
            yh = y[:, sl]
            out_ref[:, sl] = (yh * cos + pltpu.roll(yh, RET_QK_DIM // 2, 1) * sin).astype(out_ref.dtype)

    rotary(proj(0), rope_ref[0], rope_ref[1], rq_ref)
    rotary(proj(1), rope_ref[2], rope_ref[3], rk_ref)
    rv_ref[...] = proj(2).astype(rv_ref.dtype)
    rg_ref[...] = proj(3).astype(rg_ref.dtype)
    nq_ref[...] = (proj(4) * (NA_HEAD_DIM ** -0.5)).astype(nq_ref.dtype)
    nk_ref[...] = proj(5).astype(nk_ref.dtype)
    nv_ref[...] = proj(6).astype(nv_ref.dtype)
    gr_ref[...] = proj(7).astype(gr_ref.dtype)
    gn_ref[...] = proj(8).astype(gn_ref.dtype)


def _rope_tables(seq):
    half = RET_QK_DIM // 2
    pos = jnp.arange(seq, dtype=F32)
    inv = 1.0 / (ROPE_BASE ** jnp.linspace(0.0, 1.0, half, dtype=F32))
    ang = pos[:, None] * inv[None, :]
    cos, sin = jnp.cos(ang), jnp.sin(ang)
    cos2 = jnp.concatenate([cos, cos], axis=-1)
    sin2 = jnp.concatenate([-sin, sin], axis=-1)
    ks = RET_QK_DIM ** -0.5
    return jnp.stack([cos2, sin2, cos2 * ks, sin2 * ks])


def _mix_in(x, gain, w, seq):
    t = x.shape[0]
    tm = TOKEN_TILE
    tiles_per_seq = seq // tm
    out_dtypes = (BF16, BF16, BF16, BF16, BF16, F32, F32, BF16, BF16)
    return pl.pallas_call(
        _mixin_kernel,
        grid=(t // tm,),
        in_specs=[pl.BlockSpec((tm, D_MODEL), lambda i: (i, 0)),
                  _const_spec((1, D_MODEL)), _const_spec((D_MODEL, MIX_IN_W)),
                  pl.BlockSpec((4, tm, RET_QK_DIM), lambda i: (0, i % tiles_per_seq, 0))],
        out_specs=[pl.BlockSpec((tm, wd), lambda i: (i, 0)) for wd in MIX_SPLITS],
        out_shape=[jax.ShapeDtypeStruct((t, wd), dt) for wd, dt in zip(MIX_SPLITS, out_dtypes)],
        compiler_params=pltpu.CompilerParams(
            dimension_semantics=("parallel",), vmem_limit_bytes=VMEM_LIMIT),
        name="mix_in",
    )(x, gain, w, _rope_tables(seq))


def _log_sigmoid(z):
    return jnp.minimum(z, 0.0) - jnp.log1p(jnp.exp(-jnp.abs(z)))


def _ret_kernel(dec_ref, q_ref, k_ref, v_ref, g_ref, o_ref, sb_ref, dm_ref, vec_ref):
    c = RET_BLOCK
    dk = RET_QK_DIM
    dv = RET_V_DIM
    n_blocks = q_ref.shape[0] // c
    head = pl.program_id(1)
    zf = dec_ref[0, head]
    zb = dec_ref[1, head]

    def log_gamma(z, shape):
        return _log_sigmoid(jnp.full(shape, z, F32))

    ri = lax.broadcasted_iota(jnp.int32, (c, c), 0).astype(F32)
    ci = lax.broadcasted_iota(jnp.int32, (c, c), 1).astype(F32)
    diff = ri - ci
    dm_ref[...] = jnp.where(diff >= 0, jnp.exp(log_gamma(zf, (c, c)) * jnp.maximum(diff, 0.0)),
                            jnp.exp(log_gamma(zb, (c, c)) * jnp.maximum(-diff, 0.0)))
    r = lax.broadcasted_iota(jnp.int32, (c, dk), 0).astype(F32)
    lgf = log_gamma(zf, (c, dk))
    lgb = log_gamma(zb, (c, dk))
    vec_ref[0] = jnp.exp(lgf * (r + 1.0))
    vec_ref[1] = jnp.exp(lgb * (c - r))
    vec_ref[2] = jnp.exp(lgf * (c - 1.0 - r))
    vec_ref[3] = jnp.exp(lgb * r)
    blk_f = jnp.exp(log_gamma(zf, (dk, dv)) * c)
    blk_b = jnp.exp(log_gamma(zb, (dk, dv)) * c)
    tn = (((0,), (0,)), ((), ()))
    nt = (((1,), (1,)), ((), ()))

    def rows_of(n):
        return pl.ds(pl.multiple_of(n * c, c), c)

    def state_rows(n):
        return pl.ds(pl.multiple_of(n * dk, dk), dk)

    def reverse_step(i, sb):
        n = n_blocks - 1 - i
        sb_ref[state_rows(n), :] = sb.astype(BF16)
        kb = (k_ref[rows_of(n), :].astype(F32) * vec_ref[3]).astype(BF16)
        upd = lax.dot_general(kb, v_ref[rows_of(n), :], tn, preferred_element_type=F32)
        return blk_b * sb + upd

    lax.fori_loop(0, n_blocks, reverse_step, jnp.zeros((dk, dv), F32))

    def forward_step(n, sf):
        rows = rows_of(n)
        q = q_ref[rows, :]
        k = k_ref[rows, :]
        v = v_ref[rows, :]
        s = lax.dot_general(q, k, nt, preferred_element_type=F32)
        p = (s * dm_ref[...]).astype(BF16)
        qf = q.astype(F32)
        q_cat = jnp.concatenate([(qf * vec_ref[0]).astype(BF16), (qf * vec_ref[1]).astype(BF16)], axis=1)
        s_cat = jnp.concatenate([sf.astype(BF16), sb_ref[state_rows(n), :]], axis=0)
        y = (jnp.dot(p, v, preferred_element_type=F32)
             + jnp.dot(q_cat, s_cat, preferred_element_type=F32))
        yn = y * lax.rsqrt(jnp.mean(y * y, axis=-1, keepdims=True) + RMS_EPS)
        o_ref[rows, :] = (_silu(g_ref[rows, :].astype(F32)) * yn).astype(o_ref.dtype)
        kf = (k.astype(F32) * vec_ref[2]).astype(BF16)
        upd = lax.dot_general(kf, v, tn, preferred_element_type=F32)
        return blk_f * sf + upd

    lax.fori_loop(0, n_blocks, forward_step, jnp.zeros((dk, dv), F32))


def _retention(decay_logits, rq, rk, rv, rg):
    b, s, _ = rq.shape
    c = RET_BLOCK
    qk_spec = pl.BlockSpec((None, s, RET_QK_DIM), lambda i, h, dec: (i, 0, h))
    v_spec = pl.BlockSpec((None, s, RET_V_DIM), lambda i, h, dec: (i, 0, h))
    return pl.pallas_call(
        _ret_kernel,
        grid_spec=pltpu.PrefetchScalarGridSpec(
            num_scalar_prefetch=1,
            grid=(b, RET_HEADS),
            in_specs=[qk_spec, qk_spec, v_spec, v_spec],
            out_specs=v_spec,
            scratch_shapes=[pltpu.VMEM((s // c * RET_QK_DIM, RET_V_DIM), BF16),
                            pltpu.VMEM((c, c), F32),
                            pltpu.VMEM((4, c, RET_QK_DIM), F32)],
        ),
        out_shape=jax.ShapeDtypeStruct((b, s, RET_V_W), BF16),
        compiler_params=pltpu.CompilerParams(
            dimension_semantics=("parallel", "parallel"), vmem_limit_bytes=VMEM_LIMIT),
        name="retention",
    )(decay_logits, rq, rk, rv, rg)


def _na_blocks(extent, q_size, k_size, window):
    n = extent // q_size
    out = []
    for i in range(n):
        q0 = i * q_size
        k0 = min(max(q0 - window // 2, 0), extent - k_size)
        out.append((q0, k0, 0 if i == 0 else (2 if i == n - 1 else 1)))
    return tuple(out)


NA_ROWS = 32
_NA_ROW_BLOCKS = _na_blocks(NA_ROWS, NA_Q_ROWS, NA_K_ROWS, NA_WIN_ROWS)
_NA_COL_BLOCKS = _na_blocks(GRID_W, NA_Q_COLS, NA_K_COLS, NA_WIN_COLS)
_NA_CONFIGS = 9


def _na_bias_kernel(rel_ref, out_ref, col_ref):
    nk = NA_K_ROWS * NA_K_COLS
    base = pl.program_id(0) * (NA_REL_ROWS * NA_REL_COLS)
    lane = lax.broadcasted_iota(jnp.int32, (NA_Q_COLS, nk), 1)
    qc = lax.broadcasted_iota(jnp.int32, (NA_Q_COLS, nk), 0)
    kc = lane % NA_K_COLS
    kr = lane // NA_K_COLS
    col_cfgs = (_NA_COL_BLOCKS[0], _NA_COL_BLOCKS[1], _NA_COL_BLOCKS[3])
    row_cfgs = (_NA_ROW_BLOCKS[0], _NA_ROW_BLOCKS[1], _NA_ROW_BLOCKS[3])
    for b, (c0, kc0, _) in enumerate(col_cfgs):
        kc_abs = kc + kc0
        qc_abs = qc + c0
        rel_col = jnp.clip(kc_abs - qc_abs + NA_WIN_COLS - 1, 0, NA_REL_COLS - 1)
        win = jnp.clip(qc_abs - NA_WIN_COLS // 2, 0, GRID_W - NA_WIN_COLS)
        in_win = (kc_abs >= win) & (kc_abs < win + NA_WIN_COLS)

        def rel_row_body(dr, carry, rel_col=rel_col, in_win=in_win, b=b):
            def rel_col_body(dc, acc):
                return jnp.where(rel_col == dc, rel_ref[base + dr * NA_REL_COLS + dc], acc)
            vals = lax.fori_loop(0, NA_REL_COLS, rel_col_body, jnp.zeros((NA_Q_COLS, nk), F32))
            col_ref[b * NA_REL_ROWS + dr] = jnp.where(in_win, vals, NEG_INF)
            return carry

        lax.fori_loop(0, NA_REL_ROWS, rel_row_body, 0)

    for a, (r0, ks, _) in enumerate(row_cfgs):
        kr_abs = kr + ks
        for qr in range(NA_Q_ROWS):
            r_abs = r0 + qr
            rs = min(max(r_abs - NA_WIN_ROWS // 2, 0), NA_ROWS - NA_WIN_ROWS)
            hits = [kr_abs == rs + j for j in range(NA_WIN_ROWS)]
            for b in range(3):
                tile = jnp.full((NA_Q_COLS, nk), NEG_INF, F32)
                for j in range(NA_WIN_ROWS):
                    dr = rs + j - r_abs + NA_WIN_ROWS - 1
                    tile = jnp.where(hits[j], col_ref[b * NA_REL_ROWS + dr], tile)
                out_ref[0, a * 3 + b, qr * NA_Q_COLS:(qr + 1) * NA_Q_COLS, :] = tile


def _na_bias_tables(rel_bias):
    nq = NA_Q_ROWS * NA_Q_COLS
    nk = NA_K_ROWS * NA_K_COLS
    return pl.pallas_call(
        _na_bias_kernel,
        grid_spec=pltpu.PrefetchScalarGridSpec(
            num_scalar_prefetch=1,
            grid=(NA_HEADS,),
            in_specs=[],
            out_specs=pl.BlockSpec((1, _NA_CONFIGS, nq, nk), lambda h, rel: (h, 0, 0, 0)),
            scratch_shapes=[pltpu.VMEM((3 * NA_REL_ROWS, NA_Q_COLS, nk), F32)],
        ),
        out_shape=jax.ShapeDtypeStruct((NA_HEADS, _NA_CONFIGS, nq, nk), F32),
        compiler_params=pltpu.CompilerParams(dimension_semantics=("parallel",)),
        name="na_bias",
    )(rel_bias.astype(F32).reshape(-1))


def _na_kernel(q_ref, k_ref, v_ref, bias_ref, o_ref):
    nq = NA_Q_ROWS * NA_Q_COLS
    nk = NA_K_ROWS * NA_K_COLS
    first = lax.broadcasted_iota(jnp.int32, (nq, NA_LANES), 1) < NA_HEAD_DIM
    nt = (((1,), (1,)), ((), ()))
    blocks = [(rb, cb) for rb in _NA_ROW_BLOCKS for cb in _NA_COL_BLOCKS]

    def scores(block, hh):
        (r0, ks, rcfg), (c0, kc0, ccfg) = block
        q = q_ref[r0:r0 + NA_Q_ROWS, c0:c0 + NA_Q_COLS, :].reshape(nq, NA_LANES)
        k = k_ref[ks:ks + NA_K_ROWS, kc0:kc0 + NA_K_COLS, :].reshape(nk, NA_LANES).astype(BF16)
        qm = jnp.where(first if hh == 0 else jnp.logical_not(first), q, jnp.zeros_like(q))
        return lax.dot_general(qm, k, nt, preferred_element_type=F32) + bias_ref[hh, rcfg * 3 + ccfg]

    def attend(block, s):
        (_, ks, _), (_, kc0, _) = block
        v = v_ref[ks:ks + NA_K_ROWS, kc0:kc0 + NA_K_COLS, :].reshape(nk, NA_LANES).astype(BF16)
        m = jnp.max(s, axis=-1, keepdims=True)
        p = jnp.exp(s - m)
        l = jnp.sum(p, axis=-1, keepdims=True)
        return jnp.dot(p.astype(BF16), v, preferred_element_type=F32) / l

    work = [(blk, hh) for blk in blocks for hh in range(2)]
    s_next = scores(*work[0])
    outs = []
    for i, (blk, hh) in enumerate(work):
        s_cur = s_next
        if i + 1 < len(work):
            s_next = scores(*work[i + 1])
        outs.append(attend(blk, s_cur))
        if hh == 1:
            (r0, _, _), (c0, _, _) = blk
            o = jnp.where(first, outs[0], outs[1])
            o_ref[r0:r0 + NA_Q_ROWS, c0:c0 + NA_Q_COLS, :] = (
                o.reshape(NA_Q_ROWS, NA_Q_COLS, NA_LANES).astype(o_ref.dtype))
            outs = []


def _natten(nq, nk, nv, bias_tabs):
    b, rows = nq.shape[0], nq.shape[1]
    slab = pl.BlockSpec((None, rows, GRID_W, NA_LANES), lambda j, i: (i, 0, 0, j))
    nqk = NA_Q_ROWS * NA_Q_COLS
    nkk = NA_K_ROWS * NA_K_COLS
    return pl.pallas_call(
        _na_kernel,
        grid=(NA_SLABS, b),
        in_specs=[slab, slab, slab,
                  pl.BlockSpec((2, _NA_CONFIGS, nqk, nkk), lambda j, i: (j, 0, 0, 0))],
        out_specs=slab,
        out_shape=jax.ShapeDtypeStruct((b, rows, GRID_W, NA_W), BF16),
        compiler_params=pltpu.CompilerParams(
            dimension_semantics=("parallel", "parallel"), vmem_limit_bytes=VMEM_LIMIT),
        name="natten",
    )(nq, nk, nv, bias_tabs)


def _mixout_kernel(x_ref, yr_ref, na_ref, gr_ref, gn_ref, wr_ref, wn_ref, wo_ref, gain_ref, o_ref):
    y_ret = jnp.dot(yr_ref[...], wr_ref[...], preferred_element_type=F32)
    y_na = jnp.dot(na_ref[...], wn_ref[...], preferred_element_type=F32)
    merged = (jax.nn.sigmoid(gr_ref[...].astype(F32)) * y_ret
              + jax.nn.sigmoid(gn_ref[...].astype(F32)) * y_na)
    m = jnp.dot(merged.astype(BF16), wo_ref[...], preferred_element_type=F32)
    o_ref[...] = x_ref[...] + _rms(m, gain_ref[...])


def _mix_out(x, y_ret, y_na, g_ret, g_na, w_ret, w_na, w_out, gain):
    t = x.shape[0]
    tm = TOKEN_TILE

    def row(width):
        return pl.BlockSpec((tm, width), lambda i: (i, 0))

    return pl.pallas_call(
        _mixout_kernel,
        grid=(t // tm,),
        in_specs=[row(D_MODEL), row(RET_V_W), row(NA_W), row(D_MODEL), row(D_MODEL),
                  _const_spec((RET_V_W, D_MODEL)), _const_spec((NA_W, D_MODEL)),
                  _const_spec((D_MODEL, D_MODEL)), _const_spec((1, D_MODEL))],
        out_specs=row(D_MODEL),
        out_shape=jax.ShapeDtypeStruct((t, D_MODEL), F32),
        compiler_params=pltpu.CompilerParams(
            dimension_semantics=("parallel",), vmem_limit_bytes=VMEM_LIMIT),
        name="mix_out",
    )(x, y_ret, y_na, g_ret, g_na, w_ret, w_na, w_out, gain)


def kernel(x, ffn1_pre_norm, ffn1_w_in, ffn1_w_out, ffn1_post_norm, mix_pre_norm, w_mix_in,
           ret_decay_fwd, ret_decay_bwd, na_rel_bias, w_ret_out, w_na_out, w_mix_out, mix_post_norm,
           ffn2_pre_norm, ffn2_w_in, ffn2_w_out, ffn2_post_norm):
    b, s, d = x.shape
    assert d == D_MODEL and s % GRID_W == 0 and s % RET_BLOCK == 0 and (b * s) % TOKEN_TILE == 0
    assert s % TOKEN_TILE == 0
    rows = s // GRID_W
    assert rows == NA_ROWS, "the neighbourhood-attention tiling is laid out for a 32 x 64 token grid"
    t = b * s
    xt = x.reshape(t, d)
    for l in range(ffn1_w_in.shape[0]):
        gain = lambda g: g[l].reshape(1, D_MODEL).astype(F32)
        wt = lambda w: w[l].astype(BF16)
        xt = _ffn(xt, gain(ffn1_pre_norm), wt(ffn1_w_in), wt(ffn1_w_out), gain(ffn1_post_norm))
        rq, rk, rv, rg, nq, nk, nv, g_ret, g_na = _mix_in(xt, gain(mix_pre_norm), wt(w_mix_in), s)
        decay = jnp.stack([ret_decay_fwd[l], ret_decay_bwd[l]]).astype(F32)
        seq = lambda a: a.reshape(b, s, a.shape[-1])
        y_ret = _retention(decay, seq(rq), seq(rk), seq(rv), seq(rg)).reshape(t, RET_V_W)
        grid4 = lambda a: a.reshape(b, rows, GRID_W, NA_W)
        y_na = _natten(grid4(nq), grid4(nk), grid4(nv), _na_bias_tables(na_rel_bias[l]))
        xt = _mix_out(xt, y_ret, y_na.reshape(t, NA_W), g_ret, g_na,
                      wt(w_ret_out), wt(w_na_out), wt(w_mix_out), gain(mix_post_norm))
        xt = _ffn(xt, gain(ffn2_pre_norm), wt(ffn2_w_in), wt(ffn2_w_out), gain(ffn2_post_norm))
    return xt.reshape(b, s, d)
```

```python
import functools

import numpy as np
import jax
import jax.numpy as jnp
from jax import lax
from jax.experimental import pallas as pl
from jax.experimental.pallas import tpu as pltpu

F32 = jnp.float32
BF16 = jnp.bfloat16

D_MODEL = 1024
D_FF = 2816
RMS_EPS = 1e-6
ROPE_BASE = 10000.0
NEG_INF = -1e30

RET_HEADS = 4
RET_QK_DIM = 128
RET_V_DIM = 256
RET_QK_W = RET_HEADS * RET_QK_DIM
RET_V_W = RET_HEADS * RET_V_DIM
RET_BLOCK = 256

NA_HEADS = 8
NA_HEAD_DIM = 64
NA_W = NA_HEADS * NA_HEAD_DIM
GRID_W = 64
NA_WIN_ROWS = 8
NA_WIN_COLS = 16
NA_REL_ROWS = 2 * NA_WIN_ROWS - 1
NA_REL_COLS = 2 * NA_WIN_COLS - 1
NA_Q_ROWS = 8
NA_Q_COLS = 16
NA_K_ROWS = 16
NA_K_COLS = 32
NA_LANES = 128
NA_SLABS = NA_W // NA_LANES

MIX_SPLITS = (RET_QK_W, RET_QK_W, RET_V_W, RET_V_W, NA_W, NA_W, NA_W, D_MODEL, D_MODEL)
MIX_OFFS = tuple(int(v) for v in np.cumsum((0,) + MIX_SPLITS[:-1]))
MIX_IN_W = sum(MIX_SPLITS)

TOKEN_TILE = 512
FF_CHUNK = 256
VMEM_LIMIT = 56 * 1024 * 1024


def _rms(x, gain):
    ms = jnp.mean(x * x, axis=-1, keepdims=True)
    return x * lax.rsqrt(ms + RMS_EPS) * gain


def _silu(x):
    return x * jax.nn.sigmoid(x)


def _const_spec(shape):
    nd = len(shape)
    return pl.BlockSpec(shape, lambda *_: (0,) * nd, pipeline_mode=pl.Buffered(1))


def _ffn_kernel(x_ref, gpre_ref, win_ref, wout_ref, gpost_ref, o_ref, act_ref):
    x = x_ref[...]
    xn = _rms(x, gpre_ref[...]).astype(BF16)
    for c in range(D_FF // FF_CHUNK):
        lo = c * FF_CHUNK
        g = jnp.dot(xn, win_ref[:, lo:lo + FF_CHUNK], preferred_element_type=F32)
        u = jnp.dot(xn, win_ref[:, D_FF + lo:D_FF + lo + FF_CHUNK], preferred_element_type=F32)
        act_ref[:, lo:lo + FF_CHUNK] = (_silu(g) * u).astype(BF16)
    h = jnp.dot(act_ref[...], wout_ref[...], preferred_element_type=F32)
    o_ref[...] = x + 0.5 * _rms(h, gpost_ref[...])


def _ffn(x, gpre, w_in, w_out, gpost):
    t = x.shape[0]
    tm = TOKEN_TILE
    row = pl.BlockSpec((tm, D_MODEL), lambda i: (i, 0))
    return pl.pallas_call(
        _ffn_kernel,
        grid=(t // tm,),
        in_specs=[row, _const_spec((1, D_MODEL)), _const_spec((D_MODEL, 2 * D_FF)),
                  _const_spec((D_FF, D_MODEL)), _const_spec((1, D_MODEL))],
        out_specs=row,
        out_shape=jax.ShapeDtypeStruct((t, D_MODEL), F32),
        scratch_shapes=[pltpu.VMEM((tm, D_FF), BF16)],
        compiler_params=pltpu.CompilerParams(
            dimension_semantics=("parallel",), vmem_limit_bytes=VMEM_LIMIT),
        name="ffn",
    )(x, gpre, w_in, w_out, gpost)


def _mixin_kernel(x_ref, gain_ref, w_ref, rope_ref,
                  rq_ref, rk_ref, rv_ref, rg_ref, nq_ref, nk_ref, nv_ref, gr_ref, gn_ref):
    u = _rms(x_ref[...], gain_ref[...]).astype(BF16)

    def proj(idx):
        lo = MIX_OFFS[idx]
        return jnp.dot(u, w_ref[:, lo:lo + MIX_SPLITS[idx]], preferred_element_type=F32)

    def rotary(y, cos, sin, out_ref):
        for h in range(RET_HEADS):
            sl = slice(h * RET_QK_DIM, (h + 1) * RET_QK_DIM)
            yh = y[:, sl]
            out_ref[:, sl] = (yh * cos + pltpu.roll(yh, RET_QK_DIM // 2, 1) * sin).astype(out_ref.dtype)

    rotary(proj(0), rope_ref[0], rope_ref[1], rq_ref)
    rotary(proj(1), rope_ref[2], rope_ref[3], rk_ref)
    rv_ref[...] = proj(2).astype(rv_ref.dtype)
    rg_ref[...] = proj(3).astype(rg_ref.dtype)
    nq_ref[...] = (proj(4) * (NA_HEAD_DIM ** -0.5)).astype(nq_ref.dtype)
    nk_ref[...] = proj(5).astype(nk_ref.dtype)
    nv_ref[...] = proj(6).astype(nv_ref.dtype)
    gr_ref[...] = proj(7).astype(gr_ref.dtype)
    gn_ref[...] = proj(8).astype(gn_ref.dtype)


def _rope_tables(seq):
    half = RET_QK_DIM // 2
    pos = jnp.arange(seq, dtype=F32)
    inv = 1.0 / (ROPE_BASE ** jnp.linspace(0.0, 1.0, half, dtype=F32))
    ang = pos[:, None] * inv[None, :]
    cos, sin = jnp.cos(ang), jnp.sin(ang)
    cos2 = jnp.concatenate([cos, cos], axis=-1)
    sin2 = jnp.concatenate([-sin, sin], axis=-1)
    ks = RET_QK_DIM ** -0.5
    return jnp.stack([cos2, sin2, cos2 * ks, sin2 * ks])


def _mix_in(x, gain, w, seq):
    t = x.shape[0]
    tm = TOKEN_TILE
    tiles_per_seq = seq // tm
    out_dtypes = (BF16, BF16, BF16, BF16, BF16, F32, F32, BF16, BF16)
    return pl.pallas_call(
        _mixin_kernel,
        grid=(t // tm,),
        in_specs=[pl.BlockSpec((tm, D_MODEL), lambda i: (i, 0)),
                  _const_spec((1, D_MODEL)), _const_spec((D_MODEL, MIX_IN_W)),
                  pl.BlockSpec((4, tm, RET_QK_DIM), lambda i: (0, i % tiles_per_seq, 0))],
        out_specs=[pl.BlockSpec((tm, wd), lambda i: (i, 0)) for wd in MIX_SPLITS],
        out_shape=[jax.ShapeDtypeStruct((t, wd), dt) for wd, dt in zip(MIX_SPLITS, out_dtypes)],
        compiler_params=pltpu.CompilerParams(
            dimension_semantics=("parallel",), vmem_limit_bytes=VMEM_LIMIT),
        name="mix_in",
    )(x, gain, w, _rope_tables(seq))


def _log_sigmoid(z):
    return jnp.minimum(z, 0.0) - jnp.log1p(jnp.exp(-jnp.abs(z)))


def _ret_kernel(dec_ref, q_ref, k_ref, v_ref, g_ref, o_ref, u_ref, st_ref, dm_ref, vec_ref, blk_ref):
    c = RET_BLOCK
    dk = RET_QK_DIM
    dv = RET_V_DIM
    n_blocks = q_ref.shape[0] // c
    tn = (((0,), (0,)), ((), ()))
    nt = (((1,), (1,)), ((), ()))

    @pl.when(pl.program_id(1) == 0)
    def _decay_tables():
        head = pl.program_id(0)
        zf = dec_ref[0, head]
        zb = dec_ref[1, head]

        def log_gamma(z, shape):
            return _log_sigmoid(jnp.full(shape, z, F32))

        ri = lax.broadcasted_iota(jnp.int32, (c, c), 0).astype(F32)
        ci = lax.broadcasted_iota(jnp.int32, (c, c), 1).astype(F32)
        diff = ri - ci
        dm_ref[...] = jnp.where(diff >= 0, jnp.exp(log_gamma(zf, (c, c)) * jnp.maximum(diff, 0.0)),
                                jnp.exp(log_gamma(zb, (c, c)) * jnp.maximum(-diff, 0.0)))
        r = lax.broadcasted_iota(jnp.int32, (c, dk), 0).astype(F32)
        lgf = log_gamma(zf, (c, dk))
        lgb = log_gamma(zb, (c, dk))
        vec_ref[0] = jnp.exp(lgf * (r + 1.0))
        vec_ref[1] = jnp.exp(lgb * (c - r))
        vec_ref[2] = jnp.exp(lgf * (c - 1.0 - r))
        vec_ref[3] = jnp.exp(lgb * r)
        blk_ref[0] = jnp.exp(log_gamma(zf, (dk, dv)) * c)
        blk_ref[1] = jnp.exp(log_gamma(zb, (dk, dv)) * c)

    def rows(n):
        return slice(n * c, (n + 1) * c)

    for n in range(n_blocks):
        k = k_ref[rows(n), :].astype(F32)
        k_cat = jnp.concatenate([(k * vec_ref[2]).astype(BF16), (k * vec_ref[3]).astype(BF16)], axis=1)
        u_ref[n] = lax.dot_general(k_cat, v_ref[rows(n), :], tn, preferred_element_type=F32)

    sf = jnp.zeros((dk, dv), F32)
    for n in range(n_blocks):
        st_ref[n, 0:dk, :] = sf.astype(BF16)
        if n + 1 < n_blocks:
            sf = blk_ref[0] * sf + u_ref[n, 0:dk, :]
    sb = jnp.zeros((dk, dv), F32)
    for n in reversed(range(n_blocks)):
        st_ref[n, dk:2 * dk, :] = sb.astype(BF16)
        if n > 0:
            sb = blk_ref[1] * sb + u_ref[n, dk:2 * dk, :]

    def scores(n):
        return lax.dot_general(q_ref[rows(n), :], k_ref[rows(n), :], nt, preferred_element_type=F32)

    s_next = scores(0)
    for n in range(n_blocks):
        s_cur = s_next
        if n + 1 < n_blocks:
            s_next = scores(n + 1)
        p = (s_cur * dm_ref[...]).astype(BF16)
        q = q_ref[rows(n), :].astype(F32)
        lhs = jnp.concatenate([p, (q * vec_ref[0]).astype(BF16), (q * vec_ref[1]).astype(BF16)], axis=1)
        rhs = jnp.concatenate([v_ref[rows(n), :], st_ref[n]], axis=0)
        y = jnp.dot(lhs, rhs, preferred_element_type=F32)
        yn = y * lax.rsqrt(jnp.mean(y * y, axis=-1, keepdims=True) + RMS_EPS)
        o_ref[rows(n), :] = (_silu(g_ref[rows(n), :].astype(F32)) * yn).astype(o_ref.dtype)


def _retention(decay_logits, rq, rk, rv, rg):
    b, s, _ = rq.shape
    c = RET_BLOCK
    n_blocks = s // c
    qk_spec = pl.BlockSpec((None, s, RET_QK_DIM), lambda h, i, dec: (i, 0, h))
    v_spec = pl.BlockSpec((None, s, RET_V_DIM), lambda h, i, dec: (i, 0, h))
    return pl.pallas_call(
        _ret_kernel,
        grid_spec=pltpu.PrefetchScalarGridSpec(
            num_scalar_prefetch=1,
            grid=(RET_HEADS, b),
            in_specs=[qk_spec, qk_spec, v_spec, v_spec],
            out_specs=v_spec,
            scratch_shapes=[pltpu.VMEM((n_blocks, 2 * RET_QK_DIM, RET_V_DIM), F32),
                            pltpu.VMEM((n_blocks, 2 * RET_QK_DIM, RET_V_DIM), BF16),
                            pltpu.VMEM((c, c), F32),
                            pltpu.VMEM((4, c, RET_QK_DIM), F32),
                            pltpu.VMEM((2, RET_QK_DIM, RET_V_DIM), F32)],
        ),
        out_shape=jax.ShapeDtypeStruct((b, s, RET_V_W), BF16),
        compiler_params=pltpu.CompilerParams(
            dimension_semantics=("arbitrary", "arbitrary"), vmem_limit_bytes=VMEM_LIMIT),
        name="retention",
    )(decay_logits, rq, rk, rv, rg)


def _na_blocks(extent, q_size, k_size, window):
    n = extent // q_size
    out = []
    for i in range(n):
        q0 = i * q_size
        k0 = min(max(q0 - window // 2, 0), extent - k_size)
        out.append((q0, k0, 0 if i == 0 else (2 if i == n - 1 else 1)))
    return tuple(out)


NA_ROWS = 32
_NA_ROW_BLOCKS = _na_blocks(NA_ROWS, NA_Q_ROWS, NA_K_ROWS, NA_WIN_ROWS)
_NA_COL_BLOCKS = _na_blocks(GRID_W, NA_Q_COLS, NA_K_COLS, NA_WIN_COLS)
_NA_CONFIGS = 9


def _na_bias_kernel(rel_ref, out_ref, col_ref):
    nk = NA_K_ROWS * NA_K_COLS
    base = pl.program_id(0) * (NA_REL_ROWS * NA_REL_COLS)
    lane = lax.broadcasted_iota(jnp.int32, (NA_Q_COLS, nk), 1)
    qc = lax.broadcasted_iota(jnp.int32, (NA_Q_COLS, nk), 0)
    kc = lane % NA_K_COLS
    kr = lane // NA_K_COLS
    col_cfgs = (_NA_COL_BLOCKS[0], _NA_COL_BLOCKS[1], _NA_COL_BLOCKS[3])
    row_cfgs = (_NA_ROW_BLOCKS[0], _NA_ROW_BLOCKS[1], _NA_ROW_BLOCKS[3])
    for b, (c0, kc0, _) in enumerate(col_cfgs):
        kc_abs = kc + kc0
        qc_abs = qc + c0
        rel_col = jnp.clip(kc_abs - qc_abs + NA_WIN_COLS - 1, 0, NA_REL_COLS - 1)
        win = jnp.clip(qc_abs - NA_WIN_COLS // 2, 0, GRID_W - NA_WIN_COLS)
        in_win = (kc_abs >= win) & (kc_abs < win + NA_WIN_COLS)

        def rel_row_body(dr, carry, rel_col=rel_col, in_win=in_win, b=b):
            def rel_col_body(dc, acc):
                return jnp.where(rel_col == dc, rel_ref[base + dr * NA_REL_COLS + dc], acc)
            vals = lax.fori_loop(0, NA_REL_COLS, rel_col_body, jnp.zeros((NA_Q_COLS, nk), F32))
            col_ref[b * NA_REL_ROWS + dr] = jnp.where(in_win, vals, NEG_INF)
            return carry

        lax.fori_loop(0, NA_REL_ROWS, rel_row_body, 0)

    for a, (r0, ks, _) in enumerate(row_cfgs):
        kr_abs = kr + ks
        for qr in range(NA_Q_ROWS):
            r_abs = r0 + qr
            rs = min(max(r_abs - NA_WIN_ROWS // 2, 0), NA_ROWS - NA_WIN_ROWS)
            hits = [kr_abs == rs + j for j in range(NA_WIN_ROWS)]
            for b in range(3):
                tile = jnp.full((NA_Q_COLS, nk), NEG_INF, F32)
                for j in range(NA_WIN_ROWS):
                    dr = rs + j - r_abs + NA_WIN_ROWS - 1
                    tile = jnp.where(hits[j], col_ref[b * NA_REL_ROWS + dr], tile)
                out_ref[0, a * 3 + b, qr * NA_Q_COLS:(qr + 1) * NA_Q_COLS, :] = tile


def _na_bias_tables(rel_bias):
    nq = NA_Q_ROWS * NA_Q_COLS
    nk = NA_K_ROWS * NA_K_COLS
    return pl.pallas_call(
        _na_bias_kernel,
        grid_spec=pltpu.PrefetchScalarGridSpec(
            num_scalar_prefetch=1,
            grid=(NA_HEADS,),
            in_specs=[],
            out_specs=pl.BlockSpec((1, _NA_CONFIGS, nq, nk), lambda h, rel: (h, 0, 0, 0)),
            scratch_shapes=[pltpu.VMEM((3 * NA_REL_ROWS, NA_Q_COLS, nk), F32)],
        ),
        out_shape=jax.ShapeDtypeStruct((NA_HEADS, _NA_CONFIGS, nq, nk), F32),
        compiler_params=pltpu.CompilerParams(dimension_semantics=("parallel",)),
        name="na_bias",
    )(rel_bias.astype(F32).reshape(-1))


def _na_kernel(q_ref, k_ref, v_ref, bias_ref, o_ref):
    nq = NA_Q_ROWS * NA_Q_COLS
    nk = NA_K_ROWS * NA_K_COLS
    first = lax.broadcasted_iota(jnp.int32, (nq, NA_LANES), 1) < NA_HEAD_DIM
    nt = (((1,), (1,)), ((), ()))
    blocks = [(rb, cb) for rb in _NA_ROW_BLOCKS for cb in _NA_COL_BLOCKS]

    def scores(block, hh):
        (r0, ks, rcfg), (c0, kc0, ccfg) = block
        q = q_ref[r0:r0 + NA_Q_ROWS, c0:c0 + NA_Q_COLS, :].reshape(nq, NA_LANES)
        k = k_ref[ks:ks + NA_K_ROWS, kc0:kc0 + NA_K_COLS, :].reshape(nk, NA_LANES).astype(BF16)
        qm = jnp.where(first if hh == 0 else jnp.logical_not(first), q, jnp.zeros_like(q))
        return lax.dot_general(qm, k, nt, preferred_element_type=F32) + bias_ref[hh, rcfg * 3 + ccfg]

    def attend(block, s):
        (_, ks, _), (_, kc0, _) = block
        v = v_ref[ks:ks + NA_K_ROWS, kc0:kc0 + NA_K_COLS, :].reshape(nk, NA_LANES).astype(BF16)
        m = jnp.max(s, axis=-1, keepdims=True)
        p = jnp.exp(s - m)
        l = jnp.sum(p, axis=-1, keepdims=True)
        return jnp.dot(p.astype(BF16), v, preferred_element_type=F32) / l

    work = [(blk, hh) for blk in blocks for hh in range(2)]
    s_next = scores(*work[0])
    outs = []
    for i, (blk, hh) in enumerate(work):
        s_cur = s_next
        if i + 1 < len(work):
            s_next = scores(*work[i + 1])
        outs.append(attend(blk, s_cur))
        if hh == 1:
            (r0, _, _), (c0, _, _) = blk
            o = jnp.where(first, outs[0], outs[1])
            o_ref[r0:r0 + NA_Q_ROWS, c0:c0 + NA_Q_COLS, :] = (
                o.reshape(NA_Q_ROWS, NA_Q_COLS, NA_LANES).astype(o_ref.dtype))
            outs = []


def _natten(nq, nk, nv, bias_tabs):
    b, rows = nq.shape[0], nq.shape[1]
    slab = pl.BlockSpec((None, rows, GRID_W, NA_LANES), lambda j, i: (i, 0, 0, j))
    nqk = NA_Q_ROWS * NA_Q_COLS
    nkk = NA_K_ROWS * NA_K_COLS
    return pl.pallas_call(
        _na_kernel,
        grid=(NA_SLABS, b),
        in_specs=[slab, slab, slab,
                  pl.BlockSpec((2, _NA_CONFIGS, nqk, nkk), lambda j, i: (j, 0, 0, 0))],
        out_specs=slab,
        out_shape=jax.ShapeDtypeStruct((b, rows, GRID_W, NA_W), BF16),
        compiler_params=pltpu.CompilerParams(
            dimension_semantics=("parallel", "parallel"), vmem_limit_bytes=VMEM_LIMIT),
        name="natten",
    )(nq, nk, nv, bias_tabs)


def _mixout_kernel(x_ref, yr_ref, na_ref, gr_ref, gn_ref, wr_ref, wn_ref, wo_ref, gain_ref, o_ref):
    y_ret = jnp.dot(yr_ref[...], wr_ref[...], preferred_element_type=F32)
    y_na = jnp.dot(na_ref[...], wn_ref[...], preferred_element_type=F32)
    merged = (jax.nn.sigmoid(gr_ref[...].astype(F32)) * y_ret
              + jax.nn.sigmoid(gn_ref[...].astype(F32)) * y_na)
    m = jnp.dot(merged.astype(BF16), wo_ref[...], preferred_element_type=F32)
    o_ref[...] = x_ref[...] + _rms(m, gain_ref[...])


def _mix_out(x, y_ret, y_na, g_ret, g_na, w_ret, w_na, w_out, gain):
    t = x.shape[0]
    tm = TOKEN_TILE

    def row(width):
        return pl.BlockSpec((tm, width), lambda i: (i, 0))

    return pl.pallas_call(
        _mixout_kernel,
        grid=(t // tm,),
        in_specs=[row(D_MODEL), row(RET_V_W), row(NA_W), row(D_MODEL), row(D_MODEL),
                  _const_spec((RET_V_W, D_MODEL)), _const_spec((NA_W, D_MODEL)),
                  _const_spec((D_MODEL, D_MODEL)), _const_spec((1, D_MODEL))],
        out_specs=row(D_MODEL),
        out_shape=jax.ShapeDtypeStruct((t, D_MODEL), F32),
        compiler_params=pltpu.CompilerParams(
            dimension_semantics=("parallel",), vmem_limit_bytes=VMEM_LIMIT),
        name="mix_out",
    )(x, y_ret, y_na, g_ret, g_na, w_ret, w_na, w_out, gain)


def kernel(x, ffn1_pre_norm, ffn1_w_in, ffn1_w_out, ffn1_post_norm, mix_pre_norm, w_mix_in,
           ret_decay_fwd, ret_decay_bwd, na_rel_bias, w_ret_out, w_na_out, w_mix_out, mix_post_norm,
           ffn2_pre_norm, ffn2_w_in, ffn2_w_out, ffn2_post_norm):
    b, s, d = x.shape
    assert d == D_MODEL and s % GRID_W == 0 and s % RET_BLOCK == 0 and (b * s) % TOKEN_TILE == 0
    assert s % TOKEN_TILE == 0
    rows = s // GRID_W
    assert rows == NA_ROWS, "the neighbourhood-attention tiling is laid out for a 32 x 64 token grid"
    t = b * s
    xt = x.reshape(t, d)
    for l in range(ffn1_w_in.shape[0]):
        gain = lambda g: g[l].reshape(1, D_MODEL).astype(F32)
        wt = lambda w: w[l].astype(BF16)
        xt = _ffn(xt, gain(ffn1_pre_norm), wt(ffn1_w_in), wt(ffn1_w_out), gain(ffn1_post_norm))
        rq, rk, rv, rg, nq, nk, nv, g_ret, g_na = _mix_in(xt, gain(mix_pre_norm), wt(w_mix_in), s)
        decay = jnp.stack([ret_decay_fwd[l], ret_decay_bwd[l]]).astype(F32)
        seq = lambda a: a.reshape(b, s, a.shape[-1])
        y_ret = _retention(decay, seq(rq), seq(rk), seq(rv), seq(rg)).reshape(t, RET_V_W)
        grid4 = lambda a: a.reshape(b, rows, GRID_W, NA_W)
        y_na = _natten(grid4(nq), grid4(nk), grid4(nv), _na_bias_tables(na_rel_bias[l]))
        xt = _mix_out(xt, y_ret, y_na.reshape(t, NA_W), g_ret, g_na,
                      wt(w_ret_out), wt(w_na_out), wt(w_mix_out), gain(mix_post_norm))
        xt = _ffn(xt, gain(ffn2_pre_norm), wt(ffn2_w_in), wt(ffn2_w_out), gain(ffn2_post_norm))
    return xt.reshape(b, s, d)
```

```python
import functools

import numpy as np
import jax
import jax.numpy as jnp
from jax import lax
from jax.experimental import pallas as pl
from jax.experimental.pallas import tpu as pltpu

F32 = jnp.float32
BF16 = jnp.bfloat16

D_MODEL = 1024
D_FF = 2816
RMS_EPS = 1e-6
ROPE_BASE = 10000.0
NEG_INF = -1e30

RET_HEADS = 4
RET_QK_DIM = 128
RET_V_DIM = 256
RET_QK_W = RET_HEADS * RET_QK_DIM
RET_V_W = RET_HEADS * RET_V_DIM
RET_BLOCK = 256

NA_HEADS = 8
NA_HEAD_DIM = 64
NA_W = NA_HEADS * NA_HEAD_DIM
GRID_W = 64
NA_WIN_ROWS = 8
NA_WIN_COLS = 16
NA_REL_ROWS = 2 * NA_WIN_ROWS - 1
NA_REL_COLS = 2 * NA_WIN_COLS - 1
NA_Q_ROWS = 8
NA_Q_COLS = 16
NA_K_ROWS = 16
NA_K_COLS = 32
NA_LANES = 128
NA_SLABS = NA_W // NA_LANES

MIX_SPLITS = (RET_QK_W, RET_QK_W, RET_V_W, RET_V_W, NA_W, NA_W, NA_W, D_MODEL, D_MODEL)
MIX_OFFS = tuple(int(v) for v in np.cumsum((0,) + MIX_SPLITS[:-1]))
MIX_IN_W = sum(MIX_SPLITS)

TOKEN_TILE = 512
FFN_TILE = 1024
SUB_TILE = 256
FF_CHUNK = 256
VMEM_LIMIT = 56 * 1024 * 1024


def _rms(x, gain):
    ms = jnp.mean(x * x, axis=-1, keepdims=True)
    return x * lax.rsqrt(ms + RMS_EPS) * gain


def _silu(x):
    return x * jax.nn.sigmoid(x)


def _const_spec(shape):
    nd = len(shape)
    return pl.BlockSpec(shape, lambda *_: (0,) * nd, pipeline_mode=pl.Buffered(1))


def _ffn_kernel(x_ref, gpre_ref, win_ref, wout_ref, gpost_ref, o_ref, act_ref):
    n_sub = x_ref.shape[0] // SUB_TILE

    def rows(j):
        return slice(j * SUB_TILE, (j + 1) * SUB_TILE)

    def pre(j):
        return _rms(x_ref[rows(j), :], gpre_ref[...]).astype(BF16)

    def up(j, xn):
        for c in range(D_FF // FF_CHUNK):
            lo = c * FF_CHUNK
            g = jnp.dot(xn, win_ref[:, lo:lo + FF_CHUNK], preferred_element_type=F32)
            u = jnp.dot(xn, win_ref[:, D_FF + lo:D_FF + lo + FF_CHUNK], preferred_element_type=F32)
            act_ref[rows(j), lo:lo + FF_CHUNK] = (_silu(g) * u).astype(BF16)

    def down(j):
        h = jnp.dot(act_ref[rows(j), :], wout_ref[...], preferred_element_type=F32)
        o_ref[rows(j), :] = x_ref[rows(j), :] + 0.5 * _rms(h, gpost_ref[...])

    xn = pre(0)
    for j in range(n_sub):
        xn_next = pre(j + 1) if j + 1 < n_sub else None
        up(j, xn)
        if j >= 1:
            down(j - 1)
        xn = xn_next
    down(n_sub - 1)


def _ffn(x, gpre, w_in, w_out, gpost):
    t = x.shape[0]
    tm = FFN_TILE
    row = pl.BlockSpec((tm, D_MODEL), lambda i: (i, 0))
    return pl.pallas_call(
        _ffn_kernel,
        grid=(t // tm,),
        in_specs=[row, _const_spec((1, D_MODEL)), _const_spec((D_MODEL, 2 * D_FF)),
                  _const_spec((D_FF, D_MODEL)), _const_spec((1, D_MODEL))],
        out_specs=row,
        out_shape=jax.ShapeDtypeStruct((t, D_MODEL), F32),
        scratch_shapes=[pltpu.VMEM((tm, D_FF), BF16)],
        compiler_params=pltpu.CompilerParams(
            dimension_semantics=("parallel",), vmem_limit_bytes=VMEM_LIMIT),
        name="ffn",
    )(x, gpre, w_in, w_out, gpost)


def _mixin_kernel(x_ref, gain_ref, w_ref, rope_ref,
                  rq_ref, rk_ref, rv_ref, rg_ref, nq_ref, nk_ref, nv_ref, gr_ref, gn_ref):
    n_sub = x_ref.shape[0] // SUB_TILE

    def pre(j):
        rows = slice(j * SUB_TILE, (j + 1) * SUB_TILE)
        return _rms(x_ref[rows, :], gain_ref[...]).astype(BF16)

    def project(j, u):
        rows = slice(j * SUB_TILE, (j + 1) * SUB_TILE)

        def proj(idx):
            lo = MIX_OFFS[idx]
            return jnp.dot(u, w_ref[:, lo:lo + MIX_SPLITS[idx]], preferred_element_type=F32)

        def rotary(y, cos, sin, out_ref):
            for h in range(RET_HEADS):
                sl = slice(h * RET_QK_DIM, (h + 1) * RET_QK_DIM)
                yh = y[:, sl]
                out_ref[rows, sl] = (yh * cos + pltpu.roll(yh, RET_QK_DIM // 2, 1) * sin).astype(out_ref.dtype)

        rotary(proj(0), rope_ref[0, rows, :], rope_ref[1, rows, :], rq_ref)
        rotary(proj(1), rope_ref[2, rows, :], rope_ref[3, rows, :], rk_ref)
        rv_ref[rows, :] = proj(2).astype(rv_ref.dtype)
        rg_ref[rows, :] = proj(3).astype(rg_ref.dtype)
        nq_ref[rows, :] = (proj(4) * (NA_HEAD_DIM ** -0.5)).astype(nq_ref.dtype)
        nk_ref[rows, :] = proj(5).astype(nk_ref.dtype)
        nv_ref[rows, :] = proj(6).astype(nv_ref.dtype)
        gr_ref[rows, :] = proj(7).astype(gr_ref.dtype)
        gn_ref[rows, :] = proj(8).astype(gn_ref.dtype)

    u = pre(0)
    for j in range(n_sub):
        u_next = pre(j + 1) if j + 1 < n_sub else None
        project(j, u)
        u = u_next


def _rope_tables(seq):
    half = RET_QK_DIM // 2
    pos = jnp.arange(seq, dtype=F32)
    inv = 1.0 / (ROPE_BASE ** jnp.linspace(0.0, 1.0, half, dtype=F32))
    ang = pos[:, None] * inv[None, :]
    cos, sin = jnp.cos(ang), jnp.sin(ang)
    cos2 = jnp.concatenate([cos, cos], axis=-1)
    sin2 = jnp.concatenate([-sin, sin], axis=-1)
    ks = RET_QK_DIM ** -0.5
    return jnp.stack([cos2, sin2, cos2 * ks, sin2 * ks])


def _mix_in(x, gain, w, seq):
    t = x.shape[0]
    tm = TOKEN_TILE
    tiles_per_seq = seq // tm
    out_dtypes = (BF16, BF16, BF16, BF16, BF16, F32, F32, BF16, BF16)
    return pl.pallas_call(
        _mixin_kernel,
        grid=(t // tm,),
        in_specs=[pl.BlockSpec((tm, D_MODEL), lambda i: (i, 0)),
                  _const_spec((1, D_MODEL)), _const_spec((D_MODEL, MIX_IN_W)),
                  pl.BlockSpec((4, tm, RET_QK_DIM), lambda i: (0, i % tiles_per_seq, 0))],
        out_specs=[pl.BlockSpec((tm, wd), lambda i: (i, 0)) for wd in MIX_SPLITS],
        out_shape=[jax.ShapeDtypeStruct((t, wd), dt) for wd, dt in zip(MIX_SPLITS, out_dtypes)],
        compiler_params=pltpu.CompilerParams(
            dimension_semantics=("parallel",), vmem_limit_bytes=VMEM_LIMIT),
        name="mix_in",
    )(x, gain, w, _rope_tables(seq))


def _log_sigmoid(z):
    return jnp.minimum(z, 0.0) - jnp.log1p(jnp.exp(-jnp.abs(z)))


def _ret_kernel(dec_ref, q_ref, k_ref, v_ref, g_ref, o_ref, u_ref, st_ref, dm_ref, vec_ref, blk_ref):
    c = RET_BLOCK
    dk = RET_QK_DIM
    dv = RET_V_DIM
    n_blocks = q_ref.shape[0] // c
    tn = (((0,), (0,)), ((), ()))
    nt = (((1,), (1,)), ((), ()))

    @pl.when(pl.program_id(1) == 0)
    def _decay_tables():
        head = pl.program_id(0)
        zf = dec_ref[0, head]
        zb = dec_ref[1, head]

        def log_gamma(z, shape):
            return _log_sigmoid(jnp.full(shape, z, F32))

        ri = lax.broadcasted_iota(jnp.int32, (c, c), 0).astype(F32)
        ci = lax.broadcasted_iota(jnp.int32, (c, c), 1).astype(F32)
        diff = ri - ci
        dm_ref[...] = jnp.where(diff >= 0, jnp.exp(log_gamma(zf, (c, c)) * jnp.maximum(diff, 0.0)),
                                jnp.exp(log_gamma(zb, (c, c)) * jnp.maximum(-diff, 0.0)))
        r = lax.broadcasted_iota(jnp.int32, (c, dk), 0).astype(F32)
        lgf = log_gamma(zf, (c, dk))
        lgb = log_gamma(zb, (c, dk))
        vec_ref[0] = jnp.exp(lgf * (r + 1.0))
        vec_ref[1] = jnp.exp(lgb * (c - r))
        vec_ref[2] = jnp.exp(lgf * (c - 1.0 - r))
        vec_ref[3] = jnp.exp(lgb * r)
        blk_ref[0] = jnp.exp(log_gamma(zf, (dk, dv)) * c)
        blk_ref[1] = jnp.exp(log_gamma(zb, (dk, dv)) * c)

    def rows(n):
        return slice(n * c, (n + 1) * c)

    for n in range(n_blocks):
        k = k_ref[rows(n), :].astype(F32)
        k_cat = jnp.concatenate([(k * vec_ref[2]).astype(BF16), (k * vec_ref[3]).astype(BF16)], axis=1)
        u_ref[n] = lax.dot_general(k_cat, v_ref[rows(n), :], tn, preferred_element_type=F32)

    sf = jnp.zeros((dk, dv), F32)
    for n in range(n_blocks):
        st_ref[n, 0:dk, :] = sf.astype(BF16)
        if n + 1 < n_blocks:
            sf = blk_ref[0] * sf + u_ref[n, 0:dk, :]
    sb = jnp.zeros((dk, dv), F32)
    for n in reversed(range(n_blocks)):
        st_ref[n, dk:2 * dk, :] = sb.astype(BF16)
        if n > 0:
            sb = blk_ref[1] * sb + u_ref[n, dk:2 * dk, :]

    def scores(n):
        return lax.dot_general(q_ref[rows(n), :], k_ref[rows(n), :], nt, preferred_element_type=F32)

    s_next = scores(0)
    for n in range(n_blocks):
        s_cur = s_next
        if n + 1 < n_blocks:
            s_next = scores(n + 1)
        p = (s_cur * dm_ref[...]).astype(BF16)
        q = q_ref[rows(n), :].astype(F32)
        lhs = jnp.concatenate([p, (q * vec_ref[0]).astype(BF16), (q * vec_ref[1]).astype(BF16)], axis=1)
        rhs = jnp.concatenate([v_ref[rows(n), :], st_ref[n]], axis=0)
        y = jnp.dot(lhs, rhs, preferred_element_type=F32)
        yn = y * lax.rsqrt(jnp.mean(y * y, axis=-1, keepdims=True) + RMS_EPS)
        o_ref[rows(n), :] = (_silu(g_ref[rows(n), :].astype(F32)) * yn).astype(o_ref.dtype)


def _retention(decay_logits, rq, rk, rv, rg):
    b, s, _ = rq.shape
    c = RET_BLOCK
    n_blocks = s // c
    qk_spec = pl.BlockSpec((None, s, RET_QK_DIM), lambda h, i, dec: (i, 0, h))
    v_spec = pl.BlockSpec((None, s, RET_V_DIM), lambda h, i, dec: (i, 0, h))
    return pl.pallas_call(
        _ret_kernel,
        grid_spec=pltpu.PrefetchScalarGridSpec(
            num_scalar_prefetch=1,
            grid=(RET_HEADS, b),
            in_specs=[qk_spec, qk_spec, v_spec, v_spec],
            out_specs=v_spec,
            scratch_shapes=[pltpu.VMEM((n_blocks, 2 * RET_QK_DIM, RET_V_DIM), F32),
                            pltpu.VMEM((n_blocks, 2 * RET_QK_DIM, RET_V_DIM), BF16),
                            pltpu.VMEM((c, c), F32),
                            pltpu.VMEM((4, c, RET_QK_DIM), F32),
                            pltpu.VMEM((2, RET_QK_DIM, RET_V_DIM), F32)],
        ),
        out_shape=jax.ShapeDtypeStruct((b, s, RET_V_W), BF16),
        compiler_params=pltpu.CompilerParams(
            dimension_semantics=("arbitrary", "arbitrary"), vmem_limit_bytes=VMEM_LIMIT),
        name="retention",
    )(decay_logits, rq, rk, rv, rg)


def _na_blocks(extent, q_size, k_size, window):
    n = extent // q_size
    out = []
    for i in range(n):
        q0 = i * q_size
        k0 = min(max(q0 - window // 2, 0), extent - k_size)
        out.append((q0, k0, 0 if i == 0 else (2 if i == n - 1 else 1)))
    return tuple(out)


NA_ROWS = 32
_NA_ROW_BLOCKS = _na_blocks(NA_ROWS, NA_Q_ROWS, NA_K_ROWS, NA_WIN_ROWS)
_NA_COL_BLOCKS = _na_blocks(GRID_W, NA_Q_COLS, NA_K_COLS, NA_WIN_COLS)
_NA_CONFIGS = 9


def _na_bias_kernel(rel_ref, out_ref, col_ref):
    nk = NA_K_ROWS * NA_K_COLS
    base = pl.program_id(0) * (NA_REL_ROWS * NA_REL_COLS)
    lane = lax.broadcasted_iota(jnp.int32, (NA_Q_COLS, nk), 1)
    qc = lax.broadcasted_iota(jnp.int32, (NA_Q_COLS, nk), 0)
    kc = lane % NA_K_COLS
    kr = lane // NA_K_COLS
    col_cfgs = (_NA_COL_BLOCKS[0], _NA_COL_BLOCKS[1], _NA_COL_BLOCKS[3])
    row_cfgs = (_NA_ROW_BLOCKS[0], _NA_ROW_BLOCKS[1], _NA_ROW_BLOCKS[3])
    for b, (c0, kc0, _) in enumerate(col_cfgs):
        kc_abs = kc + kc0
        qc_abs = qc + c0
        rel_col = jnp.clip(kc_abs - qc_abs + NA_WIN_COLS - 1, 0, NA_REL_COLS - 1)
        win = jnp.clip(qc_abs - NA_WIN_COLS // 2, 0, GRID_W - NA_WIN_COLS)
        in_win = (kc_abs >= win) & (kc_abs < win + NA_WIN_COLS)

        def rel_row_body(dr, carry, rel_col=rel_col, in_win=in_win, b=b):
            def rel_col_body(dc, acc):
                return jnp.where(rel_col == dc, rel_ref[base + dr * NA_REL_COLS + dc], acc)
            vals = lax.fori_loop(0, NA_REL_COLS, rel_col_body, jnp.zeros((NA_Q_COLS, nk), F32))
            col_ref[b * NA_REL_ROWS + dr] = jnp.where(in_win, vals, NEG_INF)
            return carry

        lax.fori_loop(0, NA_REL_ROWS, rel_row_body, 0)

    for a, (r0, ks, _) in enumerate(row_cfgs):
        kr_abs = kr + ks
        for qr in range(NA_Q_ROWS):
            r_abs = r0 + qr
            rs = min(max(r_abs - NA_WIN_ROWS // 2, 0), NA_ROWS - NA_WIN_ROWS)
            hits = [kr_abs == rs + j for j in range(NA_WIN_ROWS)]
            for b in range(3):
                tile = jnp.full((NA_Q_COLS, nk), NEG_INF, F32)
                for j in range(NA_WIN_ROWS):
                    dr = rs + j - r_abs + NA_WIN_ROWS - 1
                    tile = jnp.where(hits[j], col_ref[b * NA_REL_ROWS + dr], tile)
                out_ref[0, a * 3 + b, qr * NA_Q_COLS:(qr + 1) * NA_Q_COLS, :] = tile


def _na_bias_tables(rel_bias):
    nq = NA_Q_ROWS * NA_Q_COLS
    nk = NA_K_ROWS * NA_K_COLS
    return pl.pallas_call(
        _na_bias_kernel,
        grid_spec=pltpu.PrefetchScalarGridSpec(
            num_scalar_prefetch=1,
            grid=(NA_HEADS,),
            in_specs=[],
            out_specs=pl.BlockSpec((1, _NA_CONFIGS, nq, nk), lambda h, rel: (h, 0, 0, 0)),
            scratch_shapes=[pltpu.VMEM((3 * NA_REL_ROWS, NA_Q_COLS, nk), F32)],
        ),
        out_shape=jax.ShapeDtypeStruct((NA_HEADS, _NA_CONFIGS, nq, nk), F32),
        compiler_params=pltpu.CompilerParams(dimension_semantics=("parallel",)),
        name="na_bias",
    )(rel_bias.astype(F32).reshape(-1))


def _na_kernel(q_ref, k_ref, v_ref, bias_ref, o_ref):
    nq = NA_Q_ROWS * NA_Q_COLS
    nk = NA_K_ROWS * NA_K_COLS
    first = lax.broadcasted_iota(jnp.int32, (nq, NA_LANES), 1) < NA_HEAD_DIM
    nt = (((1,), (1,)), ((), ()))
    blocks = [(rb, cb) for rb in _NA_ROW_BLOCKS for cb in _NA_COL_BLOCKS]

    def scores(block, hh):
        (r0, ks, rcfg), (c0, kc0, ccfg) = block
        q = q_ref[r0:r0 + NA_Q_ROWS, c0:c0 + NA_Q_COLS, :].reshape(nq, NA_LANES)
        k = k_ref[ks:ks + NA_K_ROWS, kc0:kc0 + NA_K_COLS, :].reshape(nk, NA_LANES).astype(BF16)
        qm = jnp.where(first if hh == 0 else jnp.logical_not(first), q, jnp.zeros_like(q))
        return lax.dot_general(qm, k, nt, preferred_element_type=F32) + bias_ref[hh, rcfg * 3 + ccfg]

    def attend(block, s):
        (_, ks, _), (_, kc0, _) = block
        v = v_ref[ks:ks + NA_K_ROWS, kc0:kc0 + NA_K_COLS, :].reshape(nk, NA_LANES).astype(BF16)
        m = jnp.max(s, axis=-1, keepdims=True)
        p = jnp.exp(s - m)
        l = jnp.sum(p, axis=-1, keepdims=True)
        return jnp.dot(p.astype(BF16), v, preferred_element_type=F32) / l

    work = [(blk, hh) for blk in blocks for hh in range(2)]
    s_next = scores(*work[0])
    outs = []
    for i, (blk, hh) in enumerate(work):
        s_cur = s_next
        if i + 1 < len(work):
            s_next = scores(*work[i + 1])
        outs.append(attend(blk, s_cur))
        if hh == 1:
            (r0, _, _), (c0, _, _) = blk
            o = jnp.where(first, outs[0], outs[1])
            o_ref[r0:r0 + NA_Q_ROWS, c0:c0 + NA_Q_COLS, :] = (
                o.reshape(NA_Q_ROWS, NA_Q_COLS, NA_LANES).astype(o_ref.dtype))
            outs = []


def _natten(nq, nk, nv, bias_tabs):
    b, rows = nq.shape[0], nq.shape[1]
    slab = pl.BlockSpec((None, rows, GRID_W, NA_LANES), lambda j, i: (i, 0, 0, j))
    nqk = NA_Q_ROWS * NA_Q_COLS
    nkk = NA_K_ROWS * NA_K_COLS
    return pl.pallas_call(
        _na_kernel,
        grid=(NA_SLABS, b),
        in_specs=[slab, slab, slab,
                  pl.BlockSpec((2, _NA_CONFIGS, nqk, nkk), lambda j, i: (j, 0, 0, 0))],
        out_specs=slab,
        out_shape=jax.ShapeDtypeStruct((b, rows, GRID_W, NA_W), BF16),
        compiler_params=pltpu.CompilerParams(
            dimension_semantics=("parallel", "parallel"), vmem_limit_bytes=VMEM_LIMIT),
        name="natten",
    )(nq, nk, nv, bias_tabs)


def _mixout_kernel(x_ref, yr_ref, na_ref, gr_ref, gn_ref, wr_ref, wn_ref, wo_ref, gain_ref, o_ref):
    n_sub = x_ref.shape[0] // SUB_TILE

    def rows(j):
        return slice(j * SUB_TILE, (j + 1) * SUB_TILE)

    def branches(j):
        y_ret = jnp.dot(yr_ref[rows(j), :], wr_ref[...], preferred_element_type=F32)
        y_na = jnp.dot(na_ref[rows(j), :], wn_ref[...], preferred_element_type=F32)
        merged = (jax.nn.sigmoid(gr_ref[rows(j), :].astype(F32)) * y_ret
                  + jax.nn.sigmoid(gn_ref[rows(j), :].astype(F32)) * y_na)
        return merged.astype(BF16)

    def finish(j, merged):
        m = jnp.dot(merged, wo_ref[...], preferred_element_type=F32)
        o_ref[rows(j), :] = x_ref[rows(j), :] + _rms(m, gain_ref[...])

    merged = branches(0)
    for j in range(n_sub):
        merged_next = branches(j + 1) if j + 1 < n_sub else None
        finish(j, merged)
        merged = merged_next


def _mix_out(x, y_ret, y_na, g_ret, g_na, w_ret, w_na, w_out, gain):
    t = x.shape[0]
    tm = FFN_TILE

    def row(width):
        return pl.BlockSpec((tm, width), lambda i: (i, 0))

    return pl.pallas_call(
        _mixout_kernel,
        grid=(t // tm,),
        in_specs=[row(D_MODEL), row(RET_V_W), row(NA_W), row(D_MODEL), row(D_MODEL),
                  _const_spec((RET_V_W, D_MODEL)), _const_spec((NA_W, D_MODEL)),
                  _const_spec((D_MODEL, D_MODEL)), _const_spec((1, D_MODEL))],
        out_specs=row(D_MODEL),
        out_shape=jax.ShapeDtypeStruct((t, D_MODEL), F32),
        compiler_params=pltpu.CompilerParams(
            dimension_semantics=("parallel",), vmem_limit_bytes=VMEM_LIMIT),
        name="mix_out",
    )(x, y_ret, y_na, g_ret, g_na, w_ret, w_na, w_out, gain)


def kernel(x, ffn1_pre_norm, ffn1_w_in, ffn1_w_out, ffn1_post_norm, mix_pre_norm, w_mix_in,
           ret_decay_fwd, ret_decay_bwd, na_rel_bias, w_ret_out, w_na_out, w_mix_out, mix_post_norm,
           ffn2_pre_norm, ffn2_w_in, ffn2_w_out, ffn2_post_norm):
    b, s, d = x.shape
    assert d == D_MODEL and s % GRID_W == 0 and s % RET_BLOCK == 0 and (b * s) % TOKEN_TILE == 0
    assert s % TOKEN_TILE == 0
    rows = s // GRID_W
    assert rows == NA_ROWS, "the neighbourhood-attention tiling is laid out for a 32 x 64 token grid"
    t = b * s
    xt = x.reshape(t, d)
    for l in range(ffn1_w_in.shape[0]):
        gain = lambda g: g[l].reshape(1, D_MODEL).astype(F32)
        wt = lambda w: w[l].astype(BF16)
        xt = _ffn(xt, gain(ffn1_pre_norm), wt(ffn1_w_in), wt(ffn1_w_out), gain(ffn1_post_norm))
        rq, rk, rv, rg, nq, nk, nv, g_ret, g_na = _mix_in(xt, gain(mix_pre_norm), wt(w_mix_in), s)
        decay = jnp.stack([ret_decay_fwd[l], ret_decay_bwd[l]]).astype(F32)
        seq = lambda a: a.reshape(b, s, a.shape[-1])
        y_ret = _retention(decay, seq(rq), seq(rk), seq(rv), seq(rg)).reshape(t, RET_V_W)
        grid4 = lambda a: a.reshape(b, rows, GRID_W, NA_W)
        y_na = _natten(grid4(nq), grid4(nk), grid4(nv), _na_bias_tables(na_rel_bias[l]))
        xt = _mix_out(xt, y_ret, y_na.reshape(t, NA_W), g_ret, g_na,
                      wt(w_ret_out), wt(w_na_out), wt(w_mix_out), gain(mix_post_norm))
        xt = _ffn(xt, gain(ffn2_pre_norm), wt(ffn2_w_in), wt(ffn2_w_out), gain(ffn2_post_norm))
    return xt.reshape(b, s, d)
```

```python
import functools

import numpy as np
import jax
import jax.numpy as jnp
from jax import lax
from jax.experimental import pallas as pl
from jax.experimental.pallas import tpu as pltpu

F32 = jnp.float32
BF16 = jnp.bfloat16

D_MODEL = 1024
D_FF = 2816
RMS_EPS = 1e-6
ROPE_BASE = 10000.0
NEG_INF = -1e30
LOG2E = 1.4426950408889634

RET_HEADS = 4
RET_QK_DIM = 128
RET_V_DIM = 256
RET_QK_W = RET_HEADS * RET_QK_DIM
RET_V_W = RET_HEADS * RET_V_DIM
RET_BLOCK = 256

NA_HEADS = 8
NA_HEAD_DIM = 64
NA_W = NA_HEADS * NA_HEAD_DIM
GRID_W = 64
NA_WIN_ROWS = 8
NA_WIN_COLS = 16
NA_REL_ROWS = 2 * NA_WIN_ROWS - 1
NA_REL_COLS = 2 * NA_WIN_COLS - 1
NA_Q_ROWS = 8
NA_Q_COLS = 16
NA_K_ROWS = 16
NA_K_COLS = 32
NA_LANES = 128
NA_SLABS = NA_W // NA_LANES

MIX_SPLITS = (RET_QK_W, RET_QK_W, RET_V_W, RET_V_W, NA_W, NA_W, NA_W, D_MODEL, D_MODEL)
MIX_OFFS = tuple(int(v) for v in np.cumsum((0,) + MIX_SPLITS[:-1]))
MIX_IN_W = sum(MIX_SPLITS)

TOKEN_TILE = 512
FFN_TILE = 1024
SUB_TILE = 256
FF_CHUNK = 256
VMEM_LIMIT = 56 * 1024 * 1024


def _rms(x, gain):
    ms = jnp.mean(x * x, axis=-1, keepdims=True)
    return x * lax.rsqrt(ms + RMS_EPS) * gain


def _silu(x):
    return x * jax.nn.sigmoid(x)


def _const_spec(shape):
    nd = len(shape)
    return pl.BlockSpec(shape, lambda *_: (0,) * nd, pipeline_mode=pl.Buffered(1))


def _ffn_kernel(x_ref, gpre_ref, win_ref, wout_ref, gpost_ref, o_ref, act_ref):
    n_sub = x_ref.shape[0] // SUB_TILE

    def rows(j):
        return slice(j * SUB_TILE, (j + 1) * SUB_TILE)

    def pre(j):
        return _rms(x_ref[rows(j), :], gpre_ref[...]).astype(BF16)

    def up(j, xn):
        for c in range(D_FF // FF_CHUNK):
            lo = c * FF_CHUNK
            g = jnp.dot(xn, win_ref[:, lo:lo + FF_CHUNK], preferred_element_type=F32)
            u = jnp.dot(xn, win_ref[:, D_FF + lo:D_FF + lo + FF_CHUNK], preferred_element_type=F32)
            act_ref[rows(j), lo:lo + FF_CHUNK] = (_silu(g) * u).astype(BF16)

    def down(j):
        h = jnp.dot(act_ref[rows(j), :], wout_ref[...], preferred_element_type=F32)
        o_ref[rows(j), :] = x_ref[rows(j), :] + 0.5 * _rms(h, gpost_ref[...])

    xn = pre(0)
    for j in range(n_sub):
        xn_next = pre(j + 1) if j + 1 < n_sub else None
        up(j, xn)
        if j >= 1:
            down(j - 1)
        xn = xn_next
    down(n_sub - 1)


def _ffn(x, gpre, w_in, w_out, gpost):
    t = x.shape[0]
    tm = FFN_TILE
    row = pl.BlockSpec((tm, D_MODEL), lambda i: (i, 0))
    return pl.pallas_call(
        _ffn_kernel,
        grid=(t // tm,),
        in_specs=[row, _const_spec((1, D_MODEL)), _const_spec((D_MODEL, 2 * D_FF)),
                  _const_spec((D_FF, D_MODEL)), _const_spec((1, D_MODEL))],
        out_specs=row,
        out_shape=jax.ShapeDtypeStruct((t, D_MODEL), F32),
        scratch_shapes=[pltpu.VMEM((tm, D_FF), BF16)],
        compiler_params=pltpu.CompilerParams(
            dimension_semantics=("parallel",), vmem_limit_bytes=VMEM_LIMIT),
        name="ffn",
    )(x, gpre, w_in, w_out, gpost)


def _mixin_kernel(x_ref, gain_ref, w_ref, rope_ref,
                  rq_ref, rk_ref, rv_ref, rg_ref, nq_ref, nk_ref, nv_ref, gr_ref, gn_ref):
    n_sub = x_ref.shape[0] // SUB_TILE

    def pre(j):
        rows = slice(j * SUB_TILE, (j + 1) * SUB_TILE)
        return _rms(x_ref[rows, :], gain_ref[...]).astype(BF16)

    def project(j, u):
        rows = slice(j * SUB_TILE, (j + 1) * SUB_TILE)

        def proj(idx):
            lo = MIX_OFFS[idx]
            return jnp.dot(u, w_ref[:, lo:lo + MIX_SPLITS[idx]], preferred_element_type=F32)

        def rotary(y, cos, sin, out_ref):
            for h in range(RET_HEADS):
                sl = slice(h * RET_QK_DIM, (h + 1) * RET_QK_DIM)
                yh = y[:, sl]
                out_ref[rows, sl] = (yh * cos + pltpu.roll(yh, RET_QK_DIM // 2, 1) * sin).astype(out_ref.dtype)

        rotary(proj(0), rope_ref[0, rows, :], rope_ref[1, rows, :], rq_ref)
        rotary(proj(1), rope_ref[2, rows, :], rope_ref[3, rows, :], rk_ref)
        rv_ref[rows, :] = proj(2).astype(rv_ref.dtype)
        rg_ref[rows, :] = proj(3).astype(rg_ref.dtype)
        nq_ref[rows, :] = (proj(4) * (LOG2E * NA_HEAD_DIM ** -0.5)).astype(nq_ref.dtype)
        nk_ref[rows, :] = proj(5).astype(nk_ref.dtype)
        nv_ref[rows, :] = proj(6).astype(nv_ref.dtype)
        gr_ref[rows, :] = proj(7).astype(gr_ref.dtype)
        gn_ref[rows, :] = proj(8).astype(gn_ref.dtype)

    u = pre(0)
    for j in range(n_sub):
        u_next = pre(j + 1) if j + 1 < n_sub else None
        project(j, u)
        u = u_next


def _rope_tables(seq):
    half = RET_QK_DIM // 2
    pos = jnp.arange(seq, dtype=F32)
    inv = 1.0 / (ROPE_BASE ** jnp.linspace(0.0, 1.0, half, dtype=F32))
    ang = pos[:, None] * inv[None, :]
    cos, sin = jnp.cos(ang), jnp.sin(ang)
    cos2 = jnp.concatenate([cos, cos], axis=-1)
    sin2 = jnp.concatenate([-sin, sin], axis=-1)
    ks = RET_QK_DIM ** -0.5
    return jnp.stack([cos2, sin2, cos2 * ks, sin2 * ks])


def _mix_in(x, gain, w, seq):
    t = x.shape[0]
    tm = TOKEN_TILE
    tiles_per_seq = seq // tm
    out_dtypes = (BF16, BF16, BF16, BF16, BF16, F32, F32, BF16, BF16)
    return pl.pallas_call(
        _mixin_kernel,
        grid=(t // tm,),
        in_specs=[pl.BlockSpec((tm, D_MODEL), lambda i: (i, 0)),
                  _const_spec((1, D_MODEL)), _const_spec((D_MODEL, MIX_IN_W)),
                  pl.BlockSpec((4, tm, RET_QK_DIM), lambda i: (0, i % tiles_per_seq, 0))],
        out_specs=[pl.BlockSpec((tm, wd), lambda i: (i, 0)) for wd in MIX_SPLITS],
        out_shape=[jax.ShapeDtypeStruct((t, wd), dt) for wd, dt in zip(MIX_SPLITS, out_dtypes)],
        compiler_params=pltpu.CompilerParams(
            dimension_semantics=("parallel",), vmem_limit_bytes=VMEM_LIMIT),
        name="mix_in",
    )(x, gain, w, _rope_tables(seq))


def _log_sigmoid(z):
    return jnp.minimum(z, 0.0) - jnp.log1p(jnp.exp(-jnp.abs(z)))


def _ret_kernel(dec_ref, q_ref, k_ref, v_ref, g_ref, o_ref, u_ref, st_ref, dm_ref, vec_ref, blk_ref):
    c = RET_BLOCK
    dk = RET_QK_DIM
    dv = RET_V_DIM
    n_blocks = q_ref.shape[0] // c
    tn = (((0,), (0,)), ((), ()))
    nt = (((1,), (1,)), ((), ()))

    @pl.when(pl.program_id(1) == 0)
    def _decay_tables():
        head = pl.program_id(0)
        zf = dec_ref[0, head]
        zb = dec_ref[1, head]

        def log_gamma(z, shape):
            return _log_sigmoid(jnp.full(shape, z, F32))

        ri = lax.broadcasted_iota(jnp.int32, (c, c), 0).astype(F32)
        ci = lax.broadcasted_iota(jnp.int32, (c, c), 1).astype(F32)
        diff = ri - ci
        dm_ref[...] = jnp.where(diff >= 0, jnp.exp(log_gamma(zf, (c, c)) * jnp.maximum(diff, 0.0)),
                                jnp.exp(log_gamma(zb, (c, c)) * jnp.maximum(-diff, 0.0)))
        r = lax.broadcasted_iota(jnp.int32, (c, dk), 0).astype(F32)
        lgf = log_gamma(zf, (c, dk))
        lgb = log_gamma(zb, (c, dk))
        vec_ref[0] = jnp.exp(lgf * (r + 1.0))
        vec_ref[1] = jnp.exp(lgb * (c - r))
        vec_ref[2] = jnp.exp(lgf * (c - 1.0 - r))
        vec_ref[3] = jnp.exp(lgb * r)
        blk_ref[0] = jnp.exp(log_gamma(zf, (dk, dv)) * c)
        blk_ref[1] = jnp.exp(log_gamma(zb, (dk, dv)) * c)

    def rows(n):
        return slice(n * c, (n + 1) * c)

    for n in range(n_blocks):
        k = k_ref[rows(n), :].astype(F32)
        k_cat = jnp.concatenate([(k * vec_ref[2]).astype(BF16), (k * vec_ref[3]).astype(BF16)], axis=1)
        u_ref[n] = lax.dot_general(k_cat, v_ref[rows(n), :], tn, preferred_element_type=F32)

    sf = jnp.zeros((dk, dv), F32)
    for n in range(n_blocks):
        st_ref[n, 0:dk, :] = sf.astype(BF16)
        if n + 1 < n_blocks:
            sf = blk_ref[0] * sf + u_ref[n, 0:dk, :]
    sb = jnp.zeros((dk, dv), F32)
    for n in reversed(range(n_blocks)):
        st_ref[n, dk:2 * dk, :] = sb.astype(BF16)
        if n > 0:
            sb = blk_ref[1] * sb + u_ref[n, dk:2 * dk, :]

    def scores(n):
        return lax.dot_general(q_ref[rows(n), :], k_ref[rows(n), :], nt, preferred_element_type=F32)

    s_next = scores(0)
    for n in range(n_blocks):
        s_cur = s_next
        if n + 1 < n_blocks:
            s_next = scores(n + 1)
        p = (s_cur * dm_ref[...]).astype(BF16)
        q = q_ref[rows(n), :].astype(F32)
        lhs = jnp.concatenate([p, (q * vec_ref[0]).astype(BF16), (q * vec_ref[1]).astype(BF16)], axis=1)
        rhs = jnp.concatenate([v_ref[rows(n), :], st_ref[n]], axis=0)
        y = jnp.dot(lhs, rhs, preferred_element_type=F32)
        yn = y * lax.rsqrt(jnp.mean(y * y, axis=-1, keepdims=True) + RMS_EPS)
        o_ref[rows(n), :] = (_silu(g_ref[rows(n), :].astype(F32)) * yn).astype(o_ref.dtype)


def _retention(decay_logits, rq, rk, rv, rg):
    b, s, _ = rq.shape
    c = RET_BLOCK
    n_blocks = s // c
    qk_spec = pl.BlockSpec((None, s, RET_QK_DIM), lambda h, i, dec: (i, 0, h))
    v_spec = pl.BlockSpec((None, s, RET_V_DIM), lambda h, i, dec: (i, 0, h))
    return pl.pallas_call(
        _ret_kernel,
        grid_spec=pltpu.PrefetchScalarGridSpec(
            num_scalar_prefetch=1,
            grid=(RET_HEADS, b),
            in_specs=[qk_spec, qk_spec, v_spec, v_spec],
            out_specs=v_spec,
            scratch_shapes=[pltpu.VMEM((n_blocks, 2 * RET_QK_DIM, RET_V_DIM), F32),
                            pltpu.VMEM((n_blocks, 2 * RET_QK_DIM, RET_V_DIM), BF16),
                            pltpu.VMEM((c, c), F32),
                            pltpu.VMEM((4, c, RET_QK_DIM), F32),
                            pltpu.VMEM((2, RET_QK_DIM, RET_V_DIM), F32)],
        ),
        out_shape=jax.ShapeDtypeStruct((b, s, RET_V_W), BF16),
        compiler_params=pltpu.CompilerParams(
            dimension_semantics=("arbitrary", "arbitrary"), vmem_limit_bytes=VMEM_LIMIT),
        name="retention",
    )(decay_logits, rq, rk, rv, rg)


def _na_blocks(extent, q_size, k_size, window):
    n = extent // q_size
    out = []
    for i in range(n):
        q0 = i * q_size
        k0 = min(max(q0 - window // 2, 0), extent - k_size)
        out.append((q0, k0, 0 if i == 0 else (2 if i == n - 1 else 1)))
    return tuple(out)


NA_ROWS = 32
_NA_ROW_BLOCKS = _na_blocks(NA_ROWS, NA_Q_ROWS, NA_K_ROWS, NA_WIN_ROWS)
_NA_COL_BLOCKS = _na_blocks(GRID_W, NA_Q_COLS, NA_K_COLS, NA_WIN_COLS)
_NA_CONFIGS = 9


def _na_bias_kernel(rel_ref, out_ref, col_ref):
    nk = NA_K_ROWS * NA_K_COLS
    base = pl.program_id(0) * (NA_REL_ROWS * NA_REL_COLS)
    lane = lax.broadcasted_iota(jnp.int32, (NA_Q_COLS, nk), 1)
    qc = lax.broadcasted_iota(jnp.int32, (NA_Q_COLS, nk), 0)
    kc = lane % NA_K_COLS
    kr = lane // NA_K_COLS
    col_cfgs = (_NA_COL_BLOCKS[0], _NA_COL_BLOCKS[1], _NA_COL_BLOCKS[3])
    row_cfgs = (_NA_ROW_BLOCKS[0], _NA_ROW_BLOCKS[1], _NA_ROW_BLOCKS[3])
    for b, (c0, kc0, _) in enumerate(col_cfgs):
        kc_abs = kc + kc0
        qc_abs = qc + c0
        rel_col = jnp.clip(kc_abs - qc_abs + NA_WIN_COLS - 1, 0, NA_REL_COLS - 1)
        win = jnp.clip(qc_abs - NA_WIN_COLS // 2, 0, GRID_W - NA_WIN_COLS)
        in_win = (kc_abs >= win) & (kc_abs < win + NA_WIN_COLS)

        def rel_row_body(dr, carry, rel_col=rel_col, in_win=in_win, b=b):
            vals = jnp.zeros((NA_Q_COLS, nk), F32)
            for dc in range(NA_REL_COLS):
                vals = jnp.where(rel_col == dc, rel_ref[base + dr * NA_REL_COLS + dc], vals)
            col_ref[b * NA_REL_ROWS + dr] = jnp.where(in_win, vals * LOG2E, NEG_INF)
            return carry

        lax.fori_loop(0, NA_REL_ROWS, rel_row_body, 0)

    for a, (r0, ks, _) in enumerate(row_cfgs):
        kr_abs = kr + ks
        for qr in range(NA_Q_ROWS):
            r_abs = r0 + qr
            rs = min(max(r_abs - NA_WIN_ROWS // 2, 0), NA_ROWS - NA_WIN_ROWS)
            hits = [kr_abs == rs + j for j in range(NA_WIN_ROWS)]
            for b in range(3):
                tile = jnp.full((NA_Q_COLS, nk), NEG_INF, F32)
                for j in range(NA_WIN_ROWS):
                    dr = rs + j - r_abs + NA_WIN_ROWS - 1
                    tile = jnp.where(hits[j], col_ref[b * NA_REL_ROWS + dr], tile)
                out_ref[0, a * 3 + b, qr * NA_Q_COLS:(qr + 1) * NA_Q_COLS, :] = tile


def _na_bias_tables(rel_bias):
    nq = NA_Q_ROWS * NA_Q_COLS
    nk = NA_K_ROWS * NA_K_COLS
    return pl.pallas_call(
        _na_bias_kernel,
        grid_spec=pltpu.PrefetchScalarGridSpec(
            num_scalar_prefetch=1,
            grid=(NA_HEADS,),
            in_specs=[],
            out_specs=pl.BlockSpec((1, _NA_CONFIGS, nq, nk), lambda h, rel: (h, 0, 0, 0)),
            scratch_shapes=[pltpu.VMEM((3 * NA_REL_ROWS, NA_Q_COLS, nk), F32)],
        ),
        out_shape=jax.ShapeDtypeStruct((NA_HEADS, _NA_CONFIGS, nq, nk), F32),
        compiler_params=pltpu.CompilerParams(dimension_semantics=("parallel",)),
        name="na_bias",
    )(rel_bias.astype(F32).reshape(-1))


def _na_kernel(q_ref, k_ref, v_ref, bias_ref, o_ref, kb_ref, vb_ref):
    nq = NA_Q_ROWS * NA_Q_COLS
    nk = NA_K_ROWS * NA_K_COLS
    shift = NA_WIN_COLS // 2
    kept = GRID_W - 2 * shift
    for src, dst in ((k_ref, kb_ref), (v_ref, vb_ref)):
        dst[0] = src[...].astype(BF16)
        dst[1, :, 0:kept, :] = src[:, shift:shift + kept, :].astype(BF16)

    def window(ref, block):
        (_, ks, _), (_, kc0, _) = block
        which, c = (0, kc0) if kc0 % (2 * shift) == 0 else (1, kc0 - shift)
        assert c % (2 * shift) == 0 and (which == 0 or c + NA_K_COLS <= kept)
        return ref[which, ks:ks + NA_K_ROWS, c:c + NA_K_COLS, :].reshape(nk, NA_LANES)

    first = lax.broadcasted_iota(jnp.int32, (nq, NA_LANES), 1) < NA_HEAD_DIM
    ones = jnp.ones((nk, NA_LANES), BF16)
    nt = (((1,), (1,)), ((), ()))
    blocks = [(rb, cb) for rb in _NA_ROW_BLOCKS for cb in _NA_COL_BLOCKS]

    def scores(block):
        (r0, _, rcfg), (c0, _, ccfg) = block
        q = q_ref[r0:r0 + NA_Q_ROWS, c0:c0 + NA_Q_COLS, :].reshape(nq, NA_LANES)
        zero = jnp.zeros_like(q)
        q2 = jnp.concatenate([jnp.where(first, q, zero), jnp.where(first, zero, q)], axis=0)
        cfg = rcfg * 3 + ccfg
        bias = jnp.concatenate([bias_ref[0, cfg], bias_ref[1, cfg]], axis=0)
        return lax.dot_general(q2, window(kb_ref, block), nt, preferred_element_type=F32) + bias

    def attend(block, s):
        m = jnp.max(s, axis=-1, keepdims=True)
        p = jnp.exp2(s - m).astype(BF16)
        o = jnp.dot(p, jnp.concatenate([window(vb_ref, block), ones], axis=1),
                    preferred_element_type=F32)
        o = o[:, :NA_LANES] / o[:, NA_LANES:]
        return jnp.where(first, o[:nq], o[nq:])

    s_next = scores(blocks[0])
    for i, blk in enumerate(blocks):
        s_cur = s_next
        if i + 1 < len(blocks):
            s_next = scores(blocks[i + 1])
        (r0, _, _), (c0, _, _) = blk
        o_ref[r0:r0 + NA_Q_ROWS, c0:c0 + NA_Q_COLS, :] = (
            attend(blk, s_cur).reshape(NA_Q_ROWS, NA_Q_COLS, NA_LANES).astype(o_ref.dtype))


def _natten(nq, nk, nv, bias_tabs):
    b, rows = nq.shape[0], nq.shape[1]
    slab = pl.BlockSpec((None, rows, GRID_W, NA_LANES), lambda j, i: (i, 0, 0, j))
    nqk = NA_Q_ROWS * NA_Q_COLS
    nkk = NA_K_ROWS * NA_K_COLS
    return pl.pallas_call(
        _na_kernel,
        grid=(NA_SLABS, b),
        in_specs=[slab, slab, slab,
                  pl.BlockSpec((2, _NA_CONFIGS, nqk, nkk), lambda j, i: (j, 0, 0, 0))],
        out_specs=slab,
        out_shape=jax.ShapeDtypeStruct((b, rows, GRID_W, NA_W), BF16),
        scratch_shapes=[pltpu.VMEM((2, rows, GRID_W, NA_LANES), BF16),
                        pltpu.VMEM((2, rows, GRID_W, NA_LANES), BF16)],
        compiler_params=pltpu.CompilerParams(
            dimension_semantics=("parallel", "parallel"), vmem_limit_bytes=VMEM_LIMIT),
        name="natten",
    )(nq, nk, nv, bias_tabs)


def _mixout_kernel(x_ref, yr_ref, na_ref, gr_ref, gn_ref, wr_ref, wn_ref, wo_ref, gain_ref, o_ref):
    n_sub = x_ref.shape[0] // SUB_TILE

    def rows(j):
        return slice(j * SUB_TILE, (j + 1) * SUB_TILE)

    def branches(j):
        y_ret = jnp.dot(yr_ref[rows(j), :], wr_ref[...], preferred_element_type=F32)
        y_na = jnp.dot(na_ref[rows(j), :], wn_ref[...], preferred_element_type=F32)
        merged = (jax.nn.sigmoid(gr_ref[rows(j), :].astype(F32)) * y_ret
                  + jax.nn.sigmoid(gn_ref[rows(j), :].astype(F32)) * y_na)
        return merged.astype(BF16)

    def finish(j, merged):
        m = jnp.dot(merged, wo_ref[...], preferred_element_type=F32)
        o_ref[rows(j), :] = x_ref[rows(j), :] + _rms(m, gain_ref[...])

    merged = branches(0)
    for j in range(n_sub):
        merged_next = branches(j + 1) if j + 1 < n_sub else None
        finish(j, merged)
        merged = merged_next


def _mix_out(x, y_ret, y_na, g_ret, g_na, w_ret, w_na, w_out, gain):
    t = x.shape[0]
    tm = FFN_TILE

    def row(width):
        return pl.BlockSpec((tm, width), lambda i: (i, 0))

    return pl.pallas_call(
        _mixout_kernel,
        grid=(t // tm,),
        in_specs=[row(D_MODEL), row(RET_V_W), row(NA_W), row(D_MODEL), row(D_MODEL),
                  _const_spec((RET_V_W, D_MODEL)), _const_spec((NA_W, D_MODEL)),
                  _const_spec((D_MODEL, D_MODEL)), _const_spec((1, D_MODEL))],
        out_specs=row(D_MODEL),
        out_shape=jax.ShapeDtypeStruct((t, D_MODEL), F32),
        compiler_params=pltpu.CompilerParams(
            dimension_semantics=("parallel",), vmem_limit_bytes=VMEM_LIMIT),
        name="mix_out",
    )(x, y_ret, y_na, g_ret, g_na, w_ret, w_na, w_out, gain)


def kernel(x, ffn1_pre_norm, ffn1_w_in, ffn1_w_out, ffn1_post_norm, mix_pre_norm, w_mix_in,
           ret_decay_fwd, ret_decay_bwd, na_rel_bias, w_ret_out, w_na_out, w_mix_out, mix_post_norm,
           ffn2_pre_norm, ffn2_w_in, ffn2_w_out, ffn2_post_norm):
    b, s, d = x.shape
    assert d == D_MODEL and s % GRID_W == 0 and s % RET_BLOCK == 0 and (b * s) % TOKEN_TILE == 0
    assert s % TOKEN_TILE == 0
    rows = s // GRID_W
    assert rows == NA_ROWS, "the neighbourhood-attention tiling is laid out for a 32 x 64 token grid"
    t = b * s
    xt = x.reshape(t, d)
    for l in range(ffn1_w_in.shape[0]):
        gain = lambda g: g[l].reshape(1, D_MODEL).astype(F32)
        wt = lambda w: w[l].astype(BF16)
        xt = _ffn(xt, gain(ffn1_pre_norm), wt(ffn1_w_in), wt(ffn1_w_out), gain(ffn1_post_norm))
        rq, rk, rv, rg, nq, nk, nv, g_ret, g_na = _mix_in(xt, gain(mix_pre_norm), wt(w_mix_in), s)
        decay = jnp.stack([ret_decay_fwd[l], ret_decay_bwd[l]]).astype(F32)
        seq = lambda a: a.reshape(b, s, a.shape[-1])
        y_ret = _retention(decay, seq(rq), seq(rk), seq(rv), seq(rg)).reshape(t, RET_V_W)
        grid4 = lambda a: a.reshape(b, rows, GRID_W, NA_W)
        y_na = _natten(grid4(nq), grid4(nk), grid4(nv), _na_bias_tables(na_rel_bias[l]))
        xt = _mix_out(xt, y_ret, y_na.reshape(t, NA_W), g_ret, g_na,
                      wt(w_ret_out), wt(w_na_out), wt(w_mix_out), gain(mix_post_norm))
        xt = _ffn(xt, gain(ffn2_pre_norm), wt(ffn2_w_in), wt(ffn2_w_out), gain(ffn2_post_norm))
    return xt.reshape(b, s, d)
```

```python
import functools

import numpy as np
import jax
import jax.numpy as jnp
from jax import lax
from jax.experimental import pallas as pl
from jax.experimental.pallas import tpu as pltpu

F32 = jnp.float32
BF16 = jnp.bfloat16

D_MODEL = 1024
D_FF = 2816
RMS_EPS = 1e-6
ROPE_BASE = 10000.0
NEG_INF = -1e30
LOG2E = 1.4426950408889634

RET_HEADS = 4
RET_QK_DIM = 128
RET_V_DIM = 256
RET_QK_W = RET_HEADS * RET_QK_DIM
RET_V_W = RET_HEADS * RET_V_DIM
RET_BLOCK = 256

NA_HEADS = 8
NA_HEAD_DIM = 64
NA_W = NA_HEADS * NA_HEAD_DIM
GRID_W = 64
NA_WIN_ROWS = 8
NA_WIN_COLS = 16
NA_REL_ROWS = 2 * NA_WIN_ROWS - 1
NA_REL_COLS = 2 * NA_WIN_COLS - 1
NA_Q_ROWS = 8
NA_Q_COLS = 16
NA_K_ROWS = 16
NA_K_COLS = 32
NA_LANES = 128
NA_SLABS = NA_W // NA_LANES

MIX_SPLITS = (RET_QK_W, RET_QK_W, RET_V_W, RET_V_W, NA_W, NA_W, NA_W, D_MODEL, D_MODEL)
MIX_OFFS = tuple(int(v) for v in np.cumsum((0,) + MIX_SPLITS[:-1]))
MIX_IN_W = sum(MIX_SPLITS)

TOKEN_TILE = 512
FFN_TILE = 1024
SUB_TILE = 256
FF_CHUNK = 256
VMEM_LIMIT = 56 * 1024 * 1024


def _rms(x, gain):
    ms = jnp.mean(x * x, axis=-1, keepdims=True)
    return x * lax.rsqrt(ms + RMS_EPS) * gain


def _silu(x):
    return x * jax.nn.sigmoid(x)


def _const_spec(shape):
    nd = len(shape)
    return pl.BlockSpec(shape, lambda *_: (0,) * nd, pipeline_mode=pl.Buffered(1))


def _ffn_kernel(x_ref, gpre_ref, win_ref, wout_ref, gpost_ref, o_ref, act_ref):
    n_sub = x_ref.shape[0] // SUB_TILE

    def rows(j):
        return slice(j * SUB_TILE, (j + 1) * SUB_TILE)

    def pre(j):
        return _rms(x_ref[rows(j), :], gpre_ref[...]).astype(BF16)

    def up(j, xn):
        for c in range(D_FF // FF_CHUNK):
            lo = c * FF_CHUNK
            g = jnp.dot(xn, win_ref[:, lo:lo + FF_CHUNK], preferred_element_type=F32)
            u = jnp.dot(xn, win_ref[:, D_FF + lo:D_FF + lo + FF_CHUNK], preferred_element_type=F32)
            act_ref[rows(j), lo:lo + FF_CHUNK] = (_silu(g) * u).astype(BF16)

    def down(j):
        h = jnp.dot(act_ref[rows(j), :], wout_ref[...], preferred_element_type=F32)
        o_ref[rows(j), :] = x_ref[rows(j), :] + 0.5 * _rms(h, gpost_ref[...])

    xn = pre(0)
    for j in range(n_sub):
        xn_next = pre(j + 1) if j + 1 < n_sub else None
        up(j, xn)
        if j >= 1:
            down(j - 1)
        xn = xn_next
    down(n_sub - 1)


def _ffn(x, gpre, w_in, w_out, gpost):
    t = x.shape[0]
    tm = FFN_TILE
    row = pl.BlockSpec((tm, D_MODEL), lambda i: (i, 0))
    return pl.pallas_call(
        _ffn_kernel,
        grid=(t // tm,),
        in_specs=[row, _const_spec((1, D_MODEL)), _const_spec((D_MODEL, 2 * D_FF)),
                  _const_spec((D_FF, D_MODEL)), _const_spec((1, D_MODEL))],
        out_specs=row,
        out_shape=jax.ShapeDtypeStruct((t, D_MODEL), F32),
        scratch_shapes=[pltpu.VMEM((tm, D_FF), BF16)],
        compiler_params=pltpu.CompilerParams(
            dimension_semantics=("parallel",), vmem_limit_bytes=VMEM_LIMIT),
        name="ffn",
    )(x, gpre, w_in, w_out, gpost)


def _mixin_kernel(x_ref, gain_ref, w_ref, rope_ref,
                  rq_ref, rk_ref, rv_ref, rg_ref, nq_ref, nk_ref, nv_ref, gr_ref, gn_ref):
    n_sub = x_ref.shape[0] // SUB_TILE

    def pre(j):
        rows = slice(j * SUB_TILE, (j + 1) * SUB_TILE)
        return _rms(x_ref[rows, :], gain_ref[...]).astype(BF16)

    def project(j, u):
        rows = slice(j * SUB_TILE, (j + 1) * SUB_TILE)

        def proj(idx):
            lo = MIX_OFFS[idx]
            return jnp.dot(u, w_ref[:, lo:lo + MIX_SPLITS[idx]], preferred_element_type=F32)

        def rotary(y, cos, sin, out_ref):
            for h in range(RET_HEADS):
                sl = slice(h * RET_QK_DIM, (h + 1) * RET_QK_DIM)
                yh = y[:, sl]
                out_ref[rows, sl] = (yh * cos + pltpu.roll(yh, RET_QK_DIM // 2, 1) * sin).astype(out_ref.dtype)

        rotary(proj(0), rope_ref[0, rows, :], rope_ref[1, rows, :], rq_ref)
        rotary(proj(1), rope_ref[2, rows, :], rope_ref[3, rows, :], rk_ref)
        rv_ref[rows, :] = proj(2).astype(rv_ref.dtype)
        rg_ref[rows, :] = _silu(proj(3)).astype(rg_ref.dtype)
        nq_ref[rows, :] = (proj(4) * (LOG2E * NA_HEAD_DIM ** -0.5)).astype(nq_ref.dtype)
        nk_ref[rows, :] = proj(5).astype(nk_ref.dtype)
        nv_ref[rows, :] = proj(6).astype(nv_ref.dtype)
        gr_ref[rows, :] = proj(7).astype(gr_ref.dtype)
        gn_ref[rows, :] = proj(8).astype(gn_ref.dtype)

    u = pre(0)
    for j in range(n_sub):
        u_next = pre(j + 1) if j + 1 < n_sub else None
        project(j, u)
        u = u_next


def _rope_tables(seq):
    half = RET_QK_DIM // 2
    pos = jnp.arange(seq, dtype=F32)
    inv = 1.0 / (ROPE_BASE ** jnp.linspace(0.0, 1.0, half, dtype=F32))
    ang = pos[:, None] * inv[None, :]
    cos, sin = jnp.cos(ang), jnp.sin(ang)
    cos2 = jnp.concatenate([cos, cos], axis=-1)
    sin2 = jnp.concatenate([-sin, sin], axis=-1)
    ks = RET_QK_DIM ** -0.5
    return jnp.stack([cos2, sin2, cos2 * ks, sin2 * ks])


def _mix_in(x, gain, w, seq):
    t = x.shape[0]
    tm = TOKEN_TILE
    tiles_per_seq = seq // tm
    out_dtypes = (BF16, BF16, BF16, BF16, BF16, F32, F32, BF16, BF16)
    return pl.pallas_call(
        _mixin_kernel,
        grid=(t // tm,),
        in_specs=[pl.BlockSpec((tm, D_MODEL), lambda i: (i, 0)),
                  _const_spec((1, D_MODEL)), _const_spec((D_MODEL, MIX_IN_W)),
                  pl.BlockSpec((4, tm, RET_QK_DIM), lambda i: (0, i % tiles_per_seq, 0))],
        out_specs=[pl.BlockSpec((tm, wd), lambda i: (i, 0)) for wd in MIX_SPLITS],
        out_shape=[jax.ShapeDtypeStruct((t, wd), dt) for wd, dt in zip(MIX_SPLITS, out_dtypes)],
        compiler_params=pltpu.CompilerParams(
            dimension_semantics=("parallel",), vmem_limit_bytes=VMEM_LIMIT),
        name="mix_in",
    )(x, gain, w, _rope_tables(seq))


def _log_sigmoid(z):
    return jnp.minimum(z, 0.0) - jnp.log1p(jnp.exp(-jnp.abs(z)))


def _ret_kernel(dec_ref, q_ref, k_ref, v_ref, g_ref, o_ref, u_ref, st_ref, dm_ref, vec_ref, blk_ref):
    c = RET_BLOCK
    dk = RET_QK_DIM
    dv = RET_V_DIM
    n_blocks = q_ref.shape[0] // c
    tn = (((0,), (0,)), ((), ()))
    nt = (((1,), (1,)), ((), ()))

    @pl.when(pl.program_id(1) == 0)
    def _decay_tables():
        head = pl.program_id(0)
        zf = dec_ref[0, head]
        zb = dec_ref[1, head]

        def log_gamma(z, shape):
            return _log_sigmoid(jnp.full(shape, z, F32))

        ri = lax.broadcasted_iota(jnp.int32, (c, c), 0).astype(F32)
        ci = lax.broadcasted_iota(jnp.int32, (c, c), 1).astype(F32)
        diff = ri - ci
        dm_ref[...] = jnp.where(diff >= 0, jnp.exp(log_gamma(zf, (c, c)) * jnp.maximum(diff, 0.0)),
                                jnp.exp(log_gamma(zb, (c, c)) * jnp.maximum(-diff, 0.0)))
        r = lax.broadcasted_iota(jnp.int32, (c, dk), 0).astype(F32)
        lgf = log_gamma(zf, (c, dk))
        lgb = log_gamma(zb, (c, dk))
        vec_ref[0] = jnp.exp(lgf * (r + 1.0)).astype(BF16)
        vec_ref[1] = jnp.exp(lgb * (c - r)).astype(BF16)
        vec_ref[2] = jnp.exp(lgf * (c - 1.0 - r)).astype(BF16)
        vec_ref[3] = jnp.exp(lgb * r).astype(BF16)
        blk_ref[0] = jnp.exp(log_gamma(zf, (dk, dv)) * c)
        blk_ref[1] = jnp.exp(log_gamma(zb, (dk, dv)) * c)

    def rows(n):
        return slice(n * c, (n + 1) * c)

    for n in range(n_blocks):
        k = k_ref[rows(n), :]
        k_cat = jnp.concatenate([k * vec_ref[2], k * vec_ref[3]], axis=1)
        u_ref[n] = lax.dot_general(k_cat, v_ref[rows(n), :], tn, preferred_element_type=F32)

    sf = jnp.zeros((dk, dv), F32)
    for n in range(n_blocks):
        st_ref[n, 0:dk, :] = sf.astype(BF16)
        if n + 1 < n_blocks:
            sf = blk_ref[0] * sf + u_ref[n, 0:dk, :]
    sb = jnp.zeros((dk, dv), F32)
    for n in reversed(range(n_blocks)):
        st_ref[n, dk:2 * dk, :] = sb.astype(BF16)
        if n > 0:
            sb = blk_ref[1] * sb + u_ref[n, dk:2 * dk, :]

    def scores(n):
        return lax.dot_general(q_ref[rows(n), :], k_ref[rows(n), :], nt, preferred_element_type=F32)

    s_next = scores(0)
    for n in range(n_blocks):
        s_cur = s_next
        if n + 1 < n_blocks:
            s_next = scores(n + 1)
        p = (s_cur * dm_ref[...]).astype(BF16)
        q = q_ref[rows(n), :]
        lhs = jnp.concatenate([p, q * vec_ref[0], q * vec_ref[1]], axis=1)
        rhs = jnp.concatenate([v_ref[rows(n), :], st_ref[n]], axis=0)
        y = jnp.dot(lhs, rhs, preferred_element_type=F32)
        yn = y * lax.rsqrt(jnp.mean(y * y, axis=-1, keepdims=True) + RMS_EPS)
        o_ref[rows(n), :] = (g_ref[rows(n), :].astype(F32) * yn).astype(o_ref.dtype)


def _retention(decay_logits, rq, rk, rv, rg):
    b, s, _ = rq.shape
    c = RET_BLOCK
    n_blocks = s // c
    qk_spec = pl.BlockSpec((None, s, RET_QK_DIM), lambda h, i, dec: (i, 0, h))
    v_spec = pl.BlockSpec((None, s, RET_V_DIM), lambda h, i, dec: (i, 0, h))
    return pl.pallas_call(
        _ret_kernel,
        grid_spec=pltpu.PrefetchScalarGridSpec(
            num_scalar_prefetch=1,
            grid=(RET_HEADS, b),
            in_specs=[qk_spec, qk_spec, v_spec, v_spec],
            out_specs=v_spec,
            scratch_shapes=[pltpu.VMEM((n_blocks, 2 * RET_QK_DIM, RET_V_DIM), F32),
                            pltpu.VMEM((n_blocks, 2 * RET_QK_DIM, RET_V_DIM), BF16),
                            pltpu.VMEM((c, c), F32),
                            pltpu.VMEM((4, c, RET_QK_DIM), BF16),
                            pltpu.VMEM((2, RET_QK_DIM, RET_V_DIM), F32)],
        ),
        out_shape=jax.ShapeDtypeStruct((b, s, RET_V_W), BF16),
        compiler_params=pltpu.CompilerParams(
            dimension_semantics=("arbitrary", "arbitrary"), vmem_limit_bytes=VMEM_LIMIT),
        name="retention",
    )(decay_logits, rq, rk, rv, rg)


def _na_blocks(extent, q_size, k_size, window):
    n = extent // q_size
    out = []
    for i in range(n):
        q0 = i * q_size
        k0 = min(max(q0 - window // 2, 0), extent - k_size)
        out.append((q0, k0, 0 if i == 0 else (2 if i == n - 1 else 1)))
    return tuple(out)


NA_ROWS = 32
_NA_ROW_BLOCKS = _na_blocks(NA_ROWS, NA_Q_ROWS, NA_K_ROWS, NA_WIN_ROWS)
_NA_COL_BLOCKS = _na_blocks(GRID_W, NA_Q_COLS, NA_K_COLS, NA_WIN_COLS)
_NA_CONFIGS = 9


def _na_bias_kernel(rel_ref, out_ref, col_ref):
    nk = NA_K_ROWS * NA_K_COLS
    base = pl.program_id(0) * (NA_REL_ROWS * NA_REL_COLS)
    lane = lax.broadcasted_iota(jnp.int32, (NA_Q_COLS, nk), 1)
    qc = lax.broadcasted_iota(jnp.int32, (NA_Q_COLS, nk), 0)
    kc = lane % NA_K_COLS
    kr = lane // NA_K_COLS
    col_cfgs = (_NA_COL_BLOCKS[0], _NA_COL_BLOCKS[1], _NA_COL_BLOCKS[3])
    row_cfgs = (_NA_ROW_BLOCKS[0], _NA_ROW_BLOCKS[1], _NA_ROW_BLOCKS[3])
    for b, (c0, kc0, _) in enumerate(col_cfgs):
        kc_abs = kc + kc0
        qc_abs = qc + c0
        rel_col = jnp.clip(kc_abs - qc_abs + NA_WIN_COLS - 1, 0, NA_REL_COLS - 1)
        win = jnp.clip(qc_abs - NA_WIN_COLS // 2, 0, GRID_W - NA_WIN_COLS)
        in_win = (kc_abs >= win) & (kc_abs < win + NA_WIN_COLS)

        def rel_row_body(dr, carry, rel_col=rel_col, in_win=in_win, b=b):
            vals = jnp.zeros((NA_Q_COLS, nk), F32)
            for dc in range(NA_REL_COLS):
                vals = jnp.where(rel_col == dc, rel_ref[base + dr * NA_REL_COLS + dc], vals)
            col_ref[b * NA_REL_ROWS + dr] = jnp.where(in_win, vals * LOG2E, NEG_INF)
            return carry

        lax.fori_loop(0, NA_REL_ROWS, rel_row_body, 0)

    for a, (r0, ks, _) in enumerate(row_cfgs):
        kr_abs = kr + ks
        for qr in range(NA_Q_ROWS):
            r_abs = r0 + qr
            rs = min(max(r_abs - NA_WIN_ROWS // 2, 0), NA_ROWS - NA_WIN_ROWS)
            hits = [kr_abs == rs + j for j in range(NA_WIN_ROWS)]
            for b in range(3):
                tile = jnp.full((NA_Q_COLS, nk), NEG_INF, F32)
                for j in range(NA_WIN_ROWS):
                    dr = rs + j - r_abs + NA_WIN_ROWS - 1
                    tile = jnp.where(hits[j], col_ref[b * NA_REL_ROWS + dr], tile)
                out_ref[0, a * 3 + b, qr * NA_Q_COLS:(qr + 1) * NA_Q_COLS, :] = tile


def _na_bias_tables(rel_bias):
    nq = NA_Q_ROWS * NA_Q_COLS
    nk = NA_K_ROWS * NA_K_COLS
    return pl.pallas_call(
        _na_bias_kernel,
        grid_spec=pltpu.PrefetchScalarGridSpec(
            num_scalar_prefetch=1,
            grid=(NA_HEADS,),
            in_specs=[],
            out_specs=pl.BlockSpec((1, _NA_CONFIGS, nq, nk), lambda h, rel: (h, 0, 0, 0)),
            scratch_shapes=[pltpu.VMEM((3 * NA_REL_ROWS, NA_Q_COLS, nk), F32)],
        ),
        out_shape=jax.ShapeDtypeStruct((NA_HEADS, _NA_CONFIGS, nq, nk), F32),
        compiler_params=pltpu.CompilerParams(dimension_semantics=("parallel",)),
        name="na_bias",
    )(rel_bias.astype(F32).reshape(-1))


def _na_kernel(q_ref, k_ref, v_ref, bias_ref, o_ref, kb_ref, vb_ref):
    nq = NA_Q_ROWS * NA_Q_COLS
    nk = NA_K_ROWS * NA_K_COLS
    shift = NA_WIN_COLS // 2
    kept = GRID_W - 2 * shift
    for src, dst in ((k_ref, kb_ref), (v_ref, vb_ref)):
        dst[0] = src[...].astype(BF16)
        dst[1, :, 0:kept, :] = src[:, shift:shift + kept, :].astype(BF16)

    def window(ref, block):
        (_, ks, _), (_, kc0, _) = block
        which, c = (0, kc0) if kc0 % (2 * shift) == 0 else (1, kc0 - shift)
        assert c % (2 * shift) == 0 and (which == 0 or c + NA_K_COLS <= kept)
        return ref[which, ks:ks + NA_K_ROWS, c:c + NA_K_COLS, :].reshape(nk, NA_LANES)

    first = lax.broadcasted_iota(jnp.int32, (nq, NA_LANES), 1) < NA_HEAD_DIM
    ones = jnp.ones((nk, NA_LANES), BF16)
    nt = (((1,), (1,)), ((), ()))
    blocks = [(rb, cb) for rb in _NA_ROW_BLOCKS for cb in _NA_COL_BLOCKS]

    def scores(block):
        (r0, _, rcfg), (c0, _, ccfg) = block
        q = q_ref[r0:r0 + NA_Q_ROWS, c0:c0 + NA_Q_COLS, :].reshape(nq, NA_LANES)
        zero = jnp.zeros_like(q)
        q2 = jnp.concatenate([jnp.where(first, q, zero), jnp.where(first, zero, q)], axis=0)
        cfg = rcfg * 3 + ccfg
        bias = jnp.concatenate([bias_ref[0, cfg], bias_ref[1, cfg]], axis=0)
        return lax.dot_general(q2, window(kb_ref, block), nt, preferred_element_type=F32) + bias

    def attend(block, s):
        m = jnp.max(s, axis=-1, keepdims=True)
        p = jnp.exp2(s - m).astype(BF16)
        o = jnp.dot(p, jnp.concatenate([window(vb_ref, block), ones], axis=1),
                    preferred_element_type=F32)
        o = o[:, :NA_LANES] / o[:, NA_LANES:]
        return jnp.where(first, o[:nq], o[nq:])

    s_next = scores(blocks[0])
    for i, blk in enumerate(blocks):
        s_cur = s_next
        if i + 1 < len(blocks):
            s_next = scores(blocks[i + 1])
        (r0, _, _), (c0, _, _) = blk
        o_ref[r0:r0 + NA_Q_ROWS, c0:c0 + NA_Q_COLS, :] = (
            attend(blk, s_cur).reshape(NA_Q_ROWS, NA_Q_COLS, NA_LANES).astype(o_ref.dtype))


def _natten(nq, nk, nv, bias_tabs):
    b, rows = nq.shape[0], nq.shape[1]
    slab = pl.BlockSpec((None, rows, GRID_W, NA_LANES), lambda j, i: (i, 0, 0, j))
    nqk = NA_Q_ROWS * NA_Q_COLS
    nkk = NA_K_ROWS * NA_K_COLS
    return pl.pallas_call(
        _na_kernel,
        grid=(NA_SLABS, b),
        in_specs=[slab, slab, slab,
                  pl.BlockSpec((2, _NA_CONFIGS, nqk, nkk), lambda j, i: (j, 0, 0, 0))],
        out_specs=slab,
        out_shape=jax.ShapeDtypeStruct((b, rows, GRID_W, NA_W), BF16),
        scratch_shapes=[pltpu.VMEM((2, rows, GRID_W, NA_LANES), BF16),
                        pltpu.VMEM((2, rows, GRID_W, NA_LANES), BF16)],
        compiler_params=pltpu.CompilerParams(
            dimension_semantics=("parallel", "parallel"), vmem_limit_bytes=VMEM_LIMIT),
        name="natten",
    )(nq, nk, nv, bias_tabs)


def _mixout_kernel(x_ref, yr_ref, na_ref, gr_ref, gn_ref, wr_ref, wn_ref, wo_ref, gain_ref, o_ref):
    n_sub = x_ref.shape[0] // SUB_TILE

    def rows(j):
        return slice(j * SUB_TILE, (j + 1) * SUB_TILE)

    def branches(j):
        y_ret = jnp.dot(yr_ref[rows(j), :], wr_ref[...], preferred_element_type=F32)
        y_na = jnp.dot(na_ref[rows(j), :], wn_ref[...], preferred_element_type=F32)
        merged = (jax.nn.sigmoid(gr_ref[rows(j), :].astype(F32)) * y_ret
                  + jax.nn.sigmoid(gn_ref[rows(j), :].astype(F32)) * y_na)
        return merged.astype(BF16)

    def finish(j, merged):
        m = jnp.dot(merged, wo_ref[...], preferred_element_type=F32)
        o_ref[rows(j), :] = x_ref[rows(j), :] + _rms(m, gain_ref[...])

    merged = branches(0)
    for j in range(n_sub):
        merged_next = branches(j + 1) if j + 1 < n_sub else None
        finish(j, merged)
        merged = merged_next


def _mix_out(x, y_ret, y_na, g_ret, g_na, w_ret, w_na, w_out, gain):
    t = x.shape[0]
    tm = FFN_TILE

    def row(width):
        return pl.BlockSpec((tm, width), lambda i: (i, 0))

    return pl.pallas_call(
        _mixout_kernel,
        grid=(t // tm,),
        in_specs=[row(D_MODEL), row(RET_V_W), row(NA_W), row(D_MODEL), row(D_MODEL),
                  _const_spec((RET_V_W, D_MODEL)), _const_spec((NA_W, D_MODEL)),
                  _const_spec((D_MODEL, D_MODEL)), _const_spec((1, D_MODEL))],
        out_specs=row(D_MODEL),
        out_shape=jax.ShapeDtypeStruct((t, D_MODEL), F32),
        compiler_params=pltpu.CompilerParams(
            dimension_semantics=("parallel",), vmem_limit_bytes=VMEM_LIMIT),
        name="mix_out",
    )(x, y_ret, y_na, g_ret, g_na, w_ret, w_na, w_out, gain)


def kernel(x, ffn1_pre_norm, ffn1_w_in, ffn1_w_out, ffn1_post_norm, mix_pre_norm, w_mix_in,
           ret_decay_fwd, ret_decay_bwd, na_rel_bias, w_ret_out, w_na_out, w_mix_out, mix_post_norm,
           ffn2_pre_norm, ffn2_w_in, ffn2_w_out, ffn2_post_norm):
    b, s, d = x.shape
    assert d == D_MODEL and s % GRID_W == 0 and s % RET_BLOCK == 0 and (b * s) % TOKEN_TILE == 0
    assert s % TOKEN_TILE == 0
    rows = s // GRID_W
    assert rows == NA_ROWS, "the neighbourhood-attention tiling is laid out for a 32 x 64 token grid"
    t = b * s
    xt = x.reshape(t, d)
    for l in range(ffn1_w_in.shape[0]):
        gain = lambda g: g[l].reshape(1, D_MODEL).astype(F32)
        wt = lambda w: w[l].astype(BF16)
        xt = _ffn(xt, gain(ffn1_pre_norm), wt(ffn1_w_in), wt(ffn1_w_out), gain(ffn1_post_norm))
        rq, rk, rv, rg, nq, nk, nv, g_ret, g_na = _mix_in(xt, gain(mix_pre_norm), wt(w_mix_in), s)
        decay = jnp.stack([ret_decay_fwd[l], ret_decay_bwd[l]]).astype(F32)
        seq = lambda a: a.reshape(b, s, a.shape[-1])
        y_ret = _retention(decay, seq(rq), seq(rk), seq(rv), seq(rg)).reshape(t, RET_V_W)
        grid4 = lambda a: a.reshape(b, rows, GRID_W, NA_W)
        y_na = _natten(grid4(nq), grid4(nk), grid4(nv), _na_bias_tables(na_rel_bias[l]))
        xt = _mix_out(xt, y_ret, y_na.reshape(t, NA_W), g_ret, g_na,
                      wt(w_ret_out), wt(w_na_out), wt(w_mix_out), gain(mix_post_norm))
        xt = _ffn(xt, gain(ffn2_pre_norm), wt(ffn2_w_in), wt(ffn2_w_out), gain(ffn2_post_norm))
    return xt.reshape(b, s, d)
```

```python
import functools

import numpy as np
import jax
import jax.numpy as jnp
from jax import lax
from jax.experimental import pallas as pl
from jax.experimental.pallas import tpu as pltpu

F32 = jnp.float32
BF16 = jnp.bfloat16

D_MODEL = 1024
D_FF = 2816
RMS_EPS = 1e-6
ROPE_BASE = 10000.0
NEG_INF = -1e30
LOG2E = 1.4426950408889634

RET_HEADS = 4
RET_QK_DIM = 128
RET_V_DIM = 256
RET_QK_W = RET_HEADS * RET_QK_DIM
RET_V_W = RET_HEADS * RET_V_DIM
RET_BLOCK = 256

NA_HEADS = 8
NA_HEAD_DIM = 64
NA_W = NA_HEADS * NA_HEAD_DIM
GRID_W = 64
NA_WIN_ROWS = 8
NA_WIN_COLS = 16
NA_REL_ROWS = 2 * NA_WIN_ROWS - 1
NA_REL_COLS = 2 * NA_WIN_COLS - 1
NA_Q_ROWS = 8
NA_Q_COLS = 16
NA_K_ROWS = 16
NA_K_COLS = 32
NA_LANES = 128
NA_SLABS = NA_W // NA_LANES

MIX_SPLITS = (RET_QK_W, RET_QK_W, RET_V_W, RET_V_W, NA_W, NA_W, NA_W, D_MODEL, D_MODEL)
MIX_OFFS = tuple(int(v) for v in np.cumsum((0,) + MIX_SPLITS[:-1]))
MIX_IN_W = sum(MIX_SPLITS)

TOKEN_TILE = 512
FFN_TILE = 1024
SUB_TILE = 256
FF_CHUNK = 256
VMEM_LIMIT = 56 * 1024 * 1024


def _rms(x, gain):
    ms = jnp.mean(x * x, axis=-1, keepdims=True)
    return x * lax.rsqrt(ms + RMS_EPS) * gain


def _silu(x):
    return x * jax.nn.sigmoid(x)


def _const_spec(shape):
    nd = len(shape)
    return pl.BlockSpec(shape, lambda *_: (0,) * nd, pipeline_mode=pl.Buffered(1))


def _ffn_kernel(x_ref, gpre_ref, win_ref, wout_ref, gpost_ref, o_ref, act_ref):
    n_sub = x_ref.shape[0] // SUB_TILE

    def rows(j):
        return slice(j * SUB_TILE, (j + 1) * SUB_TILE)

    def pre(j):
        return _rms(x_ref[rows(j), :], gpre_ref[...]).astype(BF16)

    def up(j, xn):
        for c in range(D_FF // FF_CHUNK):
            lo = c * FF_CHUNK
            g = jnp.dot(xn, win_ref[:, lo:lo + FF_CHUNK], preferred_element_type=F32)
            u = jnp.dot(xn, win_ref[:, D_FF + lo:D_FF + lo + FF_CHUNK], preferred_element_type=F32)
            act_ref[rows(j), lo:lo + FF_CHUNK] = (_silu(g) * u).astype(BF16)

    def down(j):
        h = jnp.dot(act_ref[rows(j), :], wout_ref[...], preferred_element_type=F32)
        o_ref[rows(j), :] = x_ref[rows(j), :] + 0.5 * _rms(h, gpost_ref[...])

    xn = pre(0)
    for j in range(n_sub):
        xn_next = pre(j + 1) if j + 1 < n_sub else None
        up(j, xn)
        if j >= 1:
            down(j - 1)
        xn = xn_next
    down(n_sub - 1)


def _ffn(x, gpre, w_in, w_out, gpost):
    t = x.shape[0]
    tm = FFN_TILE
    row = pl.BlockSpec((tm, D_MODEL), lambda i: (i, 0))
    return pl.pallas_call(
        _ffn_kernel,
        grid=(t // tm,),
        in_specs=[row, _const_spec((1, D_MODEL)), _const_spec((D_MODEL, 2 * D_FF)),
                  _const_spec((D_FF, D_MODEL)), _const_spec((1, D_MODEL))],
        out_specs=row,
        out_shape=jax.ShapeDtypeStruct((t, D_MODEL), F32),
        scratch_shapes=[pltpu.VMEM((tm, D_FF), BF16)],
        compiler_params=pltpu.CompilerParams(
            dimension_semantics=("parallel",), vmem_limit_bytes=VMEM_LIMIT),
        name="ffn",
    )(x, gpre, w_in, w_out, gpost)


def _mixin_kernel(x_ref, gain_ref, w_ref, rope_ref,
                  rq_ref, rk_ref, rv_ref, rg_ref, nq_ref, nk_ref, nv_ref, gr_ref, gn_ref):
    n_sub = x_ref.shape[0] // SUB_TILE

    def pre(j):
        rows = slice(j * SUB_TILE, (j + 1) * SUB_TILE)
        return _rms(x_ref[rows, :], gain_ref[...]).astype(BF16)

    def project(j, u):
        rows = slice(j * SUB_TILE, (j + 1) * SUB_TILE)

        def proj(idx):
            lo = MIX_OFFS[idx]
            return jnp.dot(u, w_ref[:, lo:lo + MIX_SPLITS[idx]], preferred_element_type=F32)

        def store_groups(out_ref, y):
            width = out_ref.shape[-1]
            for g in range(out_ref.shape[0]):
                out_ref[g, rows, :] = y[:, g * width:(g + 1) * width].astype(out_ref.dtype)

        def rotary(y, cos, sin, out_ref):
            for h in range(RET_HEADS):
                yh = y[:, h * RET_QK_DIM:(h + 1) * RET_QK_DIM]
                out_ref[h, rows, :] = (yh * cos + pltpu.roll(yh, RET_QK_DIM // 2, 1) * sin).astype(out_ref.dtype)

        rotary(proj(0), rope_ref[0, rows, :], rope_ref[1, rows, :], rq_ref)
        rotary(proj(1), rope_ref[2, rows, :], rope_ref[3, rows, :], rk_ref)
        store_groups(rv_ref, proj(2))
        store_groups(rg_ref, _silu(proj(3)))
        store_groups(nq_ref, proj(4) * (LOG2E * NA_HEAD_DIM ** -0.5))
        store_groups(nk_ref, proj(5))
        store_groups(nv_ref, proj(6))
        gr_ref[rows, :] = proj(7).astype(gr_ref.dtype)
        gn_ref[rows, :] = proj(8).astype(gn_ref.dtype)

    u = pre(0)
    for j in range(n_sub):
        u_next = pre(j + 1) if j + 1 < n_sub else None
        project(j, u)
        u = u_next


def _rope_tables(seq):
    half = RET_QK_DIM // 2
    pos = jnp.arange(seq, dtype=F32)
    inv = 1.0 / (ROPE_BASE ** jnp.linspace(0.0, 1.0, half, dtype=F32))
    ang = pos[:, None] * inv[None, :]
    cos, sin = jnp.cos(ang), jnp.sin(ang)
    cos2 = jnp.concatenate([cos, cos], axis=-1)
    sin2 = jnp.concatenate([-sin, sin], axis=-1)
    ks = RET_QK_DIM ** -0.5
    return jnp.stack([cos2, sin2, cos2 * ks, sin2 * ks])


def _mix_in(x, gain, w, seq):
    t = x.shape[0]
    tm = TOKEN_TILE
    tiles_per_seq = seq // tm
    out_dtypes = (BF16, BF16, BF16, BF16, BF16, F32, F32, BF16, BF16)
    out_groups = (RET_HEADS,) * 4 + (NA_SLABS,) * 3 + (1, 1)
    return pl.pallas_call(
        _mixin_kernel,
        grid=(t // tm,),
        in_specs=[pl.BlockSpec((tm, D_MODEL), lambda i: (i, 0)),
                  _const_spec((1, D_MODEL)), _const_spec((D_MODEL, MIX_IN_W)),
                  pl.BlockSpec((4, tm, RET_QK_DIM), lambda i: (0, i % tiles_per_seq, 0))],
        out_specs=[pl.BlockSpec((tm, wd), lambda i: (i, 0)) if g == 1
                   else pl.BlockSpec((g, tm, wd // g), lambda i: (0, i, 0))
                   for wd, g in zip(MIX_SPLITS, out_groups)],
        out_shape=[jax.ShapeDtypeStruct((t, wd) if g == 1 else (g, t, wd // g), dt)
                   for wd, g, dt in zip(MIX_SPLITS, out_groups, out_dtypes)],
        compiler_params=pltpu.CompilerParams(
            dimension_semantics=("parallel",), vmem_limit_bytes=VMEM_LIMIT),
        name="mix_in",
    )(x, gain, w, _rope_tables(seq))


def _log_sigmoid(z):
    return jnp.minimum(z, 0.0) - jnp.log1p(jnp.exp(-jnp.abs(z)))


def _ret_kernel(dec_ref, q_ref, k_ref, v_ref, g_ref, o_ref, u_ref, st_ref, dm_ref, vec_ref, blk_ref):
    c = RET_BLOCK
    dk = RET_QK_DIM
    dv = RET_V_DIM
    n_blocks = q_ref.shape[0] // c
    tn = (((0,), (0,)), ((), ()))
    nt = (((1,), (1,)), ((), ()))

    @pl.when(pl.program_id(1) == 0)
    def _decay_tables():
        head = pl.program_id(0)
        zf = dec_ref[0, head]
        zb = dec_ref[1, head]

        def log_gamma(z, shape):
            return _log_sigmoid(jnp.full(shape, z, F32))

        ri = lax.broadcasted_iota(jnp.int32, (c, c), 0).astype(F32)
        ci = lax.broadcasted_iota(jnp.int32, (c, c), 1).astype(F32)
        diff = ri - ci
        dm_ref[...] = jnp.where(diff >= 0, jnp.exp(log_gamma(zf, (c, c)) * jnp.maximum(diff, 0.0)),
                                jnp.exp(log_gamma(zb, (c, c)) * jnp.maximum(-diff, 0.0)))
        r = lax.broadcasted_iota(jnp.int32, (c, dk), 0).astype(F32)
        lgf = log_gamma(zf, (c, dk))
        lgb = log_gamma(zb, (c, dk))
        vec_ref[0] = jnp.exp(lgf * (r + 1.0)).astype(BF16)
        vec_ref[1] = jnp.exp(lgb * (c - r)).astype(BF16)
        vec_ref[2] = jnp.exp(lgf * (c - 1.0 - r)).astype(BF16)
        vec_ref[3] = jnp.exp(lgb * r).astype(BF16)
        blk_ref[0] = jnp.exp(log_gamma(zf, (dk, dv)) * c)
        blk_ref[1] = jnp.exp(log_gamma(zb, (dk, dv)) * c)

    def rows(n):
        return slice(n * c, (n + 1) * c)

    for n in range(n_blocks):
        k = k_ref[rows(n), :]
        k_cat = jnp.concatenate([k * vec_ref[2], k * vec_ref[3]], axis=1)
        u_ref[n] = lax.dot_general(k_cat, v_ref[rows(n), :], tn, preferred_element_type=F32)

    sf = jnp.zeros((dk, dv), F32)
    for n in range(n_blocks):
        st_ref[n, 0:dk, :] = sf.astype(BF16)
        if n + 1 < n_blocks:
            sf = blk_ref[0] * sf + u_ref[n, 0:dk, :]
    sb = jnp.zeros((dk, dv), F32)
    for n in reversed(range(n_blocks)):
        st_ref[n, dk:2 * dk, :] = sb.astype(BF16)
        if n > 0:
            sb = blk_ref[1] * sb + u_ref[n, dk:2 * dk, :]

    def scores(n):
        return lax.dot_general(q_ref[rows(n), :], k_ref[rows(n), :], nt, preferred_element_type=F32)

    s_next = scores(0)
    for n in range(n_blocks):
        s_cur = s_next
        if n + 1 < n_blocks:
            s_next = scores(n + 1)
        p = (s_cur * dm_ref[...]).astype(BF16)
        q = q_ref[rows(n), :]
        lhs = jnp.concatenate([p, q * vec_ref[0], q * vec_ref[1]], axis=1)
        rhs = jnp.concatenate([v_ref[rows(n), :], st_ref[n]], axis=0)
        y = jnp.dot(lhs, rhs, preferred_element_type=F32)
        yn = y * lax.rsqrt(jnp.mean(y * y, axis=-1, keepdims=True) + RMS_EPS)
        o_ref[rows(n), :] = (g_ref[rows(n), :].astype(F32) * yn).astype(o_ref.dtype)


def _retention(decay_logits, rq, rk, rv, rg):
    _, b, s, _ = rq.shape
    c = RET_BLOCK
    n_blocks = s // c
    qk_spec = pl.BlockSpec((None, None, s, RET_QK_DIM), lambda h, i, dec: (h, i, 0, 0))
    v_spec = pl.BlockSpec((None, None, s, RET_V_DIM), lambda h, i, dec: (h, i, 0, 0))
    return pl.pallas_call(
        _ret_kernel,
        grid_spec=pltpu.PrefetchScalarGridSpec(
            num_scalar_prefetch=1,
            grid=(RET_HEADS, b),
            in_specs=[qk_spec, qk_spec, v_spec, v_spec],
            out_specs=v_spec,
            scratch_shapes=[pltpu.VMEM((n_blocks, 2 * RET_QK_DIM, RET_V_DIM), F32),
                            pltpu.VMEM((n_blocks, 2 * RET_QK_DIM, RET_V_DIM), BF16),
                            pltpu.VMEM((c, c), F32),
                            pltpu.VMEM((4, c, RET_QK_DIM), BF16),
                            pltpu.VMEM((2, RET_QK_DIM, RET_V_DIM), F32)],
        ),
        out_shape=jax.ShapeDtypeStruct((RET_HEADS, b, s, RET_V_DIM), BF16),
        compiler_params=pltpu.CompilerParams(
            dimension_semantics=("arbitrary", "arbitrary"), vmem_limit_bytes=VMEM_LIMIT),
        name="retention",
    )(decay_logits, rq, rk, rv, rg)


def _na_blocks(extent, q_size, k_size, window):
    n = extent // q_size
    out = []
    for i in range(n):
        q0 = i * q_size
        k0 = min(max(q0 - window // 2, 0), extent - k_size)
        out.append((q0, k0, 0 if i == 0 else (2 if i == n - 1 else 1)))
    return tuple(out)


NA_ROWS = 32
_NA_ROW_BLOCKS = _na_blocks(NA_ROWS, NA_Q_ROWS, NA_K_ROWS, NA_WIN_ROWS)
_NA_COL_BLOCKS = _na_blocks(GRID_W, NA_Q_COLS, NA_K_COLS, NA_WIN_COLS)
_NA_CONFIGS = 9


def _na_bias_kernel(rel_ref, out_ref, col_ref):
    nk = NA_K_ROWS * NA_K_COLS
    base = pl.program_id(0) * (NA_REL_ROWS * NA_REL_COLS)
    lane = lax.broadcasted_iota(jnp.int32, (NA_Q_COLS, nk), 1)
    qc = lax.broadcasted_iota(jnp.int32, (NA_Q_COLS, nk), 0)
    kc = lane % NA_K_COLS
    kr = lane // NA_K_COLS
    col_cfgs = (_NA_COL_BLOCKS[0], _NA_COL_BLOCKS[1], _NA_COL_BLOCKS[3])
    row_cfgs = (_NA_ROW_BLOCKS[0], _NA_ROW_BLOCKS[1], _NA_ROW_BLOCKS[3])
    for b, (c0, kc0, _) in enumerate(col_cfgs):
        kc_abs = kc + kc0
        qc_abs = qc + c0
        rel_col = jnp.clip(kc_abs - qc_abs + NA_WIN_COLS - 1, 0, NA_REL_COLS - 1)
        win = jnp.clip(qc_abs - NA_WIN_COLS // 2, 0, GRID_W - NA_WIN_COLS)
        in_win = (kc_abs >= win) & (kc_abs < win + NA_WIN_COLS)

        def rel_row_body(dr, carry, rel_col=rel_col, in_win=in_win, b=b):
            vals = jnp.zeros((NA_Q_COLS, nk), F32)
            for dc in range(NA_REL_COLS):
                vals = jnp.where(rel_col == dc, rel_ref[base + dr * NA_REL_COLS + dc], vals)
            col_ref[b * NA_REL_ROWS + dr] = jnp.where(in_win, vals * LOG2E, NEG_INF)
            return carry

        lax.fori_loop(0, NA_REL_ROWS, rel_row_body, 0)

    for a, (r0, ks, _) in enumerate(row_cfgs):
        kr_abs = kr + ks
        for qr in range(NA_Q_ROWS):
            r_abs = r0 + qr
            rs = min(max(r_abs - NA_WIN_ROWS // 2, 0), NA_ROWS - NA_WIN_ROWS)
            hits = [kr_abs == rs + j for j in range(NA_WIN_ROWS)]
            for b in range(3):
                tile = jnp.full((NA_Q_COLS, nk), NEG_INF, F32)
                for j in range(NA_WIN_ROWS):
                    dr = rs + j - r_abs + NA_WIN_ROWS - 1
                    tile = jnp.where(hits[j], col_ref[b * NA_REL_ROWS + dr], tile)
                out_ref[0, a * 3 + b, qr * NA_Q_COLS:(qr + 1) * NA_Q_COLS, :] = tile


def _na_bias_tables(rel_bias):
    nq = NA_Q_ROWS * NA_Q_COLS
    nk = NA_K_ROWS * NA_K_COLS
    return pl.pallas_call(
        _na_bias_kernel,
        grid_spec=pltpu.PrefetchScalarGridSpec(
            num_scalar_prefetch=1,
            grid=(NA_HEADS,),
            in_specs=[],
            out_specs=pl.BlockSpec((1, _NA_CONFIGS, nq, nk), lambda h, rel: (h, 0, 0, 0)),
            scratch_shapes=[pltpu.VMEM((3 * NA_REL_ROWS, NA_Q_COLS, nk), F32)],
        ),
        out_shape=jax.ShapeDtypeStruct((NA_HEADS, _NA_CONFIGS, nq, nk), F32),
        compiler_params=pltpu.CompilerParams(dimension_semantics=("parallel",)),
        name="na_bias",
    )(rel_bias.astype(F32).reshape(-1))


def _na_kernel(q_ref, k_ref, v_ref, bias_ref, o_ref, kb_ref, vb_ref):
    nq = NA_Q_ROWS * NA_Q_COLS
    nk = NA_K_ROWS * NA_K_COLS
    shift = NA_WIN_COLS // 2
    kept = GRID_W - 2 * shift
    for src, dst in ((k_ref, kb_ref), (v_ref, vb_ref)):
        dst[0] = src[...].astype(BF16)
        dst[1, :, 0:kept, :] = src[:, shift:shift + kept, :].astype(BF16)

    def window(ref, block):
        (_, ks, _), (_, kc0, _) = block
        which, c = (0, kc0) if kc0 % (2 * shift) == 0 else (1, kc0 - shift)
        assert c % (2 * shift) == 0 and (which == 0 or c + NA_K_COLS <= kept)
        return ref[which, ks:ks + NA_K_ROWS, c:c + NA_K_COLS, :].reshape(nk, NA_LANES)

    first = lax.broadcasted_iota(jnp.int32, (nq, NA_LANES), 1) < NA_HEAD_DIM
    ones = jnp.ones((nk, NA_LANES), BF16)
    nt = (((1,), (1,)), ((), ()))
    blocks = [(rb, cb) for rb in _NA_ROW_BLOCKS for cb in _NA_COL_BLOCKS]

    def scores(block):
        (r0, _, rcfg), (c0, _, ccfg) = block
        q = q_ref[r0:r0 + NA_Q_ROWS, c0:c0 + NA_Q_COLS, :].reshape(nq, NA_LANES)
        zero = jnp.zeros_like(q)
        q2 = jnp.concatenate([jnp.where(first, q, zero), jnp.where(first, zero, q)], axis=0)
        cfg = rcfg * 3 + ccfg
        bias = jnp.concatenate([bias_ref[0, cfg], bias_ref[1, cfg]], axis=0)
        return lax.dot_general(q2, window(kb_ref, block), nt, preferred_element_type=F32) + bias

    def attend(block, s):
        m = jnp.max(s, axis=-1, keepdims=True)
        p = jnp.exp2(s - m).astype(BF16)
        o = jnp.dot(p, jnp.concatenate([window(vb_ref, block), ones], axis=1),
                    preferred_element_type=F32)
        o = o[:, :NA_LANES] / o[:, NA_LANES:]
        return jnp.where(first, o[:nq], o[nq:])

    s_next = scores(blocks[0])
    for i, blk in enumerate(blocks):
        s_cur = s_next
        if i + 1 < len(blocks):
            s_next = scores(blocks[i + 1])
        (r0, _, _), (c0, _, _) = blk
        o_ref[r0:r0 + NA_Q_ROWS, c0:c0 + NA_Q_COLS, :] = (
            attend(blk, s_cur).reshape(NA_Q_ROWS, NA_Q_COLS, NA_LANES).astype(o_ref.dtype))


def _natten(nq, nk, nv, bias_tabs):
    b, rows = nq.shape[1], nq.shape[2]
    slab = pl.BlockSpec((None, None, rows, GRID_W, NA_LANES), lambda j, i: (j, i, 0, 0, 0))
    nqk = NA_Q_ROWS * NA_Q_COLS
    nkk = NA_K_ROWS * NA_K_COLS
    return pl.pallas_call(
        _na_kernel,
        grid=(NA_SLABS, b),
        in_specs=[slab, slab, slab,
                  pl.BlockSpec((2, _NA_CONFIGS, nqk, nkk), lambda j, i: (j, 0, 0, 0))],
        out_specs=slab,
        out_shape=jax.ShapeDtypeStruct((NA_SLABS, b, rows, GRID_W, NA_LANES), BF16),
        scratch_shapes=[pltpu.VMEM((2, rows, GRID_W, NA_LANES), BF16),
                        pltpu.VMEM((2, rows, GRID_W, NA_LANES), BF16)],
        compiler_params=pltpu.CompilerParams(
            dimension_semantics=("parallel", "parallel"), vmem_limit_bytes=VMEM_LIMIT),
        name="natten",
    )(nq, nk, nv, bias_tabs)


def _mixout_kernel(x_ref, yr_ref, na_ref, gr_ref, gn_ref, wr_ref, wn_ref, wo_ref, gain_ref, o_ref):
    n_sub = x_ref.shape[0] // SUB_TILE

    def rows(j):
        return slice(j * SUB_TILE, (j + 1) * SUB_TILE)

    def branches(j):
        def gathered(ref):
            return jnp.concatenate([ref[g, rows(j), :] for g in range(ref.shape[0])], axis=1)

        y_ret = jnp.dot(gathered(yr_ref), wr_ref[...], preferred_element_type=F32)
        y_na = jnp.dot(gathered(na_ref), wn_ref[...], preferred_element_type=F32)
        merged = (jax.nn.sigmoid(gr_ref[rows(j), :].astype(F32)) * y_ret
                  + jax.nn.sigmoid(gn_ref[rows(j), :].astype(F32)) * y_na)
        return merged.astype(BF16)

    def finish(j, merged):
        m = jnp.dot(merged, wo_ref[...], preferred_element_type=F32)
        o_ref[rows(j), :] = x_ref[rows(j), :] + _rms(m, gain_ref[...])

    merged = branches(0)
    for j in range(n_sub):
        merged_next = branches(j + 1) if j + 1 < n_sub else None
        finish(j, merged)
        merged = merged_next


def _mix_out(x, y_ret, y_na, g_ret, g_na, w_ret, w_na, w_out, gain):
    t = x.shape[0]
    tm = FFN_TILE

    def row(width):
        return pl.BlockSpec((tm, width), lambda i: (i, 0))

    return pl.pallas_call(
        _mixout_kernel,
        grid=(t // tm,),
        in_specs=[row(D_MODEL),
                  pl.BlockSpec((RET_HEADS, tm, RET_V_DIM), lambda i: (0, i, 0)),
                  pl.BlockSpec((NA_SLABS, tm, NA_LANES), lambda i: (0, i, 0)),
                  row(D_MODEL), row(D_MODEL),
                  _const_spec((RET_V_W, D_MODEL)), _const_spec((NA_W, D_MODEL)),
                  _const_spec((D_MODEL, D_MODEL)), _const_spec((1, D_MODEL))],
        out_specs=row(D_MODEL),
        out_shape=jax.ShapeDtypeStruct((t, D_MODEL), F32),
        compiler_params=pltpu.CompilerParams(
            dimension_semantics=("parallel",), vmem_limit_bytes=VMEM_LIMIT),
        name="mix_out",
    )(x, y_ret, y_na, g_ret, g_na, w_ret, w_na, w_out, gain)


def kernel(x, ffn1_pre_norm, ffn1_w_in, ffn1_w_out, ffn1_post_norm, mix_pre_norm, w_mix_in,
           ret_decay_fwd, ret_decay_bwd, na_rel_bias, w_ret_out, w_na_out, w_mix_out, mix_post_norm,
           ffn2_pre_norm, ffn2_w_in, ffn2_w_out, ffn2_post_norm):
    b, s, d = x.shape
    assert d == D_MODEL and s % GRID_W == 0 and s % RET_BLOCK == 0 and (b * s) % TOKEN_TILE == 0
    assert s % TOKEN_TILE == 0
    rows = s // GRID_W
    assert rows == NA_ROWS, "the neighbourhood-attention tiling is laid out for a 32 x 64 token grid"
    t = b * s
    xt = x.reshape(t, d)
    for l in range(ffn1_w_in.shape[0]):
        gain = lambda g: g[l].reshape(1, D_MODEL).astype(F32)
        wt = lambda w: w[l].astype(BF16)
        xt = _ffn(xt, gain(ffn1_pre_norm), wt(ffn1_w_in), wt(ffn1_w_out), gain(ffn1_post_norm))
        rq, rk, rv, rg, nq, nk, nv, g_ret, g_na = _mix_in(xt, gain(mix_pre_norm), wt(w_mix_in), s)
        decay = jnp.stack([ret_decay_fwd[l], ret_decay_bwd[l]]).astype(F32)
        seq = lambda a: a.reshape(a.shape[0], b, s, a.shape[-1])
        y_ret = _retention(decay, seq(rq), seq(rk), seq(rv), seq(rg)).reshape(RET_HEADS, t, RET_V_DIM)
        grid5 = lambda a: a.reshape(NA_SLABS, b, rows, GRID_W, NA_LANES)
        y_na = _natten(grid5(nq), grid5(nk), grid5(nv), _na_bias_tables(na_rel_bias[l]))
        xt = _mix_out(xt, y_ret, y_na.reshape(NA_SLABS, t, NA_LANES), g_ret, g_na,
                      wt(w_ret_out), wt(w_na_out), wt(w_mix_out), gain(mix_post_norm))
        xt = _ffn(xt, gain(ffn2_pre_norm), wt(ffn2_w_in), wt(ffn2_w_out), gain(ffn2_post_norm))
    return xt.reshape(b, s, d)
```

```python
import functools

import numpy as np
import jax
import jax.numpy as jnp
from jax import lax
from jax.experimental import pallas as pl
from jax.experimental.pallas import tpu as pltpu

F32 = jnp.float32
BF16 = jnp.bfloat16

D_MODEL = 1024
D_FF = 2816
RMS_EPS = 1e-6
ROPE_BASE = 10000.0
NEG_INF = -1e30
LOG2E = 1.4426950408889634

RET_HEADS = 4
RET_QK_DIM = 128
RET_V_DIM = 256
RET_QK_W = RET_HEADS * RET_QK_DIM
RET_V_W = RET_HEADS * RET_V_DIM
RET_BLOCK = 256

NA_HEADS = 8
NA_HEAD_DIM = 64
NA_W = NA_HEADS * NA_HEAD_DIM
GRID_W = 64
NA_WIN_ROWS = 8
NA_WIN_COLS = 16
NA_REL_ROWS = 2 * NA_WIN_ROWS - 1
NA_REL_COLS = 2 * NA_WIN_COLS - 1
NA_Q_ROWS = 8
NA_Q_COLS = 16
NA_K_ROWS = 16
NA_K_COLS = 32
NA_LANES = 128
NA_SLABS = NA_W // NA_LANES

MIX_SPLITS = (RET_QK_W, RET_QK_W, RET_V_W, RET_V_W, NA_W, NA_W, NA_W, D_MODEL, D_MODEL)
MIX_OFFS = tuple(int(v) for v in np.cumsum((0,) + MIX_SPLITS[:-1]))
MIX_IN_W = sum(MIX_SPLITS)

TOKEN_TILE = 512
FFN_TILE = 1024
SUB_TILE = 256
FF_CHUNK = 256
VMEM_LIMIT = 56 * 1024 * 1024


def _rms(x, gain):
    ms = jnp.mean(x * x, axis=-1, keepdims=True)
    return x * lax.rsqrt(ms + RMS_EPS) * gain


def _silu(x):
    return x * jax.nn.sigmoid(x)


def _const_spec(shape):
    nd = len(shape)
    return pl.BlockSpec(shape, lambda *_: (0,) * nd, pipeline_mode=pl.Buffered(1))


def _ffn_kernel(x_ref, gpre_ref, win_ref, wout_ref, gpost_ref, o_ref, act_ref):
    n_sub = x_ref.shape[0] // SUB_TILE

    def rows(j):
        return slice(j * SUB_TILE, (j + 1) * SUB_TILE)

    def pre(j):
        return _rms(x_ref[rows(j), :], gpre_ref[...]).astype(BF16)

    def up(j, xn):
        for c in range(D_FF // FF_CHUNK):
            lo = c * FF_CHUNK
            g = jnp.dot(xn, win_ref[:, lo:lo + FF_CHUNK], preferred_element_type=F32)
            u = jnp.dot(xn, win_ref[:, D_FF + lo:D_FF + lo + FF_CHUNK], preferred_element_type=F32)
            act_ref[rows(j), lo:lo + FF_CHUNK] = (_silu(g) * u).astype(BF16)

    def down(j):
        h = jnp.dot(act_ref[rows(j), :], wout_ref[...], preferred_element_type=F32)
        o_ref[rows(j), :] = x_ref[rows(j), :] + 0.5 * _rms(h, gpost_ref[...])

    xn = pre(0)
    for j in range(n_sub):
        xn_next = pre(j + 1) if j + 1 < n_sub else None
        up(j, xn)
        if j >= 1:
            down(j - 1)
        xn = xn_next
    down(n_sub - 1)


def _ffn(x, gpre, w_in, w_out, gpost):
    t = x.shape[0]
    tm = FFN_TILE
    row = pl.BlockSpec((tm, D_MODEL), lambda i: (i, 0))
    return pl.pallas_call(
        _ffn_kernel,
        grid=(t // tm,),
        in_specs=[row, _const_spec((1, D_MODEL)), _const_spec((D_MODEL, 2 * D_FF)),
                  _const_spec((D_FF, D_MODEL)), _const_spec((1, D_MODEL))],
        out_specs=row,
        out_shape=jax.ShapeDtypeStruct((t, D_MODEL), F32),
        scratch_shapes=[pltpu.VMEM((tm, D_FF), BF16)],
        compiler_params=pltpu.CompilerParams(
            dimension_semantics=("parallel",), vmem_limit_bytes=VMEM_LIMIT),
        name="ffn",
    )(x, gpre, w_in, w_out, gpost)


def _mixin_kernel(x_ref, gain_ref, w_ref, rope_ref,
                  rq_ref, rk_ref, rv_ref, rg_ref, nq_ref, nk_ref, nv_ref, gr_ref, gn_ref):
    n_sub = x_ref.shape[0] // SUB_TILE

    def pre(j):
        rows = slice(j * SUB_TILE, (j + 1) * SUB_TILE)
        return _rms(x_ref[rows, :], gain_ref[...]).astype(BF16)

    def project(j, u):
        rows = slice(j * SUB_TILE, (j + 1) * SUB_TILE)

        def proj(idx):
            lo = MIX_OFFS[idx]
            return jnp.dot(u, w_ref[:, lo:lo + MIX_SPLITS[idx]], preferred_element_type=F32)

        def store_groups(out_ref, y):
            width = out_ref.shape[-1]
            for g in range(out_ref.shape[0]):
                out_ref[g, rows, :] = y[:, g * width:(g + 1) * width].astype(out_ref.dtype)

        def rotary(y, cos, sin, out_ref):
            for h in range(RET_HEADS):
                yh = y[:, h * RET_QK_DIM:(h + 1) * RET_QK_DIM]
                out_ref[h, rows, :] = (yh * cos + pltpu.roll(yh, RET_QK_DIM // 2, 1) * sin).astype(out_ref.dtype)

        rotary(proj(0), rope_ref[0, rows, :], rope_ref[1, rows, :], rq_ref)
        rotary(proj(1), rope_ref[2, rows, :], rope_ref[3, rows, :], rk_ref)
        store_groups(rv_ref, proj(2))
        store_groups(rg_ref, _silu(proj(3)))
        store_groups(nq_ref, proj(4) * (LOG2E * NA_HEAD_DIM ** -0.5))
        store_groups(nk_ref, proj(5))
        store_groups(nv_ref, proj(6))
        gr_ref[rows, :] = proj(7).astype(gr_ref.dtype)
        gn_ref[rows, :] = proj(8).astype(gn_ref.dtype)

    u = pre(0)
    for j in range(n_sub):
        u_next = pre(j + 1) if j + 1 < n_sub else None
        project(j, u)
        u = u_next


def _rope_tables(seq):
    half = RET_QK_DIM // 2
    pos = jnp.arange(seq, dtype=F32)
    inv = 1.0 / (ROPE_BASE ** jnp.linspace(0.0, 1.0, half, dtype=F32))
    ang = pos[:, None] * inv[None, :]
    cos, sin = jnp.cos(ang), jnp.sin(ang)
    cos2 = jnp.concatenate([cos, cos], axis=-1)
    sin2 = jnp.concatenate([-sin, sin], axis=-1)
    ks = RET_QK_DIM ** -0.5
    return jnp.stack([cos2, sin2, cos2 * ks, sin2 * ks])


def _mix_in(x, gain, w, seq):
    t = x.shape[0]
    tm = TOKEN_TILE
    tiles_per_seq = seq // tm
    out_dtypes = (BF16, BF16, BF16, BF16, BF16, F32, F32, BF16, BF16)
    out_groups = (RET_HEADS,) * 4 + (NA_SLABS,) * 3 + (1, 1)
    return pl.pallas_call(
        _mixin_kernel,
        grid=(t // tm,),
        in_specs=[pl.BlockSpec((tm, D_MODEL), lambda i: (i, 0)),
                  _const_spec((1, D_MODEL)), _const_spec((D_MODEL, MIX_IN_W)),
                  pl.BlockSpec((4, tm, RET_QK_DIM), lambda i: (0, i % tiles_per_seq, 0))],
        out_specs=[pl.BlockSpec((tm, wd), lambda i: (i, 0)) if g == 1
                   else pl.BlockSpec((g, tm, wd // g), lambda i: (0, i, 0))
                   for wd, g in zip(MIX_SPLITS, out_groups)],
        out_shape=[jax.ShapeDtypeStruct((t, wd) if g == 1 else (g, t, wd // g), dt)
                   for wd, g, dt in zip(MIX_SPLITS, out_groups, out_dtypes)],
        compiler_params=pltpu.CompilerParams(
            dimension_semantics=("parallel",), vmem_limit_bytes=VMEM_LIMIT),
        name="mix_in",
    )(x, gain, w, _rope_tables(seq))


def _log_sigmoid(z):
    return jnp.minimum(z, 0.0) - jnp.log1p(jnp.exp(-jnp.abs(z)))


def _ret_kernel(dec_ref, q_ref, k_ref, v_ref, g_ref, o_ref, u_ref, st_ref, dm_ref, vec_ref, blk_ref):
    c = RET_BLOCK
    dk = RET_QK_DIM
    dv = RET_V_DIM
    n_blocks = q_ref.shape[0] // c
    tn = (((0,), (0,)), ((), ()))
    nt = (((1,), (1,)), ((), ()))

    @pl.when(pl.program_id(1) == 0)
    def _decay_tables():
        head = pl.program_id(0)
        zf = dec_ref[0, head]
        zb = dec_ref[1, head]

        def log_gamma(z, shape):
            return _log_sigmoid(jnp.full(shape, z, F32))

        ri = lax.broadcasted_iota(jnp.int32, (c, c), 0).astype(F32)
        ci = lax.broadcasted_iota(jnp.int32, (c, c), 1).astype(F32)
        diff = ri - ci
        dm_ref[...] = jnp.where(diff >= 0, jnp.exp(log_gamma(zf, (c, c)) * jnp.maximum(diff, 0.0)),
                                jnp.exp(log_gamma(zb, (c, c)) * jnp.maximum(-diff, 0.0)))
        r = lax.broadcasted_iota(jnp.int32, (c, dk), 0).astype(F32)
        lgf = log_gamma(zf, (c, dk))
        lgb = log_gamma(zb, (c, dk))
        vec_ref[0] = jnp.exp(lgf * (r + 1.0)).astype(BF16)
        vec_ref[1] = jnp.exp(lgb * (c - r)).astype(BF16)
        vec_ref[2] = jnp.exp(lgf * (c - 1.0 - r)).astype(BF16)
        vec_ref[3] = jnp.exp(lgb * r).astype(BF16)
        blk_ref[0] = jnp.exp(log_gamma(zf, (dk, dv)) * c)
        blk_ref[1] = jnp.exp(log_gamma(zb, (dk, dv)) * c)

    def rows(n):
        return slice(n * c, (n + 1) * c)

    for n in range(n_blocks):
        k = k_ref[rows(n), :]
        k_cat = jnp.concatenate([k * vec_ref[2], k * vec_ref[3]], axis=1)
        u_ref[n] = lax.dot_general(k_cat, v_ref[rows(n), :], tn, preferred_element_type=F32)

    sf = jnp.zeros((dk, dv), F32)
    for n in range(n_blocks):
        st_ref[n, 0:dk, :] = sf.astype(BF16)
        if n + 1 < n_blocks:
            sf = blk_ref[0] * sf + u_ref[n, 0:dk, :]
    sb = jnp.zeros((dk, dv), F32)
    for n in reversed(range(n_blocks)):
        st_ref[n, dk:2 * dk, :] = sb.astype(BF16)
        if n > 0:
            sb = blk_ref[1] * sb + u_ref[n, dk:2 * dk, :]

    def scores(n):
        return lax.dot_general(q_ref[rows(n), :], k_ref[rows(n), :], nt, preferred_element_type=F32)

    s_next = scores(0)
    for n in range(n_blocks):
        s_cur = s_next
        if n + 1 < n_blocks:
            s_next = scores(n + 1)
        p = (s_cur * dm_ref[...]).astype(BF16)
        q = q_ref[rows(n), :]
        lhs = jnp.concatenate([p, q * vec_ref[0], q * vec_ref[1]], axis=1)
        rhs = jnp.concatenate([v_ref[rows(n), :], st_ref[n]], axis=0)
        y = jnp.dot(lhs, rhs, preferred_element_type=F32)
        yn = y * lax.rsqrt(jnp.mean(y * y, axis=-1, keepdims=True) + RMS_EPS)
        o_ref[rows(n), :] = (g_ref[rows(n), :].astype(F32) * yn).astype(o_ref.dtype)


def _retention(decay_logits, rq, rk, rv, rg):
    _, b, s, _ = rq.shape
    c = RET_BLOCK
    n_blocks = s // c
    qk_spec = pl.BlockSpec((None, None, s, RET_QK_DIM), lambda h, i, dec: (h, i, 0, 0))
    v_spec = pl.BlockSpec((None, None, s, RET_V_DIM), lambda h, i, dec: (h, i, 0, 0))
    return pl.pallas_call(
        _ret_kernel,
        grid_spec=pltpu.PrefetchScalarGridSpec(
            num_scalar_prefetch=1,
            grid=(RET_HEADS, b),
            in_specs=[qk_spec, qk_spec, v_spec, v_spec],
            out_specs=v_spec,
            scratch_shapes=[pltpu.VMEM((n_blocks, 2 * RET_QK_DIM, RET_V_DIM), F32),
                            pltpu.VMEM((n_blocks, 2 * RET_QK_DIM, RET_V_DIM), BF16),
                            pltpu.VMEM((c, c), F32),
                            pltpu.VMEM((4, c, RET_QK_DIM), BF16),
                            pltpu.VMEM((2, RET_QK_DIM, RET_V_DIM), F32)],
        ),
        out_shape=jax.ShapeDtypeStruct((RET_HEADS, b, s, RET_V_DIM), BF16),
        compiler_params=pltpu.CompilerParams(
            dimension_semantics=("arbitrary", "arbitrary"), vmem_limit_bytes=VMEM_LIMIT),
        name="retention",
    )(decay_logits, rq, rk, rv, rg)


def _na_blocks(extent, q_size, k_size, window):
    n = extent // q_size
    out = []
    for i in range(n):
        q0 = i * q_size
        k0 = min(max(q0 - window // 2, 0), extent - k_size)
        out.append((q0, k0, 0 if i == 0 else (2 if i == n - 1 else 1)))
    return tuple(out)


NA_ROWS = 32
_NA_ROW_BLOCKS = _na_blocks(NA_ROWS, NA_Q_ROWS, NA_K_ROWS, NA_WIN_ROWS)
_NA_COL_BLOCKS = _na_blocks(GRID_W, NA_Q_COLS, NA_K_COLS, NA_WIN_COLS)
_NA_CONFIGS = 9


def _na_bias_kernel(rel_ref, out_ref, col_ref):
    nk = NA_K_ROWS * NA_K_COLS
    base = pl.program_id(0) * (NA_REL_ROWS * NA_REL_COLS)
    lane = lax.broadcasted_iota(jnp.int32, (NA_Q_COLS, nk), 1)
    kr = lane // NA_K_COLS
    col_cfgs = (_NA_COL_BLOCKS[0], _NA_COL_BLOCKS[1], _NA_COL_BLOCKS[3])
    row_cfgs = (_NA_ROW_BLOCKS[0], _NA_ROW_BLOCKS[1], _NA_ROW_BLOCKS[3])
    reps = nk // NA_LANES
    kc_one = lax.broadcasted_iota(jnp.int32, (NA_Q_COLS, NA_LANES), 1) % NA_K_COLS
    qc_one = lax.broadcasted_iota(jnp.int32, (NA_Q_COLS, NA_LANES), 0)
    for b, (c0, kc0, _) in enumerate(col_cfgs):
        kc_abs = kc_one + kc0
        qc_abs = qc_one + c0
        rel_col = jnp.clip(kc_abs - qc_abs + NA_WIN_COLS - 1, 0, NA_REL_COLS - 1)
        win = jnp.clip(qc_abs - NA_WIN_COLS // 2, 0, GRID_W - NA_WIN_COLS)
        in_win = (kc_abs >= win) & (kc_abs < win + NA_WIN_COLS)

        def rel_row_body(dr, carry, rel_col=rel_col, in_win=in_win, b=b):
            vals = jnp.zeros((NA_Q_COLS, NA_LANES), F32)
            for dc in range(NA_REL_COLS):
                vals = jnp.where(rel_col == dc, rel_ref[base + dr * NA_REL_COLS + dc], vals)
            vals = jnp.where(in_win, vals * LOG2E, NEG_INF)
            col_ref[b * NA_REL_ROWS + dr] = jnp.concatenate([vals] * reps, axis=1)
            return carry

        lax.fori_loop(0, NA_REL_ROWS, rel_row_body, 0)

    for a, (r0, ks, _) in enumerate(row_cfgs):
        kr_abs = kr + ks
        for qr in range(NA_Q_ROWS):
            r_abs = r0 + qr
            rs = min(max(r_abs - NA_WIN_ROWS // 2, 0), NA_ROWS - NA_WIN_ROWS)
            hits = [kr_abs == rs + j for j in range(NA_WIN_ROWS)]
            for b in range(3):
                tile = jnp.full((NA_Q_COLS, nk), NEG_INF, F32)
                for j in range(NA_WIN_ROWS):
                    dr = rs + j - r_abs + NA_WIN_ROWS - 1
                    tile = jnp.where(hits[j], col_ref[b * NA_REL_ROWS + dr], tile)
                out_ref[0, a * 3 + b, qr * NA_Q_COLS:(qr + 1) * NA_Q_COLS, :] = tile


def _na_bias_tables(rel_bias):
    nq = NA_Q_ROWS * NA_Q_COLS
    nk = NA_K_ROWS * NA_K_COLS
    return pl.pallas_call(
        _na_bias_kernel,
        grid_spec=pltpu.PrefetchScalarGridSpec(
            num_scalar_prefetch=1,
            grid=(NA_HEADS,),
            in_specs=[],
            out_specs=pl.BlockSpec((1, _NA_CONFIGS, nq, nk), lambda h, rel: (h, 0, 0, 0)),
            scratch_shapes=[pltpu.VMEM((3 * NA_REL_ROWS, NA_Q_COLS, nk), F32)],
        ),
        out_shape=jax.ShapeDtypeStruct((NA_HEADS, _NA_CONFIGS, nq, nk), F32),
        compiler_params=pltpu.CompilerParams(dimension_semantics=("parallel",)),
        name="na_bias",
    )(rel_bias.astype(F32).reshape(-1))


def _na_kernel(q_ref, k_ref, v_ref, bias_ref, o_ref, kb_ref, vb_ref):
    nq = NA_Q_ROWS * NA_Q_COLS
    nk = NA_K_ROWS * NA_K_COLS
    shift = NA_WIN_COLS // 2
    kept = GRID_W - 2 * shift
    for src, dst in ((k_ref, kb_ref), (v_ref, vb_ref)):
        dst[0] = src[...].astype(BF16)
        dst[1, :, 0:kept, :] = src[:, shift:shift + kept, :].astype(BF16)

    def window(ref, block):
        (_, ks, _), (_, kc0, _) = block
        which, c = (0, kc0) if kc0 % (2 * shift) == 0 else (1, kc0 - shift)
        assert c % (2 * shift) == 0 and (which == 0 or c + NA_K_COLS <= kept)
        return ref[which, ks:ks + NA_K_ROWS, c:c + NA_K_COLS, :].reshape(nk, NA_LANES)

    first = lax.broadcasted_iota(jnp.int32, (nq, NA_LANES), 1) < NA_HEAD_DIM
    ones = jnp.ones((nk, NA_LANES), BF16)
    nt = (((1,), (1,)), ((), ()))
    blocks = [(rb, cb) for rb in _NA_ROW_BLOCKS for cb in _NA_COL_BLOCKS]

    def scores(block):
        (r0, _, rcfg), (c0, _, ccfg) = block
        q = q_ref[r0:r0 + NA_Q_ROWS, c0:c0 + NA_Q_COLS, :].reshape(nq, NA_LANES)
        zero = jnp.zeros_like(q)
        q2 = jnp.concatenate([jnp.where(first, q, zero), jnp.where(first, zero, q)], axis=0)
        cfg = rcfg * 3 + ccfg
        bias = jnp.concatenate([bias_ref[0, cfg], bias_ref[1, cfg]], axis=0)
        return lax.dot_general(q2, window(kb_ref, block), nt, preferred_element_type=F32) + bias

    def attend(block, s):
        m = jnp.max(s, axis=-1, keepdims=True)
        p = jnp.exp2(s - m).astype(BF16)
        o = jnp.dot(p, jnp.concatenate([window(vb_ref, block), ones], axis=1),
                    preferred_element_type=F32)
        o = o[:, :NA_LANES] / o[:, NA_LANES:]
        return jnp.where(first, o[:nq], o[nq:])

    s_next = scores(blocks[0])
    for i, blk in enumerate(blocks):
        s_cur = s_next
        if i + 1 < len(blocks):
            s_next = scores(blocks[i + 1])
        (r0, _, _), (c0, _, _) = blk
        o_ref[r0:r0 + NA_Q_ROWS, c0:c0 + NA_Q_COLS, :] = (
            attend(blk, s_cur).reshape(NA_Q_ROWS, NA_Q_COLS, NA_LANES).astype(o_ref.dtype))


def _natten(nq, nk, nv, bias_tabs):
    b, rows = nq.shape[1], nq.shape[2]
    slab = pl.BlockSpec((None, None, rows, GRID_W, NA_LANES), lambda j, i: (j, i, 0, 0, 0))
    nqk = NA_Q_ROWS * NA_Q_COLS
    nkk = NA_K_ROWS * NA_K_COLS
    return pl.pallas_call(
        _na_kernel,
        grid=(NA_SLABS, b),
        in_specs=[slab, slab, slab,
                  pl.BlockSpec((2, _NA_CONFIGS, nqk, nkk), lambda j, i: (j, 0, 0, 0))],
        out_specs=slab,
        out_shape=jax.ShapeDtypeStruct((NA_SLABS, b, rows, GRID_W, NA_LANES), BF16),
        scratch_shapes=[pltpu.VMEM((2, rows, GRID_W, NA_LANES), BF16),
                        pltpu.VMEM((2, rows, GRID_W, NA_LANES), BF16)],
        compiler_params=pltpu.CompilerParams(
            dimension_semantics=("parallel", "parallel"), vmem_limit_bytes=VMEM_LIMIT),
        name="natten",
    )(nq, nk, nv, bias_tabs)


def _mixout_ffn_kernel(x_ref, yr_ref, na_ref, gr_ref, gn_ref, wr_ref, wn_ref, wo_ref, gmix_ref,
                       gpre_ref, win_ref, wout_ref, gpost_ref, o_ref, x2_ref, act_ref):
    n_sub = x_ref.shape[0] // SUB_TILE

    def rows(j):
        return slice(j * SUB_TILE, (j + 1) * SUB_TILE)

    def branches(j):
        def gathered(ref):
            return jnp.concatenate([ref[g, rows(j), :] for g in range(ref.shape[0])], axis=1)

        y_ret = jnp.dot(gathered(yr_ref), wr_ref[...], preferred_element_type=F32)
        y_na = jnp.dot(gathered(na_ref), wn_ref[...], preferred_element_type=F32)
        merged = (jax.nn.sigmoid(gr_ref[rows(j), :].astype(F32)) * y_ret
                  + jax.nn.sigmoid(gn_ref[rows(j), :].astype(F32)) * y_na)
        return merged.astype(BF16)

    def mixed(j, merged):
        m = jnp.dot(merged, wo_ref[...], preferred_element_type=F32)
        x2 = x_ref[rows(j), :] + _rms(m, gmix_ref[...])
        x2_ref[rows(j), :] = x2
        return _rms(x2, gpre_ref[...]).astype(BF16)

    def up(j, xn):
        for c in range(D_FF // FF_CHUNK):
            lo = c * FF_CHUNK
            g = jnp.dot(xn, win_ref[:, lo:lo + FF_CHUNK], preferred_element_type=F32)
            u = jnp.dot(xn, win_ref[:, D_FF + lo:D_FF + lo + FF_CHUNK], preferred_element_type=F32)
            act_ref[rows(j), lo:lo + FF_CHUNK] = (_silu(g) * u).astype(BF16)

    def down(j):
        h = jnp.dot(act_ref[rows(j), :], wout_ref[...], preferred_element_type=F32)
        o_ref[rows(j), :] = x2_ref[rows(j), :] + 0.5 * _rms(h, gpost_ref[...])

    merged = [branches(j) for j in range(n_sub)]
    xn = [mixed(j, merged[j]) for j in range(n_sub)]
    for j in range(n_sub):
        up(j, xn[j])
    for j in range(n_sub):
        down(j)


def _mix_out_ffn(x, y_ret, y_na, g_ret, g_na, w_ret, w_na, w_out, gmix, gpre, w_in, w_ffn_out, gpost):
    t = x.shape[0]
    tm = TOKEN_TILE

    def row(width):
        return pl.BlockSpec((tm, width), lambda i: (i, 0))

    gain = _const_spec((1, D_MODEL))
    return pl.pallas_call(
        _mixout_ffn_kernel,
        grid=(t // tm,),
        in_specs=[row(D_MODEL),
                  pl.BlockSpec((RET_HEADS, tm, RET_V_DIM), lambda i: (0, i, 0)),
                  pl.BlockSpec((NA_SLABS, tm, NA_LANES), lambda i: (0, i, 0)),
                  row(D_MODEL), row(D_MODEL),
                  _const_spec((RET_V_W, D_MODEL)), _const_spec((NA_W, D_MODEL)),
                  _const_spec((D_MODEL, D_MODEL)), gain,
                  gain, _const_spec((D_MODEL, 2 * D_FF)), _const_spec((D_FF, D_MODEL)), gain],
        out_specs=row(D_MODEL),
        out_shape=jax.ShapeDtypeStruct((t, D_MODEL), F32),
        scratch_shapes=[pltpu.VMEM((tm, D_MODEL), F32), pltpu.VMEM((tm, D_FF), BF16)],
        compiler_params=pltpu.CompilerParams(
            dimension_semantics=("parallel",), vmem_limit_bytes=VMEM_LIMIT),
        name="mix_out_ffn",
    )(x, y_ret, y_na, g_ret, g_na, w_ret, w_na, w_out, gmix, gpre, w_in, w_ffn_out, gpost)


def kernel(x, ffn1_pre_norm, ffn1_w_in, ffn1_w_out, ffn1_post_norm, mix_pre_norm, w_mix_in,
           ret_decay_fwd, ret_decay_bwd, na_rel_bias, w_ret_out, w_na_out, w_mix_out, mix_post_norm,
           ffn2_pre_norm, ffn2_w_in, ffn2_w_out, ffn2_post_norm):
    b, s, d = x.shape
    assert d == D_MODEL and s % GRID_W == 0 and s % RET_BLOCK == 0 and (b * s) % TOKEN_TILE == 0
    assert s % TOKEN_TILE == 0
    rows = s // GRID_W
    assert rows == NA_ROWS, "the neighbourhood-attention tiling is laid out for a 32 x 64 token grid"
    t = b * s
    xt = x.reshape(t, d)
    for l in range(ffn1_w_in.shape[0]):
        gain = lambda g: g[l].reshape(1, D_MODEL).astype(F32)
        wt = lambda w: w[l].astype(BF16)
        xt = _ffn(xt, gain(ffn1_pre_norm), wt(ffn1_w_in), wt(ffn1_w_out), gain(ffn1_post_norm))
        rq, rk, rv, rg, nq, nk, nv, g_ret, g_na = _mix_in(xt, gain(mix_pre_norm), wt(w_mix_in), s)
        decay = jnp.stack([ret_decay_fwd[l], ret_decay_bwd[l]]).astype(F32)
        seq = lambda a: a.reshape(a.shape[0], b, s, a.shape[-1])
        y_ret = _retention(decay, seq(rq), seq(rk), seq(rv), seq(rg)).reshape(RET_HEADS, t, RET_V_DIM)
        grid5 = lambda a: a.reshape(NA_SLABS, b, rows, GRID_W, NA_LANES)
        y_na = _natten(grid5(nq), grid5(nk), grid5(nv), _na_bias_tables(na_rel_bias[l]))
        xt = _mix_out_ffn(xt, y_ret, y_na.reshape(NA_SLABS, t, NA_LANES), g_ret, g_na,
                          wt(w_ret_out), wt(w_na_out), wt(w_mix_out), gain(mix_post_norm),
                          gain(ffn2_pre_norm), wt(ffn2_w_in), wt(ffn2_w_out), gain(ffn2_post_norm))
    return xt.reshape(b, s, d)
```

```python
import functools

import numpy as np
import jax
import jax.numpy as jnp
from jax import lax
from jax.experimental import pallas as pl
from jax.experimental.pallas import tpu as pltpu

F32 = jnp.float32
BF16 = jnp.bfloat16

D_MODEL = 1024
D_FF = 2816
RMS_EPS = 1e-6
ROPE_BASE = 10000.0
NEG_INF = -1e30
LOG2E = 1.4426950408889634

RET_HEADS = 4
RET_QK_DIM = 128
RET_V_DIM = 256
RET_QK_W = RET_HEADS * RET_QK_DIM
RET_V_W = RET_HEADS * RET_V_DIM
RET_BLOCK = 256

NA_HEADS = 8
NA_HEAD_DIM = 64
NA_W = NA_HEADS * NA_HEAD_DIM
GRID_W = 64
NA_WIN_ROWS = 8
NA_WIN_COLS = 16
NA_REL_ROWS = 2 * NA_WIN_ROWS - 1
NA_REL_COLS = 2 * NA_WIN_COLS - 1
NA_Q_ROWS = 8
NA_Q_COLS = 16
NA_K_ROWS = 16
NA_K_COLS = 32
NA_LANES = 128
NA_SLABS = NA_W // NA_LANES

MIX_SPLITS = (RET_QK_W, RET_QK_W, RET_V_W, RET_V_W, NA_W, NA_W, NA_W, D_MODEL, D_MODEL)
MIX_OFFS = tuple(int(v) for v in np.cumsum((0,) + MIX_SPLITS[:-1]))
MIX_IN_W = sum(MIX_SPLITS)

TOKEN_TILE = 512
FFN_TILE = 1024
SUB_TILE = 256
FF_CHUNK = 256
VMEM_LIMIT = 56 * 1024 * 1024


def _rms(x, gain):
    ms = jnp.mean(x * x, axis=-1, keepdims=True)
    return x * lax.rsqrt(ms + RMS_EPS) * gain


def _silu(x):
    return x * jax.nn.sigmoid(x)


def _const_spec(shape):
    nd = len(shape)
    return pl.BlockSpec(shape, lambda *_: (0,) * nd, pipeline_mode=pl.Buffered(1))


def _ffn_kernel(x_ref, gpre_ref, win_ref, wout_ref, gpost_ref, o_ref, act_ref):
    n_sub = x_ref.shape[0] // SUB_TILE

    def rows(j):
        return slice(j * SUB_TILE, (j + 1) * SUB_TILE)

    def pre(j):
        return _rms(x_ref[rows(j), :], gpre_ref[...]).astype(BF16)

    def up(j, xn):
        for c in range(D_FF // FF_CHUNK):
            lo = c * FF_CHUNK
            g = jnp.dot(xn, win_ref[:, lo:lo + FF_CHUNK], preferred_element_type=F32)
            u = jnp.dot(xn, win_ref[:, D_FF + lo:D_FF + lo + FF_CHUNK], preferred_element_type=F32)
            act_ref[rows(j), lo:lo + FF_CHUNK] = (_silu(g) * u).astype(BF16)

    def down(j):
        h = jnp.dot(act_ref[rows(j), :], wout_ref[...], preferred_element_type=F32)
        o_ref[rows(j), :] = x_ref[rows(j), :] + 0.5 * _rms(h, gpost_ref[...])

    xn = pre(0)
    for j in range(n_sub):
        xn_next = pre(j + 1) if j + 1 < n_sub else None
        up(j, xn)
        if j >= 1:
            down(j - 1)
        xn = xn_next
    down(n_sub - 1)


def _ffn(x, gpre, w_in, w_out, gpost):
    t = x.shape[0]
    tm = FFN_TILE
    row = pl.BlockSpec((tm, D_MODEL), lambda i: (i, 0))
    return pl.pallas_call(
        _ffn_kernel,
        grid=(t // tm,),
        in_specs=[row, _const_spec((1, D_MODEL)), _const_spec((D_MODEL, 2 * D_FF)),
                  _const_spec((D_FF, D_MODEL)), _const_spec((1, D_MODEL))],
        out_specs=row,
        out_shape=jax.ShapeDtypeStruct((t, D_MODEL), F32),
        scratch_shapes=[pltpu.VMEM((tm, D_FF), BF16)],
        compiler_params=pltpu.CompilerParams(
            dimension_semantics=("parallel",), vmem_limit_bytes=VMEM_LIMIT),
        name="ffn",
    )(x, gpre, w_in, w_out, gpost)


def _mixin_kernel(x_ref, gain_ref, w_ref, rope_ref,
                  rq_ref, rk_ref, rv_ref, rg_ref, nq_ref, nk_ref, nv_ref, gr_ref, gn_ref):
    n_sub = x_ref.shape[0] // SUB_TILE

    def pre(j):
        rows = slice(j * SUB_TILE, (j + 1) * SUB_TILE)
        return _rms(x_ref[rows, :], gain_ref[...]).astype(BF16)

    def project(j, u):
        rows = slice(j * SUB_TILE, (j + 1) * SUB_TILE)

        def proj(idx):
            lo = MIX_OFFS[idx]
            return jnp.dot(u, w_ref[:, lo:lo + MIX_SPLITS[idx]], preferred_element_type=F32)

        def store_groups(out_ref, y):
            width = out_ref.shape[-1]
            for g in range(out_ref.shape[0]):
                out_ref[g, rows, :] = y[:, g * width:(g + 1) * width].astype(out_ref.dtype)

        def rotary(y, cos, sin, out_ref):
            for h in range(RET_HEADS):
                yh = y[:, h * RET_QK_DIM:(h + 1) * RET_QK_DIM]
                out_ref[h, rows, :] = (yh * cos + pltpu.roll(yh, RET_QK_DIM // 2, 1) * sin).astype(out_ref.dtype)

        rotary(proj(0), rope_ref[0, rows, :], rope_ref[1, rows, :], rq_ref)
        rotary(proj(1), rope_ref[2, rows, :], rope_ref[3, rows, :], rk_ref)
        store_groups(rv_ref, proj(2))
        store_groups(rg_ref, _silu(proj(3)))
        store_groups(nq_ref, proj(4) * (LOG2E * NA_HEAD_DIM ** -0.5))
        store_groups(nk_ref, proj(5))
        store_groups(nv_ref, proj(6))
        gr_ref[rows, :] = proj(7).astype(gr_ref.dtype)
        gn_ref[rows, :] = proj(8).astype(gn_ref.dtype)

    u = pre(0)
    for j in range(n_sub):
        u_next = pre(j + 1) if j + 1 < n_sub else None
        project(j, u)
        u = u_next


def _rope_tables(seq):
    half = RET_QK_DIM // 2
    pos = jnp.arange(seq, dtype=F32)
    inv = 1.0 / (ROPE_BASE ** jnp.linspace(0.0, 1.0, half, dtype=F32))
    ang = pos[:, None] * inv[None, :]
    cos, sin = jnp.cos(ang), jnp.sin(ang)
    cos2 = jnp.concatenate([cos, cos], axis=-1)
    sin2 = jnp.concatenate([-sin, sin], axis=-1)
    ks = RET_QK_DIM ** -0.5
    return jnp.stack([cos2, sin2, cos2 * ks, sin2 * ks])


def _mix_in(x, gain, w, seq):
    t = x.shape[0]
    tm = TOKEN_TILE
    tiles_per_seq = seq // tm
    out_dtypes = (BF16, BF16, BF16, BF16, BF16, F32, F32, BF16, BF16)
    out_groups = (RET_HEADS,) * 4 + (NA_SLABS,) * 3 + (1, 1)
    return pl.pallas_call(
        _mixin_kernel,
        grid=(t // tm,),
        in_specs=[pl.BlockSpec((tm, D_MODEL), lambda i: (i, 0)),
                  _const_spec((1, D_MODEL)), _const_spec((D_MODEL, MIX_IN_W)),
                  pl.BlockSpec((4, tm, RET_QK_DIM), lambda i: (0, i % tiles_per_seq, 0))],
        out_specs=[pl.BlockSpec((tm, wd), lambda i: (i, 0)) if g == 1
                   else pl.BlockSpec((g, tm, wd // g), lambda i: (0, i, 0))
                   for wd, g in zip(MIX_SPLITS, out_groups)],
        out_shape=[jax.ShapeDtypeStruct((t, wd) if g == 1 else (g, t, wd // g), dt)
                   for wd, g, dt in zip(MIX_SPLITS, out_groups, out_dtypes)],
        compiler_params=pltpu.CompilerParams(
            dimension_semantics=("parallel",), vmem_limit_bytes=VMEM_LIMIT),
        name="mix_in",
    )(x, gain, w, _rope_tables(seq))


def _log_sigmoid(z):
    return jnp.minimum(z, 0.0) - jnp.log1p(jnp.exp(-jnp.abs(z)))


def _ret_kernel(dec_ref, q_ref, k_ref, v_ref, g_ref, o_ref, u_ref, st_ref, dm_ref, vec_ref, blk_ref):
    c = RET_BLOCK
    dk = RET_QK_DIM
    dv = RET_V_DIM
    n_blocks = q_ref.shape[0] // c
    tn = (((0,), (0,)), ((), ()))
    nt = (((1,), (1,)), ((), ()))

    @pl.when(pl.program_id(1) == 0)
    def _decay_tables():
        head = pl.program_id(0)
        zf = dec_ref[0, head]
        zb = dec_ref[1, head]

        def log_gamma(z, shape):
            return _log_sigmoid(jnp.full(shape, z, F32))

        ri = lax.broadcasted_iota(jnp.int32, (c, c), 0).astype(F32)
        ci = lax.broadcasted_iota(jnp.int32, (c, c), 1).astype(F32)
        diff = ri - ci
        dm_ref[...] = jnp.where(diff >= 0, jnp.exp(log_gamma(zf, (c, c)) * jnp.maximum(diff, 0.0)),
                                jnp.exp(log_gamma(zb, (c, c)) * jnp.maximum(-diff, 0.0)))
        r = lax.broadcasted_iota(jnp.int32, (c, dk), 0).astype(F32)
        lgf = log_gamma(zf, (c, dk))
        lgb = log_gamma(zb, (c, dk))
        vec_ref[0] = jnp.exp(lgf * (r + 1.0)).astype(BF16)
        vec_ref[1] = jnp.exp(lgb * (c - r)).astype(BF16)
        vec_ref[2] = jnp.exp(lgf * (c - 1.0 - r)).astype(BF16)
        vec_ref[3] = jnp.exp(lgb * r).astype(BF16)
        blk_ref[0] = jnp.exp(log_gamma(zf, (dk, dv)) * c)
        blk_ref[1] = jnp.exp(log_gamma(zb, (dk, dv)) * c)

    def rows(n):
        return slice(n * c, (n + 1) * c)

    for n in range(n_blocks):
        k = k_ref[rows(n), :]
        k_cat = jnp.concatenate([k * vec_ref[2], k * vec_ref[3]], axis=1)
        u_ref[n] = lax.dot_general(k_cat, v_ref[rows(n), :], tn, preferred_element_type=F32)

    sf = jnp.zeros((dk, dv), F32)
    for n in range(n_blocks):
        st_ref[n, 0:dk, :] = sf.astype(BF16)
        if n + 1 < n_blocks:
            sf = blk_ref[0] * sf + u_ref[n, 0:dk, :]
    sb = jnp.zeros((dk, dv), F32)
    for n in reversed(range(n_blocks)):
        st_ref[n, dk:2 * dk, :] = sb.astype(BF16)
        if n > 0:
            sb = blk_ref[1] * sb + u_ref[n, dk:2 * dk, :]

    def scores(n):
        return lax.dot_general(q_ref[rows(n), :], k_ref[rows(n), :], nt, preferred_element_type=F32)

    s_next = scores(0)
    for n in range(n_blocks):
        s_cur = s_next
        if n + 1 < n_blocks:
            s_next = scores(n + 1)
        p = (s_cur * dm_ref[...]).astype(BF16)
        q = q_ref[rows(n), :]
        lhs = jnp.concatenate([p, q * vec_ref[0], q * vec_ref[1]], axis=1)
        rhs = jnp.concatenate([v_ref[rows(n), :], st_ref[n]], axis=0)
        y = jnp.dot(lhs, rhs, preferred_element_type=F32)
        yn = y * lax.rsqrt(jnp.mean(y * y, axis=-1, keepdims=True) + RMS_EPS)
        o_ref[rows(n), :] = (g_ref[rows(n), :].astype(F32) * yn).astype(o_ref.dtype)


def _retention(decay_logits, rq, rk, rv, rg):
    _, b, s, _ = rq.shape
    c = RET_BLOCK
    n_blocks = s // c
    qk_spec = pl.BlockSpec((None, None, s, RET_QK_DIM), lambda h, i, dec: (h, i, 0, 0))
    v_spec = pl.BlockSpec((None, None, s, RET_V_DIM), lambda h, i, dec: (h, i, 0, 0))
    return pl.pallas_call(
        _ret_kernel,
        grid_spec=pltpu.PrefetchScalarGridSpec(
            num_scalar_prefetch=1,
            grid=(RET_HEADS, b),
            in_specs=[qk_spec, qk_spec, v_spec, v_spec],
            out_specs=v_spec,
            scratch_shapes=[pltpu.VMEM((n_blocks, 2 * RET_QK_DIM, RET_V_DIM), F32),
                            pltpu.VMEM((n_blocks, 2 * RET_QK_DIM, RET_V_DIM), BF16),
                            pltpu.VMEM((c, c), F32),
                            pltpu.VMEM((4, c, RET_QK_DIM), BF16),
                            pltpu.VMEM((2, RET_QK_DIM, RET_V_DIM), F32)],
        ),
        out_shape=jax.ShapeDtypeStruct((RET_HEADS, b, s, RET_V_DIM), BF16),
        compiler_params=pltpu.CompilerParams(
            dimension_semantics=("arbitrary", "arbitrary"), vmem_limit_bytes=VMEM_LIMIT),
        name="retention",
    )(decay_logits, rq, rk, rv, rg)


def _na_blocks(extent, q_size, k_size, window):
    n = extent // q_size
    out = []
    for i in range(n):
        q0 = i * q_size
        k0 = min(max(q0 - window // 2, 0), extent - k_size)
        out.append((q0, k0, 0 if i == 0 else (2 if i == n - 1 else 1)))
    return tuple(out)


NA_ROWS = 32
_NA_ROW_BLOCKS = _na_blocks(NA_ROWS, NA_Q_ROWS, NA_K_ROWS, NA_WIN_ROWS)
_NA_COL_BLOCKS = _na_blocks(GRID_W, NA_Q_COLS, NA_K_COLS, NA_WIN_COLS)
_NA_CONFIGS = 9


def _na_bias_kernel(rel_ref, out_ref, col_ref):
    nk = NA_K_ROWS * NA_K_COLS
    base = pl.program_id(0) * (NA_REL_ROWS * NA_REL_COLS)
    lane = lax.broadcasted_iota(jnp.int32, (NA_Q_COLS, nk), 1)
    kr = lane // NA_K_COLS
    col_cfgs = (_NA_COL_BLOCKS[0], _NA_COL_BLOCKS[1], _NA_COL_BLOCKS[3])
    row_cfgs = (_NA_ROW_BLOCKS[0], _NA_ROW_BLOCKS[1], _NA_ROW_BLOCKS[3])
    reps = nk // NA_LANES
    kc_one = lax.broadcasted_iota(jnp.int32, (NA_Q_COLS, NA_LANES), 1) % NA_K_COLS
    qc_one = lax.broadcasted_iota(jnp.int32, (NA_Q_COLS, NA_LANES), 0)
    for b, (c0, kc0, _) in enumerate(col_cfgs):
        kc_abs = kc_one + kc0
        qc_abs = qc_one + c0
        rel_col = jnp.clip(kc_abs - qc_abs + NA_WIN_COLS - 1, 0, NA_REL_COLS - 1)
        win = jnp.clip(qc_abs - NA_WIN_COLS // 2, 0, GRID_W - NA_WIN_COLS)
        in_win = (kc_abs >= win) & (kc_abs < win + NA_WIN_COLS)

        def rel_row_body(dr, carry, rel_col=rel_col, in_win=in_win, b=b):
            vals = jnp.zeros((NA_Q_COLS, NA_LANES), F32)
            for dc in range(NA_REL_COLS):
                vals = jnp.where(rel_col == dc, rel_ref[base + dr * NA_REL_COLS + dc], vals)
            vals = jnp.where(in_win, vals * LOG2E, NEG_INF)
            col_ref[b * NA_REL_ROWS + dr] = jnp.concatenate([vals] * reps, axis=1)
            return carry

        lax.fori_loop(0, NA_REL_ROWS, rel_row_body, 0)

    for a, (r0, ks, _) in enumerate(row_cfgs):
        kr_abs = kr + ks
        for qr in range(NA_Q_ROWS):
            r_abs = r0 + qr
            rs = min(max(r_abs - NA_WIN_ROWS // 2, 0), NA_ROWS - NA_WIN_ROWS)
            hits = [kr_abs == rs + j for j in range(NA_WIN_ROWS)]
            for b in range(3):
                tile = jnp.full((NA_Q_COLS, nk), NEG_INF, F32)
                for j in range(NA_WIN_ROWS):
                    dr = rs + j - r_abs + NA_WIN_ROWS - 1
                    tile = jnp.where(hits[j], col_ref[b * NA_REL_ROWS + dr], tile)
                out_ref[0, a * 3 + b, qr * NA_Q_COLS:(qr + 1) * NA_Q_COLS, :] = tile


def _na_bias_tables(rel_bias):
    nq = NA_Q_ROWS * NA_Q_COLS
    nk = NA_K_ROWS * NA_K_COLS
    return pl.pallas_call(
        _na_bias_kernel,
        grid_spec=pltpu.PrefetchScalarGridSpec(
            num_scalar_prefetch=1,
            grid=(NA_HEADS,),
            in_specs=[],
            out_specs=pl.BlockSpec((1, _NA_CONFIGS, nq, nk), lambda h, rel: (h, 0, 0, 0)),
            scratch_shapes=[pltpu.VMEM((3 * NA_REL_ROWS, NA_Q_COLS, nk), F32)],
        ),
        out_shape=jax.ShapeDtypeStruct((NA_HEADS, _NA_CONFIGS, nq, nk), F32),
        compiler_params=pltpu.CompilerParams(dimension_semantics=("parallel",)),
        name="na_bias",
    )(rel_bias.astype(F32).reshape(-1))


def _na_kernel(q_ref, k_ref, v_ref, bias_ref, o_ref, kb_ref, vb_ref):
    nq = NA_Q_ROWS * NA_Q_COLS
    nk = NA_K_ROWS * NA_K_COLS
    shift = NA_WIN_COLS // 2
    kept = GRID_W - 2 * shift
    for src, dst in ((k_ref, kb_ref), (v_ref, vb_ref)):
        dst[0] = src[...].astype(BF16)
        dst[1, :, 0:kept, :] = src[:, shift:shift + kept, :].astype(BF16)

    def window(ref, block):
        (_, ks, _), (_, kc0, _) = block
        which, c = (0, kc0) if kc0 % (2 * shift) == 0 else (1, kc0 - shift)
        assert c % (2 * shift) == 0 and (which == 0 or c + NA_K_COLS <= kept)
        return ref[which, ks:ks + NA_K_ROWS, c:c + NA_K_COLS, :].reshape(nk, NA_LANES)

    first = lax.broadcasted_iota(jnp.int32, (nq, NA_LANES), 1) < NA_HEAD_DIM
    ones = jnp.ones((nk, NA_LANES), BF16)
    nt = (((1,), (1,)), ((), ()))
    blocks = [(rb, cb) for rb in _NA_ROW_BLOCKS for cb in _NA_COL_BLOCKS]

    def scores(block):
        (r0, _, rcfg), (c0, _, ccfg) = block
        q = q_ref[r0:r0 + NA_Q_ROWS, c0:c0 + NA_Q_COLS, :].reshape(nq, NA_LANES)
        zero = jnp.zeros_like(q)
        q2 = jnp.concatenate([jnp.where(first, q, zero), jnp.where(first, zero, q)], axis=0)
        cfg = rcfg * 3 + ccfg
        bias = jnp.concatenate([bias_ref[0, cfg], bias_ref[1, cfg]], axis=0)
        return lax.dot_general(q2, window(kb_ref, block), nt, preferred_element_type=F32) + bias

    def attend(block, s):
        m = jnp.max(s, axis=-1, keepdims=True)
        p = jnp.exp2(s - m).astype(BF16)
        o = jnp.dot(p, jnp.concatenate([window(vb_ref, block), ones], axis=1),
                    preferred_element_type=F32)
        o = o[:, :NA_LANES] / o[:, NA_LANES:]
        return jnp.where(first, o[:nq], o[nq:])

    s_next = scores(blocks[0])
    for i, blk in enumerate(blocks):
        s_cur = s_next
        if i + 1 < len(blocks):
            s_next = scores(blocks[i + 1])
        (r0, _, _), (c0, _, _) = blk
        o_ref[r0:r0 + NA_Q_ROWS, c0:c0 + NA_Q_COLS, :] = (
            attend(blk, s_cur).reshape(NA_Q_ROWS, NA_Q_COLS, NA_LANES).astype(o_ref.dtype))


def _natten(nq, nk, nv, bias_tabs):
    b, rows = nq.shape[1], nq.shape[2]
    slab = pl.BlockSpec((None, None, rows, GRID_W, NA_LANES), lambda j, i: (j, i, 0, 0, 0))
    nqk = NA_Q_ROWS * NA_Q_COLS
    nkk = NA_K_ROWS * NA_K_COLS
    return pl.pallas_call(
        _na_kernel,
        grid=(NA_SLABS, b),
        in_specs=[slab, slab, slab,
                  pl.BlockSpec((2, _NA_CONFIGS, nqk, nkk), lambda j, i: (j, 0, 0, 0))],
        out_specs=slab,
        out_shape=jax.ShapeDtypeStruct((NA_SLABS, b, rows, GRID_W, NA_LANES), BF16),
        scratch_shapes=[pltpu.VMEM((2, rows, GRID_W, NA_LANES), BF16),
                        pltpu.VMEM((2, rows, GRID_W, NA_LANES), BF16)],
        compiler_params=pltpu.CompilerParams(
            dimension_semantics=("parallel", "parallel"), vmem_limit_bytes=VMEM_LIMIT),
        name="natten",
    )(nq, nk, nv, bias_tabs)


def _mixout_kernel(x_ref, yr_ref, na_ref, gr_ref, gn_ref, wr_ref, wn_ref, wo_ref, gain_ref, o_ref):
    n_sub = x_ref.shape[0] // SUB_TILE

    def rows(j):
        return slice(j * SUB_TILE, (j + 1) * SUB_TILE)

    def branches(j):
        def gathered(ref):
            return jnp.concatenate([ref[g, rows(j), :] for g in range(ref.shape[0])], axis=1)

        y_ret = jnp.dot(gathered(yr_ref), wr_ref[...], preferred_element_type=F32)
        y_na = jnp.dot(gathered(na_ref), wn_ref[...], preferred_element_type=F32)
        merged = (jax.nn.sigmoid(gr_ref[rows(j), :].astype(F32)) * y_ret
                  + jax.nn.sigmoid(gn_ref[rows(j), :].astype(F32)) * y_na)
        return merged.astype(BF16)

    def finish(j, merged):
        m = jnp.dot(merged, wo_ref[...], preferred_element_type=F32)
        o_ref[rows(j), :] = x_ref[rows(j), :] + _rms(m, gain_ref[...])

    merged = branches(0)
    for j in range(n_sub):
        merged_next = branches(j + 1) if j + 1 < n_sub else None
        finish(j, merged)
        merged = merged_next


def _mix_out(x, y_ret, y_na, g_ret, g_na, w_ret, w_na, w_out, gain):
    t = x.shape[0]
    tm = FFN_TILE

    def row(width):
        return pl.BlockSpec((tm, width), lambda i: (i, 0))

    return pl.pallas_call(
        _mixout_kernel,
        grid=(t // tm,),
        in_specs=[row(D_MODEL),
                  pl.BlockSpec((RET_HEADS, tm, RET_V_DIM), lambda i: (0, i, 0)),
                  pl.BlockSpec((NA_SLABS, tm, NA_LANES), lambda i: (0, i, 0)),
                  row(D_MODEL), row(D_MODEL),
                  _const_spec((RET_V_W, D_MODEL)), _const_spec((NA_W, D_MODEL)),
                  _const_spec((D_MODEL, D_MODEL)), _const_spec((1, D_MODEL))],
        out_specs=row(D_MODEL),
        out_shape=jax.ShapeDtypeStruct((t, D_MODEL), F32),
        compiler_params=pltpu.CompilerParams(
            dimension_semantics=("parallel",), vmem_limit_bytes=VMEM_LIMIT),
        name="mix_out",
    )(x, y_ret, y_na, g_ret, g_na, w_ret, w_na, w_out, gain)


def kernel(x, ffn1_pre_norm, ffn1_w_in, ffn1_w_out, ffn1_post_norm, mix_pre_norm, w_mix_in,
           ret_decay_fwd, ret_decay_bwd, na_rel_bias, w_ret_out, w_na_out, w_mix_out, mix_post_norm,
           ffn2_pre_norm, ffn2_w_in, ffn2_w_out, ffn2_post_norm):
    b, s, d = x.shape
    assert d == D_MODEL and s % GRID_W == 0 and s % RET_BLOCK == 0 and (b * s) % TOKEN_TILE == 0
    assert s % TOKEN_TILE == 0
    rows = s // GRID_W
    assert rows == NA_ROWS, "the neighbourhood-attention tiling is laid out for a 32 x 64 token grid"
    t = b * s
    xt = x.reshape(t, d)
    for l in range(ffn1_w_in.shape[0]):
        gain = lambda g: g[l].reshape(1, D_MODEL).astype(F32)
        wt = lambda w: w[l].astype(BF16)
        xt = _ffn(xt, gain(ffn1_pre_norm), wt(ffn1_w_in), wt(ffn1_w_out), gain(ffn1_post_norm))
        rq, rk, rv, rg, nq, nk, nv, g_ret, g_na = _mix_in(xt, gain(mix_pre_norm), wt(w_mix_in), s)
        decay = jnp.stack([ret_decay_fwd[l], ret_decay_bwd[l]]).astype(F32)
        seq = lambda a: a.reshape(a.shape[0], b, s, a.shape[-1])
        y_ret = _retention(decay, seq(rq), seq(rk), seq(rv), seq(rg)).reshape(RET_HEADS, t, RET_V_DIM)
        grid5 = lambda a: a.reshape(NA_SLABS, b, rows, GRID_W, NA_LANES)
        y_na = _natten(grid5(nq), grid5(nk), grid5(nv), _na_bias_tables(na_rel_bias[l]))
        xt = _mix_out(xt, y_ret, y_na.reshape(NA_SLABS, t, NA_LANES), g_ret, g_na,
                      wt(w_ret_out), wt(w_na_out), wt(w_mix_out), gain(mix_post_norm))
        xt = _ffn(xt, gain(ffn2_pre_norm), wt(ffn2_w_in), wt(ffn2_w_out), gain(ffn2_post_norm))
    return xt.reshape(b, s, d)
```

```python
import functools

import numpy as np
import jax
import jax.numpy as jnp
from jax import lax
from jax.experimental import pallas as pl
from jax.experimental.pallas import tpu as pltpu

F32 = jnp.float32
BF16 = jnp.bfloat16

D_MODEL = 1024
D_FF = 2816
RMS_EPS = 1e-6
ROPE_BASE = 10000.0
NEG_INF = -1e30
LOG2E = 1.4426950408889634

RET_HEADS = 4
RET_QK_DIM = 128
RET_V_DIM = 256
RET_QK_W = RET_HEADS * RET_QK_DIM
RET_V_W = RET_HEADS * RET_V_DIM
RET_BLOCK = 256

NA_HEADS = 8
NA_HEAD_DIM = 64
NA_W = NA_HEADS * NA_HEAD_DIM
GRID_W = 64
NA_WIN_ROWS = 8
NA_WIN_COLS = 16
NA_REL_ROWS = 2 * NA_WIN_ROWS - 1
NA_REL_COLS = 2 * NA_WIN_COLS - 1
NA_Q_ROWS = 8
NA_Q_COLS = 16
NA_K_ROWS = 16
NA_K_COLS = 32
NA_LANES = 128
NA_SLABS = NA_W // NA_LANES

MIX_SPLITS = (RET_QK_W, RET_QK_W, RET_V_W, RET_V_W, NA_W, NA_W, NA_W, D_MODEL, D_MODEL)
MIX_OFFS = tuple(int(v) for v in np.cumsum((0,) + MIX_SPLITS[:-1]))
MIX_IN_W = sum(MIX_SPLITS)

TOKEN_TILE = 512
FFN_TILE = 1024
SUB_TILE = 256
FF_CHUNK = 256
VMEM_LIMIT = 56 * 1024 * 1024


def _rms(x, gain):
    ms = jnp.mean(x * x, axis=-1, keepdims=True)
    return x * lax.rsqrt(ms + RMS_EPS) * gain


def _silu(x):
    return x * jax.nn.sigmoid(x)


def _const_spec(shape):
    nd = len(shape)
    return pl.BlockSpec(shape, lambda *_: (0,) * nd, pipeline_mode=pl.Buffered(1))


def _ffn_kernel(x_ref, gpre_ref, win_ref, wout_ref, gpost_ref, o_ref, act_ref):
    n_sub = x_ref.shape[0] // SUB_TILE

    def rows(j):
        return slice(j * SUB_TILE, (j + 1) * SUB_TILE)

    def pre(j):
        return _rms(x_ref[rows(j), :], gpre_ref[...]).astype(BF16)

    def up(j, xn):
        for c in range(D_FF // FF_CHUNK):
            lo = c * FF_CHUNK
            g = jnp.dot(xn, win_ref[:, lo:lo + FF_CHUNK], preferred_element_type=F32)
            u = jnp.dot(xn, win_ref[:, D_FF + lo:D_FF + lo + FF_CHUNK], preferred_element_type=F32)
            act_ref[rows(j), lo:lo + FF_CHUNK] = (_silu(g) * u).astype(BF16)

    def down(j):
        h = jnp.dot(act_ref[rows(j), :], wout_ref[...], preferred_element_type=F32)
        o_ref[rows(j), :] = x_ref[rows(j), :] + 0.5 * _rms(h, gpost_ref[...])

    xn = pre(0)
    for j in range(n_sub):
        xn_next = pre(j + 1) if j + 1 < n_sub else None
        up(j, xn)
        if j >= 1:
            down(j - 1)
        xn = xn_next
    down(n_sub - 1)


def _ffn(x, gpre, w_in, w_out, gpost):
    t = x.shape[0]
    tm = FFN_TILE
    row = pl.BlockSpec((tm, D_MODEL), lambda i: (i, 0))
    return pl.pallas_call(
        _ffn_kernel,
        grid=(t // tm,),
        in_specs=[row, _const_spec((1, D_MODEL)), _const_spec((D_MODEL, 2 * D_FF)),
                  _const_spec((D_FF, D_MODEL)), _const_spec((1, D_MODEL))],
        out_specs=row,
        out_shape=jax.ShapeDtypeStruct((t, D_MODEL), F32),
        scratch_shapes=[pltpu.VMEM((tm, D_FF), BF16)],
        compiler_params=pltpu.CompilerParams(
            dimension_semantics=("parallel",), vmem_limit_bytes=VMEM_LIMIT),
        name="ffn",
    )(x, gpre, w_in, w_out, gpost)


def _mixin_kernel(x_ref, gain_ref, w_ref, rope_ref,
                  rq_ref, rk_ref, rv_ref, rg_ref, nq_ref, nk_ref, nv_ref, gr_ref, gn_ref):
    n_sub = x_ref.shape[0] // SUB_TILE

    def pre(j):
        rows = slice(j * SUB_TILE, (j + 1) * SUB_TILE)
        return _rms(x_ref[rows, :], gain_ref[...]).astype(BF16)

    def project(j, u):
        rows = slice(j * SUB_TILE, (j + 1) * SUB_TILE)

        def proj(idx):
            lo = MIX_OFFS[idx]
            return jnp.dot(u, w_ref[:, lo:lo + MIX_SPLITS[idx]], preferred_element_type=F32)

        def store_groups(out_ref, y):
            width = out_ref.shape[-1]
            for g in range(out_ref.shape[0]):
                out_ref[g, rows, :] = y[:, g * width:(g + 1) * width].astype(out_ref.dtype)

        def rotary(y, cos, sin, out_ref):
            for h in range(RET_HEADS):
                yh = y[:, h * RET_QK_DIM:(h + 1) * RET_QK_DIM]
                out_ref[h, rows, :] = (yh * cos + pltpu.roll(yh, RET_QK_DIM // 2, 1) * sin).astype(out_ref.dtype)

        rotary(proj(0), rope_ref[0, rows, :], rope_ref[1, rows, :], rq_ref)
        rotary(proj(1), rope_ref[2, rows, :], rope_ref[3, rows, :], rk_ref)
        store_groups(rv_ref, proj(2))
        store_groups(rg_ref, _silu(proj(3)))
        store_groups(nq_ref, proj(4) * (LOG2E * NA_HEAD_DIM ** -0.5))
        store_groups(nk_ref, proj(5))
        store_groups(nv_ref, proj(6))
        gr_ref[rows, :] = proj(7).astype(gr_ref.dtype)
        gn_ref[rows, :] = proj(8).astype(gn_ref.dtype)

    u = pre(0)
    for j in range(n_sub):
        u_next = pre(j + 1) if j + 1 < n_sub else None
        project(j, u)
        u = u_next


def _rope_tables(seq):
    half = RET_QK_DIM // 2
    pos = jnp.arange(seq, dtype=F32)
    inv = 1.0 / (ROPE_BASE ** jnp.linspace(0.0, 1.0, half, dtype=F32))
    ang = pos[:, None] * inv[None, :]
    cos, sin = jnp.cos(ang), jnp.sin(ang)
    cos2 = jnp.concatenate([cos, cos], axis=-1)
    sin2 = jnp.concatenate([-sin, sin], axis=-1)
    ks = RET_QK_DIM ** -0.5
    return jnp.stack([cos2, sin2, cos2 * ks, sin2 * ks])


def _mix_in(x, gain, w, seq):
    t = x.shape[0]
    tm = TOKEN_TILE
    tiles_per_seq = seq // tm
    out_dtypes = (BF16, BF16, BF16, BF16, BF16, F32, F32, BF16, BF16)
    out_groups = (RET_HEADS,) * 4 + (NA_SLABS,) * 3 + (1, 1)
    return pl.pallas_call(
        _mixin_kernel,
        grid=(t // tm,),
        in_specs=[pl.BlockSpec((tm, D_MODEL), lambda i: (i, 0)),
                  _const_spec((1, D_MODEL)), _const_spec((D_MODEL, MIX_IN_W)),
                  pl.BlockSpec((4, tm, RET_QK_DIM), lambda i: (0, i % tiles_per_seq, 0))],
        out_specs=[pl.BlockSpec((tm, wd), lambda i: (i, 0)) if g == 1
                   else pl.BlockSpec((g, tm, wd // g), lambda i: (0, i, 0))
                   for wd, g in zip(MIX_SPLITS, out_groups)],
        out_shape=[jax.ShapeDtypeStruct((t, wd) if g == 1 else (g, t, wd // g), dt)
                   for wd, g, dt in zip(MIX_SPLITS, out_groups, out_dtypes)],
        compiler_params=pltpu.CompilerParams(
            dimension_semantics=("parallel",), vmem_limit_bytes=VMEM_LIMIT),
        name="mix_in",
    )(x, gain, w, _rope_tables(seq))


def _log_sigmoid(z):
    return jnp.minimum(z, 0.0) - jnp.log1p(jnp.exp(-jnp.abs(z)))


def _ret_kernel(dec_ref, q_ref, k_ref, v_ref, g_ref, o_ref, u_ref, st_ref, dm_ref, vec_ref, blk_ref):
    c = RET_BLOCK
    dk = RET_QK_DIM
    dv = RET_V_DIM
    n_heads = q_ref.shape[0]
    n_blocks = q_ref.shape[1] // c
    tn = (((0,), (0,)), ((), ()))
    nt = (((1,), (1,)), ((), ()))

    @pl.when(pl.program_id(0) == 0)
    def _decay_tables():
        def log_gamma(z, shape):
            return _log_sigmoid(jnp.full(shape, z, F32))

        ri = lax.broadcasted_iota(jnp.int32, (c, c), 0).astype(F32)
        ci = lax.broadcasted_iota(jnp.int32, (c, c), 1).astype(F32)
        diff = ri - ci
        r = lax.broadcasted_iota(jnp.int32, (c, dk), 0).astype(F32)
        for h in range(n_heads):
            zf = dec_ref[0, h]
            zb = dec_ref[1, h]
            dm_ref[h] = jnp.where(diff >= 0, jnp.exp(log_gamma(zf, (c, c)) * jnp.maximum(diff, 0.0)),
                                  jnp.exp(log_gamma(zb, (c, c)) * jnp.maximum(-diff, 0.0)))
            lgf = log_gamma(zf, (c, dk))
            lgb = log_gamma(zb, (c, dk))
            vec_ref[h, 0] = jnp.exp(lgf * (r + 1.0)).astype(BF16)
            vec_ref[h, 1] = jnp.exp(lgb * (c - r)).astype(BF16)
            vec_ref[h, 2] = jnp.exp(lgf * (c - 1.0 - r)).astype(BF16)
            vec_ref[h, 3] = jnp.exp(lgb * r).astype(BF16)
            blk_ref[h, 0] = jnp.exp(log_gamma(zf, (dk, dv)) * c)
            blk_ref[h, 1] = jnp.exp(log_gamma(zb, (dk, dv)) * c)

    def rows(n):
        return slice(n * c, (n + 1) * c)

    def updates(h):
        for n in range(n_blocks):
            k = k_ref[h, rows(n), :]
            k_cat = jnp.concatenate([k * vec_ref[h, 2], k * vec_ref[h, 3]], axis=1)
            u_ref[h % 2, n] = lax.dot_general(k_cat, v_ref[h, rows(n), :], tn, preferred_element_type=F32)

    def scans(h):
        sf = jnp.zeros((dk, dv), F32)
        for n in range(n_blocks):
            st_ref[h % 2, n, 0:dk, :] = sf.astype(BF16)
            if n + 1 < n_blocks:
                sf = blk_ref[h, 0] * sf + u_ref[h % 2, n, 0:dk, :]
        sb = jnp.zeros((dk, dv), F32)
        for n in reversed(range(n_blocks)):
            st_ref[h % 2, n, dk:2 * dk, :] = sb.astype(BF16)
            if n > 0:
                sb = blk_ref[h, 1] * sb + u_ref[h % 2, n, dk:2 * dk, :]

    def outputs(h):
        def scores(n):
            return lax.dot_general(q_ref[h, rows(n), :], k_ref[h, rows(n), :], nt,
                                   preferred_element_type=F32)

        s_next = scores(0)
        for n in range(n_blocks):
            s_cur = s_next
            if n + 1 < n_blocks:
                s_next = scores(n + 1)
            p = (s_cur * dm_ref[h]).astype(BF16)
            q = q_ref[h, rows(n), :]
            lhs = jnp.concatenate([p, q * vec_ref[h, 0], q * vec_ref[h, 1]], axis=1)
            rhs = jnp.concatenate([v_ref[h, rows(n), :], st_ref[h % 2, n]], axis=0)
            y = jnp.dot(lhs, rhs, preferred_element_type=F32)
            yn = y * lax.rsqrt(jnp.mean(y * y, axis=-1, keepdims=True) + RMS_EPS)
            o_ref[h, rows(n), :] = (g_ref[h, rows(n), :].astype(F32) * yn).astype(o_ref.dtype)

    updates(0)
    for h in range(n_heads):
        if h + 1 < n_heads:
            updates(h + 1)
        scans(h)
        outputs(h)


def _retention(decay_logits, rq, rk, rv, rg):
    n_heads, b, s, _ = rq.shape
    c = RET_BLOCK
    n_blocks = s // c
    qk_spec = pl.BlockSpec((n_heads, None, s, RET_QK_DIM), lambda i, dec: (0, i, 0, 0))
    v_spec = pl.BlockSpec((n_heads, None, s, RET_V_DIM), lambda i, dec: (0, i, 0, 0))
    return pl.pallas_call(
        _ret_kernel,
        grid_spec=pltpu.PrefetchScalarGridSpec(
            num_scalar_prefetch=1,
            grid=(b,),
            in_specs=[qk_spec, qk_spec, v_spec, v_spec],
            out_specs=v_spec,
            scratch_shapes=[pltpu.VMEM((2, n_blocks, 2 * RET_QK_DIM, RET_V_DIM), F32),
                            pltpu.VMEM((2, n_blocks, 2 * RET_QK_DIM, RET_V_DIM), BF16),
                            pltpu.VMEM((n_heads, c, c), F32),
                            pltpu.VMEM((n_heads, 4, c, RET_QK_DIM), BF16),
                            pltpu.VMEM((n_heads, 2, RET_QK_DIM, RET_V_DIM), F32)],
        ),
        out_shape=jax.ShapeDtypeStruct((n_heads, b, s, RET_V_DIM), BF16),
        compiler_params=pltpu.CompilerParams(
            dimension_semantics=("arbitrary",), vmem_limit_bytes=VMEM_LIMIT),
        name="retention",
    )(decay_logits, rq, rk, rv, rg)


def _na_blocks(extent, q_size, k_size, window):
    n = extent // q_size
    out = []
    for i in range(n):
        q0 = i * q_size
        k0 = min(max(q0 - window // 2, 0), extent - k_size)
        out.append((q0, k0, 0 if i == 0 else (2 if i == n - 1 else 1)))
    return tuple(out)


NA_ROWS = 32
_NA_ROW_BLOCKS = _na_blocks(NA_ROWS, NA_Q_ROWS, NA_K_ROWS, NA_WIN_ROWS)
_NA_COL_BLOCKS = _na_blocks(GRID_W, NA_Q_COLS, NA_K_COLS, NA_WIN_COLS)
_NA_CONFIGS = 9


def _na_bias_kernel(rel_ref, out_ref, col_ref):
    nk = NA_K_ROWS * NA_K_COLS
    base = pl.program_id(0) * (NA_REL_ROWS * NA_REL_COLS)
    lane = lax.broadcasted_iota(jnp.int32, (NA_Q_COLS, nk), 1)
    kr = lane // NA_K_COLS
    col_cfgs = (_NA_COL_BLOCKS[0], _NA_COL_BLOCKS[1], _NA_COL_BLOCKS[3])
    row_cfgs = (_NA_ROW_BLOCKS[0], _NA_ROW_BLOCKS[1], _NA_ROW_BLOCKS[3])
    reps = nk // NA_LANES
    kc_one = lax.broadcasted_iota(jnp.int32, (NA_Q_COLS, NA_LANES), 1) % NA_K_COLS
    qc_one = lax.broadcasted_iota(jnp.int32, (NA_Q_COLS, NA_LANES), 0)
    for b, (c0, kc0, _) in enumerate(col_cfgs):
        kc_abs = kc_one + kc0
        qc_abs = qc_one + c0
        rel_col = jnp.clip(kc_abs - qc_abs + NA_WIN_COLS - 1, 0, NA_REL_COLS - 1)
        win = jnp.clip(qc_abs - NA_WIN_COLS // 2, 0, GRID_W - NA_WIN_COLS)
        in_win = (kc_abs >= win) & (kc_abs < win + NA_WIN_COLS)

        def rel_row_body(dr, carry, rel_col=rel_col, in_win=in_win, b=b):
            vals = jnp.zeros((NA_Q_COLS, NA_LANES), F32)
            for dc in range(NA_REL_COLS):
                vals = jnp.where(rel_col == dc, rel_ref[base + dr * NA_REL_COLS + dc], vals)
            vals = jnp.where(in_win, vals * LOG2E, NEG_INF)
            col_ref[b * NA_REL_ROWS + dr] = jnp.concatenate([vals] * reps, axis=1)
            return carry

        lax.fori_loop(0, NA_REL_ROWS, rel_row_body, 0)

    for a, (r0, ks, _) in enumerate(row_cfgs):
        kr_abs = kr + ks
        for qr in range(NA_Q_ROWS):
            r_abs = r0 + qr
            rs = min(max(r_abs - NA_WIN_ROWS // 2, 0), NA_ROWS - NA_WIN_ROWS)
            hits = [kr_abs == rs + j for j in range(NA_WIN_ROWS)]
            for b in range(3):
                tile = jnp.full((NA_Q_COLS, nk), NEG_INF, F32)
                for j in range(NA_WIN_ROWS):
                    dr = rs + j - r_abs + NA_WIN_ROWS - 1
                    tile = jnp.where(hits[j], col_ref[b * NA_REL_ROWS + dr], tile)
                out_ref[0, a * 3 + b, qr * NA_Q_COLS:(qr + 1) * NA_Q_COLS, :] = tile


def _na_bias_tables(rel_bias):
    nq = NA_Q_ROWS * NA_Q_COLS
    nk = NA_K_ROWS * NA_K_COLS
    return pl.pallas_call(
        _na_bias_kernel,
        grid_spec=pltpu.PrefetchScalarGridSpec(
            num_scalar_prefetch=1,
            grid=(NA_HEADS,),
            in_specs=[],
            out_specs=pl.BlockSpec((1, _NA_CONFIGS, nq, nk), lambda h, rel: (h, 0, 0, 0)),
            scratch_shapes=[pltpu.VMEM((3 * NA_REL_ROWS, NA_Q_COLS, nk), F32)],
        ),
        out_shape=jax.ShapeDtypeStruct((NA_HEADS, _NA_CONFIGS, nq, nk), F32),
        compiler_params=pltpu.CompilerParams(dimension_semantics=("parallel",)),
        name="na_bias",
    )(rel_bias.astype(F32).reshape(-1))


def _na_kernel(q_ref, k_ref, v_ref, bias_ref, o_ref, kb_ref, vb_ref):
    nq = NA_Q_ROWS * NA_Q_COLS
    nk = NA_K_ROWS * NA_K_COLS
    shift = NA_WIN_COLS // 2
    kept = GRID_W - 2 * shift
    for src, dst in ((k_ref, kb_ref), (v_ref, vb_ref)):
        dst[0] = src[...].astype(BF16)
        dst[1, :, 0:kept, :] = src[:, shift:shift + kept, :].astype(BF16)

    def window(ref, block):
        (_, ks, _), (_, kc0, _) = block
        which, c = (0, kc0) if kc0 % (2 * shift) == 0 else (1, kc0 - shift)
        assert c % (2 * shift) == 0 and (which == 0 or c + NA_K_COLS <= kept)
        return ref[which, ks:ks + NA_K_ROWS, c:c + NA_K_COLS, :].reshape(nk, NA_LANES)

    first = lax.broadcasted_iota(jnp.int32, (nq, NA_LANES), 1) < NA_HEAD_DIM
    ones = jnp.ones((nk, NA_LANES), BF16)
    nt = (((1,), (1,)), ((), ()))
    blocks = [(rb, cb) for rb in _NA_ROW_BLOCKS for cb in _NA_COL_BLOCKS]

    def scores(block):
        (r0, _, rcfg), (c0, _, ccfg) = block
        q = q_ref[r0:r0 + NA_Q_ROWS, c0:c0 + NA_Q_COLS, :].reshape(nq, NA_LANES)
        zero = jnp.zeros_like(q)
        q2 = jnp.concatenate([jnp.where(first, q, zero), jnp.where(first, zero, q)], axis=0)
        cfg = rcfg * 3 + ccfg
        bias = jnp.concatenate([bias_ref[0, cfg], bias_ref[1, cfg]], axis=0)
        return lax.dot_general(q2, window(kb_ref, block), nt, preferred_element_type=F32) + bias

    def attend(block, s):
        m = jnp.max(s, axis=-1, keepdims=True)
        p = jnp.exp2(s - m).astype(BF16)
        o = jnp.dot(p, jnp.concatenate([window(vb_ref, block), ones], axis=1),
                    preferred_element_type=F32)
        o = o[:, :NA_LANES] / o[:, NA_LANES:]
        return jnp.where(first, o[:nq], o[nq:])

    s_next = scores(blocks[0])
    for i, blk in enumerate(blocks):
        s_cur = s_next
        if i + 1 < len(blocks):
            s_next = scores(blocks[i + 1])
        (r0, _, _), (c0, _, _) = blk
        o_ref[r0:r0 + NA_Q_ROWS, c0:c0 + NA_Q_COLS, :] = (
            attend(blk, s_cur).reshape(NA_Q_ROWS, NA_Q_COLS, NA_LANES).astype(o_ref.dtype))


def _natten(nq, nk, nv, bias_tabs):
    b, rows = nq.shape[1], nq.shape[2]
    slab = pl.BlockSpec((None, None, rows, GRID_W, NA_LANES), lambda j, i: (j, i, 0, 0, 0))
    nqk = NA_Q_ROWS * NA_Q_COLS
    nkk = NA_K_ROWS * NA_K_COLS
    return pl.pallas_call(
        _na_kernel,
        grid=(NA_SLABS, b),
        in_specs=[slab, slab, slab,
                  pl.BlockSpec((2, _NA_CONFIGS, nqk, nkk), lambda j, i: (j, 0, 0, 0))],
        out_specs=slab,
        out_shape=jax.ShapeDtypeStruct((NA_SLABS, b, rows, GRID_W, NA_LANES), BF16),
        scratch_shapes=[pltpu.VMEM((2, rows, GRID_W, NA_LANES), BF16),
                        pltpu.VMEM((2, rows, GRID_W, NA_LANES), BF16)],
        compiler_params=pltpu.CompilerParams(
            dimension_semantics=("parallel", "parallel"), vmem_limit_bytes=VMEM_LIMIT),
        name="natten",
    )(nq, nk, nv, bias_tabs)


def _mixout_kernel(x_ref, yr_ref, na_ref, gr_ref, gn_ref, wr_ref, wn_ref, wo_ref, gain_ref, o_ref):
    n_sub = x_ref.shape[0] // SUB_TILE

    def rows(j):
        return slice(j * SUB_TILE, (j + 1) * SUB_TILE)

    def branches(j):
        def gathered(ref):
            return jnp.concatenate([ref[g, rows(j), :] for g in range(ref.shape[0])], axis=1)

        y_ret = jnp.dot(gathered(yr_ref), wr_ref[...], preferred_element_type=F32)
        y_na = jnp.dot(gathered(na_ref), wn_ref[...], preferred_element_type=F32)
        merged = (jax.nn.sigmoid(gr_ref[rows(j), :].astype(F32)) * y_ret
                  + jax.nn.sigmoid(gn_ref[rows(j), :].astype(F32)) * y_na)
        return merged.astype(BF16)

    def finish(j, merged):
        m = jnp.dot(merged, wo_ref[...], preferred_element_type=F32)
        o_ref[rows(j), :] = x_ref[rows(j), :] + _rms(m, gain_ref[...])

    merged = branches(0)
    for j in range(n_sub):
        merged_next = branches(j + 1) if j + 1 < n_sub else None
        finish(j, merged)
        merged = merged_next


def _mix_out(x, y_ret, y_na, g_ret, g_na, w_ret, w_na, w_out, gain):
    t = x.shape[0]
    tm = FFN_TILE

    def row(width):
        return pl.BlockSpec((tm, width), lambda i: (i, 0))

    return pl.pallas_call(
        _mixout_kernel,
        grid=(t // tm,),
        in_specs=[row(D_MODEL),
                  pl.BlockSpec((RET_HEADS, tm, RET_V_DIM), lambda i: (0, i, 0)),
                  pl.BlockSpec((NA_SLABS, tm, NA_LANES), lambda i: (0, i, 0)),
                  row(D_MODEL), row(D_MODEL),
                  _const_spec((RET_V_W, D_MODEL)), _const_spec((NA_W, D_MODEL)),
                  _const_spec((D_MODEL, D_MODEL)), _const_spec((1, D_MODEL))],
        out_specs=row(D_MODEL),
        out_shape=jax.ShapeDtypeStruct((t, D_MODEL), F32),
        compiler_params=pltpu.CompilerParams(
            dimension_semantics=("parallel",), vmem_limit_bytes=VMEM_LIMIT),
        name="mix_out",
    )(x, y_ret, y_na, g_ret, g_na, w_ret, w_na, w_out, gain)


def kernel(x, ffn1_pre_norm, ffn1_w_in, ffn1_w_out, ffn1_post_norm, mix_pre_norm, w_mix_in,
           ret_decay_fwd, ret_decay_bwd, na_rel_bias, w_ret_out, w_na_out, w_mix_out, mix_post_norm,
           ffn2_pre_norm, ffn2_w_in, ffn2_w_out, ffn2_post_norm):
    b, s, d = x.shape
    assert d == D_MODEL and s % GRID_W == 0 and s % RET_BLOCK == 0 and (b * s) % TOKEN_TILE == 0
    assert s % TOKEN_TILE == 0
    rows = s // GRID_W
    assert rows == NA_ROWS, "the neighbourhood-attention tiling is laid out for a 32 x 64 token grid"
    t = b * s
    xt = x.reshape(t, d)
    for l in range(ffn1_w_in.shape[0]):
        gain = lambda g: g[l].reshape(1, D_MODEL).astype(F32)
        wt = lambda w: w[l].astype(BF16)
        xt = _ffn(xt, gain(ffn1_pre_norm), wt(ffn1_w_in), wt(ffn1_w_out), gain(ffn1_post_norm))
        rq, rk, rv, rg, nq, nk, nv, g_ret, g_na = _mix_in(xt, gain(mix_pre_norm), wt(w_mix_in), s)
        decay = jnp.stack([ret_decay_fwd[l], ret_decay_bwd[l]]).astype(F32)
        seq = lambda a: a.reshape(a.shape[0], b, s, a.shape[-1])
        y_ret = _retention(decay, seq(rq), seq(rk), seq(rv), seq(rg)).reshape(RET_HEADS, t, RET_V_DIM)
        grid5 = lambda a: a.reshape(NA_SLABS, b, rows, GRID_W, NA_LANES)
        y_na = _natten(grid5(nq), grid5(nk), grid5(nv), _na_bias_tables(na_rel_bias[l]))
        xt = _mix_out(xt, y_ret, y_na.reshape(NA_SLABS, t, NA_LANES), g_ret, g_na,
                      wt(w_ret_out), wt(w_na_out), wt(w_mix_out), gain(mix_post_norm))
        xt = _ffn(xt, gain(ffn2_pre_norm), wt(ffn2_w_in), wt(ffn2_w_out), gain(ffn2_post_norm))
    return xt.reshape(b, s, d)
```

```python
import functools

import numpy as np
import jax
import jax.numpy as jnp
from jax import lax
from jax.experimental import pallas as pl
from jax.experimental.pallas import tpu as pltpu

F32 = jnp.float32
BF16 = jnp.bfloat16

D_MODEL = 1024
D_FF = 2816
RMS_EPS = 1e-6
ROPE_BASE = 10000.0
NEG_INF = -1e30
LOG2E = 1.4426950408889634

RET_HEADS = 4
RET_QK_DIM = 128
RET_V_DIM = 256
RET_QK_W = RET_HEADS * RET_QK_DIM
RET_V_W = RET_HEADS * RET_V_DIM
RET_BLOCK = 256

NA_HEADS = 8
NA_HEAD_DIM = 64
NA_W = NA_HEADS * NA_HEAD_DIM
GRID_W = 64
NA_WIN_ROWS = 8
NA_WIN_COLS = 16
NA_REL_ROWS = 2 * NA_WIN_ROWS - 1
NA_REL_COLS = 2 * NA_WIN_COLS - 1
NA_Q_ROWS = 8
NA_Q_COLS = 16
NA_K_ROWS = 16
NA_K_COLS = 32
NA_LANES = 128
NA_SLABS = NA_W // NA_LANES
NA_SLABS_PER_STEP = 2

MIX_SPLITS = (RET_QK_W, RET_QK_W, RET_V_W, RET_V_W, NA_W, NA_W, NA_W, D_MODEL, D_MODEL)
MIX_OFFS = tuple(int(v) for v in np.cumsum((0,) + MIX_SPLITS[:-1]))
MIX_IN_W = sum(MIX_SPLITS)

TOKEN_TILE = 512
FFN_TILE = 1024
SUB_TILE = 256
FF_CHUNK = 256
VMEM_LIMIT = 56 * 1024 * 1024


def _rms(x, gain):
    ms = jnp.mean(x * x, axis=-1, keepdims=True)
    return x * lax.rsqrt(ms + RMS_EPS) * gain


def _silu(x):
    return x * jax.nn.sigmoid(x)


def _const_spec(shape):
    nd = len(shape)
    return pl.BlockSpec(shape, lambda *_: (0,) * nd, pipeline_mode=pl.Buffered(1))


def _ffn_kernel(x_ref, gpre_ref, win_ref, wout_ref, gpost_ref, o_ref, act_ref):
    n_sub = x_ref.shape[0] // SUB_TILE

    def rows(j):
        return slice(j * SUB_TILE, (j + 1) * SUB_TILE)

    def pre(j):
        return _rms(x_ref[rows(j), :], gpre_ref[...]).astype(BF16)

    def up(j, xn):
        for c in range(D_FF // FF_CHUNK):
            lo = c * FF_CHUNK
            g = jnp.dot(xn, win_ref[:, lo:lo + FF_CHUNK], preferred_element_type=F32)
            u = jnp.dot(xn, win_ref[:, D_FF + lo:D_FF + lo + FF_CHUNK], preferred_element_type=F32)
            act_ref[rows(j), lo:lo + FF_CHUNK] = (_silu(g) * u).astype(BF16)

    def down(j):
        h = jnp.dot(act_ref[rows(j), :], wout_ref[...], preferred_element_type=F32)
        o_ref[rows(j), :] = x_ref[rows(j), :] + 0.5 * _rms(h, gpost_ref[...])

    xn = pre(0)
    for j in range(n_sub):
        xn_next = pre(j + 1) if j + 1 < n_sub else None
        up(j, xn)
        if j >= 1:
            down(j - 1)
        xn = xn_next
    down(n_sub - 1)


def _ffn(x, gpre, w_in, w_out, gpost):
    t = x.shape[0]
    tm = FFN_TILE
    row = pl.BlockSpec((tm, D_MODEL), lambda i: (i, 0))
    return pl.pallas_call(
        _ffn_kernel,
        grid=(t // tm,),
        in_specs=[row, _const_spec((1, D_MODEL)), _const_spec((D_MODEL, 2 * D_FF)),
                  _const_spec((D_FF, D_MODEL)), _const_spec((1, D_MODEL))],
        out_specs=row,
        out_shape=jax.ShapeDtypeStruct((t, D_MODEL), F32),
        scratch_shapes=[pltpu.VMEM((tm, D_FF), BF16)],
        compiler_params=pltpu.CompilerParams(
            dimension_semantics=("parallel",), vmem_limit_bytes=VMEM_LIMIT),
        name="ffn",
    )(x, gpre, w_in, w_out, gpost)


def _mixin_kernel(x_ref, gain_ref, w_ref, rope_ref,
                  rq_ref, rk_ref, rv_ref, rg_ref, nq_ref, nk_ref, nv_ref, gr_ref, gn_ref):
    n_sub = x_ref.shape[0] // SUB_TILE

    def pre(j):
        rows = slice(j * SUB_TILE, (j + 1) * SUB_TILE)
        return _rms(x_ref[rows, :], gain_ref[...]).astype(BF16)

    def project(j, u):
        rows = slice(j * SUB_TILE, (j + 1) * SUB_TILE)

        def proj(idx):
            lo = MIX_OFFS[idx]
            return jnp.dot(u, w_ref[:, lo:lo + MIX_SPLITS[idx]], preferred_element_type=F32)

        def store_groups(out_ref, y):
            width = out_ref.shape[-1]
            for g in range(out_ref.shape[0]):
                out_ref[g, rows, :] = y[:, g * width:(g + 1) * width].astype(out_ref.dtype)

        def rotary(y, cos, sin, out_ref):
            for h in range(RET_HEADS):
                yh = y[:, h * RET_QK_DIM:(h + 1) * RET_QK_DIM]
                out_ref[h, rows, :] = (yh * cos + pltpu.roll(yh, RET_QK_DIM // 2, 1) * sin).astype(out_ref.dtype)

        rotary(proj(0), rope_ref[0, rows, :], rope_ref[1, rows, :], rq_ref)
        rotary(proj(1), rope_ref[2, rows, :], rope_ref[3, rows, :], rk_ref)
        store_groups(rv_ref, proj(2))
        store_groups(rg_ref, _silu(proj(3)))
        store_groups(nq_ref, proj(4) * (LOG2E * NA_HEAD_DIM ** -0.5))
        store_groups(nk_ref, proj(5))
        store_groups(nv_ref, proj(6))
        gr_ref[rows, :] = proj(7).astype(gr_ref.dtype)
        gn_ref[rows, :] = proj(8).astype(gn_ref.dtype)

    u = pre(0)
    for j in range(n_sub):
        u_next = pre(j + 1) if j + 1 < n_sub else None
        project(j, u)
        u = u_next


def _rope_tables(seq):
    half = RET_QK_DIM // 2
    pos = jnp.arange(seq, dtype=F32)
    inv = 1.0 / (ROPE_BASE ** jnp.linspace(0.0, 1.0, half, dtype=F32))
    ang = pos[:, None] * inv[None, :]
    cos, sin = jnp.cos(ang), jnp.sin(ang)
    cos2 = jnp.concatenate([cos, cos], axis=-1)
    sin2 = jnp.concatenate([-sin, sin], axis=-1)
    ks = RET_QK_DIM ** -0.5
    return jnp.stack([cos2, sin2, cos2 * ks, sin2 * ks])


def _mix_in(x, gain, w, seq):
    t = x.shape[0]
    tm = TOKEN_TILE
    tiles_per_seq = seq // tm
    out_dtypes = (BF16, BF16, BF16, BF16, BF16, F32, F32, BF16, BF16)
    out_groups = (RET_HEADS,) * 4 + (NA_SLABS,) * 3 + (1, 1)
    return pl.pallas_call(
        _mixin_kernel,
        grid=(t // tm,),
        in_specs=[pl.BlockSpec((tm, D_MODEL), lambda i: (i, 0)),
                  _const_spec((1, D_MODEL)), _const_spec((D_MODEL, MIX_IN_W)),
                  pl.BlockSpec((4, tm, RET_QK_DIM), lambda i: (0, i % tiles_per_seq, 0))],
        out_specs=[pl.BlockSpec((tm, wd), lambda i: (i, 0)) if g == 1
                   else pl.BlockSpec((g, tm, wd // g), lambda i: (0, i, 0))
                   for wd, g in zip(MIX_SPLITS, out_groups)],
        out_shape=[jax.ShapeDtypeStruct((t, wd) if g == 1 else (g, t, wd // g), dt)
                   for wd, g, dt in zip(MIX_SPLITS, out_groups, out_dtypes)],
        compiler_params=pltpu.CompilerParams(
            dimension_semantics=("parallel",), vmem_limit_bytes=VMEM_LIMIT),
        name="mix_in",
    )(x, gain, w, _rope_tables(seq))


def _log_sigmoid(z):
    return jnp.minimum(z, 0.0) - jnp.log1p(jnp.exp(-jnp.abs(z)))


def _ret_kernel(dec_ref, q_ref, k_ref, v_ref, g_ref, o_ref, u_ref, st_ref, dm_ref, vec_ref, blk_ref):
    c = RET_BLOCK
    dk = RET_QK_DIM
    dv = RET_V_DIM
    n_heads = q_ref.shape[0]
    n_blocks = q_ref.shape[1] // c
    tn = (((0,), (0,)), ((), ()))
    nt = (((1,), (1,)), ((), ()))

    @pl.when(pl.program_id(0) == 0)
    def _decay_tables():
        def log_gamma(z, shape):
            return _log_sigmoid(jnp.full(shape, z, F32))

        ri = lax.broadcasted_iota(jnp.int32, (c, c), 0).astype(F32)
        ci = lax.broadcasted_iota(jnp.int32, (c, c), 1).astype(F32)
        diff = ri - ci
        r = lax.broadcasted_iota(jnp.int32, (c, dk), 0).astype(F32)
        for h in range(n_heads):
            zf = dec_ref[0, h]
            zb = dec_ref[1, h]
            dm_ref[h] = jnp.where(diff >= 0, jnp.exp(log_gamma(zf, (c, c)) * jnp.maximum(diff, 0.0)),
                                  jnp.exp(log_gamma(zb, (c, c)) * jnp.maximum(-diff, 0.0)))
            lgf = log_gamma(zf, (c, dk))
            lgb = log_gamma(zb, (c, dk))
            vec_ref[h, 0] = jnp.exp(lgf * (r + 1.0)).astype(BF16)
            vec_ref[h, 1] = jnp.exp(lgb * (c - r)).astype(BF16)
            vec_ref[h, 2] = jnp.exp(lgf * (c - 1.0 - r)).astype(BF16)
            vec_ref[h, 3] = jnp.exp(lgb * r).astype(BF16)
            blk_ref[h, 0] = jnp.exp(log_gamma(zf, (dk, dv)) * c)
            blk_ref[h, 1] = jnp.exp(log_gamma(zb, (dk, dv)) * c)

    def rows(n):
        return slice(n * c, (n + 1) * c)

    def updates(h):
        for n in range(n_blocks):
            k = k_ref[h, rows(n), :]
            k_cat = jnp.concatenate([k * vec_ref[h, 2], k * vec_ref[h, 3]], axis=1)
            u_ref[h % 2, n] = lax.dot_general(k_cat, v_ref[h, rows(n), :], tn, preferred_element_type=F32)

    def scans(h):
        sf = jnp.zeros((dk, dv), F32)
        for n in range(n_blocks):
            st_ref[h % 2, n, 0:dk, :] = sf.astype(BF16)
            if n + 1 < n_blocks:
                sf = blk_ref[h, 0] * sf + u_ref[h % 2, n, 0:dk, :]
        sb = jnp.zeros((dk, dv), F32)
        for n in reversed(range(n_blocks)):
            st_ref[h % 2, n, dk:2 * dk, :] = sb.astype(BF16)
            if n > 0:
                sb = blk_ref[h, 1] * sb + u_ref[h % 2, n, dk:2 * dk, :]

    def outputs(h):
        def scores(n):
            return lax.dot_general(q_ref[h, rows(n), :], k_ref[h, rows(n), :], nt,
                                   preferred_element_type=F32)

        s_next = scores(0)
        for n in range(n_blocks):
            s_cur = s_next
            if n + 1 < n_blocks:
                s_next = scores(n + 1)
            p = (s_cur * dm_ref[h]).astype(BF16)
            q = q_ref[h, rows(n), :]
            lhs = jnp.concatenate([p, q * vec_ref[h, 0], q * vec_ref[h, 1]], axis=1)
            rhs = jnp.concatenate([v_ref[h, rows(n), :], st_ref[h % 2, n]], axis=0)
            y = jnp.dot(lhs, rhs, preferred_element_type=F32)
            yn = y * lax.rsqrt(jnp.mean(y * y, axis=-1, keepdims=True) + RMS_EPS)
            o_ref[h, rows(n), :] = (g_ref[h, rows(n), :].astype(F32) * yn).astype(o_ref.dtype)

    updates(0)
    for h in range(n_heads):
        if h + 1 < n_heads:
            updates(h + 1)
        scans(h)
        outputs(h)


def _retention(decay_logits, rq, rk, rv, rg):
    n_heads, b, s, _ = rq.shape
    c = RET_BLOCK
    n_blocks = s // c
    qk_spec = pl.BlockSpec((n_heads, None, s, RET_QK_DIM), lambda i, dec: (0, i, 0, 0))
    v_spec = pl.BlockSpec((n_heads, None, s, RET_V_DIM), lambda i, dec: (0, i, 0, 0))
    return pl.pallas_call(
        _ret_kernel,
        grid_spec=pltpu.PrefetchScalarGridSpec(
            num_scalar_prefetch=1,
            grid=(b,),
            in_specs=[qk_spec, qk_spec, v_spec, v_spec],
            out_specs=v_spec,
            scratch_shapes=[pltpu.VMEM((2, n_blocks, 2 * RET_QK_DIM, RET_V_DIM), F32),
                            pltpu.VMEM((2, n_blocks, 2 * RET_QK_DIM, RET_V_DIM), BF16),
                            pltpu.VMEM((n_heads, c, c), F32),
                            pltpu.VMEM((n_heads, 4, c, RET_QK_DIM), BF16),
                            pltpu.VMEM((n_heads, 2, RET_QK_DIM, RET_V_DIM), F32)],
        ),
        out_shape=jax.ShapeDtypeStruct((n_heads, b, s, RET_V_DIM), BF16),
        compiler_params=pltpu.CompilerParams(
            dimension_semantics=("arbitrary",), vmem_limit_bytes=VMEM_LIMIT),
        name="retention",
    )(decay_logits, rq, rk, rv, rg)


def _na_blocks(extent, q_size, k_size, window):
    n = extent // q_size
    out = []
    for i in range(n):
        q0 = i * q_size
        k0 = min(max(q0 - window // 2, 0), extent - k_size)
        out.append((q0, k0, 0 if i == 0 else (2 if i == n - 1 else 1)))
    return tuple(out)


NA_ROWS = 32
_NA_ROW_BLOCKS = _na_blocks(NA_ROWS, NA_Q_ROWS, NA_K_ROWS, NA_WIN_ROWS)
_NA_COL_BLOCKS = _na_blocks(GRID_W, NA_Q_COLS, NA_K_COLS, NA_WIN_COLS)
_NA_CONFIGS = 9


def _na_bias_kernel(rel_ref, out_ref, col_ref):
    nk = NA_K_ROWS * NA_K_COLS
    base = pl.program_id(0) * (NA_REL_ROWS * NA_REL_COLS)
    lane = lax.broadcasted_iota(jnp.int32, (NA_Q_COLS, nk), 1)
    kr = lane // NA_K_COLS
    col_cfgs = (_NA_COL_BLOCKS[0], _NA_COL_BLOCKS[1], _NA_COL_BLOCKS[3])
    row_cfgs = (_NA_ROW_BLOCKS[0], _NA_ROW_BLOCKS[1], _NA_ROW_BLOCKS[3])
    reps = nk // NA_LANES
    kc_one = lax.broadcasted_iota(jnp.int32, (NA_Q_COLS, NA_LANES), 1) % NA_K_COLS
    qc_one = lax.broadcasted_iota(jnp.int32, (NA_Q_COLS, NA_LANES), 0)
    for b, (c0, kc0, _) in enumerate(col_cfgs):
        kc_abs = kc_one + kc0
        qc_abs = qc_one + c0
        rel_col = jnp.clip(kc_abs - qc_abs + NA_WIN_COLS - 1, 0, NA_REL_COLS - 1)
        win = jnp.clip(qc_abs - NA_WIN_COLS // 2, 0, GRID_W - NA_WIN_COLS)
        in_win = (kc_abs >= win) & (kc_abs < win + NA_WIN_COLS)

        def rel_row_body(dr, carry, rel_col=rel_col, in_win=in_win, b=b):
            vals = jnp.zeros((NA_Q_COLS, NA_LANES), F32)
            for dc in range(NA_REL_COLS):
                vals = jnp.where(rel_col == dc, rel_ref[base + dr * NA_REL_COLS + dc], vals)
            vals = jnp.where(in_win, vals * LOG2E, NEG_INF)
            col_ref[b * NA_REL_ROWS + dr] = jnp.concatenate([vals] * reps, axis=1)
            return carry

        lax.fori_loop(0, NA_REL_ROWS, rel_row_body, 0)

    for a, (r0, ks, _) in enumerate(row_cfgs):
        kr_abs = kr + ks
        for qr in range(NA_Q_ROWS):
            r_abs = r0 + qr
            rs = min(max(r_abs - NA_WIN_ROWS // 2, 0), NA_ROWS - NA_WIN_ROWS)
            hits = [kr_abs == rs + j for j in range(NA_WIN_ROWS)]
            for b in range(3):
                tile = jnp.full((NA_Q_COLS, nk), NEG_INF, F32)
                for j in range(NA_WIN_ROWS):
                    dr = rs + j - r_abs + NA_WIN_ROWS - 1
                    tile = jnp.where(hits[j], col_ref[b * NA_REL_ROWS + dr], tile)
                out_ref[0, a * 3 + b, qr * NA_Q_COLS:(qr + 1) * NA_Q_COLS, :] = tile


def _na_bias_tables(rel_bias):
    nq = NA_Q_ROWS * NA_Q_COLS
    nk = NA_K_ROWS * NA_K_COLS
    return pl.pallas_call(
        _na_bias_kernel,
        grid_spec=pltpu.PrefetchScalarGridSpec(
            num_scalar_prefetch=1,
            grid=(NA_HEADS,),
            in_specs=[],
            out_specs=pl.BlockSpec((1, _NA_CONFIGS, nq, nk), lambda h, rel: (h, 0, 0, 0)),
            scratch_shapes=[pltpu.VMEM((3 * NA_REL_ROWS, NA_Q_COLS, nk), F32)],
        ),
        out_shape=jax.ShapeDtypeStruct((NA_HEADS, _NA_CONFIGS, nq, nk), F32),
        compiler_params=pltpu.CompilerParams(dimension_semantics=("parallel",)),
        name="na_bias",
    )(rel_bias.astype(F32).reshape(-1))


def _na_kernel(q_ref, k_ref, v_ref, bias_ref, o_ref, kb_ref, vb_ref):
    n_slabs = q_ref.shape[0]
    nq = NA_Q_ROWS * NA_Q_COLS
    nk = NA_K_ROWS * NA_K_COLS
    shift = NA_WIN_COLS // 2
    kept = GRID_W - 2 * shift
    for sl in range(n_slabs):
        for src, dst in ((k_ref, kb_ref), (v_ref, vb_ref)):
            dst[sl, 0] = src[sl].astype(BF16)
            dst[sl, 1, :, 0:kept, :] = src[sl, :, shift:shift + kept, :].astype(BF16)

    def window(ref, sl, block):
        (_, ks, _), (_, kc0, _) = block
        which, c = (0, kc0) if kc0 % (2 * shift) == 0 else (1, kc0 - shift)
        assert c % (2 * shift) == 0 and (which == 0 or c + NA_K_COLS <= kept)
        return ref[sl, which, ks:ks + NA_K_ROWS, c:c + NA_K_COLS, :].reshape(nk, NA_LANES)

    first = lax.broadcasted_iota(jnp.int32, (nq, NA_LANES), 1) < NA_HEAD_DIM
    ones = jnp.ones((nk, NA_LANES), BF16)
    nt = (((1,), (1,)), ((), ()))
    work = [(sl, (rb, cb)) for sl in range(n_slabs) for rb in _NA_ROW_BLOCKS for cb in _NA_COL_BLOCKS]

    def scores(sl, block):
        (r0, _, rcfg), (c0, _, ccfg) = block
        q = q_ref[sl, r0:r0 + NA_Q_ROWS, c0:c0 + NA_Q_COLS, :].reshape(nq, NA_LANES)
        zero = jnp.zeros_like(q)
        q2 = jnp.concatenate([jnp.where(first, q, zero), jnp.where(first, zero, q)], axis=0)
        cfg = rcfg * 3 + ccfg
        bias = jnp.concatenate([bias_ref[2 * sl, cfg], bias_ref[2 * sl + 1, cfg]], axis=0)
        return lax.dot_general(q2, window(kb_ref, sl, block), nt, preferred_element_type=F32) + bias

    def attend(sl, block, s):
        m = jnp.max(s, axis=-1, keepdims=True)
        p = jnp.exp2(s - m).astype(BF16)
        o = jnp.dot(p, jnp.concatenate([window(vb_ref, sl, block), ones], axis=1),
                    preferred_element_type=F32)
        o = o[:, :NA_LANES] / o[:, NA_LANES:]
        return jnp.where(first, o[:nq], o[nq:])

    s_next = scores(*work[0])
    for i, (sl, blk) in enumerate(work):
        s_cur = s_next
        if i + 1 < len(work):
            s_next = scores(*work[i + 1])
        (r0, _, _), (c0, _, _) = blk
        o_ref[sl, r0:r0 + NA_Q_ROWS, c0:c0 + NA_Q_COLS, :] = (
            attend(sl, blk, s_cur).reshape(NA_Q_ROWS, NA_Q_COLS, NA_LANES).astype(o_ref.dtype))


def _natten(nq, nk, nv, bias_tabs):
    b, rows = nq.shape[1], nq.shape[2]
    per = NA_SLABS_PER_STEP
    slab = pl.BlockSpec((per, None, rows, GRID_W, NA_LANES), lambda j, i: (j, i, 0, 0, 0))
    nqk = NA_Q_ROWS * NA_Q_COLS
    nkk = NA_K_ROWS * NA_K_COLS
    return pl.pallas_call(
        _na_kernel,
        grid=(NA_SLABS // per, b),
        in_specs=[slab, slab, slab,
                  pl.BlockSpec((2 * per, _NA_CONFIGS, nqk, nkk), lambda j, i: (j, 0, 0, 0),
                               pipeline_mode=pl.Buffered(1))],
        out_specs=slab,
        out_shape=jax.ShapeDtypeStruct((NA_SLABS, b, rows, GRID_W, NA_LANES), BF16),
        scratch_shapes=[pltpu.VMEM((per, 2, rows, GRID_W, NA_LANES), BF16),
                        pltpu.VMEM((per, 2, rows, GRID_W, NA_LANES), BF16)],
        compiler_params=pltpu.CompilerParams(
            dimension_semantics=("parallel", "parallel"), vmem_limit_bytes=VMEM_LIMIT),
        name="natten",
    )(nq, nk, nv, bias_tabs)


def _mixout_kernel(x_ref, yr_ref, na_ref, gr_ref, gn_ref, wr_ref, wn_ref, wo_ref, gain_ref, o_ref):
    n_sub = x_ref.shape[0] // SUB_TILE

    def rows(j):
        return slice(j * SUB_TILE, (j + 1) * SUB_TILE)

    def branches(j):
        def gathered(ref):
            return jnp.concatenate([ref[g, rows(j), :] for g in range(ref.shape[0])], axis=1)

        y_ret = jnp.dot(gathered(yr_ref), wr_ref[...], preferred_element_type=F32)
        y_na = jnp.dot(gathered(na_ref), wn_ref[...], preferred_element_type=F32)
        merged = (jax.nn.sigmoid(gr_ref[rows(j), :].astype(F32)) * y_ret
                  + jax.nn.sigmoid(gn_ref[rows(j), :].astype(F32)) * y_na)
        return merged.astype(BF16)

    def finish(j, merged):
        m = jnp.dot(merged, wo_ref[...], preferred_element_type=F32)
        o_ref[rows(j), :] = x_ref[rows(j), :] + _rms(m, gain_ref[...])

    merged = branches(0)
    for j in range(n_sub):
        merged_next = branches(j + 1) if j + 1 < n_sub else None
        finish(j, merged)
        merged = merged_next


def _mix_out(x, y_ret, y_na, g_ret, g_na, w_ret, w_na, w_out, gain):
    t = x.shape[0]
    tm = FFN_TILE

    def row(width):
        return pl.BlockSpec((tm, width), lambda i: (i, 0))

    return pl.pallas_call(
        _mixout_kernel,
        grid=(t // tm,),
        in_specs=[row(D_MODEL),
                  pl.BlockSpec((RET_HEADS, tm, RET_V_DIM), lambda i: (0, i, 0)),
                  pl.BlockSpec((NA_SLABS, tm, NA_LANES), lambda i: (0, i, 0)),
                  row(D_MODEL), row(D_MODEL),
                  _const_spec((RET_V_W, D_MODEL)), _const_spec((NA_W, D_MODEL)),
                  _const_spec((D_MODEL, D_MODEL)), _const_spec((1, D_MODEL))],
        out_specs=row(D_MODEL),
        out_shape=jax.ShapeDtypeStruct((t, D_MODEL), F32),
        compiler_params=pltpu.CompilerParams(
            dimension_semantics=("parallel",), vmem_limit_bytes=VMEM_LIMIT),
        name="mix_out",
    )(x, y_ret, y_na, g_ret, g_na, w_ret, w_na, w_out, gain)


def kernel(x, ffn1_pre_norm, ffn1_w_in, ffn1_w_out, ffn1_post_norm, mix_pre_norm, w_mix_in,
           ret_decay_fwd, ret_decay_bwd, na_rel_bias, w_ret_out, w_na_out, w_mix_out, mix_post_norm,
           ffn2_pre_norm, ffn2_w_in, ffn2_w_out, ffn2_post_norm):
    b, s, d = x.shape
    assert d == D_MODEL and s % GRID_W == 0 and s % RET_BLOCK == 0 and (b * s) % TOKEN_TILE == 0
    assert s % TOKEN_TILE == 0
    rows = s // GRID_W
    assert rows == NA_ROWS, "the neighbourhood-attention tiling is laid out for a 32 x 64 token grid"
    t = b * s
    xt = x.reshape(t, d)
    for l in range(ffn1_w_in.shape[0]):
        gain = lambda g: g[l].reshape(1, D_MODEL).astype(F32)
        wt = lambda w: w[l].astype(BF16)
        xt = _ffn(xt, gain(ffn1_pre_norm), wt(ffn1_w_in), wt(ffn1_w_out), gain(ffn1_post_norm))
        rq, rk, rv, rg, nq, nk, nv, g_ret, g_na = _mix_in(xt, gain(mix_pre_norm), wt(w_mix_in), s)
        decay = jnp.stack([ret_decay_fwd[l], ret_decay_bwd[l]]).astype(F32)
        seq = lambda a: a.reshape(a.shape[0], b, s, a.shape[-1])
        y_ret = _retention(decay, seq(rq), seq(rk), seq(rv), seq(rg)).reshape(RET_HEADS, t, RET_V_DIM)
        grid5 = lambda a: a.reshape(NA_SLABS, b, rows, GRID_W, NA_LANES)
        y_na = _natten(grid5(nq), grid5(nk), grid5(nv), _na_bias_tables(na_rel_bias[l]))
        xt = _mix_out(xt, y_ret, y_na.reshape(NA_SLABS, t, NA_LANES), g_ret, g_na,
                      wt(w_ret_out), wt(w_na_out), wt(w_mix_out), gain(mix_post_norm))
        xt = _ffn(xt, gain(ffn2_pre_norm), wt(ffn2_w_in), wt(ffn2_w_out), gain(ffn2_post_norm))
    return xt.reshape(b, s, d)
```

```python
import functools

import numpy as np
import jax
import jax.numpy as jnp
from jax import lax
from jax.experimental import pallas as pl
from jax.experimental.pallas import tpu as pltpu

F32 = jnp.float32
BF16 = jnp.bfloat16

D_MODEL = 1024
D_FF = 2816
RMS_EPS = 1e-6
ROPE_BASE = 10000.0
NEG_INF = -1e30
LOG2E = 1.4426950408889634

RET_HEADS = 4
RET_QK_DIM = 128
RET_V_DIM = 256
RET_QK_W = RET_HEADS * RET_QK_DIM
RET_V_W = RET_HEADS * RET_V_DIM
RET_BLOCK = 256

NA_HEADS = 8
NA_HEAD_DIM = 64
NA_W = NA_HEADS * NA_HEAD_DIM
GRID_W = 64
NA_WIN_ROWS = 8
NA_WIN_COLS = 16
NA_REL_ROWS = 2 * NA_WIN_ROWS - 1
NA_REL_COLS = 2 * NA_WIN_COLS - 1
NA_Q_ROWS = 8
NA_Q_COLS = 16
NA_K_ROWS = 16
NA_K_COLS = 32
NA_LANES = 128
NA_SLABS = NA_W // NA_LANES
NA_SLABS_PER_STEP = 4

MIX_SPLITS = (RET_QK_W, RET_QK_W, RET_V_W, RET_V_W, NA_W, NA_W, NA_W, D_MODEL, D_MODEL)
MIX_OFFS = tuple(int(v) for v in np.cumsum((0,) + MIX_SPLITS[:-1]))
MIX_IN_W = sum(MIX_SPLITS)

TOKEN_TILE = 512
FFN_TILE = 1024
SUB_TILE = 256
FF_CHUNK = 256
VMEM_LIMIT = 56 * 1024 * 1024


def _rms(x, gain):
    ms = jnp.mean(x * x, axis=-1, keepdims=True)
    return x * lax.rsqrt(ms + RMS_EPS) * gain


def _silu(x):
    return x * jax.nn.sigmoid(x)


def _const_spec(shape):
    nd = len(shape)
    return pl.BlockSpec(shape, lambda *_: (0,) * nd, pipeline_mode=pl.Buffered(1))


def _ffn_kernel(x_ref, gpre_ref, win_ref, wout_ref, gpost_ref, o_ref, act_ref):
    n_sub = x_ref.shape[0] // SUB_TILE

    def rows(j):
        return slice(j * SUB_TILE, (j + 1) * SUB_TILE)

    def pre(j):
        return _rms(x_ref[rows(j), :], gpre_ref[...]).astype(BF16)

    def up(j, xn):
        for c in range(D_FF // FF_CHUNK):
            lo = c * FF_CHUNK
            g = jnp.dot(xn, win_ref[:, lo:lo + FF_CHUNK], preferred_element_type=F32)
            u = jnp.dot(xn, win_ref[:, D_FF + lo:D_FF + lo + FF_CHUNK], preferred_element_type=F32)
            act_ref[rows(j), lo:lo + FF_CHUNK] = (_silu(g) * u).astype(BF16)

    def down(j):
        h = jnp.dot(act_ref[rows(j), :], wout_ref[...], preferred_element_type=F32)
        o_ref[rows(j), :] = x_ref[rows(j), :] + 0.5 * _rms(h, gpost_ref[...])

    xn = pre(0)
    for j in range(n_sub):
        xn_next = pre(j + 1) if j + 1 < n_sub else None
        up(j, xn)
        if j >= 1:
            down(j - 1)
        xn = xn_next
    down(n_sub - 1)


def _ffn(x, gpre, w_in, w_out, gpost):
    t = x.shape[0]
    tm = FFN_TILE
    row = pl.BlockSpec((tm, D_MODEL), lambda i: (i, 0))
    return pl.pallas_call(
        _ffn_kernel,
        grid=(t // tm,),
        in_specs=[row, _const_spec((1, D_MODEL)), _const_spec((D_MODEL, 2 * D_FF)),
                  _const_spec((D_FF, D_MODEL)), _const_spec((1, D_MODEL))],
        out_specs=row,
        out_shape=jax.ShapeDtypeStruct((t, D_MODEL), F32),
        scratch_shapes=[pltpu.VMEM((tm, D_FF), BF16)],
        compiler_params=pltpu.CompilerParams(
            dimension_semantics=("parallel",), vmem_limit_bytes=VMEM_LIMIT),
        name="ffn",
    )(x, gpre, w_in, w_out, gpost)


def _mixin_kernel(x_ref, gain_ref, w_ref, rope_ref,
                  rq_ref, rk_ref, rv_ref, rg_ref, nq_ref, nk_ref, nv_ref, gr_ref, gn_ref):
    n_sub = x_ref.shape[0] // SUB_TILE

    def pre(j):
        rows = slice(j * SUB_TILE, (j + 1) * SUB_TILE)
        return _rms(x_ref[rows, :], gain_ref[...]).astype(BF16)

    def project(j, u):
        rows = slice(j * SUB_TILE, (j + 1) * SUB_TILE)

        def proj(idx):
            lo = MIX_OFFS[idx]
            return jnp.dot(u, w_ref[:, lo:lo + MIX_SPLITS[idx]], preferred_element_type=F32)

        def store_groups(out_ref, y):
            width = out_ref.shape[-1]
            for g in range(out_ref.shape[0]):
                out_ref[g, rows, :] = y[:, g * width:(g + 1) * width].astype(out_ref.dtype)

        def rotary(y, cos, sin, out_ref):
            for h in range(RET_HEADS):
                yh = y[:, h * RET_QK_DIM:(h + 1) * RET_QK_DIM]
                out_ref[h, rows, :] = (yh * cos + pltpu.roll(yh, RET_QK_DIM // 2, 1) * sin).astype(out_ref.dtype)

        rotary(proj(0), rope_ref[0, rows, :], rope_ref[1, rows, :], rq_ref)
        rotary(proj(1), rope_ref[2, rows, :], rope_ref[3, rows, :], rk_ref)
        store_groups(rv_ref, proj(2))
        store_groups(rg_ref, _silu(proj(3)))
        store_groups(nq_ref, proj(4) * (LOG2E * NA_HEAD_DIM ** -0.5))
        store_groups(nk_ref, proj(5))
        store_groups(nv_ref, proj(6))
        gr_ref[rows, :] = proj(7).astype(gr_ref.dtype)
        gn_ref[rows, :] = proj(8).astype(gn_ref.dtype)

    u = pre(0)
    for j in range(n_sub):
        u_next = pre(j + 1) if j + 1 < n_sub else None
        project(j, u)
        u = u_next


def _rope_tables(seq):
    half = RET_QK_DIM // 2
    pos = jnp.arange(seq, dtype=F32)
    inv = 1.0 / (ROPE_BASE ** jnp.linspace(0.0, 1.0, half, dtype=F32))
    ang = pos[:, None] * inv[None, :]
    cos, sin = jnp.cos(ang), jnp.sin(ang)
    cos2 = jnp.concatenate([cos, cos], axis=-1)
    sin2 = jnp.concatenate([-sin, sin], axis=-1)
    ks = RET_QK_DIM ** -0.5
    return jnp.stack([cos2, sin2, cos2 * ks, sin2 * ks])


def _mix_in(x, gain, w, seq):
    t = x.shape[0]
    tm = TOKEN_TILE
    tiles_per_seq = seq // tm
    out_dtypes = (BF16, BF16, BF16, BF16, BF16, F32, F32, BF16, BF16)
    out_groups = (RET_HEADS,) * 4 + (NA_SLABS,) * 3 + (1, 1)
    return pl.pallas_call(
        _mixin_kernel,
        grid=(t // tm,),
        in_specs=[pl.BlockSpec((tm, D_MODEL), lambda i: (i, 0)),
                  _const_spec((1, D_MODEL)), _const_spec((D_MODEL, MIX_IN_W)),
                  pl.BlockSpec((4, tm, RET_QK_DIM), lambda i: (0, i % tiles_per_seq, 0))],
        out_specs=[pl.BlockSpec((tm, wd), lambda i: (i, 0)) if g == 1
                   else pl.BlockSpec((g, tm, wd // g), lambda i: (0, i, 0))
                   for wd, g in zip(MIX_SPLITS, out_groups)],
        out_shape=[jax.ShapeDtypeStruct((t, wd) if g == 1 else (g, t, wd // g), dt)
                   for wd, g, dt in zip(MIX_SPLITS, out_groups, out_dtypes)],
        compiler_params=pltpu.CompilerParams(
            dimension_semantics=("parallel",), vmem_limit_bytes=VMEM_LIMIT),
        name="mix_in",
    )(x, gain, w, _rope_tables(seq))


def _log_sigmoid(z):
    return jnp.minimum(z, 0.0) - jnp.log1p(jnp.exp(-jnp.abs(z)))


def _ret_kernel(dec_ref, q_ref, k_ref, v_ref, g_ref, o_ref, u_ref, st_ref, dm_ref, vec_ref, blk_ref):
    c = RET_BLOCK
    dk = RET_QK_DIM
    dv = RET_V_DIM
    n_heads = q_ref.shape[0]
    n_blocks = q_ref.shape[1] // c
    tn = (((0,), (0,)), ((), ()))
    nt = (((1,), (1,)), ((), ()))

    @pl.when(pl.program_id(0) == 0)
    def _decay_tables():
        def log_gamma(z, shape):
            return _log_sigmoid(jnp.full(shape, z, F32))

        ri = lax.broadcasted_iota(jnp.int32, (c, c), 0).astype(F32)
        ci = lax.broadcasted_iota(jnp.int32, (c, c), 1).astype(F32)
        diff = ri - ci
        r = lax.broadcasted_iota(jnp.int32, (c, dk), 0).astype(F32)
        for h in range(n_heads):
            zf = dec_ref[0, h]
            zb = dec_ref[1, h]
            dm_ref[h] = jnp.where(diff >= 0, jnp.exp(log_gamma(zf, (c, c)) * jnp.maximum(diff, 0.0)),
                                  jnp.exp(log_gamma(zb, (c, c)) * jnp.maximum(-diff, 0.0)))
            lgf = log_gamma(zf, (c, dk))
            lgb = log_gamma(zb, (c, dk))
            vec_ref[h, 0] = jnp.exp(lgf * (r + 1.0)).astype(BF16)
            vec_ref[h, 1] = jnp.exp(lgb * (c - r)).astype(BF16)
            vec_ref[h, 2] = jnp.exp(lgf * (c - 1.0 - r)).astype(BF16)
            vec_ref[h, 3] = jnp.exp(lgb * r).astype(BF16)
            blk_ref[h, 0] = jnp.exp(log_gamma(zf, (dk, dv)) * c)
            blk_ref[h, 1] = jnp.exp(log_gamma(zb, (dk, dv)) * c)

    def rows(n):
        return slice(n * c, (n + 1) * c)

    def updates(h):
        for n in range(n_blocks):
            k = k_ref[h, rows(n), :]
            k_cat = jnp.concatenate([k * vec_ref[h, 2], k * vec_ref[h, 3]], axis=1)
            u_ref[h % 2, n] = lax.dot_general(k_cat, v_ref[h, rows(n), :], tn, preferred_element_type=F32)

    def scans(h):
        sf = jnp.zeros((dk, dv), F32)
        for n in range(n_blocks):
            st_ref[h % 2, n, 0:dk, :] = sf.astype(BF16)
            if n + 1 < n_blocks:
                sf = blk_ref[h, 0] * sf + u_ref[h % 2, n, 0:dk, :]
        sb = jnp.zeros((dk, dv), F32)
        for n in reversed(range(n_blocks)):
            st_ref[h % 2, n, dk:2 * dk, :] = sb.astype(BF16)
            if n > 0:
                sb = blk_ref[h, 1] * sb + u_ref[h % 2, n, dk:2 * dk, :]

    def outputs(h):
        def scores(n):
            return lax.dot_general(q_ref[h, rows(n), :], k_ref[h, rows(n), :], nt,
                                   preferred_element_type=F32)

        s_next = scores(0)
        for n in range(n_blocks):
            s_cur = s_next
            if n + 1 < n_blocks:
                s_next = scores(n + 1)
            p = (s_cur * dm_ref[h]).astype(BF16)
            q = q_ref[h, rows(n), :]
            lhs = jnp.concatenate([p, q * vec_ref[h, 0], q * vec_ref[h, 1]], axis=1)
            rhs = jnp.concatenate([v_ref[h, rows(n), :], st_ref[h % 2, n]], axis=0)
            y = jnp.dot(lhs, rhs, preferred_element_type=F32)
            yn = y * lax.rsqrt(jnp.mean(y * y, axis=-1, keepdims=True) + RMS_EPS)
            o_ref[h, rows(n), :] = (g_ref[h, rows(n), :].astype(F32) * yn).astype(o_ref.dtype)

    updates(0)
    for h in range(n_heads):
        if h + 1 < n_heads:
            updates(h + 1)
        scans(h)
        outputs(h)


def _retention(decay_logits, rq, rk, rv, rg):
    n_heads, b, s, _ = rq.shape
    c = RET_BLOCK
    n_blocks = s // c
    qk_spec = pl.BlockSpec((n_heads, None, s, RET_QK_DIM), lambda i, dec: (0, i, 0, 0))
    v_spec = pl.BlockSpec((n_heads, None, s, RET_V_DIM), lambda i, dec: (0, i, 0, 0))
    return pl.pallas_call(
        _ret_kernel,
        grid_spec=pltpu.PrefetchScalarGridSpec(
            num_scalar_prefetch=1,
            grid=(b,),
            in_specs=[qk_spec, qk_spec, v_spec, v_spec],
            out_specs=v_spec,
            scratch_shapes=[pltpu.VMEM((2, n_blocks, 2 * RET_QK_DIM, RET_V_DIM), F32),
                            pltpu.VMEM((2, n_blocks, 2 * RET_QK_DIM, RET_V_DIM), BF16),
                            pltpu.VMEM((n_heads, c, c), F32),
                            pltpu.VMEM((n_heads, 4, c, RET_QK_DIM), BF16),
                            pltpu.VMEM((n_heads, 2, RET_QK_DIM, RET_V_DIM), F32)],
        ),
        out_shape=jax.ShapeDtypeStruct((n_heads, b, s, RET_V_DIM), BF16),
        compiler_params=pltpu.CompilerParams(
            dimension_semantics=("arbitrary",), vmem_limit_bytes=VMEM_LIMIT),
        name="retention",
    )(decay_logits, rq, rk, rv, rg)


def _na_blocks(extent, q_size, k_size, window):
    n = extent // q_size
    out = []
    for i in range(n):
        q0 = i * q_size
        k0 = min(max(q0 - window // 2, 0), extent - k_size)
        out.append((q0, k0, 0 if i == 0 else (2 if i == n - 1 else 1)))
    return tuple(out)


NA_ROWS = 32
_NA_ROW_BLOCKS = _na_blocks(NA_ROWS, NA_Q_ROWS, NA_K_ROWS, NA_WIN_ROWS)
_NA_COL_BLOCKS = _na_blocks(GRID_W, NA_Q_COLS, NA_K_COLS, NA_WIN_COLS)
_NA_CONFIGS = 9


def _na_bias_kernel(rel_ref, out_ref, col_ref):
    nk = NA_K_ROWS * NA_K_COLS
    base = pl.program_id(0) * (NA_REL_ROWS * NA_REL_COLS)
    lane = lax.broadcasted_iota(jnp.int32, (NA_Q_COLS, nk), 1)
    kr = lane // NA_K_COLS
    col_cfgs = (_NA_COL_BLOCKS[0], _NA_COL_BLOCKS[1], _NA_COL_BLOCKS[3])
    row_cfgs = (_NA_ROW_BLOCKS[0], _NA_ROW_BLOCKS[1], _NA_ROW_BLOCKS[3])
    reps = nk // NA_LANES
    kc_one = lax.broadcasted_iota(jnp.int32, (NA_Q_COLS, NA_LANES), 1) % NA_K_COLS
    qc_one = lax.broadcasted_iota(jnp.int32, (NA_Q_COLS, NA_LANES), 0)
    for b, (c0, kc0, _) in enumerate(col_cfgs):
        kc_abs = kc_one + kc0
        qc_abs = qc_one + c0
        rel_col = jnp.clip(kc_abs - qc_abs + NA_WIN_COLS - 1, 0, NA_REL_COLS - 1)
        win = jnp.clip(qc_abs - NA_WIN_COLS // 2, 0, GRID_W - NA_WIN_COLS)
        in_win = (kc_abs >= win) & (kc_abs < win + NA_WIN_COLS)

        def rel_row_body(dr, carry, rel_col=rel_col, in_win=in_win, b=b):
            vals = jnp.zeros((NA_Q_COLS, NA_LANES), F32)
            for dc in range(NA_REL_COLS):
                vals = jnp.where(rel_col == dc, rel_ref[base + dr * NA_REL_COLS + dc], vals)
            vals = jnp.where(in_win, vals * LOG2E, NEG_INF)
            col_ref[b * NA_REL_ROWS + dr] = jnp.concatenate([vals] * reps, axis=1)
            return carry

        lax.fori_loop(0, NA_REL_ROWS, rel_row_body, 0)

    for a, (r0, ks, _) in enumerate(row_cfgs):
        kr_abs = kr + ks
        for qr in range(NA_Q_ROWS):
            r_abs = r0 + qr
            rs = min(max(r_abs - NA_WIN_ROWS // 2, 0), NA_ROWS - NA_WIN_ROWS)
            hits = [kr_abs == rs + j for j in range(NA_WIN_ROWS)]
            for b in range(3):
                tile = jnp.full((NA_Q_COLS, nk), NEG_INF, F32)
                for j in range(NA_WIN_ROWS):
                    dr = rs + j - r_abs + NA_WIN_ROWS - 1
                    tile = jnp.where(hits[j], col_ref[b * NA_REL_ROWS + dr], tile)
                out_ref[0, a * 3 + b, qr * NA_Q_COLS:(qr + 1) * NA_Q_COLS, :] = tile


def _na_bias_tables(rel_bias):
    nq = NA_Q_ROWS * NA_Q_COLS
    nk = NA_K_ROWS * NA_K_COLS
    return pl.pallas_call(
        _na_bias_kernel,
        grid_spec=pltpu.PrefetchScalarGridSpec(
            num_scalar_prefetch=1,
            grid=(NA_HEADS,),
            in_specs=[],
            out_specs=pl.BlockSpec((1, _NA_CONFIGS, nq, nk), lambda h, rel: (h, 0, 0, 0)),
            scratch_shapes=[pltpu.VMEM((3 * NA_REL_ROWS, NA_Q_COLS, nk), F32)],
        ),
        out_shape=jax.ShapeDtypeStruct((NA_HEADS, _NA_CONFIGS, nq, nk), F32),
        compiler_params=pltpu.CompilerParams(dimension_semantics=("parallel",)),
        name="na_bias",
    )(rel_bias.astype(F32).reshape(-1))


def _na_kernel(q_ref, k_ref, v_ref, bias_ref, o_ref, kb_ref, vb_ref):
    n_slabs = q_ref.shape[0]
    nq = NA_Q_ROWS * NA_Q_COLS
    nk = NA_K_ROWS * NA_K_COLS
    shift = NA_WIN_COLS // 2
    kept = GRID_W - 2 * shift
    for sl in range(n_slabs):
        for src, dst in ((k_ref, kb_ref), (v_ref, vb_ref)):
            dst[sl, 0] = src[sl].astype(BF16)
            dst[sl, 1, :, 0:kept, :] = src[sl, :, shift:shift + kept, :].astype(BF16)

    def window(ref, sl, block):
        (_, ks, _), (_, kc0, _) = block
        which, c = (0, kc0) if kc0 % (2 * shift) == 0 else (1, kc0 - shift)
        assert c % (2 * shift) == 0 and (which == 0 or c + NA_K_COLS <= kept)
        return ref[sl, which, ks:ks + NA_K_ROWS, c:c + NA_K_COLS, :].reshape(nk, NA_LANES)

    first = lax.broadcasted_iota(jnp.int32, (nq, NA_LANES), 1) < NA_HEAD_DIM
    ones = jnp.ones((nk, NA_LANES), BF16)
    nt = (((1,), (1,)), ((), ()))
    work = [(sl, (rb, cb)) for sl in range(n_slabs) for rb in _NA_ROW_BLOCKS for cb in _NA_COL_BLOCKS]

    def scores(sl, block):
        (r0, _, rcfg), (c0, _, ccfg) = block
        q = q_ref[sl, r0:r0 + NA_Q_ROWS, c0:c0 + NA_Q_COLS, :].reshape(nq, NA_LANES)
        zero = jnp.zeros_like(q)
        q2 = jnp.concatenate([jnp.where(first, q, zero), jnp.where(first, zero, q)], axis=0)
        cfg = rcfg * 3 + ccfg
        bias = jnp.concatenate([bias_ref[2 * sl, cfg], bias_ref[2 * sl + 1, cfg]], axis=0)
        return lax.dot_general(q2, window(kb_ref, sl, block), nt, preferred_element_type=F32) + bias

    def attend(sl, block, s):
        m = jnp.max(s, axis=-1, keepdims=True)
        p = jnp.exp2(s - m).astype(BF16)
        o = jnp.dot(p, jnp.concatenate([window(vb_ref, sl, block), ones], axis=1),
                    preferred_element_type=F32)
        o = o[:, :NA_LANES] / o[:, NA_LANES:]
        return jnp.where(first, o[:nq], o[nq:])

    s_next = scores(*work[0])
    for i, (sl, blk) in enumerate(work):
        s_cur = s_next
        if i + 1 < len(work):
            s_next = scores(*work[i + 1])
        (r0, _, _), (c0, _, _) = blk
        o_ref[sl, r0:r0 + NA_Q_ROWS, c0:c0 + NA_Q_COLS, :] = (
            attend(sl, blk, s_cur).reshape(NA_Q_ROWS, NA_Q_COLS, NA_LANES).astype(o_ref.dtype))


def _natten(nq, nk, nv, bias_tabs):
    b, rows = nq.shape[1], nq.shape[2]
    per = NA_SLABS_PER_STEP
    slab = pl.BlockSpec((per, None, rows, GRID_W, NA_LANES), lambda j, i: (j, i, 0, 0, 0))
    nqk = NA_Q_ROWS * NA_Q_COLS
    nkk = NA_K_ROWS * NA_K_COLS
    return pl.pallas_call(
        _na_kernel,
        grid=(NA_SLABS // per, b),
        in_specs=[slab, slab, slab,
                  pl.BlockSpec((2 * per, _NA_CONFIGS, nqk, nkk), lambda j, i: (j, 0, 0, 0),
                               pipeline_mode=pl.Buffered(1))],
        out_specs=slab,
        out_shape=jax.ShapeDtypeStruct((NA_SLABS, b, rows, GRID_W, NA_LANES), BF16),
        scratch_shapes=[pltpu.VMEM((per, 2, rows, GRID_W, NA_LANES), BF16),
                        pltpu.VMEM((per, 2, rows, GRID_W, NA_LANES), BF16)],
        compiler_params=pltpu.CompilerParams(
            dimension_semantics=("parallel", "parallel"), vmem_limit_bytes=VMEM_LIMIT),
        name="natten",
    )(nq, nk, nv, bias_tabs)


def _mixout_kernel(x_ref, yr_ref, na_ref, gr_ref, gn_ref, wr_ref, wn_ref, wo_ref, gain_ref, o_ref):
    n_sub = x_ref.shape[0] // SUB_TILE

    def rows(j):
        return slice(j * SUB_TILE, (j + 1) * SUB_TILE)

    def branches(j):
        def gathered(ref):
            return jnp.concatenate([ref[g, rows(j), :] for g in range(ref.shape[0])], axis=1)

        y_ret = jnp.dot(gathered(yr_ref), wr_ref[...], preferred_element_type=F32)
        y_na = jnp.dot(gathered(na_ref), wn_ref[...], preferred_element_type=F32)
        merged = (jax.nn.sigmoid(gr_ref[rows(j), :].astype(F32)) * y_ret
                  + jax.nn.sigmoid(gn_ref[rows(j), :].astype(F32)) * y_na)
        return merged.astype(BF16)

    def finish(j, merged):
        m = jnp.dot(merged, wo_ref[...], preferred_element_type=F32)
        o_ref[rows(j), :] = x_ref[rows(j), :] + _rms(m, gain_ref[...])

    merged = branches(0)
    for j in range(n_sub):
        merged_next = branches(j + 1) if j + 1 < n_sub else None
        finish(j, merged)
        merged = merged_next


def _mix_out(x, y_ret, y_na, g_ret, g_na, w_ret, w_na, w_out, gain):
    t = x.shape[0]
    tm = FFN_TILE

    def row(width):
        return pl.BlockSpec((tm, width), lambda i: (i, 0))

    return pl.pallas_call(
        _mixout_kernel,
        grid=(t // tm,),
        in_specs=[row(D_MODEL),
                  pl.BlockSpec((RET_HEADS, tm, RET_V_DIM), lambda i: (0, i, 0)),
                  pl.BlockSpec((NA_SLABS, tm, NA_LANES), lambda i: (0, i, 0)),
                  row(D_MODEL), row(D_MODEL),
                  _const_spec((RET_V_W, D_MODEL)), _const_spec((NA_W, D_MODEL)),
                  _const_spec((D_MODEL, D_MODEL)), _const_spec((1, D_MODEL))],
        out_specs=row(D_MODEL),
        out_shape=jax.ShapeDtypeStruct((t, D_MODEL), F32),
        compiler_params=pltpu.CompilerParams(
            dimension_semantics=("parallel",), vmem_limit_bytes=VMEM_LIMIT),
        name="mix_out",
    )(x, y_ret, y_na, g_ret, g_na, w_ret, w_na, w_out, gain)


def kernel(x, ffn1_pre_norm, ffn1_w_in, ffn1_w_out, ffn1_post_norm, mix_pre_norm, w_mix_in,
           ret_decay_fwd, ret_decay_bwd, na_rel_bias, w_ret_out, w_na_out, w_mix_out, mix_post_norm,
           ffn2_pre_norm, ffn2_w_in, ffn2_w_out, ffn2_post_norm):
    b, s, d = x.shape
    assert d == D_MODEL and s % GRID_W == 0 and s % RET_BLOCK == 0 and (b * s) % TOKEN_TILE == 0
    assert s % TOKEN_TILE == 0
    rows = s // GRID_W
    assert rows == NA_ROWS, "the neighbourhood-attention tiling is laid out for a 32 x 64 token grid"
    t = b * s
    xt = x.reshape(t, d)
    for l in range(ffn1_w_in.shape[0]):
        gain = lambda g: g[l].reshape(1, D_MODEL).astype(F32)
        wt = lambda w: w[l].astype(BF16)
        xt = _ffn(xt, gain(ffn1_pre_norm), wt(ffn1_w_in), wt(ffn1_w_out), gain(ffn1_post_norm))
        rq, rk, rv, rg, nq, nk, nv, g_ret, g_na = _mix_in(xt, gain(mix_pre_norm), wt(w_mix_in), s)
        decay = jnp.stack([ret_decay_fwd[l], ret_decay_bwd[l]]).astype(F32)
        seq = lambda a: a.reshape(a.shape[0], b, s, a.shape[-1])
        y_ret = _retention(decay, seq(rq), seq(rk), seq(rv), seq(rg)).reshape(RET_HEADS, t, RET_V_DIM)
        grid5 = lambda a: a.reshape(NA_SLABS, b, rows, GRID_W, NA_LANES)
        y_na = _natten(grid5(nq), grid5(nk), grid5(nv), _na_bias_tables(na_rel_bias[l]))
        xt = _mix_out(xt, y_ret, y_na.reshape(NA_SLABS, t, NA_LANES), g_ret, g_na,
                      wt(w_ret_out), wt(w_na_out), wt(w_mix_out), gain(mix_post_norm))
        xt = _ffn(xt, gain(ffn2_pre_norm), wt(ffn2_w_in), wt(ffn2_w_out), gain(ffn2_post_norm))
    return xt.reshape(b, s, d)
```

```python
import functools

import numpy as np
import jax
import jax.numpy as jnp
from jax import lax
from jax.experimental import pallas as pl
from jax.experimental.pallas import tpu as pltpu

F32 = jnp.float32
BF16 = jnp.bfloat16

D_MODEL = 1024
D_FF = 2816
RMS_EPS = 1e-6
ROPE_BASE = 10000.0
NEG_INF = -1e30
LOG2E = 1.4426950408889634

RET_HEADS = 4
RET_QK_DIM = 128
RET_V_DIM = 256
RET_QK_W = RET_HEADS * RET_QK_DIM
RET_V_W = RET_HEADS * RET_V_DIM
RET_BLOCK = 256

NA_HEADS = 8
NA_HEAD_DIM = 64
NA_W = NA_HEADS * NA_HEAD_DIM
GRID_W = 64
NA_WIN_ROWS = 8
NA_WIN_COLS = 16
NA_REL_ROWS = 2 * NA_WIN_ROWS - 1
NA_REL_COLS = 2 * NA_WIN_COLS - 1
NA_Q_ROWS = 8
NA_Q_COLS = 16
NA_K_ROWS = 16
NA_K_COLS = 32
NA_LANES = 128
NA_SLABS = NA_W // NA_LANES
NA_SLABS_PER_STEP = 4

MIX_SPLITS = (RET_QK_W, RET_QK_W, RET_V_W, RET_V_W, NA_W, NA_W, NA_W, D_MODEL, D_MODEL)
MIX_OFFS = tuple(int(v) for v in np.cumsum((0,) + MIX_SPLITS[:-1]))
MIX_IN_W = sum(MIX_SPLITS)

TOKEN_TILE = 512
FFN_TILE = 1024
SUB_TILE = 256
FF_CHUNK = 256
VMEM_LIMIT = 56 * 1024 * 1024


def _rms(x, gain):
    ms = jnp.mean(x * x, axis=-1, keepdims=True)
    return x * lax.rsqrt(ms + RMS_EPS) * gain


def _silu(x):
    return x * jax.nn.sigmoid(x)


def _const_spec(shape):
    nd = len(shape)
    return pl.BlockSpec(shape, lambda *_: (0,) * nd, pipeline_mode=pl.Buffered(1))


def _ffn_kernel(x_ref, gpre_ref, win_ref, wout_ref, gpost_ref, o_ref, act_ref):
    n_sub = x_ref.shape[0] // SUB_TILE

    def rows(j):
        return slice(j * SUB_TILE, (j + 1) * SUB_TILE)

    def pre(j):
        return _rms(x_ref[rows(j), :], gpre_ref[...]).astype(BF16)

    def up(j, xn):
        for c in range(D_FF // FF_CHUNK):
            lo = c * FF_CHUNK
            g = jnp.dot(xn, win_ref[:, lo:lo + FF_CHUNK], preferred_element_type=F32)
            u = jnp.dot(xn, win_ref[:, D_FF + lo:D_FF + lo + FF_CHUNK], preferred_element_type=F32)
            act_ref[rows(j), lo:lo + FF_CHUNK] = (_silu(g) * u).astype(BF16)

    def down(j):
        h = jnp.dot(act_ref[rows(j), :], wout_ref[...], preferred_element_type=F32)
        o_ref[rows(j), :] = x_ref[rows(j), :] + 0.5 * _rms(h, gpost_ref[...])

    xn = pre(0)
    for j in range(n_sub):
        xn_next = pre(j + 1) if j + 1 < n_sub else None
        up(j, xn)
        if j >= 1:
            down(j - 1)
        xn = xn_next
    down(n_sub - 1)


def _ffn(x, gpre, w_in, w_out, gpost):
    t = x.shape[0]
    tm = FFN_TILE
    row = pl.BlockSpec((tm, D_MODEL), lambda i: (i, 0))
    return pl.pallas_call(
        _ffn_kernel,
        grid=(t // tm,),
        in_specs=[row, _const_spec((1, D_MODEL)), _const_spec((D_MODEL, 2 * D_FF)),
                  _const_spec((D_FF, D_MODEL)), _const_spec((1, D_MODEL))],
        out_specs=row,
        out_shape=jax.ShapeDtypeStruct((t, D_MODEL), F32),
        scratch_shapes=[pltpu.VMEM((tm, D_FF), BF16)],
        compiler_params=pltpu.CompilerParams(
            dimension_semantics=("parallel",), vmem_limit_bytes=VMEM_LIMIT),
        name="ffn",
    )(x, gpre, w_in, w_out, gpost)


def _mixin_kernel(x_ref, gain_ref, w_ref, rope_ref,
                  rq_ref, rk_ref, rv_ref, rg_ref, nq_ref, nk_ref, nv_ref, gr_ref, gn_ref):
    n_sub = x_ref.shape[0] // SUB_TILE

    def pre(j):
        rows = slice(j * SUB_TILE, (j + 1) * SUB_TILE)
        return _rms(x_ref[rows, :], gain_ref[...]).astype(BF16)

    def project(j, u):
        rows = slice(j * SUB_TILE, (j + 1) * SUB_TILE)

        def proj(idx):
            lo = MIX_OFFS[idx]
            return jnp.dot(u, w_ref[:, lo:lo + MIX_SPLITS[idx]], preferred_element_type=F32)

        def store_groups(out_ref, y):
            width = out_ref.shape[-1]
            for g in range(out_ref.shape[0]):
                out_ref[g, rows, :] = y[:, g * width:(g + 1) * width].astype(out_ref.dtype)

        def rotary(y, cos, sin, out_ref):
            for h in range(RET_HEADS):
                yh = y[:, h * RET_QK_DIM:(h + 1) * RET_QK_DIM]
                out_ref[h, rows, :] = (yh * cos + pltpu.roll(yh, RET_QK_DIM // 2, 1) * sin).astype(out_ref.dtype)

        rotary(proj(0), rope_ref[0, rows, :], rope_ref[1, rows, :], rq_ref)
        rotary(proj(1), rope_ref[2, rows, :], rope_ref[3, rows, :], rk_ref)
        store_groups(rv_ref, proj(2))
        store_groups(rg_ref, _silu(proj(3)))
        store_groups(nq_ref, proj(4) * (LOG2E * NA_HEAD_DIM ** -0.5))
        store_groups(nk_ref, proj(5))
        store_groups(nv_ref, proj(6))
        gr_ref[rows, :] = proj(7).astype(gr_ref.dtype)
        gn_ref[rows, :] = proj(8).astype(gn_ref.dtype)

    u = pre(0)
    for j in range(n_sub):
        u_next = pre(j + 1) if j + 1 < n_sub else None
        project(j, u)
        u = u_next


def _rope_tables(seq):
    half = RET_QK_DIM // 2
    pos = jnp.arange(seq, dtype=F32)
    inv = 1.0 / (ROPE_BASE ** jnp.linspace(0.0, 1.0, half, dtype=F32))
    ang = pos[:, None] * inv[None, :]
    cos, sin = jnp.cos(ang), jnp.sin(ang)
    cos2 = jnp.concatenate([cos, cos], axis=-1)
    sin2 = jnp.concatenate([-sin, sin], axis=-1)
    ks = RET_QK_DIM ** -0.5
    return jnp.stack([cos2, sin2, cos2 * ks, sin2 * ks])


def _mix_in(x, gain, w, seq):
    t = x.shape[0]
    tm = TOKEN_TILE
    tiles_per_seq = seq // tm
    out_dtypes = (BF16,) * len(MIX_SPLITS)
    out_groups = (RET_HEADS,) * 4 + (NA_SLABS,) * 3 + (1, 1)
    return pl.pallas_call(
        _mixin_kernel,
        grid=(t // tm,),
        in_specs=[pl.BlockSpec((tm, D_MODEL), lambda i: (i, 0)),
                  _const_spec((1, D_MODEL)), _const_spec((D_MODEL, MIX_IN_W)),
                  pl.BlockSpec((4, tm, RET_QK_DIM), lambda i: (0, i % tiles_per_seq, 0))],
        out_specs=[pl.BlockSpec((tm, wd), lambda i: (i, 0)) if g == 1
                   else pl.BlockSpec((g, tm, wd // g), lambda i: (0, i, 0))
                   for wd, g in zip(MIX_SPLITS, out_groups)],
        out_shape=[jax.ShapeDtypeStruct((t, wd) if g == 1 else (g, t, wd // g), dt)
                   for wd, g, dt in zip(MIX_SPLITS, out_groups, out_dtypes)],
        compiler_params=pltpu.CompilerParams(
            dimension_semantics=("parallel",), vmem_limit_bytes=VMEM_LIMIT),
        name="mix_in",
    )(x, gain, w, _rope_tables(seq))


def _log_sigmoid(z):
    return jnp.minimum(z, 0.0) - jnp.log1p(jnp.exp(-jnp.abs(z)))


def _ret_kernel(dec_ref, q_ref, k_ref, v_ref, g_ref, o_ref, u_ref, st_ref, dm_ref, vec_ref, blk_ref):
    c = RET_BLOCK
    dk = RET_QK_DIM
    dv = RET_V_DIM
    n_heads = q_ref.shape[0]
    n_blocks = q_ref.shape[1] // c
    tn = (((0,), (0,)), ((), ()))
    nt = (((1,), (1,)), ((), ()))

    @pl.when(pl.program_id(0) == 0)
    def _decay_tables():
        def log_gamma(z, shape):
            return _log_sigmoid(jnp.full(shape, z, F32))

        ri = lax.broadcasted_iota(jnp.int32, (c, c), 0).astype(F32)
        ci = lax.broadcasted_iota(jnp.int32, (c, c), 1).astype(F32)
        diff = ri - ci
        r = lax.broadcasted_iota(jnp.int32, (c, dk), 0).astype(F32)
        for h in range(n_heads):
            zf = dec_ref[0, h]
            zb = dec_ref[1, h]
            dm_ref[h] = jnp.where(diff >= 0, jnp.exp(log_gamma(zf, (c, c)) * jnp.maximum(diff, 0.0)),
                                  jnp.exp(log_gamma(zb, (c, c)) * jnp.maximum(-diff, 0.0)))
            lgf = log_gamma(zf, (c, dk))
            lgb = log_gamma(zb, (c, dk))
            vec_ref[h, 0] = jnp.exp(lgf * (r + 1.0)).astype(BF16)
            vec_ref[h, 1] = jnp.exp(lgb * (c - r)).astype(BF16)
            vec_ref[h, 2] = jnp.exp(lgf * (c - 1.0 - r)).astype(BF16)
            vec_ref[h, 3] = jnp.exp(lgb * r).astype(BF16)
            blk_ref[h, 0] = jnp.exp(log_gamma(zf, (dk, dv)) * c)
            blk_ref[h, 1] = jnp.exp(log_gamma(zb, (dk, dv)) * c)

    def rows(n):
        return slice(n * c, (n + 1) * c)

    def updates(h):
        for n in range(n_blocks):
            k = k_ref[h, rows(n), :]
            k_cat = jnp.concatenate([k * vec_ref[h, 2], k * vec_ref[h, 3]], axis=1)
            u_ref[h % 2, n] = lax.dot_general(k_cat, v_ref[h, rows(n), :], tn, preferred_element_type=F32)

    def scans(h):
        sf = jnp.zeros((dk, dv), F32)
        for n in range(n_blocks):
            st_ref[h % 2, n, 0:dk, :] = sf.astype(BF16)
            if n + 1 < n_blocks:
                sf = blk_ref[h, 0] * sf + u_ref[h % 2, n, 0:dk, :]
        sb = jnp.zeros((dk, dv), F32)
        for n in reversed(range(n_blocks)):
            st_ref[h % 2, n, dk:2 * dk, :] = sb.astype(BF16)
            if n > 0:
                sb = blk_ref[h, 1] * sb + u_ref[h % 2, n, dk:2 * dk, :]

    def outputs(h):
        def scores(n):
            return lax.dot_general(q_ref[h, rows(n), :], k_ref[h, rows(n), :], nt,
                                   preferred_element_type=F32)

        s_next = scores(0)
        for n in range(n_blocks):
            s_cur = s_next
            if n + 1 < n_blocks:
                s_next = scores(n + 1)
            p = (s_cur * dm_ref[h]).astype(BF16)
            q = q_ref[h, rows(n), :]
            lhs = jnp.concatenate([p, q * vec_ref[h, 0], q * vec_ref[h, 1]], axis=1)
            rhs = jnp.concatenate([v_ref[h, rows(n), :], st_ref[h % 2, n]], axis=0)
            y = jnp.dot(lhs, rhs, preferred_element_type=F32)
            yn = y * lax.rsqrt(jnp.mean(y * y, axis=-1, keepdims=True) + RMS_EPS)
            o_ref[h, rows(n), :] = (g_ref[h, rows(n), :].astype(F32) * yn).astype(o_ref.dtype)

    updates(0)
    for h in range(n_heads):
        if h + 1 < n_heads:
            updates(h + 1)
        scans(h)
        outputs(h)


def _retention(decay_logits, rq, rk, rv, rg):
    n_heads, b, s, _ = rq.shape
    c = RET_BLOCK
    n_blocks = s // c
    qk_spec = pl.BlockSpec((n_heads, None, s, RET_QK_DIM), lambda i, dec: (0, i, 0, 0))
    v_spec = pl.BlockSpec((n_heads, None, s, RET_V_DIM), lambda i, dec: (0, i, 0, 0))
    return pl.pallas_call(
        _ret_kernel,
        grid_spec=pltpu.PrefetchScalarGridSpec(
            num_scalar_prefetch=1,
            grid=(b,),
            in_specs=[qk_spec, qk_spec, v_spec, v_spec],
            out_specs=v_spec,
            scratch_shapes=[pltpu.VMEM((2, n_blocks, 2 * RET_QK_DIM, RET_V_DIM), F32),
                            pltpu.VMEM((2, n_blocks, 2 * RET_QK_DIM, RET_V_DIM), BF16),
                            pltpu.VMEM((n_heads, c, c), F32),
                            pltpu.VMEM((n_heads, 4, c, RET_QK_DIM), BF16),
                            pltpu.VMEM((n_heads, 2, RET_QK_DIM, RET_V_DIM), F32)],
        ),
        out_shape=jax.ShapeDtypeStruct((n_heads, b, s, RET_V_DIM), BF16),
        compiler_params=pltpu.CompilerParams(
            dimension_semantics=("arbitrary",), vmem_limit_bytes=VMEM_LIMIT),
        name="retention",
    )(decay_logits, rq, rk, rv, rg)


def _na_blocks(extent, q_size, k_size, window):
    n = extent // q_size
    out = []
    for i in range(n):
        q0 = i * q_size
        k0 = min(max(q0 - window // 2, 0), extent - k_size)
        out.append((q0, k0, 0 if i == 0 else (2 if i == n - 1 else 1)))
    return tuple(out)


NA_ROWS = 32
_NA_ROW_BLOCKS = _na_blocks(NA_ROWS, NA_Q_ROWS, NA_K_ROWS, NA_WIN_ROWS)
_NA_COL_BLOCKS = _na_blocks(GRID_W, NA_Q_COLS, NA_K_COLS, NA_WIN_COLS)
_NA_CONFIGS = 9


def _na_bias_kernel(rel_ref, out_ref, col_ref):
    nk = NA_K_ROWS * NA_K_COLS
    base = pl.program_id(0) * (NA_REL_ROWS * NA_REL_COLS)
    lane = lax.broadcasted_iota(jnp.int32, (NA_Q_COLS, nk), 1)
    kr = lane // NA_K_COLS
    col_cfgs = (_NA_COL_BLOCKS[0], _NA_COL_BLOCKS[1], _NA_COL_BLOCKS[3])
    row_cfgs = (_NA_ROW_BLOCKS[0], _NA_ROW_BLOCKS[1], _NA_ROW_BLOCKS[3])
    reps = nk // NA_LANES
    kc_one = lax.broadcasted_iota(jnp.int32, (NA_Q_COLS, NA_LANES), 1) % NA_K_COLS
    qc_one = lax.broadcasted_iota(jnp.int32, (NA_Q_COLS, NA_LANES), 0)
    for b, (c0, kc0, _) in enumerate(col_cfgs):
        kc_abs = kc_one + kc0
        qc_abs = qc_one + c0
        rel_col = jnp.clip(kc_abs - qc_abs + NA_WIN_COLS - 1, 0, NA_REL_COLS - 1)
        win = jnp.clip(qc_abs - NA_WIN_COLS // 2, 0, GRID_W - NA_WIN_COLS)
        in_win = (kc_abs >= win) & (kc_abs < win + NA_WIN_COLS)

        def rel_row_body(dr, carry, rel_col=rel_col, in_win=in_win, b=b):
            vals = jnp.zeros((NA_Q_COLS, NA_LANES), F32)
            for dc in range(NA_REL_COLS):
                vals = jnp.where(rel_col == dc, rel_ref[base + dr * NA_REL_COLS + dc], vals)
            vals = jnp.where(in_win, vals * LOG2E, NEG_INF)
            col_ref[b * NA_REL_ROWS + dr] = jnp.concatenate([vals] * reps, axis=1)
            return carry

        lax.fori_loop(0, NA_REL_ROWS, rel_row_body, 0)

    for a, (r0, ks, _) in enumerate(row_cfgs):
        kr_abs = kr + ks
        for qr in range(NA_Q_ROWS):
            r_abs = r0 + qr
            rs = min(max(r_abs - NA_WIN_ROWS // 2, 0), NA_ROWS - NA_WIN_ROWS)
            hits = [kr_abs == rs + j for j in range(NA_WIN_ROWS)]
            for b in range(3):
                tile = jnp.full((NA_Q_COLS, nk), NEG_INF, F32)
                for j in range(NA_WIN_ROWS):
                    dr = rs + j - r_abs + NA_WIN_ROWS - 1
                    tile = jnp.where(hits[j], col_ref[b * NA_REL_ROWS + dr], tile)
                out_ref[0, a * 3 + b, qr * NA_Q_COLS:(qr + 1) * NA_Q_COLS, :] = tile


def _na_bias_tables(rel_bias):
    nq = NA_Q_ROWS * NA_Q_COLS
    nk = NA_K_ROWS * NA_K_COLS
    return pl.pallas_call(
        _na_bias_kernel,
        grid_spec=pltpu.PrefetchScalarGridSpec(
            num_scalar_prefetch=1,
            grid=(NA_HEADS,),
            in_specs=[],
            out_specs=pl.BlockSpec((1, _NA_CONFIGS, nq, nk), lambda h, rel: (h, 0, 0, 0)),
            scratch_shapes=[pltpu.VMEM((3 * NA_REL_ROWS, NA_Q_COLS, nk), F32)],
        ),
        out_shape=jax.ShapeDtypeStruct((NA_HEADS, _NA_CONFIGS, nq, nk), F32),
        compiler_params=pltpu.CompilerParams(dimension_semantics=("parallel",)),
        name="na_bias",
    )(rel_bias.astype(F32).reshape(-1))


def _na_kernel(q_ref, k_ref, v_ref, bias_ref, o_ref, ks_ref, vs_ref):
    n_slabs = q_ref.shape[0]
    nq = NA_Q_ROWS * NA_Q_COLS
    nk = NA_K_ROWS * NA_K_COLS
    shift = NA_WIN_COLS // 2
    kept = GRID_W - 2 * shift
    for sl in range(n_slabs):
        for src, dst in ((k_ref, ks_ref), (v_ref, vs_ref)):
            dst[sl, :, 0:kept, :] = src[sl, :, shift:shift + kept, :]

    def window(refs, sl, block):
        (_, kr0, _), (_, kc0, _) = block
        aligned = kc0 % (2 * shift) == 0
        ref, c = (refs[0], kc0) if aligned else (refs[1], kc0 - shift)
        assert c % (2 * shift) == 0 and (aligned or c + NA_K_COLS <= kept)
        return ref[sl, kr0:kr0 + NA_K_ROWS, c:c + NA_K_COLS, :].reshape(nk, NA_LANES)

    first = lax.broadcasted_iota(jnp.int32, (nq, NA_LANES), 1) < NA_HEAD_DIM
    ones = jnp.ones((nk, NA_LANES), BF16)
    nt = (((1,), (1,)), ((), ()))
    work = [(sl, (rb, cb)) for sl in range(n_slabs) for rb in _NA_ROW_BLOCKS for cb in _NA_COL_BLOCKS]

    def scores(sl, block):
        (r0, _, rcfg), (c0, _, ccfg) = block
        q = q_ref[sl, r0:r0 + NA_Q_ROWS, c0:c0 + NA_Q_COLS, :].reshape(nq, NA_LANES)
        zero = jnp.zeros_like(q)
        q2 = jnp.concatenate([jnp.where(first, q, zero), jnp.where(first, zero, q)], axis=0)
        cfg = rcfg * 3 + ccfg
        bias = jnp.concatenate([bias_ref[2 * sl, cfg], bias_ref[2 * sl + 1, cfg]], axis=0)
        return lax.dot_general(q2, window((k_ref, ks_ref), sl, block), nt,
                               preferred_element_type=F32) + bias

    def attend(sl, block, s):
        m = jnp.max(s, axis=-1, keepdims=True)
        p = jnp.exp2(s - m).astype(BF16)
        o = jnp.dot(p, jnp.concatenate([window((v_ref, vs_ref), sl, block), ones], axis=1),
                    preferred_element_type=F32)
        o = o[:, :NA_LANES] / o[:, NA_LANES:]
        return jnp.where(first, o[:nq], o[nq:])

    s_next = scores(*work[0])
    for i, (sl, blk) in enumerate(work):
        s_cur = s_next
        if i + 1 < len(work):
            s_next = scores(*work[i + 1])
        (r0, _, _), (c0, _, _) = blk
        o_ref[sl, r0:r0 + NA_Q_ROWS, c0:c0 + NA_Q_COLS, :] = (
            attend(sl, blk, s_cur).reshape(NA_Q_ROWS, NA_Q_COLS, NA_LANES).astype(o_ref.dtype))


def _natten(nq, nk, nv, bias_tabs):
    b, rows = nq.shape[1], nq.shape[2]
    per = NA_SLABS_PER_STEP
    slab = pl.BlockSpec((per, None, rows, GRID_W, NA_LANES), lambda j, i: (j, i, 0, 0, 0))
    nqk = NA_Q_ROWS * NA_Q_COLS
    nkk = NA_K_ROWS * NA_K_COLS
    return pl.pallas_call(
        _na_kernel,
        grid=(NA_SLABS // per, b),
        in_specs=[slab, slab, slab,
                  pl.BlockSpec((2 * per, _NA_CONFIGS, nqk, nkk), lambda j, i: (j, 0, 0, 0),
                               pipeline_mode=pl.Buffered(1))],
        out_specs=slab,
        out_shape=jax.ShapeDtypeStruct((NA_SLABS, b, rows, GRID_W, NA_LANES), BF16),
        scratch_shapes=[pltpu.VMEM((per, rows, GRID_W, NA_LANES), BF16),
                        pltpu.VMEM((per, rows, GRID_W, NA_LANES), BF16)],
        compiler_params=pltpu.CompilerParams(
            dimension_semantics=("parallel", "parallel"), vmem_limit_bytes=VMEM_LIMIT),
        name="natten",
    )(nq, nk, nv, bias_tabs)


def _mixout_kernel(x_ref, yr_ref, na_ref, gr_ref, gn_ref, wr_ref, wn_ref, wo_ref, gain_ref, o_ref):
    n_sub = x_ref.shape[0] // SUB_TILE

    def rows(j):
        return slice(j * SUB_TILE, (j + 1) * SUB_TILE)

    def branches(j):
        def gathered(ref):
            return jnp.concatenate([ref[g, rows(j), :] for g in range(ref.shape[0])], axis=1)

        y_ret = jnp.dot(gathered(yr_ref), wr_ref[...], preferred_element_type=F32)
        y_na = jnp.dot(gathered(na_ref), wn_ref[...], preferred_element_type=F32)
        merged = (jax.nn.sigmoid(gr_ref[rows(j), :].astype(F32)) * y_ret
                  + jax.nn.sigmoid(gn_ref[rows(j), :].astype(F32)) * y_na)
        return merged.astype(BF16)

    def finish(j, merged):
        m = jnp.dot(merged, wo_ref[...], preferred_element_type=F32)
        o_ref[rows(j), :] = x_ref[rows(j), :] + _rms(m, gain_ref[...])

    merged = branches(0)
    for j in range(n_sub):
        merged_next = branches(j + 1) if j + 1 < n_sub else None
        finish(j, merged)
        merged = merged_next


def _mix_out(x, y_ret, y_na, g_ret, g_na, w_ret, w_na, w_out, gain):
    t = x.shape[0]
    tm = FFN_TILE

    def row(width):
        return pl.BlockSpec((tm, width), lambda i: (i, 0))

    return pl.pallas_call(
        _mixout_kernel,
        grid=(t // tm,),
        in_specs=[row(D_MODEL),
                  pl.BlockSpec((RET_HEADS, tm, RET_V_DIM), lambda i: (0, i, 0)),
                  pl.BlockSpec((NA_SLABS, tm, NA_LANES), lambda i: (0, i, 0)),
                  row(D_MODEL), row(D_MODEL),
                  _const_spec((RET_V_W, D_MODEL)), _const_spec((NA_W, D_MODEL)),
                  _const_spec((D_MODEL, D_MODEL)), _const_spec((1, D_MODEL))],
        out_specs=row(D_MODEL),
        out_shape=jax.ShapeDtypeStruct((t, D_MODEL), F32),
        compiler_params=pltpu.CompilerParams(
            dimension_semantics=("parallel",), vmem_limit_bytes=VMEM_LIMIT),
        name="mix_out",
    )(x, y_ret, y_na, g_ret, g_na, w_ret, w_na, w_out, gain)


def kernel(x, ffn1_pre_norm, ffn1_w_in, ffn1_w_out, ffn1_post_norm, mix_pre_norm, w_mix_in,
           ret_decay_fwd, ret_decay_bwd, na_rel_bias, w_ret_out, w_na_out, w_mix_out, mix_post_norm,
           ffn2_pre_norm, ffn2_w_in, ffn2_w_out, ffn2_post_norm):
    b, s, d = x.shape
    assert d == D_MODEL and s % GRID_W == 0 and s % RET_BLOCK == 0 and (b * s) % TOKEN_TILE == 0
    assert s % TOKEN_TILE == 0
    rows = s // GRID_W
    assert rows == NA_ROWS, "the neighbourhood-attention tiling is laid out for a 32 x 64 token grid"
    t = b * s
    xt = x.reshape(t, d)
    for l in range(ffn1_w_in.shape[0]):
        gain = lambda g: g[l].reshape(1, D_MODEL).astype(F32)
        wt = lambda w: w[l].astype(BF16)
        xt = _ffn(xt, gain(ffn1_pre_norm), wt(ffn1_w_in), wt(ffn1_w_out), gain(ffn1_post_norm))
        rq, rk, rv, rg, nq, nk, nv, g_ret, g_na = _mix_in(xt, gain(mix_pre_norm), wt(w_mix_in), s)
        decay = jnp.stack([ret_decay_fwd[l], ret_decay_bwd[l]]).astype(F32)
        seq = lambda a: a.reshape(a.shape[0], b, s, a.shape[-1])
        y_ret = _retention(decay, seq(rq), seq(rk), seq(rv), seq(rg)).reshape(RET_HEADS, t, RET_V_DIM)
        grid5 = lambda a: a.reshape(NA_SLABS, b, rows, GRID_W, NA_LANES)
        y_na = _natten(grid5(nq), grid5(nk), grid5(nv), _na_bias_tables(na_rel_bias[l]))
        xt = _mix_out(xt, y_ret, y_na.reshape(NA_SLABS, t, NA_LANES), g_ret, g_na,
                      wt(w_ret_out), wt(w_na_out), wt(w_mix_out), gain(mix_post_norm))
        xt = _ffn(xt, gain(ffn2_pre_norm), wt(ffn2_w_in), wt(ffn2_w_out), gain(ffn2_post_norm))
    return xt.reshape(b, s, d)
```

```python
import numpy as np
import jax
import jax.numpy as jnp
from jax import lax
from jax.experimental import pallas as pl
from jax.experimental.pallas import tpu as pltpu

F32 = jnp.float32
BF16 = jnp.bfloat16

D_MODEL = 1024
D_FF = 2816
RMS_EPS = 1e-6
ROPE_BASE = 10000.0
NEG_INF = -1e30
LOG2E = 1.4426950408889634

RET_HEADS = 4
RET_QK_DIM = 128
RET_V_DIM = 256
RET_QK_W = RET_HEADS * RET_QK_DIM
RET_V_W = RET_HEADS * RET_V_DIM
RET_BLOCK = 256

NA_HEADS = 8
NA_HEAD_DIM = 64
NA_W = NA_HEADS * NA_HEAD_DIM
GRID_W = 64
NA_WIN_ROWS = 8
NA_WIN_COLS = 16
NA_REL_ROWS = 2 * NA_WIN_ROWS - 1
NA_REL_COLS = 2 * NA_WIN_COLS - 1
NA_Q_ROWS = 8
NA_Q_COLS = 16
NA_K_ROWS = 16
NA_K_COLS = 32
NA_LANES = 128
NA_SLABS = NA_W // NA_LANES
NA_SLABS_PER_STEP = 4

MIX_SPLITS = (RET_QK_W, RET_QK_W, RET_V_W, RET_V_W, NA_W, NA_W, NA_W, D_MODEL, D_MODEL)
MIX_OFFS = tuple(int(v) for v in np.cumsum((0,) + MIX_SPLITS[:-1]))
MIX_IN_W = sum(MIX_SPLITS)

TOKEN_TILE = 512
FFN_TILE = 1024
SUB_TILE = 256
FF_CHUNK = 256
VMEM_LIMIT = 56 * 1024 * 1024


def _rms(x, gain):
    ms = jnp.mean(x * x, axis=-1, keepdims=True)
    return x * lax.rsqrt(ms + RMS_EPS) * gain


def _silu(x):
    return x * jax.nn.sigmoid(x)


def _const_spec(shape):
    nd = len(shape)
    return pl.BlockSpec(shape, lambda *_: (0,) * nd, pipeline_mode=pl.Buffered(1))


def _ffn_kernel(x_ref, gpre_ref, win_ref, wout_ref, gpost_ref, o_ref, act_ref):
    n_sub = x_ref.shape[0] // SUB_TILE

    def rows(j):
        return slice(j * SUB_TILE, (j + 1) * SUB_TILE)

    def pre(j):
        return _rms(x_ref[rows(j), :], gpre_ref[...]).astype(BF16)

    def up(j, xn):
        for c in range(D_FF // FF_CHUNK):
            lo = c * FF_CHUNK
            g = jnp.dot(xn, win_ref[:, lo:lo + FF_CHUNK], preferred_element_type=F32)
            u = jnp.dot(xn, win_ref[:, D_FF + lo:D_FF + lo + FF_CHUNK], preferred_element_type=F32)
            act_ref[rows(j), lo:lo + FF_CHUNK] = (_silu(g) * u).astype(BF16)

    def down(j):
        h = jnp.dot(act_ref[rows(j), :], wout_ref[...], preferred_element_type=F32)
        o_ref[rows(j), :] = x_ref[rows(j), :] + 0.5 * _rms(h, gpost_ref[...])

    xn = pre(0)
    for j in range(n_sub):
        xn_next = pre(j + 1) if j + 1 < n_sub else None
        up(j, xn)
        if j >= 1:
            down(j - 1)
        xn = xn_next
    down(n_sub - 1)


def _ffn(x, gpre, w_in, w_out, gpost):
    t = x.shape[0]
    tm = FFN_TILE
    row = pl.BlockSpec((tm, D_MODEL), lambda i: (i, 0))
    return pl.pallas_call(
        _ffn_kernel,
        grid=(t // tm,),
        in_specs=[row, _const_spec((1, D_MODEL)), _const_spec((D_MODEL, 2 * D_FF)),
                  _const_spec((D_FF, D_MODEL)), _const_spec((1, D_MODEL))],
        out_specs=row,
        out_shape=jax.ShapeDtypeStruct((t, D_MODEL), F32),
        scratch_shapes=[pltpu.VMEM((tm, D_FF), BF16)],
        compiler_params=pltpu.CompilerParams(
            dimension_semantics=("parallel",), vmem_limit_bytes=VMEM_LIMIT),
        name="ffn",
    )(x, gpre, w_in, w_out, gpost)


def _mixin_kernel(x_ref, gain_ref, w_ref, rope_ref,
                  rq_ref, rk_ref, rv_ref, rg_ref, nq_ref, nk_ref, nv_ref, gr_ref, gn_ref):
    n_sub = x_ref.shape[0] // SUB_TILE

    def pre(j):
        rows = slice(j * SUB_TILE, (j + 1) * SUB_TILE)
        return _rms(x_ref[rows, :], gain_ref[...]).astype(BF16)

    def project(j, u):
        rows = slice(j * SUB_TILE, (j + 1) * SUB_TILE)

        def proj(idx):
            lo = MIX_OFFS[idx]
            return jnp.dot(u, w_ref[:, lo:lo + MIX_SPLITS[idx]], preferred_element_type=F32)

        def store_groups(out_ref, y):
            width = out_ref.shape[-1]
            for g in range(out_ref.shape[0]):
                out_ref[g, rows, :] = y[:, g * width:(g + 1) * width].astype(out_ref.dtype)

        def rotary(y, cos, sin, out_ref):
            for h in range(RET_HEADS):
                yh = y[:, h * RET_QK_DIM:(h + 1) * RET_QK_DIM]
                out_ref[h, rows, :] = (yh * cos + pltpu.roll(yh, RET_QK_DIM // 2, 1) * sin).astype(out_ref.dtype)

        rotary(proj(0), rope_ref[0, rows, :], rope_ref[1, rows, :], rq_ref)
        rotary(proj(1), rope_ref[2, rows, :], rope_ref[3, rows, :], rk_ref)
        store_groups(rv_ref, proj(2))
        store_groups(rg_ref, _silu(proj(3)))
        store_groups(nq_ref, proj(4) * (LOG2E * NA_HEAD_DIM ** -0.5))
        store_groups(nk_ref, proj(5))
        store_groups(nv_ref, proj(6))
        gr_ref[rows, :] = proj(7).astype(gr_ref.dtype)
        gn_ref[rows, :] = proj(8).astype(gn_ref.dtype)

    u = pre(0)
    for j in range(n_sub):
        u_next = pre(j + 1) if j + 1 < n_sub else None
        project(j, u)
        u = u_next


def _rope_tables(seq):
    half = RET_QK_DIM // 2
    pos = jnp.arange(seq, dtype=F32)
    inv = 1.0 / (ROPE_BASE ** jnp.linspace(0.0, 1.0, half, dtype=F32))
    ang = pos[:, None] * inv[None, :]
    cos, sin = jnp.cos(ang), jnp.sin(ang)
    cos2 = jnp.concatenate([cos, cos], axis=-1)
    sin2 = jnp.concatenate([-sin, sin], axis=-1)
    ks = RET_QK_DIM ** -0.5
    return jnp.stack([cos2, sin2, cos2 * ks, sin2 * ks])


def _mix_in(x, gain, w, seq):
    t = x.shape[0]
    tm = TOKEN_TILE
    tiles_per_seq = seq // tm
    out_dtypes = (BF16,) * len(MIX_SPLITS)
    out_groups = (RET_HEADS,) * 4 + (NA_SLABS,) * 3 + (1, 1)
    return pl.pallas_call(
        _mixin_kernel,
        grid=(t // tm,),
        in_specs=[pl.BlockSpec((tm, D_MODEL), lambda i: (i, 0)),
                  _const_spec((1, D_MODEL)), _const_spec((D_MODEL, MIX_IN_W)),
                  pl.BlockSpec((4, tm, RET_QK_DIM), lambda i: (0, i % tiles_per_seq, 0))],
        out_specs=[pl.BlockSpec((tm, wd), lambda i: (i, 0)) if g == 1
                   else pl.BlockSpec((g, tm, wd // g), lambda i: (0, i, 0))
                   for wd, g in zip(MIX_SPLITS, out_groups)],
        out_shape=[jax.ShapeDtypeStruct((t, wd) if g == 1 else (g, t, wd // g), dt)
                   for wd, g, dt in zip(MIX_SPLITS, out_groups, out_dtypes)],
        compiler_params=pltpu.CompilerParams(
            dimension_semantics=("parallel",), vmem_limit_bytes=VMEM_LIMIT),
        name="mix_in",
    )(x, gain, w, _rope_tables(seq))


def _log_sigmoid(z):
    return jnp.minimum(z, 0.0) - jnp.log1p(jnp.exp(-jnp.abs(z)))


def _ret_kernel(dec_ref, q_ref, k_ref, v_ref, g_ref, o_ref, u_ref, st_ref, dm_ref, vec_ref, blk_ref):
    c = RET_BLOCK
    dk = RET_QK_DIM
    dv = RET_V_DIM
    n_heads = q_ref.shape[0]
    n_blocks = q_ref.shape[1] // c
    tn = (((0,), (0,)), ((), ()))
    nt = (((1,), (1,)), ((), ()))

    @pl.when(pl.program_id(0) == 0)
    def _decay_tables():
        def log_gamma(z, shape):
            return _log_sigmoid(jnp.full(shape, z, F32))

        ri = lax.broadcasted_iota(jnp.int32, (c, c), 0).astype(F32)
        ci = lax.broadcasted_iota(jnp.int32, (c, c), 1).astype(F32)
        diff = ri - ci
        r = lax.broadcasted_iota(jnp.int32, (c, dk), 0).astype(F32)
        for h in range(n_heads):
            zf = dec_ref[0, h]
            zb = dec_ref[1, h]
            dm_ref[h] = jnp.where(diff >= 0, jnp.exp(log_gamma(zf, (c, c)) * jnp.maximum(diff, 0.0)),
                                  jnp.exp(log_gamma(zb, (c, c)) * jnp.maximum(-diff, 0.0)))
            lgf = log_gamma(zf, (c, dk))
            lgb = log_gamma(zb, (c, dk))
            vec_ref[h, 0] = jnp.exp(lgf * (r + 1.0)).astype(BF16)
            vec_ref[h, 1] = jnp.exp(lgb * (c - r)).astype(BF16)
            vec_ref[h, 2] = jnp.exp(lgf * (c - 1.0 - r)).astype(BF16)
            vec_ref[h, 3] = jnp.exp(lgb * r).astype(BF16)
            blk_ref[h, 0] = jnp.exp(log_gamma(zf, (dk, dv)) * c)
            blk_ref[h, 1] = jnp.exp(log_gamma(zb, (dk, dv)) * c)

    def rows(n):
        return slice(n * c, (n + 1) * c)

    def updates(h):
        for n in range(n_blocks):
            k = k_ref[h, rows(n), :]
            k_cat = jnp.concatenate([k * vec_ref[h, 2], k * vec_ref[h, 3]], axis=1)
            u_ref[h % 2, n] = lax.dot_general(k_cat, v_ref[h, rows(n), :], tn, preferred_element_type=F32)

    def scans(h):
        sf = jnp.zeros((dk, dv), F32)
        for n in range(n_blocks):
            st_ref[h % 2, n, 0:dk, :] = sf.astype(BF16)
            if n + 1 < n_blocks:
                sf = blk_ref[h, 0] * sf + u_ref[h % 2, n, 0:dk, :]
        sb = jnp.zeros((dk, dv), F32)
        for n in reversed(range(n_blocks)):
            st_ref[h % 2, n, dk:2 * dk, :] = sb.astype(BF16)
            if n > 0:
                sb = blk_ref[h, 1] * sb + u_ref[h % 2, n, dk:2 * dk, :]

    def outputs(h):
        def scores(n):
            return lax.dot_general(q_ref[h, rows(n), :], k_ref[h, rows(n), :], nt,
                                   preferred_element_type=F32)

        s_next = scores(0)
        for n in range(n_blocks):
            s_cur = s_next
            if n + 1 < n_blocks:
                s_next = scores(n + 1)
            p = (s_cur * dm_ref[h]).astype(BF16)
            q = q_ref[h, rows(n), :]
            lhs = jnp.concatenate([p, q * vec_ref[h, 0], q * vec_ref[h, 1]], axis=1)
            rhs = jnp.concatenate([v_ref[h, rows(n), :], st_ref[h % 2, n]], axis=0)
            y = jnp.dot(lhs, rhs, preferred_element_type=F32)
            yn = y * lax.rsqrt(jnp.mean(y * y, axis=-1, keepdims=True) + RMS_EPS)
            o_ref[h, rows(n), :] = (g_ref[h, rows(n), :].astype(F32) * yn).astype(o_ref.dtype)

    updates(0)
    for h in range(n_heads):
        if h + 1 < n_heads:
            updates(h + 1)
        scans(h)
        outputs(h)


def _retention(decay_logits, rq, rk, rv, rg):
    n_heads, b, s, _ = rq.shape
    c = RET_BLOCK
    n_blocks = s // c
    qk_spec = pl.BlockSpec((n_heads, None, s, RET_QK_DIM), lambda i, dec: (0, i, 0, 0))
    v_spec = pl.BlockSpec((n_heads, None, s, RET_V_DIM), lambda i, dec: (0, i, 0, 0))
    return pl.pallas_call(
        _ret_kernel,
        grid_spec=pltpu.PrefetchScalarGridSpec(
            num_scalar_prefetch=1,
            grid=(b,),
            in_specs=[qk_spec, qk_spec, v_spec, v_spec],
            out_specs=v_spec,
            scratch_shapes=[pltpu.VMEM((2, n_blocks, 2 * RET_QK_DIM, RET_V_DIM), F32),
                            pltpu.VMEM((2, n_blocks, 2 * RET_QK_DIM, RET_V_DIM), BF16),
                            pltpu.VMEM((n_heads, c, c), F32),
                            pltpu.VMEM((n_heads, 4, c, RET_QK_DIM), BF16),
                            pltpu.VMEM((n_heads, 2, RET_QK_DIM, RET_V_DIM), F32)],
        ),
        out_shape=jax.ShapeDtypeStruct((n_heads, b, s, RET_V_DIM), BF16),
        compiler_params=pltpu.CompilerParams(
            dimension_semantics=("arbitrary",), vmem_limit_bytes=VMEM_LIMIT),
        name="retention",
    )(decay_logits, rq, rk, rv, rg)


def _na_blocks(extent, q_size, k_size, window):
    n = extent // q_size
    out = []
    for i in range(n):
        q0 = i * q_size
        k0 = min(max(q0 - window // 2, 0), extent - k_size)
        out.append((q0, k0, 0 if i == 0 else (2 if i == n - 1 else 1)))
    return tuple(out)


NA_ROWS = 32
_NA_ROW_BLOCKS = _na_blocks(NA_ROWS, NA_Q_ROWS, NA_K_ROWS, NA_WIN_ROWS)
_NA_COL_BLOCKS = _na_blocks(GRID_W, NA_Q_COLS, NA_K_COLS, NA_WIN_COLS)
_NA_CONFIGS = 9


def _na_bias_kernel(rel_ref, out_ref, col_ref):
    nk = NA_K_ROWS * NA_K_COLS
    base = pl.program_id(0) * (NA_REL_ROWS * NA_REL_COLS)
    lane = lax.broadcasted_iota(jnp.int32, (NA_Q_COLS, nk), 1)
    kr = lane // NA_K_COLS
    col_cfgs = (_NA_COL_BLOCKS[0], _NA_COL_BLOCKS[1], _NA_COL_BLOCKS[3])
    row_cfgs = (_NA_ROW_BLOCKS[0], _NA_ROW_BLOCKS[1], _NA_ROW_BLOCKS[3])
    reps = nk // NA_LANES
    kc_one = lax.broadcasted_iota(jnp.int32, (NA_Q_COLS, NA_LANES), 1) % NA_K_COLS
    qc_one = lax.broadcasted_iota(jnp.int32, (NA_Q_COLS, NA_LANES), 0)
    for b, (c0, kc0, _) in enumerate(col_cfgs):
        kc_abs = kc_one + kc0
        qc_abs = qc_one + c0
        rel_col = jnp.clip(kc_abs - qc_abs + NA_WIN_COLS - 1, 0, NA_REL_COLS - 1)
        win = jnp.clip(qc_abs - NA_WIN_COLS // 2, 0, GRID_W - NA_WIN_COLS)
        in_win = (kc_abs >= win) & (kc_abs < win + NA_WIN_COLS)

        def rel_row_body(dr, carry, rel_col=rel_col, in_win=in_win, b=b):
            vals = jnp.zeros((NA_Q_COLS, NA_LANES), F32)
            for dc in range(NA_REL_COLS):
                vals = jnp.where(rel_col == dc, rel_ref[base + dr * NA_REL_COLS + dc], vals)
            vals = jnp.where(in_win, vals * LOG2E, NEG_INF)
            col_ref[b * NA_REL_ROWS + dr] = jnp.concatenate([vals] * reps, axis=1)
            return carry

        lax.fori_loop(0, NA_REL_ROWS, rel_row_body, 0)

    for a, (r0, ks, _) in enumerate(row_cfgs):
        kr_abs = kr + ks
        for qr in range(NA_Q_ROWS):
            r_abs = r0 + qr
            rs = min(max(r_abs - NA_WIN_ROWS // 2, 0), NA_ROWS - NA_WIN_ROWS)
            hits = [kr_abs == rs + j for j in range(NA_WIN_ROWS)]
            for b in range(3):
                tile = jnp.full((NA_Q_COLS, nk), NEG_INF, F32)
                for j in range(NA_WIN_ROWS):
                    dr = rs + j - r_abs + NA_WIN_ROWS - 1
                    tile = jnp.where(hits[j], col_ref[b * NA_REL_ROWS + dr], tile)
                out_ref[0, a * 3 + b, qr * NA_Q_COLS:(qr + 1) * NA_Q_COLS, :] = tile


def _na_bias_tables(rel_bias):
    nq = NA_Q_ROWS * NA_Q_COLS
    nk = NA_K_ROWS * NA_K_COLS
    return pl.pallas_call(
        _na_bias_kernel,
        grid_spec=pltpu.PrefetchScalarGridSpec(
            num_scalar_prefetch=1,
            grid=(NA_HEADS,),
            in_specs=[],
            out_specs=pl.BlockSpec((1, _NA_CONFIGS, nq, nk), lambda h, rel: (h, 0, 0, 0)),
            scratch_shapes=[pltpu.VMEM((3 * NA_REL_ROWS, NA_Q_COLS, nk), F32)],
        ),
        out_shape=jax.ShapeDtypeStruct((NA_HEADS, _NA_CONFIGS, nq, nk), F32),
        compiler_params=pltpu.CompilerParams(dimension_semantics=("parallel",)),
        name="na_bias",
    )(rel_bias.astype(F32).reshape(-1))


def _na_kernel(q_ref, k_ref, v_ref, bias_ref, o_ref, ks_ref, vs_ref):
    n_slabs = q_ref.shape[0]
    nq = NA_Q_ROWS * NA_Q_COLS
    nk = NA_K_ROWS * NA_K_COLS
    shift = NA_WIN_COLS // 2
    kept = GRID_W - 2 * shift
    for sl in range(n_slabs):
        for src, dst in ((k_ref, ks_ref), (v_ref, vs_ref)):
            dst[sl, :, 0:kept, :] = src[sl, :, shift:shift + kept, :]

    def window(refs, sl, block):
        (_, kr0, _), (_, kc0, _) = block
        aligned = kc0 % (2 * shift) == 0
        ref, c = (refs[0], kc0) if aligned else (refs[1], kc0 - shift)
        assert c % (2 * shift) == 0 and (aligned or c + NA_K_COLS <= kept)
        return ref[sl, kr0:kr0 + NA_K_ROWS, c:c + NA_K_COLS, :].reshape(nk, NA_LANES)

    first = lax.broadcasted_iota(jnp.int32, (nq, NA_LANES), 1) < NA_HEAD_DIM
    ones = jnp.ones((nk, NA_LANES), BF16)
    nt = (((1,), (1,)), ((), ()))
    work = [(sl, (rb, cb)) for sl in range(n_slabs) for rb in _NA_ROW_BLOCKS for cb in _NA_COL_BLOCKS]

    def scores(sl, block):
        (r0, _, rcfg), (c0, _, ccfg) = block
        q = q_ref[sl, r0:r0 + NA_Q_ROWS, c0:c0 + NA_Q_COLS, :].reshape(nq, NA_LANES)
        zero = jnp.zeros_like(q)
        q2 = jnp.concatenate([jnp.where(first, q, zero), jnp.where(first, zero, q)], axis=0)
        cfg = rcfg * 3 + ccfg
        bias = jnp.concatenate([bias_ref[2 * sl, cfg], bias_ref[2 * sl + 1, cfg]], axis=0)
        return lax.dot_general(q2, window((k_ref, ks_ref), sl, block), nt,
                               preferred_element_type=F32) + bias

    def attend(sl, block, s):
        m = jnp.max(s, axis=-1, keepdims=True)
        p = jnp.exp2(s - m).astype(BF16)
        o = jnp.dot(p, jnp.concatenate([window((v_ref, vs_ref), sl, block), ones], axis=1),
                    preferred_element_type=F32)
        o = o[:, :NA_LANES] / o[:, NA_LANES:]
        return jnp.where(first, o[:nq], o[nq:])

    s_next = scores(*work[0])
    for i, (sl, blk) in enumerate(work):
        s_cur = s_next
        if i + 1 < len(work):
            s_next = scores(*work[i + 1])
        (r0, _, _), (c0, _, _) = blk
        o_ref[sl, r0:r0 + NA_Q_ROWS, c0:c0 + NA_Q_COLS, :] = (
            attend(sl, blk, s_cur).reshape(NA_Q_ROWS, NA_Q_COLS, NA_LANES).astype(o_ref.dtype))


def _natten(nq, nk, nv, bias_tabs):
    b, rows = nq.shape[1], nq.shape[2]
    per = NA_SLABS_PER_STEP
    slab = pl.BlockSpec((per, None, rows, GRID_W, NA_LANES), lambda j, i: (j, i, 0, 0, 0))
    nqk = NA_Q_ROWS * NA_Q_COLS
    nkk = NA_K_ROWS * NA_K_COLS
    return pl.pallas_call(
        _na_kernel,
        grid=(NA_SLABS // per, b),
        in_specs=[slab, slab, slab,
                  pl.BlockSpec((2 * per, _NA_CONFIGS, nqk, nkk), lambda j, i: (j, 0, 0, 0),
                               pipeline_mode=pl.Buffered(1))],
        out_specs=slab,
        out_shape=jax.ShapeDtypeStruct((NA_SLABS, b, rows, GRID_W, NA_LANES), BF16),
        scratch_shapes=[pltpu.VMEM((per, rows, GRID_W, NA_LANES), BF16),
                        pltpu.VMEM((per, rows, GRID_W, NA_LANES), BF16)],
        compiler_params=pltpu.CompilerParams(
            dimension_semantics=("parallel", "parallel"), vmem_limit_bytes=VMEM_LIMIT),
        name="natten",
    )(nq, nk, nv, bias_tabs)


def _mixout_kernel(x_ref, yr_ref, na_ref, gr_ref, gn_ref, wr_ref, wn_ref, wo_ref, gain_ref, o_ref):
    n_sub = x_ref.shape[0] // SUB_TILE

    def rows(j):
        return slice(j * SUB_TILE, (j + 1) * SUB_TILE)

    def branches(j):
        def gathered(ref):
            return jnp.concatenate([ref[g, rows(j), :] for g in range(ref.shape[0])], axis=1)

        y_ret = jnp.dot(gathered(yr_ref), wr_ref[...], preferred_element_type=F32)
        y_na = jnp.dot(gathered(na_ref), wn_ref[...], preferred_element_type=F32)
        merged = (jax.nn.sigmoid(gr_ref[rows(j), :].astype(F32)) * y_ret
                  + jax.nn.sigmoid(gn_ref[rows(j), :].astype(F32)) * y_na)
        return merged.astype(BF16)

    def finish(j, merged):
        m = jnp.dot(merged, wo_ref[...], preferred_element_type=F32)
        o_ref[rows(j), :] = x_ref[rows(j), :] + _rms(m, gain_ref[...])

    merged = branches(0)
    for j in range(n_sub):
        merged_next = branches(j + 1) if j + 1 < n_sub else None
        finish(j, merged)
        merged = merged_next


def _mix_out(x, y_ret, y_na, g_ret, g_na, w_ret, w_na, w_out, gain):
    t = x.shape[0]
    tm = FFN_TILE

    def row(width):
        return pl.BlockSpec((tm, width), lambda i: (i, 0))

    return pl.pallas_call(
        _mixout_kernel,
        grid=(t // tm,),
        in_specs=[row(D_MODEL),
                  pl.BlockSpec((RET_HEADS, tm, RET_V_DIM), lambda i: (0, i, 0)),
                  pl.BlockSpec((NA_SLABS, tm, NA_LANES), lambda i: (0, i, 0)),
                  row(D_MODEL), row(D_MODEL),
                  _const_spec((RET_V_W, D_MODEL)), _const_spec((NA_W, D_MODEL)),
                  _const_spec((D_MODEL, D_MODEL)), _const_spec((1, D_MODEL))],
        out_specs=row(D_MODEL),
        out_shape=jax.ShapeDtypeStruct((t, D_MODEL), F32),
        compiler_params=pltpu.CompilerParams(
            dimension_semantics=("parallel",), vmem_limit_bytes=VMEM_LIMIT),
        name="mix_out",
    )(x, y_ret, y_na, g_ret, g_na, w_ret, w_na, w_out, gain)


def kernel(x, ffn1_pre_norm, ffn1_w_in, ffn1_w_out, ffn1_post_norm, mix_pre_norm, w_mix_in,
           ret_decay_fwd, ret_decay_bwd, na_rel_bias, w_ret_out, w_na_out, w_mix_out, mix_post_norm,
           ffn2_pre_norm, ffn2_w_in, ffn2_w_out, ffn2_post_norm):
    b, s, d = x.shape
    assert d == D_MODEL and s % GRID_W == 0 and s % RET_BLOCK == 0 and (b * s) % TOKEN_TILE == 0
    assert s % TOKEN_TILE == 0
    rows = s // GRID_W
    assert rows == NA_ROWS, "the neighbourhood-attention tiling is laid out for a 32 x 64 token grid"
    t = b * s
    xt = x.reshape(t, d)
    for l in range(ffn1_w_in.shape[0]):
        gain = lambda g: g[l].reshape(1, D_MODEL).astype(F32)
        wt = lambda w: w[l].astype(BF16)
        xt = _ffn(xt, gain(ffn1_pre_norm), wt(ffn1_w_in), wt(ffn1_w_out), gain(ffn1_post_norm))
        rq, rk, rv, rg, nq, nk, nv, g_ret, g_na = _mix_in(xt, gain(mix_pre_norm), wt(w_mix_in), s)
        decay = jnp.stack([ret_decay_fwd[l], ret_decay_bwd[l]]).astype(F32)
        seq = lambda a: a.reshape(a.shape[0], b, s, a.shape[-1])
        y_ret = _retention(decay, seq(rq), seq(rk), seq(rv), seq(rg)).reshape(RET_HEADS, t, RET_V_DIM)
        grid5 = lambda a: a.reshape(NA_SLABS, b, rows, GRID_W, NA_LANES)
        y_na = _natten(grid5(nq), grid5(nk), grid5(nv), _na_bias_tables(na_rel_bias[l]))
        xt = _mix_out(xt, y_ret, y_na.reshape(NA_SLABS, t, NA_LANES), g_ret, g_na,
                      wt(w_ret_out), wt(w_na_out), wt(w_mix_out), gain(mix_post_norm))
        xt = _ffn(xt, gain(ffn2_pre_norm), wt(ffn2_w_in), wt(ffn2_w_out), gain(ffn2_post_norm))
    return xt.reshape(b, s, d)
```

```python
import numpy as np
import jax
import jax.numpy as jnp
from jax import lax
from jax.experimental import pallas as pl
from jax.experimental.pallas import tpu as pltpu

F32 = jnp.float32
BF16 = jnp.bfloat16

D_MODEL = 1024
D_FF = 2816
RMS_EPS = 1e-6
ROPE_BASE = 10000.0
NEG_INF = -1e30
LOG2E = 1.4426950408889634

RET_HEADS = 4
RET_QK_DIM = 128
RET_V_DIM = 256
RET_QK_W = RET_HEADS * RET_QK_DIM
RET_V_W = RET_HEADS * RET_V_DIM
RET_BLOCK = 256

NA_HEADS = 8
NA_HEAD_DIM = 64
NA_W = NA_HEADS * NA_HEAD_DIM
GRID_W = 64
NA_WIN_ROWS = 8
NA_WIN_COLS = 16
NA_REL_ROWS = 2 * NA_WIN_ROWS - 1
NA_REL_COLS = 2 * NA_WIN_COLS - 1
NA_Q_ROWS = 8
NA_Q_COLS = 16
NA_K_ROWS = 16
NA_K_COLS = 32
NA_LANES = 128
NA_SLABS = NA_W // NA_LANES
NA_SLABS_PER_STEP = 4

MIX_SPLITS = (RET_QK_W, RET_QK_W, RET_V_W, RET_V_W, NA_W, NA_W, NA_W, D_MODEL, D_MODEL)
MIX_OFFS = tuple(int(v) for v in np.cumsum((0,) + MIX_SPLITS[:-1]))
MIX_IN_W = sum(MIX_SPLITS)

TOKEN_TILE = 1024
FFN_TILE = 1024
SUB_TILE = 256
FF_CHUNK = 256
VMEM_LIMIT = 56 * 1024 * 1024


def _rms(x, gain):
    ms = jnp.mean(x * x, axis=-1, keepdims=True)
    return x * lax.rsqrt(ms + RMS_EPS) * gain


def _silu(x):
    return x * jax.nn.sigmoid(x)


def _const_spec(shape):
    nd = len(shape)
    return pl.BlockSpec(shape, lambda *_: (0,) * nd, pipeline_mode=pl.Buffered(1))


def _ffn_kernel(x_ref, gpre_ref, win_ref, wout_ref, gpost_ref, o_ref, act_ref):
    n_sub = x_ref.shape[0] // SUB_TILE

    def rows(j):
        return slice(j * SUB_TILE, (j + 1) * SUB_TILE)

    def pre(j):
        return _rms(x_ref[rows(j), :], gpre_ref[...]).astype(BF16)

    def up(j, xn):
        for c in range(D_FF // FF_CHUNK):
            lo = c * FF_CHUNK
            g = jnp.dot(xn, win_ref[:, lo:lo + FF_CHUNK], preferred_element_type=F32)
            u = jnp.dot(xn, win_ref[:, D_FF + lo:D_FF + lo + FF_CHUNK], preferred_element_type=F32)
            act_ref[rows(j), lo:lo + FF_CHUNK] = (_silu(g) * u).astype(BF16)

    def down(j):
        h = jnp.dot(act_ref[rows(j), :], wout_ref[...], preferred_element_type=F32)
        o_ref[rows(j), :] = x_ref[rows(j), :] + 0.5 * _rms(h, gpost_ref[...])

    xn = pre(0)
    for j in range(n_sub):
        xn_next = pre(j + 1) if j + 1 < n_sub else None
        up(j, xn)
        if j >= 1:
            down(j - 1)
        xn = xn_next
    down(n_sub - 1)


def _ffn(x, gpre, w_in, w_out, gpost):
    t = x.shape[0]
    tm = FFN_TILE
    row = pl.BlockSpec((tm, D_MODEL), lambda i: (i, 0))
    return pl.pallas_call(
        _ffn_kernel,
        grid=(t // tm,),
        in_specs=[row, _const_spec((1, D_MODEL)), _const_spec((D_MODEL, 2 * D_FF)),
                  _const_spec((D_FF, D_MODEL)), _const_spec((1, D_MODEL))],
        out_specs=row,
        out_shape=jax.ShapeDtypeStruct((t, D_MODEL), F32),
        scratch_shapes=[pltpu.VMEM((tm, D_FF), BF16)],
        compiler_params=pltpu.CompilerParams(
            dimension_semantics=("parallel",), vmem_limit_bytes=VMEM_LIMIT),
        name="ffn",
    )(x, gpre, w_in, w_out, gpost)


def _mixin_kernel(x_ref, gain_ref, w_ref, rope_ref,
                  rq_ref, rk_ref, rv_ref, rg_ref, nq_ref, nk_ref, nv_ref, gr_ref, gn_ref):
    n_sub = x_ref.shape[0] // SUB_TILE

    def pre(j):
        rows = slice(j * SUB_TILE, (j + 1) * SUB_TILE)
        return _rms(x_ref[rows, :], gain_ref[...]).astype(BF16)

    def project(j, u):
        rows = slice(j * SUB_TILE, (j + 1) * SUB_TILE)

        def proj(idx):
            lo = MIX_OFFS[idx]
            return jnp.dot(u, w_ref[:, lo:lo + MIX_SPLITS[idx]], preferred_element_type=F32)

        def store_groups(out_ref, y):
            width = out_ref.shape[-1]
            for g in range(out_ref.shape[0]):
                out_ref[g, rows, :] = y[:, g * width:(g + 1) * width].astype(out_ref.dtype)

        def rotary(y, cos, sin, out_ref):
            for h in range(RET_HEADS):
                yh = y[:, h * RET_QK_DIM:(h + 1) * RET_QK_DIM]
                out_ref[h, rows, :] = (yh * cos + pltpu.roll(yh, RET_QK_DIM // 2, 1) * sin).astype(out_ref.dtype)

        rotary(proj(0), rope_ref[0, rows, :], rope_ref[1, rows, :], rq_ref)
        rotary(proj(1), rope_ref[2, rows, :], rope_ref[3, rows, :], rk_ref)
        store_groups(rv_ref, proj(2))
        store_groups(rg_ref, _silu(proj(3)))
        store_groups(nq_ref, proj(4) * (LOG2E * NA_HEAD_DIM ** -0.5))
        store_groups(nk_ref, proj(5))
        store_groups(nv_ref, proj(6))
        gr_ref[rows, :] = proj(7).astype(gr_ref.dtype)
        gn_ref[rows, :] = proj(8).astype(gn_ref.dtype)

    u = pre(0)
    for j in range(n_sub):
        u_next = pre(j + 1) if j + 1 < n_sub else None
        project(j, u)
        u = u_next


def _rope_tables(seq):
    half = RET_QK_DIM // 2
    pos = jnp.arange(seq, dtype=F32)
    inv = 1.0 / (ROPE_BASE ** jnp.linspace(0.0, 1.0, half, dtype=F32))
    ang = pos[:, None] * inv[None, :]
    cos, sin = jnp.cos(ang), jnp.sin(ang)
    cos2 = jnp.concatenate([cos, cos], axis=-1)
    sin2 = jnp.concatenate([-sin, sin], axis=-1)
    ks = RET_QK_DIM ** -0.5
    return jnp.stack([cos2, sin2, cos2 * ks, sin2 * ks])


def _mix_in(x, gain, w, seq):
    t = x.shape[0]
    tm = TOKEN_TILE
    tiles_per_seq = seq // tm
    out_dtypes = (BF16,) * len(MIX_SPLITS)
    out_groups = (RET_HEADS,) * 4 + (NA_SLABS,) * 3 + (1, 1)
    return pl.pallas_call(
        _mixin_kernel,
        grid=(t // tm,),
        in_specs=[pl.BlockSpec((tm, D_MODEL), lambda i: (i, 0)),
                  _const_spec((1, D_MODEL)), _const_spec((D_MODEL, MIX_IN_W)),
                  pl.BlockSpec((4, tm, RET_QK_DIM), lambda i: (0, i % tiles_per_seq, 0))],
        out_specs=[pl.BlockSpec((tm, wd), lambda i: (i, 0)) if g == 1
                   else pl.BlockSpec((g, tm, wd // g), lambda i: (0, i, 0))
                   for wd, g in zip(MIX_SPLITS, out_groups)],
        out_shape=[jax.ShapeDtypeStruct((t, wd) if g == 1 else (g, t, wd // g), dt)
                   for wd, g, dt in zip(MIX_SPLITS, out_groups, out_dtypes)],
        compiler_params=pltpu.CompilerParams(
            dimension_semantics=("parallel",), vmem_limit_bytes=VMEM_LIMIT),
        name="mix_in",
    )(x, gain, w, _rope_tables(seq))


def _log_sigmoid(z):
    return jnp.minimum(z, 0.0) - jnp.log1p(jnp.exp(-jnp.abs(z)))


def _ret_kernel(dec_ref, q_ref, k_ref, v_ref, g_ref, o_ref, u_ref, st_ref, dm_ref, vec_ref, blk_ref):
    c = RET_BLOCK
    dk = RET_QK_DIM
    dv = RET_V_DIM
    n_heads = q_ref.shape[0]
    n_blocks = q_ref.shape[1] // c
    tn = (((0,), (0,)), ((), ()))
    nt = (((1,), (1,)), ((), ()))

    @pl.when(pl.program_id(0) == 0)
    def _decay_tables():
        def log_gamma(z, shape):
            return _log_sigmoid(jnp.full(shape, z, F32))

        ri = lax.broadcasted_iota(jnp.int32, (c, c), 0).astype(F32)
        ci = lax.broadcasted_iota(jnp.int32, (c, c), 1).astype(F32)
        diff = ri - ci
        r = lax.broadcasted_iota(jnp.int32, (c, dk), 0).astype(F32)
        for h in range(n_heads):
            zf = dec_ref[0, h]
            zb = dec_ref[1, h]
            dm_ref[h] = jnp.where(diff >= 0, jnp.exp(log_gamma(zf, (c, c)) * jnp.maximum(diff, 0.0)),
                                  jnp.exp(log_gamma(zb, (c, c)) * jnp.maximum(-diff, 0.0)))
            lgf = log_gamma(zf, (c, dk))
            lgb = log_gamma(zb, (c, dk))
            vec_ref[h, 0] = jnp.exp(lgf * (r + 1.0)).astype(BF16)
            vec_ref[h, 1] = jnp.exp(lgb * (c - r)).astype(BF16)
            vec_ref[h, 2] = jnp.exp(lgf * (c - 1.0 - r)).astype(BF16)
            vec_ref[h, 3] = jnp.exp(lgb * r).astype(BF16)
            blk_ref[h, 0] = jnp.exp(log_gamma(zf, (dk, dv)) * c)
            blk_ref[h, 1] = jnp.exp(log_gamma(zb, (dk, dv)) * c)

    def rows(n):
        return slice(n * c, (n + 1) * c)

    def updates(h):
        for n in range(n_blocks):
            k = k_ref[h, rows(n), :]
            k_cat = jnp.concatenate([k * vec_ref[h, 2], k * vec_ref[h, 3]], axis=1)
            u_ref[h % 2, n] = lax.dot_general(k_cat, v_ref[h, rows(n), :], tn, preferred_element_type=F32)

    def scans(h):
        sf = jnp.zeros((dk, dv), F32)
        for n in range(n_blocks):
            st_ref[h % 2, n, 0:dk, :] = sf.astype(BF16)
            if n + 1 < n_blocks:
                sf = blk_ref[h, 0] * sf + u_ref[h % 2, n, 0:dk, :]
        sb = jnp.zeros((dk, dv), F32)
        for n in reversed(range(n_blocks)):
            st_ref[h % 2, n, dk:2 * dk, :] = sb.astype(BF16)
            if n > 0:
                sb = blk_ref[h, 1] * sb + u_ref[h % 2, n, dk:2 * dk, :]

    def outputs(h):
        def scores(n):
            return lax.dot_general(q_ref[h, rows(n), :], k_ref[h, rows(n), :], nt,
                                   preferred_element_type=F32)

        s_next = scores(0)
        for n in range(n_blocks):
            s_cur = s_next
            if n + 1 < n_blocks:
                s_next = scores(n + 1)
            p = (s_cur * dm_ref[h]).astype(BF16)
            q = q_ref[h, rows(n), :]
            lhs = jnp.concatenate([p, q * vec_ref[h, 0], q * vec_ref[h, 1]], axis=1)
            rhs = jnp.concatenate([v_ref[h, rows(n), :], st_ref[h % 2, n]], axis=0)
            y = jnp.dot(lhs, rhs, preferred_element_type=F32)
            yn = y * lax.rsqrt(jnp.mean(y * y, axis=-1, keepdims=True) + RMS_EPS)
            o_ref[h, rows(n), :] = (g_ref[h, rows(n), :].astype(F32) * yn).astype(o_ref.dtype)

    updates(0)
    for h in range(n_heads):
        if h + 1 < n_heads:
            updates(h + 1)
        scans(h)
        outputs(h)


def _retention(decay_logits, rq, rk, rv, rg):
    n_heads, b, s, _ = rq.shape
    c = RET_BLOCK
    n_blocks = s // c
    qk_spec = pl.BlockSpec((n_heads, None, s, RET_QK_DIM), lambda i, dec: (0, i, 0, 0))
    v_spec = pl.BlockSpec((n_heads, None, s, RET_V_DIM), lambda i, dec: (0, i, 0, 0))
    return pl.pallas_call(
        _ret_kernel,
        grid_spec=pltpu.PrefetchScalarGridSpec(
            num_scalar_prefetch=1,
            grid=(b,),
            in_specs=[qk_spec, qk_spec, v_spec, v_spec],
            out_specs=v_spec,
            scratch_shapes=[pltpu.VMEM((2, n_blocks, 2 * RET_QK_DIM, RET_V_DIM), F32),
                            pltpu.VMEM((2, n_blocks, 2 * RET_QK_DIM, RET_V_DIM), BF16),
                            pltpu.VMEM((n_heads, c, c), F32),
                            pltpu.VMEM((n_heads, 4, c, RET_QK_DIM), BF16),
                            pltpu.VMEM((n_heads, 2, RET_QK_DIM, RET_V_DIM), F32)],
        ),
        out_shape=jax.ShapeDtypeStruct((n_heads, b, s, RET_V_DIM), BF16),
        compiler_params=pltpu.CompilerParams(
            dimension_semantics=("arbitrary",), vmem_limit_bytes=VMEM_LIMIT),
        name="retention",
    )(decay_logits, rq, rk, rv, rg)


def _na_blocks(extent, q_size, k_size, window):
    n = extent // q_size
    out = []
    for i in range(n):
        q0 = i * q_size
        k0 = min(max(q0 - window // 2, 0), extent - k_size)
        out.append((q0, k0, 0 if i == 0 else (2 if i == n - 1 else 1)))
    return tuple(out)


NA_ROWS = 32
_NA_ROW_BLOCKS = _na_blocks(NA_ROWS, NA_Q_ROWS, NA_K_ROWS, NA_WIN_ROWS)
_NA_COL_BLOCKS = _na_blocks(GRID_W, NA_Q_COLS, NA_K_COLS, NA_WIN_COLS)
_NA_CONFIGS = 9


def _na_bias_kernel(rel_ref, out_ref, col_ref):
    nk = NA_K_ROWS * NA_K_COLS
    base = pl.program_id(0) * (NA_REL_ROWS * NA_REL_COLS)
    lane = lax.broadcasted_iota(jnp.int32, (NA_Q_COLS, nk), 1)
    kr = lane // NA_K_COLS
    col_cfgs = (_NA_COL_BLOCKS[0], _NA_COL_BLOCKS[1], _NA_COL_BLOCKS[3])
    row_cfgs = (_NA_ROW_BLOCKS[0], _NA_ROW_BLOCKS[1], _NA_ROW_BLOCKS[3])
    reps = nk // NA_LANES
    kc_one = lax.broadcasted_iota(jnp.int32, (NA_Q_COLS, NA_LANES), 1) % NA_K_COLS
    qc_one = lax.broadcasted_iota(jnp.int32, (NA_Q_COLS, NA_LANES), 0)
    for b, (c0, kc0, _) in enumerate(col_cfgs):
        kc_abs = kc_one + kc0
        qc_abs = qc_one + c0
        rel_col = jnp.clip(kc_abs - qc_abs + NA_WIN_COLS - 1, 0, NA_REL_COLS - 1)
        win = jnp.clip(qc_abs - NA_WIN_COLS // 2, 0, GRID_W - NA_WIN_COLS)
        in_win = (kc_abs >= win) & (kc_abs < win + NA_WIN_COLS)

        def rel_row_body(dr, carry, rel_col=rel_col, in_win=in_win, b=b):
            vals = jnp.zeros((NA_Q_COLS, NA_LANES), F32)
            for dc in range(NA_REL_COLS):
                vals = jnp.where(rel_col == dc, rel_ref[base + dr * NA_REL_COLS + dc], vals)
            vals = jnp.where(in_win, vals * LOG2E, NEG_INF)
            col_ref[b * NA_REL_ROWS + dr] = jnp.concatenate([vals] * reps, axis=1)
            return carry

        lax.fori_loop(0, NA_REL_ROWS, rel_row_body, 0)

    for a, (r0, ks, _) in enumerate(row_cfgs):
        kr_abs = kr + ks
        for qr in range(NA_Q_ROWS):
            r_abs = r0 + qr
            rs = min(max(r_abs - NA_WIN_ROWS // 2, 0), NA_ROWS - NA_WIN_ROWS)
            hits = [kr_abs == rs + j for j in range(NA_WIN_ROWS)]
            for b in range(3):
                tile = jnp.full((NA_Q_COLS, nk), NEG_INF, F32)
                for j in range(NA_WIN_ROWS):
                    dr = rs + j - r_abs + NA_WIN_ROWS - 1
                    tile = jnp.where(hits[j], col_ref[b * NA_REL_ROWS + dr], tile)
                out_ref[0, a * 3 + b, qr * NA_Q_COLS:(qr + 1) * NA_Q_COLS, :] = tile


def _na_bias_tables(rel_bias):
    nq = NA_Q_ROWS * NA_Q_COLS
    nk = NA_K_ROWS * NA_K_COLS
    return pl.pallas_call(
        _na_bias_kernel,
        grid_spec=pltpu.PrefetchScalarGridSpec(
            num_scalar_prefetch=1,
            grid=(NA_HEADS,),
            in_specs=[],
            out_specs=pl.BlockSpec((1, _NA_CONFIGS, nq, nk), lambda h, rel: (h, 0, 0, 0)),
            scratch_shapes=[pltpu.VMEM((3 * NA_REL_ROWS, NA_Q_COLS, nk), F32)],
        ),
        out_shape=jax.ShapeDtypeStruct((NA_HEADS, _NA_CONFIGS, nq, nk), F32),
        compiler_params=pltpu.CompilerParams(dimension_semantics=("parallel",)),
        name="na_bias",
    )(rel_bias.astype(F32).reshape(-1))


def _na_kernel(q_ref, k_ref, v_ref, bias_ref, o_ref, ks_ref, vs_ref):
    n_slabs = q_ref.shape[0]
    nq = NA_Q_ROWS * NA_Q_COLS
    nk = NA_K_ROWS * NA_K_COLS
    shift = NA_WIN_COLS // 2
    kept = GRID_W - 2 * shift
    for sl in range(n_slabs):
        for src, dst in ((k_ref, ks_ref), (v_ref, vs_ref)):
            dst[sl, :, 0:kept, :] = src[sl, :, shift:shift + kept, :]

    def window(refs, sl, block):
        (_, kr0, _), (_, kc0, _) = block
        aligned = kc0 % (2 * shift) == 0
        ref, c = (refs[0], kc0) if aligned else (refs[1], kc0 - shift)
        assert c % (2 * shift) == 0 and (aligned or c + NA_K_COLS <= kept)
        return ref[sl, kr0:kr0 + NA_K_ROWS, c:c + NA_K_COLS, :].reshape(nk, NA_LANES)

    first = lax.broadcasted_iota(jnp.int32, (nq, NA_LANES), 1) < NA_HEAD_DIM
    ones = jnp.ones((nk, NA_LANES), BF16)
    nt = (((1,), (1,)), ((), ()))
    work = [(sl, (rb, cb)) for sl in range(n_slabs) for rb in _NA_ROW_BLOCKS for cb in _NA_COL_BLOCKS]

    def scores(sl, block):
        (r0, _, rcfg), (c0, _, ccfg) = block
        q = q_ref[sl, r0:r0 + NA_Q_ROWS, c0:c0 + NA_Q_COLS, :].reshape(nq, NA_LANES)
        zero = jnp.zeros_like(q)
        q2 = jnp.concatenate([jnp.where(first, q, zero), jnp.where(first, zero, q)], axis=0)
        cfg = rcfg * 3 + ccfg
        bias = jnp.concatenate([bias_ref[2 * sl, cfg], bias_ref[2 * sl + 1, cfg]], axis=0)
        return lax.dot_general(q2, window((k_ref, ks_ref), sl, block), nt,
                               preferred_element_type=F32) + bias

    def attend(sl, block, s):
        m = jnp.max(s, axis=-1, keepdims=True)
        p = jnp.exp2(s - m).astype(BF16)
        o = jnp.dot(p, jnp.concatenate([window((v_ref, vs_ref), sl, block), ones], axis=1),
                    preferred_element_type=F32)
        o = o[:, :NA_LANES] / o[:, NA_LANES:]
        return jnp.where(first, o[:nq], o[nq:])

    s_next = scores(*work[0])
    for i, (sl, blk) in enumerate(work):
        s_cur = s_next
        if i + 1 < len(work):
            s_next = scores(*work[i + 1])
        (r0, _, _), (c0, _, _) = blk
        o_ref[sl, r0:r0 + NA_Q_ROWS, c0:c0 + NA_Q_COLS, :] = (
            attend(sl, blk, s_cur).reshape(NA_Q_ROWS, NA_Q_COLS, NA_LANES).astype(o_ref.dtype))


def _natten(nq, nk, nv, bias_tabs):
    b, rows = nq.shape[1], nq.shape[2]
    per = NA_SLABS_PER_STEP
    slab = pl.BlockSpec((per, None, rows, GRID_W, NA_LANES), lambda j, i: (j, i, 0, 0, 0))
    nqk = NA_Q_ROWS * NA_Q_COLS
    nkk = NA_K_ROWS * NA_K_COLS
    return pl.pallas_call(
        _na_kernel,
        grid=(NA_SLABS // per, b),
        in_specs=[slab, slab, slab,
                  pl.BlockSpec((2 * per, _NA_CONFIGS, nqk, nkk), lambda j, i: (j, 0, 0, 0),
                               pipeline_mode=pl.Buffered(1))],
        out_specs=slab,
        out_shape=jax.ShapeDtypeStruct((NA_SLABS, b, rows, GRID_W, NA_LANES), BF16),
        scratch_shapes=[pltpu.VMEM((per, rows, GRID_W, NA_LANES), BF16),
                        pltpu.VMEM((per, rows, GRID_W, NA_LANES), BF16)],
        compiler_params=pltpu.CompilerParams(
            dimension_semantics=("parallel", "parallel"), vmem_limit_bytes=VMEM_LIMIT),
        name="natten",
    )(nq, nk, nv, bias_tabs)


def _mixout_kernel(x_ref, yr_ref, na_ref, gr_ref, gn_ref, wr_ref, wn_ref, wo_ref, gain_ref, o_ref):
    n_sub = x_ref.shape[0] // SUB_TILE

    def rows(j):
        return slice(j * SUB_TILE, (j + 1) * SUB_TILE)

    def branches(j):
        def gathered(ref):
            return jnp.concatenate([ref[g, rows(j), :] for g in range(ref.shape[0])], axis=1)

        y_ret = jnp.dot(gathered(yr_ref), wr_ref[...], preferred_element_type=F32)
        y_na = jnp.dot(gathered(na_ref), wn_ref[...], preferred_element_type=F32)
        merged = (jax.nn.sigmoid(gr_ref[rows(j), :].astype(F32)) * y_ret
                  + jax.nn.sigmoid(gn_ref[rows(j), :].astype(F32)) * y_na)
        return merged.astype(BF16)

    def finish(j, merged):
        m = jnp.dot(merged, wo_ref[...], preferred_element_type=F32)
        o_ref[rows(j), :] = x_ref[rows(j), :] + _rms(m, gain_ref[...])

    merged = branches(0)
    for j in range(n_sub):
        merged_next = branches(j + 1) if j + 1 < n_sub else None
        finish(j, merged)
        merged = merged_next


def _mix_out(x, y_ret, y_na, g_ret, g_na, w_ret, w_na, w_out, gain):
    t = x.shape[0]
    tm = FFN_TILE

    def row(width):
        return pl.BlockSpec((tm, width), lambda i: (i, 0))

    return pl.pallas_call(
        _mixout_kernel,
        grid=(t // tm,),
        in_specs=[row(D_MODEL),
                  pl.BlockSpec((RET_HEADS, tm, RET_V_DIM), lambda i: (0, i, 0)),
                  pl.BlockSpec((NA_SLABS, tm, NA_LANES), lambda i: (0, i, 0)),
                  row(D_MODEL), row(D_MODEL),
                  _const_spec((RET_V_W, D_MODEL)), _const_spec((NA_W, D_MODEL)),
                  _const_spec((D_MODEL, D_MODEL)), _const_spec((1, D_MODEL))],
        out_specs=row(D_MODEL),
        out_shape=jax.ShapeDtypeStruct((t, D_MODEL), F32),
        compiler_params=pltpu.CompilerParams(
            dimension_semantics=("parallel",), vmem_limit_bytes=VMEM_LIMIT),
        name="mix_out",
    )(x, y_ret, y_na, g_ret, g_na, w_ret, w_na, w_out, gain)


def kernel(x, ffn1_pre_norm, ffn1_w_in, ffn1_w_out, ffn1_post_norm, mix_pre_norm, w_mix_in,
           ret_decay_fwd, ret_decay_bwd, na_rel_bias, w_ret_out, w_na_out, w_mix_out, mix_post_norm,
           ffn2_pre_norm, ffn2_w_in, ffn2_w_out, ffn2_post_norm):
    b, s, d = x.shape
    assert d == D_MODEL and s % GRID_W == 0 and s % RET_BLOCK == 0 and (b * s) % TOKEN_TILE == 0
    assert s % TOKEN_TILE == 0
    rows = s // GRID_W
    assert rows == NA_ROWS, "the neighbourhood-attention tiling is laid out for a 32 x 64 token grid"
    t = b * s
    xt = x.reshape(t, d)
    for l in range(ffn1_w_in.shape[0]):
        gain = lambda g: g[l].reshape(1, D_MODEL).astype(F32)
        wt = lambda w: w[l].astype(BF16)
        xt = _ffn(xt, gain(ffn1_pre_norm), wt(ffn1_w_in), wt(ffn1_w_out), gain(ffn1_post_norm))
        rq, rk, rv, rg, nq, nk, nv, g_ret, g_na = _mix_in(xt, gain(mix_pre_norm), wt(w_mix_in), s)
        decay = jnp.stack([ret_decay_fwd[l], ret_decay_bwd[l]]).astype(F32)
        seq = lambda a: a.reshape(a.shape[0], b, s, a.shape[-1])
        y_ret = _retention(decay, seq(rq), seq(rk), seq(rv), seq(rg)).reshape(RET_HEADS, t, RET_V_DIM)
        grid5 = lambda a: a.reshape(NA_SLABS, b, rows, GRID_W, NA_LANES)
        y_na = _natten(grid5(nq), grid5(nk), grid5(nv), _na_bias_tables(na_rel_bias[l]))
        xt = _mix_out(xt, y_ret, y_na.reshape(NA_SLABS, t, NA_LANES), g_ret, g_na,
                      wt(w_ret_out), wt(w_na_out), wt(w_mix_out), gain(mix_post_norm))
        xt = _ffn(xt, gain(ffn2_pre_norm), wt(ffn2_w_in), wt(ffn2_w_out), gain(ffn2_post_norm))
    return xt.reshape(b, s, d)
```

```python
import functools

import numpy as np
import jax
import jax.numpy as jnp
from jax import lax
from jax.experimental import pallas as pl
from jax.experimental.pallas import tpu as pltpu

F32 = jnp.float32
BF16 = jnp.bfloat16

D_MODEL = 1024
D_FF = 2816
RMS_EPS = 1e-6
ROPE_BASE = 10000.0
NEG_INF = -1e30
LOG2E = 1.4426950408889634

RET_HEADS = 4
RET_QK_DIM = 128
RET_V_DIM = 256
RET_QK_W = RET_HEADS * RET_QK_DIM
RET_V_W = RET_HEADS * RET_V_DIM
RET_BLOCK = 256

NA_HEADS = 8
NA_HEAD_DIM = 64
NA_W = NA_HEADS * NA_HEAD_DIM
GRID_W = 64
NA_WIN_ROWS = 8
NA_WIN_COLS = 16
NA_REL_ROWS = 2 * NA_WIN_ROWS - 1
NA_REL_COLS = 2 * NA_WIN_COLS - 1
NA_Q_ROWS = 8
NA_Q_COLS = 16
NA_K_ROWS = 16
NA_K_COLS = 32
NA_LANES = 128
NA_SLABS = NA_W // NA_LANES
NA_SLABS_PER_STEP = 4

MIX_SPLITS = (RET_QK_W, RET_QK_W, RET_V_W, RET_V_W, NA_W, NA_W, NA_W, D_MODEL, D_MODEL)
MIX_OFFS = tuple(int(v) for v in np.cumsum((0,) + MIX_SPLITS[:-1]))
MIX_IN_W = sum(MIX_SPLITS)

TOKEN_TILE = 512
FFN_TILE = 1024
SUB_TILE = 256
FF_CHUNK = 256
VMEM_LIMIT = 56 * 1024 * 1024
BF16_SUBLANES = 16


def _rms(x, gain):
    ms = jnp.mean(x * x, axis=-1, keepdims=True)
    return x * lax.rsqrt(ms + RMS_EPS) * gain


def _silu(x):
    return x * jax.nn.sigmoid(x)


def _cast_riders(weights, steps):
    in_specs, out_shapes = [], []
    for w in weights:
        rows, cols = w.shape
        block = min(d for d in range(BF16_SUBLANES, rows + 1, BF16_SUBLANES)
                    if rows % d == 0 and rows // d <= steps)
        last = rows // block - 1
        in_specs.append(pl.BlockSpec((block, cols), lambda i, last=last: (jnp.minimum(i, last), 0)))
        out_shapes.append(jax.ShapeDtypeStruct(w.shape, BF16))
    return in_specs, out_shapes


def _run_cast_riders(src_refs, dst_refs):
    for src, dst in zip(src_refs, dst_refs):
        dst[...] = src[...].astype(dst.dtype)


def _const_spec(shape):
    nd = len(shape)
    return pl.BlockSpec(shape, lambda *_: (0,) * nd, pipeline_mode=pl.Buffered(1))


def _ffn_kernel(n_riders, x_ref, gpre_ref, win_ref, wout_ref, gpost_ref, *rest):
    o_ref, act_ref = rest[n_riders], rest[-1]
    _run_cast_riders(rest[:n_riders], rest[n_riders + 1:-1])
    n_sub = x_ref.shape[0] // SUB_TILE

    def rows(j):
        return slice(j * SUB_TILE, (j + 1) * SUB_TILE)

    def pre(j):
        return _rms(x_ref[rows(j), :], gpre_ref[...]).astype(BF16)

    def up(j, xn):
        for c in range(D_FF // FF_CHUNK):
            lo = c * FF_CHUNK
            g = jnp.dot(xn, win_ref[:, lo:lo + FF_CHUNK], preferred_element_type=F32)
            u = jnp.dot(xn, win_ref[:, D_FF + lo:D_FF + lo + FF_CHUNK], preferred_element_type=F32)
            act_ref[rows(j), lo:lo + FF_CHUNK] = (_silu(g) * u).astype(BF16)

    def down(j):
        h = jnp.dot(act_ref[rows(j), :], wout_ref[...], preferred_element_type=F32)
        o_ref[rows(j), :] = x_ref[rows(j), :] + 0.5 * _rms(h, gpost_ref[...])

    xn = pre(0)
    for j in range(n_sub):
        xn_next = pre(j + 1) if j + 1 < n_sub else None
        up(j, xn)
        if j >= 1:
            down(j - 1)
        xn = xn_next
    down(n_sub - 1)


def _ffn(x, gpre, w_in, w_out, gpost, riders=()):
    t = x.shape[0]
    tm = FFN_TILE
    row = pl.BlockSpec((tm, D_MODEL), lambda i: (i, 0))
    rider_specs, rider_shapes = _cast_riders(riders, t // tm)
    out, *cast = pl.pallas_call(
        functools.partial(_ffn_kernel, len(riders)),
        grid=(t // tm,),
        in_specs=[row, _const_spec((1, D_MODEL)), _const_spec((D_MODEL, 2 * D_FF)),
                  _const_spec((D_FF, D_MODEL)), _const_spec((1, D_MODEL))] + rider_specs,
        out_specs=[row] + rider_specs,
        out_shape=[jax.ShapeDtypeStruct((t, D_MODEL), F32)] + rider_shapes,
        scratch_shapes=[pltpu.VMEM((tm, D_FF), BF16)],
        compiler_params=pltpu.CompilerParams(
            dimension_semantics=("arbitrary",), vmem_limit_bytes=VMEM_LIMIT),
        name="ffn",
    )(x, gpre, w_in, w_out, gpost, *riders)
    return out, cast


def _mixin_kernel(n_riders, x_ref, gain_ref, w_ref, rope_ref, *rest):
    rq_ref, rk_ref, rv_ref, rg_ref, nq_ref, nk_ref, nv_ref, gr_ref, gn_ref = rest[n_riders:n_riders + 9]
    _run_cast_riders(rest[:n_riders], rest[n_riders + 9:])
    n_sub = x_ref.shape[0] // SUB_TILE

    def pre(j):
        rows = slice(j * SUB_TILE, (j + 1) * SUB_TILE)
        return _rms(x_ref[rows, :], gain_ref[...]).astype(BF16)

    def project(j, u):
        rows = slice(j * SUB_TILE, (j + 1) * SUB_TILE)

        def proj(idx):
            lo = MIX_OFFS[idx]
            return jnp.dot(u, w_ref[:, lo:lo + MIX_SPLITS[idx]], preferred_element_type=F32)

        def store_groups(out_ref, y):
            width = out_ref.shape[-1]
            for g in range(out_ref.shape[0]):
                out_ref[g, rows, :] = y[:, g * width:(g + 1) * width].astype(out_ref.dtype)

        def rotary(y, cos, sin, out_ref):
            for h in range(RET_HEADS):
                yh = y[:, h * RET_QK_DIM:(h + 1) * RET_QK_DIM]
                out_ref[h, rows, :] = (yh * cos + pltpu.roll(yh, RET_QK_DIM // 2, 1) * sin).astype(out_ref.dtype)

        rotary(proj(0), rope_ref[0, rows, :], rope_ref[1, rows, :], rq_ref)
        rotary(proj(1), rope_ref[2, rows, :], rope_ref[3, rows, :], rk_ref)
        store_groups(rv_ref, proj(2))
        store_groups(rg_ref, _silu(proj(3)))
        store_groups(nq_ref, proj(4) * (LOG2E * NA_HEAD_DIM ** -0.5))
        store_groups(nk_ref, proj(5))
        store_groups(nv_ref, proj(6))
        gr_ref[rows, :] = proj(7).astype(gr_ref.dtype)
        gn_ref[rows, :] = proj(8).astype(gn_ref.dtype)

    u = pre(0)
    for j in range(n_sub):
        u_next = pre(j + 1) if j + 1 < n_sub else None
        project(j, u)
        u = u_next


def _rope_tables(seq):
    half = RET_QK_DIM // 2
    pos = jnp.arange(seq, dtype=F32)
    inv = 1.0 / (ROPE_BASE ** jnp.linspace(0.0, 1.0, half, dtype=F32))
    ang = pos[:, None] * inv[None, :]
    cos, sin = jnp.cos(ang), jnp.sin(ang)
    cos2 = jnp.concatenate([cos, cos], axis=-1)
    sin2 = jnp.concatenate([-sin, sin], axis=-1)
    ks = RET_QK_DIM ** -0.5
    return jnp.stack([cos2, sin2, cos2 * ks, sin2 * ks])


def _mix_in(x, gain, w, seq, riders=()):
    t = x.shape[0]
    tm = TOKEN_TILE
    tiles_per_seq = seq // tm
    out_groups = (RET_HEADS,) * 4 + (NA_SLABS,) * 3 + (1, 1)
    rider_specs, rider_shapes = _cast_riders(riders, t // tm)
    outs = pl.pallas_call(
        functools.partial(_mixin_kernel, len(riders)),
        grid=(t // tm,),
        in_specs=[pl.BlockSpec((tm, D_MODEL), lambda i: (i, 0)),
                  _const_spec((1, D_MODEL)), _const_spec((D_MODEL, MIX_IN_W)),
                  pl.BlockSpec((4, tm, RET_QK_DIM), lambda i: (0, i % tiles_per_seq, 0))] + rider_specs,
        out_specs=[pl.BlockSpec((tm, wd), lambda i: (i, 0)) if g == 1
                   else pl.BlockSpec((g, tm, wd // g), lambda i: (0, i, 0))
                   for wd, g in zip(MIX_SPLITS, out_groups)] + rider_specs,
        out_shape=[jax.ShapeDtypeStruct((t, wd) if g == 1 else (g, t, wd // g), BF16)
                   for wd, g in zip(MIX_SPLITS, out_groups)] + rider_shapes,
        compiler_params=pltpu.CompilerParams(
            dimension_semantics=("arbitrary",), vmem_limit_bytes=VMEM_LIMIT),
        name="mix_in",
    )(x, gain, w, _rope_tables(seq), *riders)
    return outs[:len(MIX_SPLITS)], outs[len(MIX_SPLITS):]


def _log_sigmoid(z):
    return jnp.minimum(z, 0.0) - jnp.log1p(jnp.exp(-jnp.abs(z)))


def _ret_kernel(dec_ref, q_ref, k_ref, v_ref, g_ref, o_ref, u_ref, st_ref, dm_ref, vec_ref, blk_ref):
    c = RET_BLOCK
    dk = RET_QK_DIM
    dv = RET_V_DIM
    n_heads = q_ref.shape[0]
    n_blocks = q_ref.shape[1] // c
    tn = (((0,), (0,)), ((), ()))
    nt = (((1,), (1,)), ((), ()))

    @pl.when(pl.program_id(0) == 0)
    def _decay_tables():
        def log_gamma(z, shape):
            return _log_sigmoid(jnp.full(shape, z, F32))

        ri = lax.broadcasted_iota(jnp.int32, (c, c), 0).astype(F32)
        ci = lax.broadcasted_iota(jnp.int32, (c, c), 1).astype(F32)
        diff = ri - ci
        r = lax.broadcasted_iota(jnp.int32, (c, dk), 0).astype(F32)
        for h in range(n_heads):
            zf = dec_ref[0, h]
            zb = dec_ref[1, h]
            dm_ref[h] = jnp.where(diff >= 0, jnp.exp(log_gamma(zf, (c, c)) * jnp.maximum(diff, 0.0)),
                                  jnp.exp(log_gamma(zb, (c, c)) * jnp.maximum(-diff, 0.0)))
            lgf = log_gamma(zf, (c, dk))
            lgb = log_gamma(zb, (c, dk))
            vec_ref[h, 0] = jnp.exp(lgf * (r + 1.0)).astype(BF16)
            vec_ref[h, 1] = jnp.exp(lgb * (c - r)).astype(BF16)
            vec_ref[h, 2] = jnp.exp(lgf * (c - 1.0 - r)).astype(BF16)
            vec_ref[h, 3] = jnp.exp(lgb * r).astype(BF16)
            blk_ref[h, 0] = jnp.exp(log_gamma(zf, (dk, dv)) * c)
            blk_ref[h, 1] = jnp.exp(log_gamma(zb, (dk, dv)) * c)

    def rows(n):
        return slice(n * c, (n + 1) * c)

    def updates(h):
        for n in range(n_blocks):
            k = k_ref[h, rows(n), :]
            k_cat = jnp.concatenate([k * vec_ref[h, 2], k * vec_ref[h, 3]], axis=1)
            u_ref[h % 2, n] = lax.dot_general(k_cat, v_ref[h, rows(n), :], tn, preferred_element_type=F32)

    def scans(h):
        sf = jnp.zeros((dk, dv), F32)
        for n in range(n_blocks):
            st_ref[h % 2, n, 0:dk, :] = sf.astype(BF16)
            if n + 1 < n_blocks:
                sf = blk_ref[h, 0] * sf + u_ref[h % 2, n, 0:dk, :]
        sb = jnp.zeros((dk, dv), F32)
        for n in reversed(range(n_blocks)):
            st_ref[h % 2, n, dk:2 * dk, :] = sb.astype(BF16)
            if n > 0:
                sb = blk_ref[h, 1] * sb + u_ref[h % 2, n, dk:2 * dk, :]

    def outputs(h):
        def scores(n):
            return lax.dot_general(q_ref[h, rows(n), :], k_ref[h, rows(n), :], nt,
                                   preferred_element_type=F32)

        s_next = scores(0)
        for n in range(n_blocks):
            s_cur = s_next
            if n + 1 < n_blocks:
                s_next = scores(n + 1)
            p = (s_cur * dm_ref[h]).astype(BF16)
            q = q_ref[h, rows(n), :]
            lhs = jnp.concatenate([p, q * vec_ref[h, 0], q * vec_ref[h, 1]], axis=1)
            rhs = jnp.concatenate([v_ref[h, rows(n), :], st_ref[h % 2, n]], axis=0)
            y = jnp.dot(lhs, rhs, preferred_element_type=F32)
            yn = y * lax.rsqrt(jnp.mean(y * y, axis=-1, keepdims=True) + RMS_EPS)
            o_ref[h, rows(n), :] = (g_ref[h, rows(n), :].astype(F32) * yn).astype(o_ref.dtype)

    updates(0)
    for h in range(n_heads):
        if h + 1 < n_heads:
            updates(h + 1)
        scans(h)
        outputs(h)


def _retention(decay_logits, rq, rk, rv, rg):
    n_heads, b, s, _ = rq.shape
    c = RET_BLOCK
    n_blocks = s // c
    qk_spec = pl.BlockSpec((n_heads, None, s, RET_QK_DIM), lambda i, dec: (0, i, 0, 0))
    v_spec = pl.BlockSpec((n_heads, None, s, RET_V_DIM), lambda i, dec: (0, i, 0, 0))
    return pl.pallas_call(
        _ret_kernel,
        grid_spec=pltpu.PrefetchScalarGridSpec(
            num_scalar_prefetch=1,
            grid=(b,),
            in_specs=[qk_spec, qk_spec, v_spec, v_spec],
            out_specs=v_spec,
            scratch_shapes=[pltpu.VMEM((2, n_blocks, 2 * RET_QK_DIM, RET_V_DIM), F32),
                            pltpu.VMEM((2, n_blocks, 2 * RET_QK_DIM, RET_V_DIM), BF16),
                            pltpu.VMEM((n_heads, c, c), F32),
                            pltpu.VMEM((n_heads, 4, c, RET_QK_DIM), BF16),
                            pltpu.VMEM((n_heads, 2, RET_QK_DIM, RET_V_DIM), F32)],
        ),
        out_shape=jax.ShapeDtypeStruct((n_heads, b, s, RET_V_DIM), BF16),
        compiler_params=pltpu.CompilerParams(
            dimension_semantics=("arbitrary",), vmem_limit_bytes=VMEM_LIMIT),
        name="retention",
    )(decay_logits, rq, rk, rv, rg)


def _na_blocks(extent, q_size, k_size, window):
    n = extent // q_size
    out = []
    for i in range(n):
        q0 = i * q_size
        k0 = min(max(q0 - window // 2, 0), extent - k_size)
        out.append((q0, k0, 0 if i == 0 else (2 if i == n - 1 else 1)))
    return tuple(out)


NA_ROWS = 32
_NA_ROW_BLOCKS = _na_blocks(NA_ROWS, NA_Q_ROWS, NA_K_ROWS, NA_WIN_ROWS)
_NA_COL_BLOCKS = _na_blocks(GRID_W, NA_Q_COLS, NA_K_COLS, NA_WIN_COLS)
_NA_CONFIGS = 9


def _na_bias_kernel(rel_ref, out_ref, col_ref):
    nk = NA_K_ROWS * NA_K_COLS
    base = pl.program_id(0) * (NA_REL_ROWS * NA_REL_COLS)
    lane = lax.broadcasted_iota(jnp.int32, (NA_Q_COLS, nk), 1)
    kr = lane // NA_K_COLS
    col_cfgs = (_NA_COL_BLOCKS[0], _NA_COL_BLOCKS[1], _NA_COL_BLOCKS[3])
    row_cfgs = (_NA_ROW_BLOCKS[0], _NA_ROW_BLOCKS[1], _NA_ROW_BLOCKS[3])
    reps = nk // NA_LANES
    kc_one = lax.broadcasted_iota(jnp.int32, (NA_Q_COLS, NA_LANES), 1) % NA_K_COLS
    qc_one = lax.broadcasted_iota(jnp.int32, (NA_Q_COLS, NA_LANES), 0)
    for b, (c0, kc0, _) in enumerate(col_cfgs):
        kc_abs = kc_one + kc0
        qc_abs = qc_one + c0
        rel_col = jnp.clip(kc_abs - qc_abs + NA_WIN_COLS - 1, 0, NA_REL_COLS - 1)
        win = jnp.clip(qc_abs - NA_WIN_COLS // 2, 0, GRID_W - NA_WIN_COLS)
        in_win = (kc_abs >= win) & (kc_abs < win + NA_WIN_COLS)

        def rel_row_body(dr, carry, rel_col=rel_col, in_win=in_win, b=b):
            vals = jnp.zeros((NA_Q_COLS, NA_LANES), F32)
            for dc in range(NA_REL_COLS):
                vals = jnp.where(rel_col == dc, rel_ref[base + dr * NA_REL_COLS + dc], vals)
            vals = jnp.where(in_win, vals * LOG2E, NEG_INF)
            col_ref[b * NA_REL_ROWS + dr] = jnp.concatenate([vals] * reps, axis=1)
            return carry

        lax.fori_loop(0, NA_REL_ROWS, rel_row_body, 0)

    for a, (r0, ks, _) in enumerate(row_cfgs):
        kr_abs = kr + ks
        for qr in range(NA_Q_ROWS):
            r_abs = r0 + qr
            rs = min(max(r_abs - NA_WIN_ROWS // 2, 0), NA_ROWS - NA_WIN_ROWS)
            hits = [kr_abs == rs + j for j in range(NA_WIN_ROWS)]
            for b in range(3):
                tile = jnp.full((NA_Q_COLS, nk), NEG_INF, F32)
                for j in range(NA_WIN_ROWS):
                    dr = rs + j - r_abs + NA_WIN_ROWS - 1
                    tile = jnp.where(hits[j], col_ref[b * NA_REL_ROWS + dr], tile)
                out_ref[0, a * 3 + b, qr * NA_Q_COLS:(qr + 1) * NA_Q_COLS, :] = tile


def _na_bias_tables(rel_bias):
    nq = NA_Q_ROWS * NA_Q_COLS
    nk = NA_K_ROWS * NA_K_COLS
    return pl.pallas_call(
        _na_bias_kernel,
        grid_spec=pltpu.PrefetchScalarGridSpec(
            num_scalar_prefetch=1,
            grid=(NA_HEADS,),
            in_specs=[],
            out_specs=pl.BlockSpec((1, _NA_CONFIGS, nq, nk), lambda h, rel: (h, 0, 0, 0)),
            scratch_shapes=[pltpu.VMEM((3 * NA_REL_ROWS, NA_Q_COLS, nk), F32)],
        ),
        out_shape=jax.ShapeDtypeStruct((NA_HEADS, _NA_CONFIGS, nq, nk), F32),
        compiler_params=pltpu.CompilerParams(dimension_semantics=("parallel",)),
        name="na_bias",
    )(rel_bias.astype(F32).reshape(-1))


def _na_kernel(q_ref, k_ref, v_ref, bias_ref, o_ref, ks_ref, vs_ref):
    n_slabs = q_ref.shape[0]
    nq = NA_Q_ROWS * NA_Q_COLS
    nk = NA_K_ROWS * NA_K_COLS
    shift = NA_WIN_COLS // 2
    kept = GRID_W - 2 * shift
    for sl in range(n_slabs):
        for src, dst in ((k_ref, ks_ref), (v_ref, vs_ref)):
            dst[sl, :, 0:kept, :] = src[sl, :, shift:shift + kept, :]

    def window(refs, sl, block):
        (_, kr0, _), (_, kc0, _) = block
        aligned = kc0 % (2 * shift) == 0
        ref, c = (refs[0], kc0) if aligned else (refs[1], kc0 - shift)
        assert c % (2 * shift) == 0 and (aligned or c + NA_K_COLS <= kept)
        return ref[sl, kr0:kr0 + NA_K_ROWS, c:c + NA_K_COLS, :].reshape(nk, NA_LANES)

    first = lax.broadcasted_iota(jnp.int32, (nq, NA_LANES), 1) < NA_HEAD_DIM
    ones = jnp.ones((nk, NA_LANES), BF16)
    nt = (((1,), (1,)), ((), ()))
    work = [(sl, (rb, cb)) for sl in range(n_slabs) for rb in _NA_ROW_BLOCKS for cb in _NA_COL_BLOCKS]

    def scores(sl, block):
        (r0, _, rcfg), (c0, _, ccfg) = block
        q = q_ref[sl, r0:r0 + NA_Q_ROWS, c0:c0 + NA_Q_COLS, :].reshape(nq, NA_LANES)
        zero = jnp.zeros_like(q)
        q2 = jnp.concatenate([jnp.where(first, q, zero), jnp.where(first, zero, q)], axis=0)
        cfg = rcfg * 3 + ccfg
        bias = jnp.concatenate([bias_ref[2 * sl, cfg], bias_ref[2 * sl + 1, cfg]], axis=0)
        return lax.dot_general(q2, window((k_ref, ks_ref), sl, block), nt,
                               preferred_element_type=F32) + bias

    def attend(sl, block, s):
        m = jnp.max(s, axis=-1, keepdims=True)
        p = jnp.exp2(s - m).astype(BF16)
        o = jnp.dot(p, jnp.concatenate([window((v_ref, vs_ref), sl, block), ones], axis=1),
                    preferred_element_type=F32)
        o = o[:, :NA_LANES] / o[:, NA_LANES:]
        return jnp.where(first, o[:nq], o[nq:])

    s_next = scores(*work[0])
    for i, (sl, blk) in enumerate(work):
        s_cur = s_next
        if i + 1 < len(work):
            s_next = scores(*work[i + 1])
        (r0, _, _), (c0, _, _) = blk
        o_ref[sl, r0:r0 + NA_Q_ROWS, c0:c0 + NA_Q_COLS, :] = (
            attend(sl, blk, s_cur).reshape(NA_Q_ROWS, NA_Q_COLS, NA_LANES).astype(o_ref.dtype))


def _natten(nq, nk, nv, bias_tabs):
    b, rows = nq.shape[1], nq.shape[2]
    per = NA_SLABS_PER_STEP
    slab = pl.BlockSpec((per, None, rows, GRID_W, NA_LANES), lambda j, i: (j, i, 0, 0, 0))
    nqk = NA_Q_ROWS * NA_Q_COLS
    nkk = NA_K_ROWS * NA_K_COLS
    return pl.pallas_call(
        _na_kernel,
        grid=(NA_SLABS // per, b),
        in_specs=[slab, slab, slab,
                  pl.BlockSpec((2 * per, _NA_CONFIGS, nqk, nkk), lambda j, i: (j, 0, 0, 0),
                               pipeline_mode=pl.Buffered(1))],
        out_specs=slab,
        out_shape=jax.ShapeDtypeStruct((NA_SLABS, b, rows, GRID_W, NA_LANES), BF16),
        scratch_shapes=[pltpu.VMEM((per, rows, GRID_W, NA_LANES), BF16),
                        pltpu.VMEM((per, rows, GRID_W, NA_LANES), BF16)],
        compiler_params=pltpu.CompilerParams(
            dimension_semantics=("parallel", "parallel"), vmem_limit_bytes=VMEM_LIMIT),
        name="natten",
    )(nq, nk, nv, bias_tabs)


def _mixout_kernel(x_ref, yr_ref, na_ref, gr_ref, gn_ref, wr_ref, wn_ref, wo_ref, gain_ref, o_ref):
    n_sub = x_ref.shape[0] // SUB_TILE

    def rows(j):
        return slice(j * SUB_TILE, (j + 1) * SUB_TILE)

    def branches(j):
        def gathered(ref):
            return jnp.concatenate([ref[g, rows(j), :] for g in range(ref.shape[0])], axis=1)

        y_ret = jnp.dot(gathered(yr_ref), wr_ref[...], preferred_element_type=F32)
        y_na = jnp.dot(gathered(na_ref), wn_ref[...], preferred_element_type=F32)
        merged = (jax.nn.sigmoid(gr_ref[rows(j), :].astype(F32)) * y_ret
                  + jax.nn.sigmoid(gn_ref[rows(j), :].astype(F32)) * y_na)
        return merged.astype(BF16)

    def finish(j, merged):
        m = jnp.dot(merged, wo_ref[...], preferred_element_type=F32)
        o_ref[rows(j), :] = x_ref[rows(j), :] + _rms(m, gain_ref[...])

    merged = branches(0)
    for j in range(n_sub):
        merged_next = branches(j + 1) if j + 1 < n_sub else None
        finish(j, merged)
        merged = merged_next


def _mix_out(x, y_ret, y_na, g_ret, g_na, w_ret, w_na, w_out, gain):
    t = x.shape[0]
    tm = FFN_TILE

    def row(width):
        return pl.BlockSpec((tm, width), lambda i: (i, 0))

    return pl.pallas_call(
        _mixout_kernel,
        grid=(t // tm,),
        in_specs=[row(D_MODEL),
                  pl.BlockSpec((RET_HEADS, tm, RET_V_DIM), lambda i: (0, i, 0)),
                  pl.BlockSpec((NA_SLABS, tm, NA_LANES), lambda i: (0, i, 0)),
                  row(D_MODEL), row(D_MODEL),
                  _const_spec((RET_V_W, D_MODEL)), _const_spec((NA_W, D_MODEL)),
                  _const_spec((D_MODEL, D_MODEL)), _const_spec((1, D_MODEL))],
        out_specs=row(D_MODEL),
        out_shape=jax.ShapeDtypeStruct((t, D_MODEL), F32),
        compiler_params=pltpu.CompilerParams(
            dimension_semantics=("parallel",), vmem_limit_bytes=VMEM_LIMIT),
        name="mix_out",
    )(x, y_ret, y_na, g_ret, g_na, w_ret, w_na, w_out, gain)


def kernel(x, ffn1_pre_norm, ffn1_w_in, ffn1_w_out, ffn1_post_norm, mix_pre_norm, w_mix_in,
           ret_decay_fwd, ret_decay_bwd, na_rel_bias, w_ret_out, w_na_out, w_mix_out, mix_post_norm,
           ffn2_pre_norm, ffn2_w_in, ffn2_w_out, ffn2_post_norm):
    b, s, d = x.shape
    assert d == D_MODEL and s % GRID_W == 0 and s % RET_BLOCK == 0 and (b * s) % TOKEN_TILE == 0
    assert s % TOKEN_TILE == 0
    rows = s // GRID_W
    assert rows == NA_ROWS, "the neighbourhood-attention tiling is laid out for a 32 x 64 token grid"
    t = b * s
    xt = x.reshape(t, d)
    for l in range(ffn1_w_in.shape[0]):
        gain = lambda g: g[l].reshape(1, D_MODEL).astype(F32)
        wt = lambda w: w[l].astype(BF16)
        xt, (w_mix, w_ret, w_na, w_out) = _ffn(
            xt, gain(ffn1_pre_norm), wt(ffn1_w_in), wt(ffn1_w_out), gain(ffn1_post_norm),
            riders=(w_mix_in[l], w_ret_out[l], w_na_out[l], w_mix_out[l]))
        (rq, rk, rv, rg, nq, nk, nv, g_ret, g_na), (w_ffn2_in, w_ffn2_out) = _mix_in(
            xt, gain(mix_pre_norm), w_mix, s, riders=(ffn2_w_in[l], ffn2_w_out[l]))
        decay = jnp.stack([ret_decay_fwd[l], ret_decay_bwd[l]]).astype(F32)
        seq = lambda a: a.reshape(a.shape[0], b, s, a.shape[-1])
        y_ret = _retention(decay, seq(rq), seq(rk), seq(rv), seq(rg)).reshape(RET_HEADS, t, RET_V_DIM)
        grid5 = lambda a: a.reshape(NA_SLABS, b, rows, GRID_W, NA_LANES)
        y_na = _natten(grid5(nq), grid5(nk), grid5(nv), _na_bias_tables(na_rel_bias[l]))
        xt = _mix_out(xt, y_ret, y_na.reshape(NA_SLABS, t, NA_LANES), g_ret, g_na,
                      w_ret, w_na, w_out, gain(mix_post_norm))
        xt, _ = _ffn(xt, gain(ffn2_pre_norm), w_ffn2_in, w_ffn2_out, gain(ffn2_post_norm))
    return xt.reshape(b, s, d)
```

```python
import functools

import numpy as np
import jax
import jax.numpy as jnp
from jax import lax
from jax.experimental import pallas as pl
from jax.experimental.pallas import tpu as pltpu

F32 = jnp.float32
BF16 = jnp.bfloat16

D_MODEL = 1024
D_FF = 2816
RMS_EPS = 1e-6
ROPE_BASE = 10000.0
NEG_INF = -1e30
LOG2E = 1.4426950408889634

RET_HEADS = 4
RET_QK_DIM = 128
RET_V_DIM = 256
RET_QK_W = RET_HEADS * RET_QK_DIM
RET_V_W = RET_HEADS * RET_V_DIM
RET_BLOCK = 256

NA_HEADS = 8
NA_HEAD_DIM = 64
NA_W = NA_HEADS * NA_HEAD_DIM
GRID_W = 64
NA_WIN_ROWS = 8
NA_WIN_COLS = 16
NA_REL_ROWS = 2 * NA_WIN_ROWS - 1
NA_REL_COLS = 2 * NA_WIN_COLS - 1
NA_Q_ROWS = 8
NA_Q_COLS = 16
NA_K_ROWS = 16
NA_K_COLS = 32
NA_LANES = 128
NA_SLABS = NA_W // NA_LANES
NA_SLABS_PER_STEP = 4

MIX_SPLITS = (RET_QK_W, RET_QK_W, RET_V_W, RET_V_W, NA_W, NA_W, NA_W, D_MODEL, D_MODEL)
MIX_OFFS = tuple(int(v) for v in np.cumsum((0,) + MIX_SPLITS[:-1]))
MIX_IN_W = sum(MIX_SPLITS)

TOKEN_TILE = 512
FFN_TILE = 1024
SUB_TILE = 256
FF_CHUNK = 256
VMEM_LIMIT = 56 * 1024 * 1024
BF16_SUBLANES = 16


def _rms(x, gain):
    ms = jnp.mean(x * x, axis=-1, keepdims=True)
    return x * lax.rsqrt(ms + RMS_EPS) * gain


def _silu(x):
    return x * jax.nn.sigmoid(x)


def _cast_riders(weights, steps):
    in_specs, out_shapes = [], []
    for w in weights:
        rows, cols = w.shape
        block = min(d for d in range(BF16_SUBLANES, rows + 1, BF16_SUBLANES)
                    if rows % d == 0 and rows // d <= steps)
        last = rows // block - 1
        in_specs.append(pl.BlockSpec((block, cols), lambda i, last=last: (jnp.minimum(i, last), 0)))
        out_shapes.append(jax.ShapeDtypeStruct(w.shape, BF16))
    return in_specs, out_shapes


def _run_cast_riders(src_refs, dst_refs):
    for src, dst in zip(src_refs, dst_refs):
        dst[...] = src[...].astype(dst.dtype)


def _const_spec(shape):
    nd = len(shape)
    return pl.BlockSpec(shape, lambda *_: (0,) * nd, pipeline_mode=pl.Buffered(1))


def _ffn_kernel(n_riders, x_ref, gpre_ref, win_ref, wout_ref, gpost_ref, *rest):
    o_ref, act_ref = rest[n_riders], rest[-1]
    _run_cast_riders(rest[:n_riders], rest[n_riders + 1:-1])
    n_sub = x_ref.shape[0] // SUB_TILE

    def rows(j):
        return slice(j * SUB_TILE, (j + 1) * SUB_TILE)

    def pre(j):
        return _rms(x_ref[rows(j), :], gpre_ref[...]).astype(BF16)

    def up(j, xn):
        for c in range(D_FF // FF_CHUNK):
            lo = c * FF_CHUNK
            g = jnp.dot(xn, win_ref[:, lo:lo + FF_CHUNK], preferred_element_type=F32)
            u = jnp.dot(xn, win_ref[:, D_FF + lo:D_FF + lo + FF_CHUNK], preferred_element_type=F32)
            act_ref[rows(j), lo:lo + FF_CHUNK] = (_silu(g) * u).astype(BF16)

    def down(j):
        h = jnp.dot(act_ref[rows(j), :], wout_ref[...], preferred_element_type=F32)
        o_ref[rows(j), :] = x_ref[rows(j), :] + 0.5 * _rms(h, gpost_ref[...])

    xn = pre(0)
    for j in range(n_sub):
        xn_next = pre(j + 1) if j + 1 < n_sub else None
        up(j, xn)
        if j >= 1:
            down(j - 1)
        xn = xn_next
    down(n_sub - 1)


def _ffn(x, gpre, w_in, w_out, gpost, riders=()):
    t = x.shape[0]
    tm = FFN_TILE
    row = pl.BlockSpec((tm, D_MODEL), lambda i: (i, 0))
    rider_specs, rider_shapes = _cast_riders(riders, t // tm)
    out, *cast = pl.pallas_call(
        functools.partial(_ffn_kernel, len(riders)),
        grid=(t // tm,),
        in_specs=[row, _const_spec((1, D_MODEL)), _const_spec((D_MODEL, 2 * D_FF)),
                  _const_spec((D_FF, D_MODEL)), _const_spec((1, D_MODEL))] + rider_specs,
        out_specs=[row] + rider_specs,
        out_shape=[jax.ShapeDtypeStruct((t, D_MODEL), F32)] + rider_shapes,
        scratch_shapes=[pltpu.VMEM((tm, D_FF), BF16)],
        compiler_params=pltpu.CompilerParams(
            dimension_semantics=("arbitrary",), vmem_limit_bytes=VMEM_LIMIT),
        name="ffn",
    )(x, gpre, w_in, w_out, gpost, *riders)
    return out, cast


def _mixin_kernel(n_riders, x_ref, gain_ref, w_ref, rope_ref, *rest):
    rq_ref, rk_ref, rv_ref, rg_ref, nq_ref, nk_ref, nv_ref, gr_ref, gn_ref = rest[n_riders:n_riders + 9]
    _run_cast_riders(rest[:n_riders], rest[n_riders + 9:])
    n_sub = x_ref.shape[0] // SUB_TILE

    def pre(j):
        rows = slice(j * SUB_TILE, (j + 1) * SUB_TILE)
        return _rms(x_ref[rows, :], gain_ref[...]).astype(BF16)

    def project(j, u):
        rows = slice(j * SUB_TILE, (j + 1) * SUB_TILE)

        def proj(idx):
            lo = MIX_OFFS[idx]
            return jnp.dot(u, w_ref[:, lo:lo + MIX_SPLITS[idx]], preferred_element_type=F32)

        def store_groups(out_ref, y):
            width = out_ref.shape[-1]
            for g in range(out_ref.shape[0]):
                out_ref[g, rows, :] = y[:, g * width:(g + 1) * width].astype(out_ref.dtype)

        def rotary(y, cos, sin, out_ref):
            for h in range(RET_HEADS):
                yh = y[:, h * RET_QK_DIM:(h + 1) * RET_QK_DIM]
                out_ref[h, rows, :] = (yh * cos + pltpu.roll(yh, RET_QK_DIM // 2, 1) * sin).astype(out_ref.dtype)

        rotary(proj(0), rope_ref[0, rows, :], rope_ref[1, rows, :], rq_ref)
        rotary(proj(1), rope_ref[2, rows, :], rope_ref[3, rows, :], rk_ref)
        store_groups(rv_ref, proj(2))
        store_groups(rg_ref, _silu(proj(3)))
        store_groups(nq_ref, proj(4) * (LOG2E * NA_HEAD_DIM ** -0.5))
        store_groups(nk_ref, proj(5))
        store_groups(nv_ref, proj(6))
        gr_ref[rows, :] = proj(7).astype(gr_ref.dtype)
        gn_ref[rows, :] = proj(8).astype(gn_ref.dtype)

    u = pre(0)
    for j in range(n_sub):
        u_next = pre(j + 1) if j + 1 < n_sub else None
        project(j, u)
        u = u_next


def _rope_tables(seq):
    half = RET_QK_DIM // 2
    pos = jnp.arange(seq, dtype=F32)
    inv = 1.0 / (ROPE_BASE ** jnp.linspace(0.0, 1.0, half, dtype=F32))
    ang = pos[:, None] * inv[None, :]
    cos, sin = jnp.cos(ang), jnp.sin(ang)
    cos2 = jnp.concatenate([cos, cos], axis=-1)
    sin2 = jnp.concatenate([-sin, sin], axis=-1)
    ks = RET_QK_DIM ** -0.5
    return jnp.stack([cos2, sin2, cos2 * ks, sin2 * ks])


def _mix_in(x, gain, w, seq, riders=()):
    t = x.shape[0]
    tm = TOKEN_TILE
    tiles_per_seq = seq // tm
    out_groups = (RET_HEADS,) * 4 + (NA_SLABS,) * 3 + (1, 1)
    rider_specs, rider_shapes = _cast_riders(riders, t // tm)
    outs = pl.pallas_call(
        functools.partial(_mixin_kernel, len(riders)),
        grid=(t // tm,),
        in_specs=[pl.BlockSpec((tm, D_MODEL), lambda i: (i, 0)),
                  _const_spec((1, D_MODEL)), _const_spec((D_MODEL, MIX_IN_W)),
                  pl.BlockSpec((4, tm, RET_QK_DIM), lambda i: (0, i % tiles_per_seq, 0))] + rider_specs,
        out_specs=[pl.BlockSpec((tm, wd), lambda i: (i, 0)) if g == 1
                   else pl.BlockSpec((g, tm, wd // g), lambda i: (0, i, 0))
                   for wd, g in zip(MIX_SPLITS, out_groups)] + rider_specs,
        out_shape=[jax.ShapeDtypeStruct((t, wd) if g == 1 else (g, t, wd // g), BF16)
                   for wd, g in zip(MIX_SPLITS, out_groups)] + rider_shapes,
        compiler_params=pltpu.CompilerParams(
            dimension_semantics=("arbitrary",), vmem_limit_bytes=VMEM_LIMIT),
        name="mix_in",
    )(x, gain, w, _rope_tables(seq), *riders)
    return outs[:len(MIX_SPLITS)], outs[len(MIX_SPLITS):]


def _log_sigmoid(z):
    return jnp.minimum(z, 0.0) - jnp.log1p(jnp.exp(-jnp.abs(z)))


def _ret_kernel(dec_ref, q_ref, k_ref, v_ref, g_ref, o_ref, u_ref, st_ref, dm_ref, vec_ref, blk_ref):
    c = RET_BLOCK
    dk = RET_QK_DIM
    dv = RET_V_DIM
    n_heads = q_ref.shape[0]
    n_blocks = q_ref.shape[1] // c
    tn = (((0,), (0,)), ((), ()))
    nt = (((1,), (1,)), ((), ()))

    @pl.when(pl.program_id(0) == 0)
    def _decay_tables():
        def log_gamma(z, shape):
            return _log_sigmoid(jnp.full(shape, z, F32))

        ri = lax.broadcasted_iota(jnp.int32, (c, c), 0).astype(F32)
        ci = lax.broadcasted_iota(jnp.int32, (c, c), 1).astype(F32)
        diff = ri - ci
        r = lax.broadcasted_iota(jnp.int32, (c, dk), 0).astype(F32)
        for h in range(n_heads):
            zf = dec_ref[0, h]
            zb = dec_ref[1, h]
            dm_ref[h] = jnp.where(diff >= 0, jnp.exp(log_gamma(zf, (c, c)) * jnp.maximum(diff, 0.0)),
                                  jnp.exp(log_gamma(zb, (c, c)) * jnp.maximum(-diff, 0.0)))
            lgf = log_gamma(zf, (c, dk))
            lgb = log_gamma(zb, (c, dk))
            vec_ref[h, 0] = jnp.exp(lgf * (r + 1.0)).astype(BF16)
            vec_ref[h, 1] = jnp.exp(lgb * (c - r)).astype(BF16)
            vec_ref[h, 2] = jnp.exp(lgf * (c - 1.0 - r)).astype(BF16)
            vec_ref[h, 3] = jnp.exp(lgb * r).astype(BF16)
            blk_ref[h, 0] = jnp.exp(log_gamma(zf, (dk, dv)) * c)
            blk_ref[h, 1] = jnp.exp(log_gamma(zb, (dk, dv)) * c)

    def rows(n):
        return slice(n * c, (n + 1) * c)

    def updates(h):
        for n in range(n_blocks):
            k = k_ref[h, rows(n), :]
            k_cat = jnp.concatenate([k * vec_ref[h, 2], k * vec_ref[h, 3]], axis=1)
            u_ref[h % 2, n] = lax.dot_general(k_cat, v_ref[h, rows(n), :], tn, preferred_element_type=F32)

    def scans(h):
        sf = jnp.zeros((dk, dv), F32)
        for n in range(n_blocks):
            st_ref[h % 2, n, 0:dk, :] = sf.astype(BF16)
            if n + 1 < n_blocks:
                sf = blk_ref[h, 0] * sf + u_ref[h % 2, n, 0:dk, :]
        sb = jnp.zeros((dk, dv), F32)
        for n in reversed(range(n_blocks)):
            st_ref[h % 2, n, dk:2 * dk, :] = sb.astype(BF16)
            if n > 0:
                sb = blk_ref[h, 1] * sb + u_ref[h % 2, n, dk:2 * dk, :]

    def outputs(h):
        def scores(n):
            return lax.dot_general(q_ref[h, rows(n), :], k_ref[h, rows(n), :], nt,
                                   preferred_element_type=F32)

        s_next = scores(0)
        for n in range(n_blocks):
            s_cur = s_next
            if n + 1 < n_blocks:
                s_next = scores(n + 1)
            p = (s_cur * dm_ref[h]).astype(BF16)
            q = q_ref[h, rows(n), :]
            lhs = jnp.concatenate([p, q * vec_ref[h, 0], q * vec_ref[h, 1]], axis=1)
            rhs = jnp.concatenate([v_ref[h, rows(n), :], st_ref[h % 2, n]], axis=0)
            y = jnp.dot(lhs, rhs, preferred_element_type=F32)
            yn = y * lax.rsqrt(jnp.mean(y * y, axis=-1, keepdims=True) + RMS_EPS)
            o_ref[h, rows(n), :] = (g_ref[h, rows(n), :].astype(F32) * yn).astype(o_ref.dtype)

    updates(0)
    for h in range(n_heads):
        if h + 1 < n_heads:
            updates(h + 1)
        scans(h)
        outputs(h)


def _retention(decay_logits, rq, rk, rv, rg):
    n_heads, b, s, _ = rq.shape
    c = RET_BLOCK
    n_blocks = s // c
    qk_spec = pl.BlockSpec((n_heads, None, s, RET_QK_DIM), lambda i, dec: (0, i, 0, 0))
    v_spec = pl.BlockSpec((n_heads, None, s, RET_V_DIM), lambda i, dec: (0, i, 0, 0))
    return pl.pallas_call(
        _ret_kernel,
        grid_spec=pltpu.PrefetchScalarGridSpec(
            num_scalar_prefetch=1,
            grid=(b,),
            in_specs=[qk_spec, qk_spec, v_spec, v_spec],
            out_specs=v_spec,
            scratch_shapes=[pltpu.VMEM((2, n_blocks, 2 * RET_QK_DIM, RET_V_DIM), F32),
                            pltpu.VMEM((2, n_blocks, 2 * RET_QK_DIM, RET_V_DIM), BF16),
                            pltpu.VMEM((n_heads, c, c), F32),
                            pltpu.VMEM((n_heads, 4, c, RET_QK_DIM), BF16),
                            pltpu.VMEM((n_heads, 2, RET_QK_DIM, RET_V_DIM), F32)],
        ),
        out_shape=jax.ShapeDtypeStruct((n_heads, b, s, RET_V_DIM), BF16),
        compiler_params=pltpu.CompilerParams(
            dimension_semantics=("arbitrary",), vmem_limit_bytes=VMEM_LIMIT),
        name="retention",
    )(decay_logits, rq, rk, rv, rg)


def _na_blocks(extent, q_size, k_size, window):
    n = extent // q_size
    out = []
    for i in range(n):
        q0 = i * q_size
        k0 = min(max(q0 - window // 2, 0), extent - k_size)
        out.append((q0, k0, 0 if i == 0 else (2 if i == n - 1 else 1)))
    return tuple(out)


NA_ROWS = 32
_NA_ROW_BLOCKS = _na_blocks(NA_ROWS, NA_Q_ROWS, NA_K_ROWS, NA_WIN_ROWS)
_NA_COL_BLOCKS = _na_blocks(GRID_W, NA_Q_COLS, NA_K_COLS, NA_WIN_COLS)
_NA_CONFIGS = 9


def _na_bias_kernel(rel_ref, out_ref, col_ref):
    nk = NA_K_ROWS * NA_K_COLS
    base = pl.program_id(0) * (NA_REL_ROWS * NA_REL_COLS)
    lane = lax.broadcasted_iota(jnp.int32, (NA_Q_COLS, nk), 1)
    kr = lane // NA_K_COLS
    col_cfgs = (_NA_COL_BLOCKS[0], _NA_COL_BLOCKS[1], _NA_COL_BLOCKS[3])
    row_cfgs = (_NA_ROW_BLOCKS[0], _NA_ROW_BLOCKS[1], _NA_ROW_BLOCKS[3])
    reps = nk // NA_LANES
    kc_one = lax.broadcasted_iota(jnp.int32, (NA_Q_COLS, NA_LANES), 1) % NA_K_COLS
    qc_one = lax.broadcasted_iota(jnp.int32, (NA_Q_COLS, NA_LANES), 0)
    for b, (c0, kc0, _) in enumerate(col_cfgs):
        kc_abs = kc_one + kc0
        qc_abs = qc_one + c0
        rel_col = jnp.clip(kc_abs - qc_abs + NA_WIN_COLS - 1, 0, NA_REL_COLS - 1)
        win = jnp.clip(qc_abs - NA_WIN_COLS // 2, 0, GRID_W - NA_WIN_COLS)
        in_win = (kc_abs >= win) & (kc_abs < win + NA_WIN_COLS)

        def rel_row_body(dr, carry, rel_col=rel_col, in_win=in_win, b=b):
            vals = jnp.zeros((NA_Q_COLS, NA_LANES), F32)
            for dc in range(NA_REL_COLS):
                vals = jnp.where(rel_col == dc, rel_ref[base + dr * NA_REL_COLS + dc], vals)
            vals = jnp.where(in_win, vals * LOG2E, NEG_INF)
            col_ref[b * NA_REL_ROWS + dr] = jnp.concatenate([vals] * reps, axis=1)
            return carry

        lax.fori_loop(0, NA_REL_ROWS, rel_row_body, 0)

    for a, (r0, ks, _) in enumerate(row_cfgs):
        kr_abs = kr + ks
        for qr in range(NA_Q_ROWS):
            r_abs = r0 + qr
            rs = min(max(r_abs - NA_WIN_ROWS // 2, 0), NA_ROWS - NA_WIN_ROWS)
            hits = [kr_abs == rs + j for j in range(NA_WIN_ROWS)]
            for b in range(3):
                tile = jnp.full((NA_Q_COLS, nk), NEG_INF, F32)
                for j in range(NA_WIN_ROWS):
                    dr = rs + j - r_abs + NA_WIN_ROWS - 1
                    tile = jnp.where(hits[j], col_ref[b * NA_REL_ROWS + dr], tile)
                out_ref[0, a * 3 + b, qr * NA_Q_COLS:(qr + 1) * NA_Q_COLS, :] = tile


def _na_bias_tables(rel_bias):
    nq = NA_Q_ROWS * NA_Q_COLS
    nk = NA_K_ROWS * NA_K_COLS
    return pl.pallas_call(
        _na_bias_kernel,
        grid_spec=pltpu.PrefetchScalarGridSpec(
            num_scalar_prefetch=1,
            grid=(NA_HEADS,),
            in_specs=[],
            out_specs=pl.BlockSpec((1, _NA_CONFIGS, nq, nk), lambda h, rel: (h, 0, 0, 0)),
            scratch_shapes=[pltpu.VMEM((3 * NA_REL_ROWS, NA_Q_COLS, nk), F32)],
        ),
        out_shape=jax.ShapeDtypeStruct((NA_HEADS, _NA_CONFIGS, nq, nk), F32),
        compiler_params=pltpu.CompilerParams(dimension_semantics=("parallel",)),
        name="na_bias",
    )(rel_bias.astype(F32).reshape(-1))


def _na_kernel(q_ref, k_ref, v_ref, bias_ref, o_ref, ks_ref, vs_ref):
    n_slabs = q_ref.shape[0]
    nq = NA_Q_ROWS * NA_Q_COLS
    nk = NA_K_ROWS * NA_K_COLS
    shift = NA_WIN_COLS // 2
    kept = GRID_W - 2 * shift
    for sl in range(n_slabs):
        for src, dst in ((k_ref, ks_ref), (v_ref, vs_ref)):
            dst[sl, :, 0:kept, :] = src[sl, :, shift:shift + kept, :]

    def window(refs, sl, block):
        (_, kr0, _), (_, kc0, _) = block
        aligned = kc0 % (2 * shift) == 0
        ref, c = (refs[0], kc0) if aligned else (refs[1], kc0 - shift)
        assert c % (2 * shift) == 0 and (aligned or c + NA_K_COLS <= kept)
        return ref[sl, kr0:kr0 + NA_K_ROWS, c:c + NA_K_COLS, :].reshape(nk, NA_LANES)

    first = lax.broadcasted_iota(jnp.int32, (nq, NA_LANES), 1) < NA_HEAD_DIM
    ones = jnp.ones((nk, NA_LANES), BF16)
    nt = (((1,), (1,)), ((), ()))
    work = [(sl, (rb, cb)) for sl in range(n_slabs) for rb in _NA_ROW_BLOCKS for cb in _NA_COL_BLOCKS]

    def scores(sl, block):
        (r0, _, rcfg), (c0, _, ccfg) = block
        q = q_ref[sl, r0:r0 + NA_Q_ROWS, c0:c0 + NA_Q_COLS, :].reshape(nq, NA_LANES)
        zero = jnp.zeros_like(q)
        q2 = jnp.concatenate([jnp.where(first, q, zero), jnp.where(first, zero, q)], axis=0)
        cfg = rcfg * 3 + ccfg
        bias = jnp.concatenate([bias_ref[2 * sl, cfg], bias_ref[2 * sl + 1, cfg]], axis=0)
        return lax.dot_general(q2, window((k_ref, ks_ref), sl, block), nt,
                               preferred_element_type=F32) + bias

    def attend(sl, block, s):
        m = jnp.max(s, axis=-1, keepdims=True)
        p = jnp.exp2(s - m).astype(BF16)
        o = jnp.dot(p, jnp.concatenate([window((v_ref, vs_ref), sl, block), ones], axis=1),
                    preferred_element_type=F32)
        o = o[:, :NA_LANES] / o[:, NA_LANES:]
        return jnp.where(first, o[:nq], o[nq:])

    s_next = scores(*work[0])
    for i, (sl, blk) in enumerate(work):
        s_cur = s_next
        if i + 1 < len(work):
            s_next = scores(*work[i + 1])
        (r0, _, _), (c0, _, _) = blk
        o_ref[sl, r0:r0 + NA_Q_ROWS, c0:c0 + NA_Q_COLS, :] = (
            attend(sl, blk, s_cur).reshape(NA_Q_ROWS, NA_Q_COLS, NA_LANES).astype(o_ref.dtype))


def _natten(nq, nk, nv, bias_tabs):
    b, rows = nq.shape[1], nq.shape[2]
    per = NA_SLABS_PER_STEP
    slab = pl.BlockSpec((per, None, rows, GRID_W, NA_LANES), lambda j, i: (j, i, 0, 0, 0))
    nqk = NA_Q_ROWS * NA_Q_COLS
    nkk = NA_K_ROWS * NA_K_COLS
    return pl.pallas_call(
        _na_kernel,
        grid=(NA_SLABS // per, b),
        in_specs=[slab, slab, slab,
                  pl.BlockSpec((2 * per, _NA_CONFIGS, nqk, nkk), lambda j, i: (j, 0, 0, 0),
                               pipeline_mode=pl.Buffered(1))],
        out_specs=slab,
        out_shape=jax.ShapeDtypeStruct((NA_SLABS, b, rows, GRID_W, NA_LANES), BF16),
        scratch_shapes=[pltpu.VMEM((per, rows, GRID_W, NA_LANES), BF16),
                        pltpu.VMEM((per, rows, GRID_W, NA_LANES), BF16)],
        compiler_params=pltpu.CompilerParams(
            dimension_semantics=("parallel", "parallel"), vmem_limit_bytes=VMEM_LIMIT),
        name="natten",
    )(nq, nk, nv, bias_tabs)


def _mixout_kernel(x_ref, yr_ref, na_ref, gr_ref, gn_ref, wr_ref, wn_ref, wo_ref, gain_ref, o_ref):
    n_sub = x_ref.shape[0] // SUB_TILE

    def rows(j):
        return slice(j * SUB_TILE, (j + 1) * SUB_TILE)

    def branches(j):
        def gathered(ref):
            return jnp.concatenate([ref[g, rows(j), :] for g in range(ref.shape[0])], axis=1)

        y_ret = jnp.dot(gathered(yr_ref), wr_ref[...], preferred_element_type=F32)
        y_na = jnp.dot(gathered(na_ref), wn_ref[...], preferred_element_type=F32)
        merged = (jax.nn.sigmoid(gr_ref[rows(j), :].astype(F32)) * y_ret
                  + jax.nn.sigmoid(gn_ref[rows(j), :].astype(F32)) * y_na)
        return merged.astype(BF16)

    def finish(j, merged):
        m = jnp.dot(merged, wo_ref[...], preferred_element_type=F32)
        o_ref[rows(j), :] = x_ref[rows(j), :] + _rms(m, gain_ref[...])

    merged = branches(0)
    for j in range(n_sub):
        merged_next = branches(j + 1) if j + 1 < n_sub else None
        finish(j, merged)
        merged = merged_next


def _mix_out(x, y_ret, y_na, g_ret, g_na, w_ret, w_na, w_out, gain):
    t = x.shape[0]
    tm = FFN_TILE

    def row(width):
        return pl.BlockSpec((tm, width), lambda i: (i, 0))

    return pl.pallas_call(
        _mixout_kernel,
        grid=(t // tm,),
        in_specs=[row(D_MODEL),
                  pl.BlockSpec((RET_HEADS, tm, RET_V_DIM), lambda i: (0, i, 0)),
                  pl.BlockSpec((NA_SLABS, tm, NA_LANES), lambda i: (0, i, 0)),
                  row(D_MODEL), row(D_MODEL),
                  _const_spec((RET_V_W, D_MODEL)), _const_spec((NA_W, D_MODEL)),
                  _const_spec((D_MODEL, D_MODEL)), _const_spec((1, D_MODEL))],
        out_specs=row(D_MODEL),
        out_shape=jax.ShapeDtypeStruct((t, D_MODEL), F32),
        compiler_params=pltpu.CompilerParams(
            dimension_semantics=("parallel",), vmem_limit_bytes=VMEM_LIMIT),
        name="mix_out",
    )(x, y_ret, y_na, g_ret, g_na, w_ret, w_na, w_out, gain)


def kernel(x, ffn1_pre_norm, ffn1_w_in, ffn1_w_out, ffn1_post_norm, mix_pre_norm, w_mix_in,
           ret_decay_fwd, ret_decay_bwd, na_rel_bias, w_ret_out, w_na_out, w_mix_out, mix_post_norm,
           ffn2_pre_norm, ffn2_w_in, ffn2_w_out, ffn2_post_norm):
    b, s, d = x.shape
    assert d == D_MODEL and s % GRID_W == 0 and s % RET_BLOCK == 0 and (b * s) % TOKEN_TILE == 0
    assert s % TOKEN_TILE == 0
    rows = s // GRID_W
    assert rows == NA_ROWS, "the neighbourhood-attention tiling is laid out for a 32 x 64 token grid"
    t = b * s
    xt = x.reshape(t, d)
    for l in range(ffn1_w_in.shape[0]):
        gain = lambda g: g[l].reshape(1, D_MODEL).astype(F32)
        wt = lambda w: w[l].astype(BF16)
        xt, (w_mix, w_ret, w_na, w_out, w_ffn2_in, w_ffn2_out) = _ffn(
            xt, gain(ffn1_pre_norm), wt(ffn1_w_in), wt(ffn1_w_out), gain(ffn1_post_norm),
            riders=(w_mix_in[l], w_ret_out[l], w_na_out[l], w_mix_out[l], ffn2_w_in[l], ffn2_w_out[l]))
        (rq, rk, rv, rg, nq, nk, nv, g_ret, g_na), _ = _mix_in(xt, gain(mix_pre_norm), w_mix, s)
        decay = jnp.stack([ret_decay_fwd[l], ret_decay_bwd[l]]).astype(F32)
        seq = lambda a: a.reshape(a.shape[0], b, s, a.shape[-1])
        y_ret = _retention(decay, seq(rq), seq(rk), seq(rv), seq(rg)).reshape(RET_HEADS, t, RET_V_DIM)
        grid5 = lambda a: a.reshape(NA_SLABS, b, rows, GRID_W, NA_LANES)
        y_na = _natten(grid5(nq), grid5(nk), grid5(nv), _na_bias_tables(na_rel_bias[l]))
        xt = _mix_out(xt, y_ret, y_na.reshape(NA_SLABS, t, NA_LANES), g_ret, g_na,
                      w_ret, w_na, w_out, gain(mix_post_norm))
        xt, _ = _ffn(xt, gain(ffn2_pre_norm), w_ffn2_in, w_ffn2_out, gain(ffn2_post_norm))
    return xt.reshape(b, s, d)
```

```python
import functools

import numpy as np
import jax
import jax.numpy as jnp
from jax import lax
from jax.experimental import pallas as pl
from jax.experimental.pallas import tpu as pltpu

F32 = jnp.float32
BF16 = jnp.bfloat16

D_MODEL = 1024
D_FF = 2816
RMS_EPS = 1e-6
ROPE_BASE = 10000.0
NEG_INF = -1e30
LOG2E = 1.4426950408889634

RET_HEADS = 4
RET_QK_DIM = 128
RET_V_DIM = 256
RET_QK_W = RET_HEADS * RET_QK_DIM
RET_V_W = RET_HEADS * RET_V_DIM
RET_BLOCK = 256

NA_HEADS = 8
NA_HEAD_DIM = 64
NA_W = NA_HEADS * NA_HEAD_DIM
GRID_W = 64
NA_WIN_ROWS = 8
NA_WIN_COLS = 16
NA_REL_ROWS = 2 * NA_WIN_ROWS - 1
NA_REL_COLS = 2 * NA_WIN_COLS - 1
NA_Q_ROWS = 8
NA_Q_COLS = 16
NA_K_ROWS = 16
NA_K_COLS = 32
NA_LANES = 128
NA_SLABS = NA_W // NA_LANES
NA_SLABS_PER_STEP = 4

MIX_SPLITS = (RET_QK_W, RET_QK_W, RET_V_W, RET_V_W, NA_W, NA_W, NA_W, D_MODEL, D_MODEL)
MIX_OFFS = tuple(int(v) for v in np.cumsum((0,) + MIX_SPLITS[:-1]))
MIX_IN_W = sum(MIX_SPLITS)

TOKEN_TILE = 512
FFN_TILE = 1024
SUB_TILE = 256
FF_CHUNK = 256
VMEM_LIMIT = 56 * 1024 * 1024
BF16_SUBLANES = 16


def _rms(x, gain):
    ms = jnp.mean(x * x, axis=-1, keepdims=True)
    return x * lax.rsqrt(ms + RMS_EPS) * gain


def _silu(x):
    return x * jax.nn.sigmoid(x)


def _cast_riders(weights, steps):
    in_specs, out_shapes = [], []
    for w in weights:
        rows, cols = w.shape
        block = min(d for d in range(BF16_SUBLANES, rows + 1, BF16_SUBLANES)
                    if rows % d == 0 and rows // d <= steps)
        last = rows // block - 1
        in_specs.append(pl.BlockSpec((block, cols), lambda i, last=last: (jnp.minimum(i, last), 0)))
        out_shapes.append(jax.ShapeDtypeStruct(w.shape, BF16))
    return in_specs, out_shapes


def _run_cast_riders(src_refs, dst_refs):
    for src, dst in zip(src_refs, dst_refs):
        dst[...] = src[...].astype(dst.dtype)


def _const_spec(shape):
    nd = len(shape)
    return pl.BlockSpec(shape, lambda *_: (0,) * nd, pipeline_mode=pl.Buffered(1))


def _ffn_kernel(n_riders, x_ref, gpre_ref, win_ref, wout_ref, gpost_ref, *rest):
    o_ref, act_ref = rest[n_riders], rest[-1]
    _run_cast_riders(rest[:n_riders], rest[n_riders + 1:-1])
    n_sub = x_ref.shape[0] // SUB_TILE

    def rows(j):
        return slice(j * SUB_TILE, (j + 1) * SUB_TILE)

    def pre(j):
        return _rms(x_ref[rows(j), :], gpre_ref[...]).astype(BF16)

    def up(j, xn):
        for c in range(D_FF // FF_CHUNK):
            lo = c * FF_CHUNK
            g = jnp.dot(xn, win_ref[:, lo:lo + FF_CHUNK], preferred_element_type=F32)
            u = jnp.dot(xn, win_ref[:, D_FF + lo:D_FF + lo + FF_CHUNK], preferred_element_type=F32)
            act_ref[rows(j), lo:lo + FF_CHUNK] = (_silu(g) * u).astype(BF16)

    def down(j):
        h = jnp.dot(act_ref[rows(j), :], wout_ref[...], preferred_element_type=F32)
        o_ref[rows(j), :] = x_ref[rows(j), :] + 0.5 * _rms(h, gpost_ref[...])

    xn = pre(0)
    for j in range(n_sub):
        xn_next = pre(j + 1) if j + 1 < n_sub else None
        up(j, xn)
        if j >= 1:
            down(j - 1)
        xn = xn_next
    down(n_sub - 1)


def _ffn(x, gpre, w_in, w_out, gpost, riders=()):
    t = x.shape[0]
    tm = FFN_TILE
    row = pl.BlockSpec((tm, D_MODEL), lambda i: (i, 0))
    rider_specs, rider_shapes = _cast_riders(riders, t // tm)
    out, *cast = pl.pallas_call(
        functools.partial(_ffn_kernel, len(riders)),
        grid=(t // tm,),
        in_specs=[row, _const_spec((1, D_MODEL)), _const_spec((D_MODEL, 2 * D_FF)),
                  _const_spec((D_FF, D_MODEL)), _const_spec((1, D_MODEL))] + rider_specs,
        out_specs=[row] + rider_specs,
        out_shape=[jax.ShapeDtypeStruct((t, D_MODEL), F32)] + rider_shapes,
        scratch_shapes=[pltpu.VMEM((tm, D_FF), BF16)],
        compiler_params=pltpu.CompilerParams(
            dimension_semantics=("arbitrary",), vmem_limit_bytes=VMEM_LIMIT),
        name="ffn",
    )(x, gpre, w_in, w_out, gpost, *riders)
    return out, cast


def _mixin_kernel(n_riders, x_ref, gain_ref, w_ref, rope_ref, *rest):
    rq_ref, rk_ref, rv_ref, rg_ref, nq_ref, nk_ref, nv_ref, gr_ref, gn_ref = rest[n_riders:n_riders + 9]
    _run_cast_riders(rest[:n_riders], rest[n_riders + 9:])
    n_sub = x_ref.shape[0] // SUB_TILE

    def pre(j):
        rows = slice(j * SUB_TILE, (j + 1) * SUB_TILE)
        return _rms(x_ref[rows, :], gain_ref[...]).astype(BF16)

    def project(j, u):
        rows = slice(j * SUB_TILE, (j + 1) * SUB_TILE)

        def proj(idx):
            lo = MIX_OFFS[idx]
            return jnp.dot(u, w_ref[:, lo:lo + MIX_SPLITS[idx]], preferred_element_type=F32)

        def store_groups(out_ref, y):
            width = out_ref.shape[-1]
            for g in range(out_ref.shape[0]):
                out_ref[g, rows, :] = y[:, g * width:(g + 1) * width].astype(out_ref.dtype)

        def rotary(y, cos, sin, out_ref):
            for h in range(RET_HEADS):
                yh = y[:, h * RET_QK_DIM:(h + 1) * RET_QK_DIM]
                out_ref[h, rows, :] = (yh * cos + pltpu.roll(yh, RET_QK_DIM // 2, 1) * sin).astype(out_ref.dtype)

        rotary(proj(0), rope_ref[0, rows, :], rope_ref[1, rows, :], rq_ref)
        rotary(proj(1), rope_ref[2, rows, :], rope_ref[3, rows, :], rk_ref)
        store_groups(rv_ref, proj(2))
        store_groups(rg_ref, _silu(proj(3)))
        store_groups(nq_ref, proj(4) * (LOG2E * NA_HEAD_DIM ** -0.5))
        store_groups(nk_ref, proj(5))
        store_groups(nv_ref, proj(6))
        gr_ref[rows, :] = proj(7).astype(gr_ref.dtype)
        gn_ref[rows, :] = proj(8).astype(gn_ref.dtype)

    u = pre(0)
    for j in range(n_sub):
        u_next = pre(j + 1) if j + 1 < n_sub else None
        project(j, u)
        u = u_next


def _rope_tables(seq):
    half = RET_QK_DIM // 2
    pos = jnp.arange(seq, dtype=F32)
    inv = 1.0 / (ROPE_BASE ** jnp.linspace(0.0, 1.0, half, dtype=F32))
    ang = pos[:, None] * inv[None, :]
    cos, sin = jnp.cos(ang), jnp.sin(ang)
    cos2 = jnp.concatenate([cos, cos], axis=-1)
    sin2 = jnp.concatenate([-sin, sin], axis=-1)
    ks = RET_QK_DIM ** -0.5
    return jnp.stack([cos2, sin2, cos2 * ks, sin2 * ks])


def _mix_in(x, gain, w, seq, riders=()):
    t = x.shape[0]
    tm = TOKEN_TILE
    tiles_per_seq = seq // tm
    out_groups = (RET_HEADS,) * 4 + (NA_SLABS,) * 3 + (1, 1)
    rider_specs, rider_shapes = _cast_riders(riders, t // tm)
    outs = pl.pallas_call(
        functools.partial(_mixin_kernel, len(riders)),
        grid=(t // tm,),
        in_specs=[pl.BlockSpec((tm, D_MODEL), lambda i: (i, 0)),
                  _const_spec((1, D_MODEL)), _const_spec((D_MODEL, MIX_IN_W)),
                  pl.BlockSpec((4, tm, RET_QK_DIM), lambda i: (0, i % tiles_per_seq, 0))] + rider_specs,
        out_specs=[pl.BlockSpec((tm, wd), lambda i: (i, 0)) if g == 1
                   else pl.BlockSpec((g, tm, wd // g), lambda i: (0, i, 0))
                   for wd, g in zip(MIX_SPLITS, out_groups)] + rider_specs,
        out_shape=[jax.ShapeDtypeStruct((t, wd) if g == 1 else (g, t, wd // g), BF16)
                   for wd, g in zip(MIX_SPLITS, out_groups)] + rider_shapes,
        compiler_params=pltpu.CompilerParams(
            dimension_semantics=("arbitrary",), vmem_limit_bytes=VMEM_LIMIT),
        name="mix_in",
    )(x, gain, w, _rope_tables(seq), *riders)
    return outs[:len(MIX_SPLITS)], outs[len(MIX_SPLITS):]


def _log_sigmoid(z):
    return jnp.minimum(z, 0.0) - jnp.log1p(jnp.exp(-jnp.abs(z)))


def _ret_kernel(dec_ref, q_ref, k_ref, v_ref, o_ref, u_ref, st_ref, dm_ref, vec_ref, blk_ref):
    c = RET_BLOCK
    dk = RET_QK_DIM
    dv = RET_V_DIM
    n_heads = q_ref.shape[0]
    n_blocks = q_ref.shape[1] // c
    tn = (((0,), (0,)), ((), ()))
    nt = (((1,), (1,)), ((), ()))

    @pl.when(pl.program_id(0) == 0)
    def _decay_tables():
        def log_gamma(z, shape):
            return _log_sigmoid(jnp.full(shape, z, F32))

        ri = lax.broadcasted_iota(jnp.int32, (c, c), 0).astype(F32)
        ci = lax.broadcasted_iota(jnp.int32, (c, c), 1).astype(F32)
        diff = ri - ci
        r = lax.broadcasted_iota(jnp.int32, (c, dk), 0).astype(F32)
        for h in range(n_heads):
            zf = dec_ref[0, h]
            zb = dec_ref[1, h]
            dm_ref[h] = jnp.where(diff >= 0, jnp.exp(log_gamma(zf, (c, c)) * jnp.maximum(diff, 0.0)),
                                  jnp.exp(log_gamma(zb, (c, c)) * jnp.maximum(-diff, 0.0)))
            lgf = log_gamma(zf, (c, dk))
            lgb = log_gamma(zb, (c, dk))
            vec_ref[h, 0] = jnp.exp(lgf * (r + 1.0)).astype(BF16)
            vec_ref[h, 1] = jnp.exp(lgb * (c - r)).astype(BF16)
            vec_ref[h, 2] = jnp.exp(lgf * (c - 1.0 - r)).astype(BF16)
            vec_ref[h, 3] = jnp.exp(lgb * r).astype(BF16)
            blk_ref[h, 0] = jnp.exp(log_gamma(zf, (dk, dv)) * c)
            blk_ref[h, 1] = jnp.exp(log_gamma(zb, (dk, dv)) * c)

    def rows(n):
        return slice(n * c, (n + 1) * c)

    def updates(h):
        for n in range(n_blocks):
            k = k_ref[h, rows(n), :]
            k_cat = jnp.concatenate([k * vec_ref[h, 2], k * vec_ref[h, 3]], axis=1)
            u_ref[h % 2, n] = lax.dot_general(k_cat, v_ref[h, rows(n), :], tn, preferred_element_type=F32)

    def scans(h):
        sf = jnp.zeros((dk, dv), F32)
        for n in range(n_blocks):
            st_ref[h % 2, n, 0:dk, :] = sf.astype(BF16)
            if n + 1 < n_blocks:
                sf = blk_ref[h, 0] * sf + u_ref[h % 2, n, 0:dk, :]
        sb = jnp.zeros((dk, dv), F32)
        for n in reversed(range(n_blocks)):
            st_ref[h % 2, n, dk:2 * dk, :] = sb.astype(BF16)
            if n > 0:
                sb = blk_ref[h, 1] * sb + u_ref[h % 2, n, dk:2 * dk, :]

    def outputs(h):
        def scores(n):
            return lax.dot_general(q_ref[h, rows(n), :], k_ref[h, rows(n), :], nt,
                                   preferred_element_type=F32)

        s_next = scores(0)
        for n in range(n_blocks):
            s_cur = s_next
            if n + 1 < n_blocks:
                s_next = scores(n + 1)
            p = (s_cur * dm_ref[h]).astype(BF16)
            q = q_ref[h, rows(n), :]
            lhs = jnp.concatenate([p, q * vec_ref[h, 0], q * vec_ref[h, 1]], axis=1)
            rhs = jnp.concatenate([v_ref[h, rows(n), :], st_ref[h % 2, n]], axis=0)
            y = jnp.dot(lhs, rhs, preferred_element_type=F32)
            yn = y * lax.rsqrt(jnp.mean(y * y, axis=-1, keepdims=True) + RMS_EPS)
            o_ref[h, rows(n), :] = yn.astype(o_ref.dtype)

    updates(0)
    for h in range(n_heads):
        if h + 1 < n_heads:
            updates(h + 1)
        scans(h)
        outputs(h)


def _retention(decay_logits, rq, rk, rv):
    n_heads, b, s, _ = rq.shape
    c = RET_BLOCK
    n_blocks = s // c
    qk_spec = pl.BlockSpec((n_heads, None, s, RET_QK_DIM), lambda i, dec: (0, i, 0, 0))
    v_spec = pl.BlockSpec((n_heads, None, s, RET_V_DIM), lambda i, dec: (0, i, 0, 0))
    return pl.pallas_call(
        _ret_kernel,
        grid_spec=pltpu.PrefetchScalarGridSpec(
            num_scalar_prefetch=1,
            grid=(b,),
            in_specs=[qk_spec, qk_spec, v_spec],
            out_specs=v_spec,
            scratch_shapes=[pltpu.VMEM((2, n_blocks, 2 * RET_QK_DIM, RET_V_DIM), F32),
                            pltpu.VMEM((2, n_blocks, 2 * RET_QK_DIM, RET_V_DIM), BF16),
                            pltpu.VMEM((n_heads, c, c), F32),
                            pltpu.VMEM((n_heads, 4, c, RET_QK_DIM), BF16),
                            pltpu.VMEM((n_heads, 2, RET_QK_DIM, RET_V_DIM), F32)],
        ),
        out_shape=jax.ShapeDtypeStruct((n_heads, b, s, RET_V_DIM), BF16),
        compiler_params=pltpu.CompilerParams(
            dimension_semantics=("arbitrary",), vmem_limit_bytes=VMEM_LIMIT),
        name="retention",
    )(decay_logits, rq, rk, rv)


def _na_blocks(extent, q_size, k_size, window):
    n = extent // q_size
    out = []
    for i in range(n):
        q0 = i * q_size
        k0 = min(max(q0 - window // 2, 0), extent - k_size)
        out.append((q0, k0, 0 if i == 0 else (2 if i == n - 1 else 1)))
    return tuple(out)


NA_ROWS = 32
_NA_ROW_BLOCKS = _na_blocks(NA_ROWS, NA_Q_ROWS, NA_K_ROWS, NA_WIN_ROWS)
_NA_COL_BLOCKS = _na_blocks(GRID_W, NA_Q_COLS, NA_K_COLS, NA_WIN_COLS)
_NA_CONFIGS = 9


def _na_bias_kernel(rel_ref, out_ref, col_ref):
    nk = NA_K_ROWS * NA_K_COLS
    base = pl.program_id(0) * (NA_REL_ROWS * NA_REL_COLS)
    lane = lax.broadcasted_iota(jnp.int32, (NA_Q_COLS, nk), 1)
    kr = lane // NA_K_COLS
    col_cfgs = (_NA_COL_BLOCKS[0], _NA_COL_BLOCKS[1], _NA_COL_BLOCKS[3])
    row_cfgs = (_NA_ROW_BLOCKS[0], _NA_ROW_BLOCKS[1], _NA_ROW_BLOCKS[3])
    reps = nk // NA_LANES
    kc_one = lax.broadcasted_iota(jnp.int32, (NA_Q_COLS, NA_LANES), 1) % NA_K_COLS
    qc_one = lax.broadcasted_iota(jnp.int32, (NA_Q_COLS, NA_LANES), 0)
    for b, (c0, kc0, _) in enumerate(col_cfgs):
        kc_abs = kc_one + kc0
        qc_abs = qc_one + c0
        rel_col = jnp.clip(kc_abs - qc_abs + NA_WIN_COLS - 1, 0, NA_REL_COLS - 1)
        win = jnp.clip(qc_abs - NA_WIN_COLS // 2, 0, GRID_W - NA_WIN_COLS)
        in_win = (kc_abs >= win) & (kc_abs < win + NA_WIN_COLS)

        def rel_row_body(dr, carry, rel_col=rel_col, in_win=in_win, b=b):
            vals = jnp.zeros((NA_Q_COLS, NA_LANES), F32)
            for dc in range(NA_REL_COLS):
                vals = jnp.where(rel_col == dc, rel_ref[base + dr * NA_REL_COLS + dc], vals)
            vals = jnp.where(in_win, vals * LOG2E, NEG_INF)
            col_ref[b * NA_REL_ROWS + dr] = jnp.concatenate([vals] * reps, axis=1)
            return carry

        lax.fori_loop(0, NA_REL_ROWS, rel_row_body, 0)

    for a, (r0, ks, _) in enumerate(row_cfgs):
        kr_abs = kr + ks
        for qr in range(NA_Q_ROWS):
            r_abs = r0 + qr
            rs = min(max(r_abs - NA_WIN_ROWS // 2, 0), NA_ROWS - NA_WIN_ROWS)
            hits = [kr_abs == rs + j for j in range(NA_WIN_ROWS)]
            for b in range(3):
                tile = jnp.full((NA_Q_COLS, nk), NEG_INF, F32)
                for j in range(NA_WIN_ROWS):
                    dr = rs + j - r_abs + NA_WIN_ROWS - 1
                    tile = jnp.where(hits[j], col_ref[b * NA_REL_ROWS + dr], tile)
                out_ref[0, a * 3 + b, qr * NA_Q_COLS:(qr + 1) * NA_Q_COLS, :] = tile


def _na_bias_tables(rel_bias):
    nq = NA_Q_ROWS * NA_Q_COLS
    nk = NA_K_ROWS * NA_K_COLS
    return pl.pallas_call(
        _na_bias_kernel,
        grid_spec=pltpu.PrefetchScalarGridSpec(
            num_scalar_prefetch=1,
            grid=(NA_HEADS,),
            in_specs=[],
            out_specs=pl.BlockSpec((1, _NA_CONFIGS, nq, nk), lambda h, rel: (h, 0, 0, 0)),
            scratch_shapes=[pltpu.VMEM((3 * NA_REL_ROWS, NA_Q_COLS, nk), F32)],
        ),
        out_shape=jax.ShapeDtypeStruct((NA_HEADS, _NA_CONFIGS, nq, nk), F32),
        compiler_params=pltpu.CompilerParams(dimension_semantics=("parallel",)),
        name="na_bias",
    )(rel_bias.astype(F32).reshape(-1))


def _na_kernel(q_ref, k_ref, v_ref, bias_ref, o_ref, ks_ref, vs_ref):
    n_slabs = q_ref.shape[0]
    nq = NA_Q_ROWS * NA_Q_COLS
    nk = NA_K_ROWS * NA_K_COLS
    shift = NA_WIN_COLS // 2
    kept = GRID_W - 2 * shift
    for sl in range(n_slabs):
        for src, dst in ((k_ref, ks_ref), (v_ref, vs_ref)):
            dst[sl, :, 0:kept, :] = src[sl, :, shift:shift + kept, :]

    def window(refs, sl, block):
        (_, kr0, _), (_, kc0, _) = block
        aligned = kc0 % (2 * shift) == 0
        ref, c = (refs[0], kc0) if aligned else (refs[1], kc0 - shift)
        assert c % (2 * shift) == 0 and (aligned or c + NA_K_COLS <= kept)
        return ref[sl, kr0:kr0 + NA_K_ROWS, c:c + NA_K_COLS, :].reshape(nk, NA_LANES)

    first = lax.broadcasted_iota(jnp.int32, (nq, NA_LANES), 1) < NA_HEAD_DIM
    ones = jnp.ones((nk, NA_LANES), BF16)
    nt = (((1,), (1,)), ((), ()))
    work = [(sl, (rb, cb)) for sl in range(n_slabs) for rb in _NA_ROW_BLOCKS for cb in _NA_COL_BLOCKS]

    def scores(sl, block):
        (r0, _, rcfg), (c0, _, ccfg) = block
        q = q_ref[sl, r0:r0 + NA_Q_ROWS, c0:c0 + NA_Q_COLS, :].reshape(nq, NA_LANES)
        zero = jnp.zeros_like(q)
        q2 = jnp.concatenate([jnp.where(first, q, zero), jnp.where(first, zero, q)], axis=0)
        cfg = rcfg * 3 + ccfg
        bias = jnp.concatenate([bias_ref[2 * sl, cfg], bias_ref[2 * sl + 1, cfg]], axis=0)
        return lax.dot_general(q2, window((k_ref, ks_ref), sl, block), nt,
                               preferred_element_type=F32) + bias

    def attend(sl, block, s):
        m = jnp.max(s, axis=-1, keepdims=True)
        p = jnp.exp2(s - m).astype(BF16)
        o = jnp.dot(p, jnp.concatenate([window((v_ref, vs_ref), sl, block), ones], axis=1),
                    preferred_element_type=F32)
        o = o[:, :NA_LANES] / o[:, NA_LANES:]
        return jnp.where(first, o[:nq], o[nq:])

    s_next = scores(*work[0])
    for i, (sl, blk) in enumerate(work):
        s_cur = s_next
        if i + 1 < len(work):
            s_next = scores(*work[i + 1])
        (r0, _, _), (c0, _, _) = blk
        o_ref[sl, r0:r0 + NA_Q_ROWS, c0:c0 + NA_Q_COLS, :] = (
            attend(sl, blk, s_cur).reshape(NA_Q_ROWS, NA_Q_COLS, NA_LANES).astype(o_ref.dtype))


def _natten(nq, nk, nv, bias_tabs):
    b, rows = nq.shape[1], nq.shape[2]
    per = NA_SLABS_PER_STEP
    slab = pl.BlockSpec((per, None, rows, GRID_W, NA_LANES), lambda j, i: (j, i, 0, 0, 0))
    nqk = NA_Q_ROWS * NA_Q_COLS
    nkk = NA_K_ROWS * NA_K_COLS
    return pl.pallas_call(
        _na_kernel,
        grid=(NA_SLABS // per, b),
        in_specs=[slab, slab, slab,
                  pl.BlockSpec((2 * per, _NA_CONFIGS, nqk, nkk), lambda j, i: (j, 0, 0, 0),
                               pipeline_mode=pl.Buffered(1))],
        out_specs=slab,
        out_shape=jax.ShapeDtypeStruct((NA_SLABS, b, rows, GRID_W, NA_LANES), BF16),
        scratch_shapes=[pltpu.VMEM((per, rows, GRID_W, NA_LANES), BF16),
                        pltpu.VMEM((per, rows, GRID_W, NA_LANES), BF16)],
        compiler_params=pltpu.CompilerParams(
            dimension_semantics=("parallel", "parallel"), vmem_limit_bytes=VMEM_LIMIT),
        name="natten",
    )(nq, nk, nv, bias_tabs)


def _mixout_kernel(x_ref, yr_ref, rg_ref, na_ref, gr_ref, gn_ref, wr_ref, wn_ref, wo_ref, gain_ref, o_ref):
    n_sub = x_ref.shape[0] // SUB_TILE

    def rows(j):
        return slice(j * SUB_TILE, (j + 1) * SUB_TILE)

    def branches(j):
        def gathered(ref):
            return jnp.concatenate([ref[g, rows(j), :] for g in range(ref.shape[0])], axis=1)

        y_ret = jnp.dot(gathered(yr_ref) * gathered(rg_ref), wr_ref[...], preferred_element_type=F32)
        y_na = jnp.dot(gathered(na_ref), wn_ref[...], preferred_element_type=F32)
        merged = (jax.nn.sigmoid(gr_ref[rows(j), :].astype(F32)) * y_ret
                  + jax.nn.sigmoid(gn_ref[rows(j), :].astype(F32)) * y_na)
        return merged.astype(BF16)

    def finish(j, merged):
        m = jnp.dot(merged, wo_ref[...], preferred_element_type=F32)
        o_ref[rows(j), :] = x_ref[rows(j), :] + _rms(m, gain_ref[...])

    merged = branches(0)
    for j in range(n_sub):
        merged_next = branches(j + 1) if j + 1 < n_sub else None
        finish(j, merged)
        merged = merged_next


def _mix_out(x, y_ret, rg, y_na, g_ret, g_na, w_ret, w_na, w_out, gain):
    t = x.shape[0]
    tm = FFN_TILE

    def row(width):
        return pl.BlockSpec((tm, width), lambda i: (i, 0))

    return pl.pallas_call(
        _mixout_kernel,
        grid=(t // tm,),
        in_specs=[row(D_MODEL),
                  pl.BlockSpec((RET_HEADS, tm, RET_V_DIM), lambda i: (0, i, 0)),
                  pl.BlockSpec((RET_HEADS, tm, RET_V_DIM), lambda i: (0, i, 0)),
                  pl.BlockSpec((NA_SLABS, tm, NA_LANES), lambda i: (0, i, 0)),
                  row(D_MODEL), row(D_MODEL),
                  _const_spec((RET_V_W, D_MODEL)), _const_spec((NA_W, D_MODEL)),
                  _const_spec((D_MODEL, D_MODEL)), _const_spec((1, D_MODEL))],
        out_specs=row(D_MODEL),
        out_shape=jax.ShapeDtypeStruct((t, D_MODEL), F32),
        compiler_params=pltpu.CompilerParams(
            dimension_semantics=("parallel",), vmem_limit_bytes=VMEM_LIMIT),
        name="mix_out",
    )(x, y_ret, rg, y_na, g_ret, g_na, w_ret, w_na, w_out, gain)


def kernel(x, ffn1_pre_norm, ffn1_w_in, ffn1_w_out, ffn1_post_norm, mix_pre_norm, w_mix_in,
           ret_decay_fwd, ret_decay_bwd, na_rel_bias, w_ret_out, w_na_out, w_mix_out, mix_post_norm,
           ffn2_pre_norm, ffn2_w_in, ffn2_w_out, ffn2_post_norm):
    b, s, d = x.shape
    assert d == D_MODEL and s % GRID_W == 0 and s % RET_BLOCK == 0 and (b * s) % TOKEN_TILE == 0
    assert s % TOKEN_TILE == 0
    rows = s // GRID_W
    assert rows == NA_ROWS, "the neighbourhood-attention tiling is laid out for a 32 x 64 token grid"
    t = b * s
    xt = x.reshape(t, d)
    for l in range(ffn1_w_in.shape[0]):
        gain = lambda g: g[l].reshape(1, D_MODEL).astype(F32)
        wt = lambda w: w[l].astype(BF16)
        xt, (w_mix, w_ret, w_na, w_out) = _ffn(
            xt, gain(ffn1_pre_norm), wt(ffn1_w_in), wt(ffn1_w_out), gain(ffn1_post_norm),
            riders=(w_mix_in[l], w_ret_out[l], w_na_out[l], w_mix_out[l]))
        (rq, rk, rv, rg, nq, nk, nv, g_ret, g_na), (w_ffn2_in, w_ffn2_out) = _mix_in(
            xt, gain(mix_pre_norm), w_mix, s, riders=(ffn2_w_in[l], ffn2_w_out[l]))
        decay = jnp.stack([ret_decay_fwd[l], ret_decay_bwd[l]]).astype(F32)
        seq = lambda a: a.reshape(a.shape[0], b, s, a.shape[-1])
        y_ret = _retention(decay, seq(rq), seq(rk), seq(rv)).reshape(RET_HEADS, t, RET_V_DIM)
        grid5 = lambda a: a.reshape(NA_SLABS, b, rows, GRID_W, NA_LANES)
        y_na = _natten(grid5(nq), grid5(nk), grid5(nv), _na_bias_tables(na_rel_bias[l]))
        xt = _mix_out(xt, y_ret, rg, y_na.reshape(NA_SLABS, t, NA_LANES), g_ret, g_na,
                      w_ret, w_na, w_out, gain(mix_post_norm))
        xt, _ = _ffn(xt, gain(ffn2_pre_norm), w_ffn2_in, w_ffn2_out, gain(ffn2_post_norm))
    return xt.reshape(b, s, d)
```

```python
import functools

import numpy as np
import jax
import jax.numpy as jnp
from jax import lax
from jax.experimental import pallas as pl
from jax.experimental.pallas import tpu as pltpu

F32 = jnp.float32
BF16 = jnp.bfloat16

D_MODEL = 1024
D_FF = 2816
RMS_EPS = 1e-6
ROPE_BASE = 10000.0
NEG_INF = -1e30
LOG2E = 1.4426950408889634

RET_HEADS = 4
RET_QK_DIM = 128
RET_V_DIM = 256
RET_QK_W = RET_HEADS * RET_QK_DIM
RET_V_W = RET_HEADS * RET_V_DIM
RET_BLOCK = 256

NA_HEADS = 8
NA_HEAD_DIM = 64
NA_W = NA_HEADS * NA_HEAD_DIM
GRID_W = 64
NA_WIN_ROWS = 8
NA_WIN_COLS = 16
NA_REL_ROWS = 2 * NA_WIN_ROWS - 1
NA_REL_COLS = 2 * NA_WIN_COLS - 1
NA_Q_ROWS = 8
NA_Q_COLS = 16
NA_K_ROWS = 16
NA_K_COLS = 32
NA_LANES = 128
NA_SLABS = NA_W // NA_LANES
NA_SLABS_PER_STEP = 4

MIX_SPLITS = (RET_QK_W, RET_QK_W, RET_V_W, RET_V_W, NA_W, NA_W, NA_W, D_MODEL, D_MODEL)
MIX_OFFS = tuple(int(v) for v in np.cumsum((0,) + MIX_SPLITS[:-1]))
MIX_IN_W = sum(MIX_SPLITS)

TOKEN_TILE = 512
FFN_TILE = 1024
SUB_TILE = 256
FF_CHUNK = 256
VMEM_LIMIT = 56 * 1024 * 1024
BF16_SUBLANES = 16


def _rms(x, gain):
    ms = jnp.mean(x * x, axis=-1, keepdims=True)
    return x * lax.rsqrt(ms + RMS_EPS) * gain


def _silu(x):
    return x * jax.nn.sigmoid(x)


def _cast_riders(weights, steps):
    in_specs, out_shapes = [], []
    for w in weights:
        rows, cols = w.shape
        block = min(d for d in range(BF16_SUBLANES, rows + 1, BF16_SUBLANES)
                    if rows % d == 0 and rows // d <= steps)
        last = rows // block - 1
        in_specs.append(pl.BlockSpec((block, cols), lambda i, last=last: (jnp.minimum(i, last), 0)))
        out_shapes.append(jax.ShapeDtypeStruct(w.shape, BF16))
    return in_specs, out_shapes


def _run_cast_riders(src_refs, dst_refs):
    for src, dst in zip(src_refs, dst_refs):
        dst[...] = src[...].astype(dst.dtype)


def _const_spec(shape):
    nd = len(shape)
    return pl.BlockSpec(shape, lambda *_: (0,) * nd, pipeline_mode=pl.Buffered(1))


def _ffn_kernel(n_riders, has_inc, x_ref, *rest):
    if has_inc:
        inc_ref, rest, res_ref = rest[0], rest[1:-1], rest[-1]
    gpre_ref, win_ref, wout_ref, gpost_ref = rest[:4]
    rest = rest[4:]
    o_ref, act_ref = rest[n_riders], rest[-1]
    _run_cast_riders(rest[:n_riders], rest[n_riders + 1:-1])
    n_sub = x_ref.shape[0] // SUB_TILE

    def rows(j):
        return slice(j * SUB_TILE, (j + 1) * SUB_TILE)

    def residual(j):
        return res_ref[rows(j), :] if has_inc else x_ref[rows(j), :]

    def pre(j):
        x = x_ref[rows(j), :]
        if has_inc:
            x = x + inc_ref[rows(j), :].astype(F32)
            res_ref[rows(j), :] = x
        return _rms(x, gpre_ref[...]).astype(BF16)

    def up(j, xn):
        for c in range(D_FF // FF_CHUNK):
            lo = c * FF_CHUNK
            g = jnp.dot(xn, win_ref[:, lo:lo + FF_CHUNK], preferred_element_type=F32)
            u = jnp.dot(xn, win_ref[:, D_FF + lo:D_FF + lo + FF_CHUNK], preferred_element_type=F32)
            act_ref[rows(j), lo:lo + FF_CHUNK] = (_silu(g) * u).astype(BF16)

    def down(j):
        h = jnp.dot(act_ref[rows(j), :], wout_ref[...], preferred_element_type=F32)
        o_ref[rows(j), :] = residual(j) + 0.5 * _rms(h, gpost_ref[...])

    xn = pre(0)
    for j in range(n_sub):
        xn_next = pre(j + 1) if j + 1 < n_sub else None
        up(j, xn)
        if j >= 1:
            down(j - 1)
        xn = xn_next
    down(n_sub - 1)


def _ffn(x, gpre, w_in, w_out, gpost, riders=(), inc=None):
    t = x.shape[0]
    tm = FFN_TILE
    row = pl.BlockSpec((tm, D_MODEL), lambda i: (i, 0))
    rider_specs, rider_shapes = _cast_riders(riders, t // tm)
    has_inc = inc is not None
    out, *cast = pl.pallas_call(
        functools.partial(_ffn_kernel, len(riders), has_inc),
        grid=(t // tm,),
        in_specs=[row] + ([row] if has_inc else [])
        + [_const_spec((1, D_MODEL)), _const_spec((D_MODEL, 2 * D_FF)),
           _const_spec((D_FF, D_MODEL)), _const_spec((1, D_MODEL))] + rider_specs,
        out_specs=[row] + rider_specs,
        out_shape=[jax.ShapeDtypeStruct((t, D_MODEL), F32)] + rider_shapes,
        scratch_shapes=[pltpu.VMEM((tm, D_FF), BF16)]
        + ([pltpu.VMEM((tm, D_MODEL), F32)] if has_inc else []),
        compiler_params=pltpu.CompilerParams(
            dimension_semantics=("arbitrary",), vmem_limit_bytes=VMEM_LIMIT),
        name="ffn",
    )(x, *([inc] if has_inc else []), gpre, w_in, w_out, gpost, *riders)
    return out, cast


def _mixin_kernel(n_riders, x_ref, gain_ref, w_ref, rope_ref, *rest):
    rq_ref, rk_ref, rv_ref, rg_ref, nq_ref, nk_ref, nv_ref, gr_ref, gn_ref = rest[n_riders:n_riders + 9]
    _run_cast_riders(rest[:n_riders], rest[n_riders + 9:])
    n_sub = x_ref.shape[0] // SUB_TILE

    def pre(j):
        rows = slice(j * SUB_TILE, (j + 1) * SUB_TILE)
        return _rms(x_ref[rows, :], gain_ref[...]).astype(BF16)

    def project(j, u):
        rows = slice(j * SUB_TILE, (j + 1) * SUB_TILE)

        def proj(idx):
            lo = MIX_OFFS[idx]
            return jnp.dot(u, w_ref[:, lo:lo + MIX_SPLITS[idx]], preferred_element_type=F32)

        def store_groups(out_ref, y):
            width = out_ref.shape[-1]
            for g in range(out_ref.shape[0]):
                out_ref[g, rows, :] = y[:, g * width:(g + 1) * width].astype(out_ref.dtype)

        def rotary(y, cos, sin, out_ref):
            for h in range(RET_HEADS):
                yh = y[:, h * RET_QK_DIM:(h + 1) * RET_QK_DIM]
                out_ref[h, rows, :] = (yh * cos + pltpu.roll(yh, RET_QK_DIM // 2, 1) * sin).astype(out_ref.dtype)

        rotary(proj(0), rope_ref[0, rows, :], rope_ref[1, rows, :], rq_ref)
        rotary(proj(1), rope_ref[2, rows, :], rope_ref[3, rows, :], rk_ref)
        store_groups(rv_ref, proj(2))
        store_groups(rg_ref, _silu(proj(3)))
        store_groups(nq_ref, proj(4) * (LOG2E * NA_HEAD_DIM ** -0.5))
        store_groups(nk_ref, proj(5))
        store_groups(nv_ref, proj(6))
        gr_ref[rows, :] = proj(7).astype(gr_ref.dtype)
        gn_ref[rows, :] = proj(8).astype(gn_ref.dtype)

    u = pre(0)
    for j in range(n_sub):
        u_next = pre(j + 1) if j + 1 < n_sub else None
        project(j, u)
        u = u_next


def _rope_tables(seq):
    half = RET_QK_DIM // 2
    pos = jnp.arange(seq, dtype=F32)
    inv = 1.0 / (ROPE_BASE ** jnp.linspace(0.0, 1.0, half, dtype=F32))
    ang = pos[:, None] * inv[None, :]
    cos, sin = jnp.cos(ang), jnp.sin(ang)
    cos2 = jnp.concatenate([cos, cos], axis=-1)
    sin2 = jnp.concatenate([-sin, sin], axis=-1)
    ks = RET_QK_DIM ** -0.5
    return jnp.stack([cos2, sin2, cos2 * ks, sin2 * ks])


def _mix_in(x, gain, w, seq, riders=()):
    t = x.shape[0]
    tm = TOKEN_TILE
    tiles_per_seq = seq // tm
    out_groups = (RET_HEADS,) * 4 + (NA_SLABS,) * 3 + (1, 1)
    rider_specs, rider_shapes = _cast_riders(riders, t // tm)
    outs = pl.pallas_call(
        functools.partial(_mixin_kernel, len(riders)),
        grid=(t // tm,),
        in_specs=[pl.BlockSpec((tm, D_MODEL), lambda i: (i, 0)),
                  _const_spec((1, D_MODEL)), _const_spec((D_MODEL, MIX_IN_W)),
                  pl.BlockSpec((4, tm, RET_QK_DIM), lambda i: (0, i % tiles_per_seq, 0))] + rider_specs,
        out_specs=[pl.BlockSpec((tm, wd), lambda i: (i, 0)) if g == 1
                   else pl.BlockSpec((g, tm, wd // g), lambda i: (0, i, 0))
                   for wd, g in zip(MIX_SPLITS, out_groups)] + rider_specs,
        out_shape=[jax.ShapeDtypeStruct((t, wd) if g == 1 else (g, t, wd // g), BF16)
                   for wd, g in zip(MIX_SPLITS, out_groups)] + rider_shapes,
        compiler_params=pltpu.CompilerParams(
            dimension_semantics=("arbitrary",), vmem_limit_bytes=VMEM_LIMIT),
        name="mix_in",
    )(x, gain, w, _rope_tables(seq), *riders)
    return outs[:len(MIX_SPLITS)], outs[len(MIX_SPLITS):]


def _log_sigmoid(z):
    return jnp.minimum(z, 0.0) - jnp.log1p(jnp.exp(-jnp.abs(z)))


def _ret_kernel(dec_ref, q_ref, k_ref, v_ref, g_ref, o_ref, u_ref, st_ref, dm_ref, vec_ref, blk_ref):
    c = RET_BLOCK
    dk = RET_QK_DIM
    dv = RET_V_DIM
    n_heads = q_ref.shape[0]
    n_blocks = q_ref.shape[1] // c
    tn = (((0,), (0,)), ((), ()))
    nt = (((1,), (1,)), ((), ()))

    @pl.when(pl.program_id(0) == 0)
    def _decay_tables():
        def log_gamma(z, shape):
            return _log_sigmoid(jnp.full(shape, z, F32))

        ri = lax.broadcasted_iota(jnp.int32, (c, c), 0).astype(F32)
        ci = lax.broadcasted_iota(jnp.int32, (c, c), 1).astype(F32)
        diff = ri - ci
        r = lax.broadcasted_iota(jnp.int32, (c, dk), 0).astype(F32)
        for h in range(n_heads):
            zf = dec_ref[0, h]
            zb = dec_ref[1, h]
            dm_ref[h] = jnp.where(diff >= 0, jnp.exp(log_gamma(zf, (c, c)) * jnp.maximum(diff, 0.0)),
                                  jnp.exp(log_gamma(zb, (c, c)) * jnp.maximum(-diff, 0.0)))
            lgf = log_gamma(zf, (c, dk))
            lgb = log_gamma(zb, (c, dk))
            vec_ref[h, 0] = jnp.exp(lgf * (r + 1.0)).astype(BF16)
            vec_ref[h, 1] = jnp.exp(lgb * (c - r)).astype(BF16)
            vec_ref[h, 2] = jnp.exp(lgf * (c - 1.0 - r)).astype(BF16)
            vec_ref[h, 3] = jnp.exp(lgb * r).astype(BF16)
            blk_ref[h, 0] = jnp.exp(log_gamma(zf, (dk, dv)) * c)
            blk_ref[h, 1] = jnp.exp(log_gamma(zb, (dk, dv)) * c)

    def rows(n):
        return slice(n * c, (n + 1) * c)

    def updates(h):
        for n in range(n_blocks):
            k = k_ref[h, rows(n), :]
            k_cat = jnp.concatenate([k * vec_ref[h, 2], k * vec_ref[h, 3]], axis=1)
            u_ref[h % 2, n] = lax.dot_general(k_cat, v_ref[h, rows(n), :], tn, preferred_element_type=F32)

    def scans(h):
        sf = jnp.zeros((dk, dv), F32)
        for n in range(n_blocks):
            st_ref[h % 2, n, 0:dk, :] = sf.astype(BF16)
            if n + 1 < n_blocks:
                sf = blk_ref[h, 0] * sf + u_ref[h % 2, n, 0:dk, :]
        sb = jnp.zeros((dk, dv), F32)
        for n in reversed(range(n_blocks)):
            st_ref[h % 2, n, dk:2 * dk, :] = sb.astype(BF16)
            if n > 0:
                sb = blk_ref[h, 1] * sb + u_ref[h % 2, n, dk:2 * dk, :]

    def outputs(h):
        def scores(n):
            return lax.dot_general(q_ref[h, rows(n), :], k_ref[h, rows(n), :], nt,
                                   preferred_element_type=F32)

        s_next = scores(0)
        for n in range(n_blocks):
            s_cur = s_next
            if n + 1 < n_blocks:
                s_next = scores(n + 1)
            p = (s_cur * dm_ref[h]).astype(BF16)
            q = q_ref[h, rows(n), :]
            lhs = jnp.concatenate([p, q * vec_ref[h, 0], q * vec_ref[h, 1]], axis=1)
            rhs = jnp.concatenate([v_ref[h, rows(n), :], st_ref[h % 2, n]], axis=0)
            y = jnp.dot(lhs, rhs, preferred_element_type=F32)
            yn = y * lax.rsqrt(jnp.mean(y * y, axis=-1, keepdims=True) + RMS_EPS)
            o_ref[h, rows(n), :] = (g_ref[h, rows(n), :].astype(F32) * yn).astype(o_ref.dtype)

    updates(0)
    for h in range(n_heads):
        if h + 1 < n_heads:
            updates(h + 1)
        scans(h)
        outputs(h)


def _retention(decay_logits, rq, rk, rv, rg):
    n_heads, b, s, _ = rq.shape
    c = RET_BLOCK
    n_blocks = s // c
    qk_spec = pl.BlockSpec((n_heads, None, s, RET_QK_DIM), lambda i, dec: (0, i, 0, 0))
    v_spec = pl.BlockSpec((n_heads, None, s, RET_V_DIM), lambda i, dec: (0, i, 0, 0))
    return pl.pallas_call(
        _ret_kernel,
        grid_spec=pltpu.PrefetchScalarGridSpec(
            num_scalar_prefetch=1,
            grid=(b,),
            in_specs=[qk_spec, qk_spec, v_spec, v_spec],
            out_specs=v_spec,
            scratch_shapes=[pltpu.VMEM((2, n_blocks, 2 * RET_QK_DIM, RET_V_DIM), F32),
                            pltpu.VMEM((2, n_blocks, 2 * RET_QK_DIM, RET_V_DIM), BF16),
                            pltpu.VMEM((n_heads, c, c), F32),
                            pltpu.VMEM((n_heads, 4, c, RET_QK_DIM), BF16),
                            pltpu.VMEM((n_heads, 2, RET_QK_DIM, RET_V_DIM), F32)],
        ),
        out_shape=jax.ShapeDtypeStruct((n_heads, b, s, RET_V_DIM), BF16),
        compiler_params=pltpu.CompilerParams(
            dimension_semantics=("arbitrary",), vmem_limit_bytes=VMEM_LIMIT),
        name="retention",
    )(decay_logits, rq, rk, rv, rg)


def _na_blocks(extent, q_size, k_size, window):
    n = extent // q_size
    out = []
    for i in range(n):
        q0 = i * q_size
        k0 = min(max(q0 - window // 2, 0), extent - k_size)
        out.append((q0, k0, 0 if i == 0 else (2 if i == n - 1 else 1)))
    return tuple(out)


NA_ROWS = 32
_NA_ROW_BLOCKS = _na_blocks(NA_ROWS, NA_Q_ROWS, NA_K_ROWS, NA_WIN_ROWS)
_NA_COL_BLOCKS = _na_blocks(GRID_W, NA_Q_COLS, NA_K_COLS, NA_WIN_COLS)
_NA_CONFIGS = 9


def _na_bias_kernel(rel_ref, out_ref, col_ref):
    nk = NA_K_ROWS * NA_K_COLS
    base = pl.program_id(0) * (NA_REL_ROWS * NA_REL_COLS)
    lane = lax.broadcasted_iota(jnp.int32, (NA_Q_COLS, nk), 1)
    kr = lane // NA_K_COLS
    col_cfgs = (_NA_COL_BLOCKS[0], _NA_COL_BLOCKS[1], _NA_COL_BLOCKS[3])
    row_cfgs = (_NA_ROW_BLOCKS[0], _NA_ROW_BLOCKS[1], _NA_ROW_BLOCKS[3])
    reps = nk // NA_LANES
    kc_one = lax.broadcasted_iota(jnp.int32, (NA_Q_COLS, NA_LANES), 1) % NA_K_COLS
    qc_one = lax.broadcasted_iota(jnp.int32, (NA_Q_COLS, NA_LANES), 0)
    for b, (c0, kc0, _) in enumerate(col_cfgs):
        kc_abs = kc_one + kc0
        qc_abs = qc_one + c0
        rel_col = jnp.clip(kc_abs - qc_abs + NA_WIN_COLS - 1, 0, NA_REL_COLS - 1)
        win = jnp.clip(qc_abs - NA_WIN_COLS // 2, 0, GRID_W - NA_WIN_COLS)
        in_win = (kc_abs >= win) & (kc_abs < win + NA_WIN_COLS)

        def rel_row_body(dr, carry, rel_col=rel_col, in_win=in_win, b=b):
            vals = jnp.zeros((NA_Q_COLS, NA_LANES), F32)
            for dc in range(NA_REL_COLS):
                vals = jnp.where(rel_col == dc, rel_ref[base + dr * NA_REL_COLS + dc], vals)
            vals = jnp.where(in_win, vals * LOG2E, NEG_INF)
            col_ref[b * NA_REL_ROWS + dr] = jnp.concatenate([vals] * reps, axis=1)
            return carry

        lax.fori_loop(0, NA_REL_ROWS, rel_row_body, 0)

    for a, (r0, ks, _) in enumerate(row_cfgs):
        kr_abs = kr + ks
        for qr in range(NA_Q_ROWS):
            r_abs = r0 + qr
            rs = min(max(r_abs - NA_WIN_ROWS // 2, 0), NA_ROWS - NA_WIN_ROWS)
            hits = [kr_abs == rs + j for j in range(NA_WIN_ROWS)]
            for b in range(3):
                tile = jnp.full((NA_Q_COLS, nk), NEG_INF, F32)
                for j in range(NA_WIN_ROWS):
                    dr = rs + j - r_abs + NA_WIN_ROWS - 1
                    tile = jnp.where(hits[j], col_ref[b * NA_REL_ROWS + dr], tile)
                out_ref[0, a * 3 + b, qr * NA_Q_COLS:(qr + 1) * NA_Q_COLS, :] = tile


def _na_bias_tables(rel_bias):
    nq = NA_Q_ROWS * NA_Q_COLS
    nk = NA_K_ROWS * NA_K_COLS
    return pl.pallas_call(
        _na_bias_kernel,
        grid_spec=pltpu.PrefetchScalarGridSpec(
            num_scalar_prefetch=1,
            grid=(NA_HEADS,),
            in_specs=[],
            out_specs=pl.BlockSpec((1, _NA_CONFIGS, nq, nk), lambda h, rel: (h, 0, 0, 0)),
            scratch_shapes=[pltpu.VMEM((3 * NA_REL_ROWS, NA_Q_COLS, nk), F32)],
        ),
        out_shape=jax.ShapeDtypeStruct((NA_HEADS, _NA_CONFIGS, nq, nk), F32),
        compiler_params=pltpu.CompilerParams(dimension_semantics=("parallel",)),
        name="na_bias",
    )(rel_bias.astype(F32).reshape(-1))


def _na_kernel(q_ref, k_ref, v_ref, bias_ref, o_ref, ks_ref, vs_ref):
    n_slabs = q_ref.shape[0]
    nq = NA_Q_ROWS * NA_Q_COLS
    nk = NA_K_ROWS * NA_K_COLS
    shift = NA_WIN_COLS // 2
    kept = GRID_W - 2 * shift
    for sl in range(n_slabs):
        for src, dst in ((k_ref, ks_ref), (v_ref, vs_ref)):
            dst[sl, :, 0:kept, :] = src[sl, :, shift:shift + kept, :]

    def window(refs, sl, block):
        (_, kr0, _), (_, kc0, _) = block
        aligned = kc0 % (2 * shift) == 0
        ref, c = (refs[0], kc0) if aligned else (refs[1], kc0 - shift)
        assert c % (2 * shift) == 0 and (aligned or c + NA_K_COLS <= kept)
        return ref[sl, kr0:kr0 + NA_K_ROWS, c:c + NA_K_COLS, :].reshape(nk, NA_LANES)

    first = lax.broadcasted_iota(jnp.int32, (nq, NA_LANES), 1) < NA_HEAD_DIM
    ones = jnp.ones((nk, NA_LANES), BF16)
    nt = (((1,), (1,)), ((), ()))
    work = [(sl, (rb, cb)) for sl in range(n_slabs) for rb in _NA_ROW_BLOCKS for cb in _NA_COL_BLOCKS]

    def scores(sl, block):
        (r0, _, rcfg), (c0, _, ccfg) = block
        q = q_ref[sl, r0:r0 + NA_Q_ROWS, c0:c0 + NA_Q_COLS, :].reshape(nq, NA_LANES)
        zero = jnp.zeros_like(q)
        q2 = jnp.concatenate([jnp.where(first, q, zero), jnp.where(first, zero, q)], axis=0)
        cfg = rcfg * 3 + ccfg
        bias = jnp.concatenate([bias_ref[2 * sl, cfg], bias_ref[2 * sl + 1, cfg]], axis=0)
        return lax.dot_general(q2, window((k_ref, ks_ref), sl, block), nt,
                               preferred_element_type=F32) + bias

    def attend(sl, block, s):
        m = jnp.max(s, axis=-1, keepdims=True)
        p = jnp.exp2(s - m).astype(BF16)
        o = jnp.dot(p, jnp.concatenate([window((v_ref, vs_ref), sl, block), ones], axis=1),
                    preferred_element_type=F32)
        o = o[:, :NA_LANES] / o[:, NA_LANES:]
        return jnp.where(first, o[:nq], o[nq:])

    s_next = scores(*work[0])
    for i, (sl, blk) in enumerate(work):
        s_cur = s_next
        if i + 1 < len(work):
            s_next = scores(*work[i + 1])
        (r0, _, _), (c0, _, _) = blk
        o_ref[sl, r0:r0 + NA_Q_ROWS, c0:c0 + NA_Q_COLS, :] = (
            attend(sl, blk, s_cur).reshape(NA_Q_ROWS, NA_Q_COLS, NA_LANES).astype(o_ref.dtype))


def _natten(nq, nk, nv, bias_tabs):
    b, rows = nq.shape[1], nq.shape[2]
    per = NA_SLABS_PER_STEP
    slab = pl.BlockSpec((per, None, rows, GRID_W, NA_LANES), lambda j, i: (j, i, 0, 0, 0))
    nqk = NA_Q_ROWS * NA_Q_COLS
    nkk = NA_K_ROWS * NA_K_COLS
    return pl.pallas_call(
        _na_kernel,
        grid=(NA_SLABS // per, b),
        in_specs=[slab, slab, slab,
                  pl.BlockSpec((2 * per, _NA_CONFIGS, nqk, nkk), lambda j, i: (j, 0, 0, 0),
                               pipeline_mode=pl.Buffered(1))],
        out_specs=slab,
        out_shape=jax.ShapeDtypeStruct((NA_SLABS, b, rows, GRID_W, NA_LANES), BF16),
        scratch_shapes=[pltpu.VMEM((per, rows, GRID_W, NA_LANES), BF16),
                        pltpu.VMEM((per, rows, GRID_W, NA_LANES), BF16)],
        compiler_params=pltpu.CompilerParams(
            dimension_semantics=("parallel", "parallel"), vmem_limit_bytes=VMEM_LIMIT),
        name="natten",
    )(nq, nk, nv, bias_tabs)


def _mixout_kernel(yr_ref, na_ref, gr_ref, gn_ref, wr_ref, wn_ref, wo_ref, gain_ref, o_ref):
    n_sub = o_ref.shape[0] // SUB_TILE

    def rows(j):
        return slice(j * SUB_TILE, (j + 1) * SUB_TILE)

    def branches(j):
        def gathered(ref):
            return jnp.concatenate([ref[g, rows(j), :] for g in range(ref.shape[0])], axis=1)

        y_ret = jnp.dot(gathered(yr_ref), wr_ref[...], preferred_element_type=F32)
        y_na = jnp.dot(gathered(na_ref), wn_ref[...], preferred_element_type=F32)
        merged = (jax.nn.sigmoid(gr_ref[rows(j), :].astype(F32)) * y_ret
                  + jax.nn.sigmoid(gn_ref[rows(j), :].astype(F32)) * y_na)
        return merged.astype(BF16)

    def finish(j, merged):
        m = jnp.dot(merged, wo_ref[...], preferred_element_type=F32)
        o_ref[rows(j), :] = _rms(m, gain_ref[...]).astype(o_ref.dtype)

    merged = branches(0)
    for j in range(n_sub):
        merged_next = branches(j + 1) if j + 1 < n_sub else None
        finish(j, merged)
        merged = merged_next


def _mix_out(y_ret, y_na, g_ret, g_na, w_ret, w_na, w_out, gain):
    t = g_ret.shape[0]
    tm = FFN_TILE

    def row(width):
        return pl.BlockSpec((tm, width), lambda i: (i, 0))

    return pl.pallas_call(
        _mixout_kernel,
        grid=(t // tm,),
        in_specs=[pl.BlockSpec((RET_HEADS, tm, RET_V_DIM), lambda i: (0, i, 0)),
                  pl.BlockSpec((NA_SLABS, tm, NA_LANES), lambda i: (0, i, 0)),
                  row(D_MODEL), row(D_MODEL),
                  _const_spec((RET_V_W, D_MODEL)), _const_spec((NA_W, D_MODEL)),
                  _const_spec((D_MODEL, D_MODEL)), _const_spec((1, D_MODEL))],
        out_specs=row(D_MODEL),
        out_shape=jax.ShapeDtypeStruct((t, D_MODEL), BF16),
        compiler_params=pltpu.CompilerParams(
            dimension_semantics=("parallel",), vmem_limit_bytes=VMEM_LIMIT),
        name="mix_out",
    )(y_ret, y_na, g_ret, g_na, w_ret, w_na, w_out, gain)


def kernel(x, ffn1_pre_norm, ffn1_w_in, ffn1_w_out, ffn1_post_norm, mix_pre_norm, w_mix_in,
           ret_decay_fwd, ret_decay_bwd, na_rel_bias, w_ret_out, w_na_out, w_mix_out, mix_post_norm,
           ffn2_pre_norm, ffn2_w_in, ffn2_w_out, ffn2_post_norm):
    b, s, d = x.shape
    assert d == D_MODEL and s % GRID_W == 0 and s % RET_BLOCK == 0 and (b * s) % TOKEN_TILE == 0
    assert s % TOKEN_TILE == 0
    rows = s // GRID_W
    assert rows == NA_ROWS, "the neighbourhood-attention tiling is laid out for a 32 x 64 token grid"
    t = b * s
    xt = x.reshape(t, d)
    for l in range(ffn1_w_in.shape[0]):
        gain = lambda g: g[l].reshape(1, D_MODEL).astype(F32)
        wt = lambda w: w[l].astype(BF16)
        xt, (w_mix, w_ret, w_na, w_out) = _ffn(
            xt, gain(ffn1_pre_norm), wt(ffn1_w_in), wt(ffn1_w_out), gain(ffn1_post_norm),
            riders=(w_mix_in[l], w_ret_out[l], w_na_out[l], w_mix_out[l]))
        (rq, rk, rv, rg, nq, nk, nv, g_ret, g_na), (w_ffn2_in, w_ffn2_out) = _mix_in(
            xt, gain(mix_pre_norm), w_mix, s, riders=(ffn2_w_in[l], ffn2_w_out[l]))
        decay = jnp.stack([ret_decay_fwd[l], ret_decay_bwd[l]]).astype(F32)
        seq = lambda a: a.reshape(a.shape[0], b, s, a.shape[-1])
        y_ret = _retention(decay, seq(rq), seq(rk), seq(rv), seq(rg)).reshape(RET_HEADS, t, RET_V_DIM)
        grid5 = lambda a: a.reshape(NA_SLABS, b, rows, GRID_W, NA_LANES)
        y_na = _natten(grid5(nq), grid5(nk), grid5(nv), _na_bias_tables(na_rel_bias[l]))
        inc = _mix_out(y_ret, y_na.reshape(NA_SLABS, t, NA_LANES), g_ret, g_na,
                       w_ret, w_na, w_out, gain(mix_post_norm))
        xt, _ = _ffn(xt, gain(ffn2_pre_norm), w_ffn2_in, w_ffn2_out, gain(ffn2_post_norm), inc=inc)
    return xt.reshape(b, s, d)
```

```python
import functools

import numpy as np
import jax
import jax.numpy as jnp
from jax import lax
from jax.experimental import pallas as pl
from jax.experimental.pallas import tpu as pltpu

F32 = jnp.float32
BF16 = jnp.bfloat16

D_MODEL = 1024
D_FF = 2816
RMS_EPS = 1e-6
ROPE_BASE = 10000.0
NEG_INF = -1e30
LOG2E = 1.4426950408889634

RET_HEADS = 4
RET_QK_DIM = 128
RET_V_DIM = 256
RET_QK_W = RET_HEADS * RET_QK_DIM
RET_V_W = RET_HEADS * RET_V_DIM
RET_BLOCK = 256

NA_HEADS = 8
NA_HEAD_DIM = 64
NA_W = NA_HEADS * NA_HEAD_DIM
GRID_W = 64
NA_WIN_ROWS = 8
NA_WIN_COLS = 16
NA_REL_ROWS = 2 * NA_WIN_ROWS - 1
NA_REL_COLS = 2 * NA_WIN_COLS - 1
NA_Q_ROWS = 8
NA_Q_COLS = 16
NA_K_ROWS = 16
NA_K_COLS = 32
NA_LANES = 128
NA_SLABS = NA_W // NA_LANES
NA_SLABS_PER_STEP = 4

MIX_SPLITS = (RET_QK_W, RET_QK_W, RET_V_W, RET_V_W, NA_W, NA_W, NA_W, D_MODEL, D_MODEL)
MIX_OFFS = tuple(int(v) for v in np.cumsum((0,) + MIX_SPLITS[:-1]))
MIX_IN_W = sum(MIX_SPLITS)

TOKEN_TILE = 512
FFN_TILE = 1024
SUB_TILE = 256
FF_CHUNK = 256
VMEM_LIMIT = 56 * 1024 * 1024
BF16_SUBLANES = 16


def _rms(x, gain):
    ms = jnp.mean(x * x, axis=-1, keepdims=True)
    return x * lax.rsqrt(ms + RMS_EPS) * gain


def _silu(x):
    return x * jax.nn.sigmoid(x)


def _cast_riders(weights, steps):
    in_specs, out_shapes = [], []
    for w in weights:
        rows, cols = w.shape
        block = min(d for d in range(BF16_SUBLANES, rows + 1, BF16_SUBLANES)
                    if rows % d == 0 and rows // d <= steps)
        last = rows // block - 1
        in_specs.append(pl.BlockSpec((block, cols), lambda i, last=last: (jnp.minimum(i, last), 0)))
        out_shapes.append(jax.ShapeDtypeStruct(w.shape, BF16))
    return in_specs, out_shapes


def _run_cast_riders(src_refs, dst_refs):
    for src, dst in zip(src_refs, dst_refs):
        dst[...] = src[...].astype(dst.dtype)


def _const_spec(shape):
    nd = len(shape)
    return pl.BlockSpec(shape, lambda *_: (0,) * nd, pipeline_mode=pl.Buffered(1))


def _ffn_kernel(n_riders, x_ref, gpre_ref, win_ref, wout_ref, gpost_ref, *rest):
    o_ref, act_ref = rest[n_riders], rest[-1]
    _run_cast_riders(rest[:n_riders], rest[n_riders + 1:-1])
    n_sub = x_ref.shape[0] // SUB_TILE

    def rows(j):
        return slice(j * SUB_TILE, (j + 1) * SUB_TILE)

    def pre(j):
        return _rms(x_ref[rows(j), :], gpre_ref[...]).astype(BF16)

    def up(j, xn):
        for c in range(D_FF // FF_CHUNK):
            lo = c * FF_CHUNK
            g = jnp.dot(xn, win_ref[:, lo:lo + FF_CHUNK], preferred_element_type=F32)
            u = jnp.dot(xn, win_ref[:, D_FF + lo:D_FF + lo + FF_CHUNK], preferred_element_type=F32)
            act_ref[rows(j), lo:lo + FF_CHUNK] = (_silu(g) * u).astype(BF16)

    def down(j):
        h = jnp.dot(act_ref[rows(j), :], wout_ref[...], preferred_element_type=F32)
        o_ref[rows(j), :] = x_ref[rows(j), :] + 0.5 * _rms(h, gpost_ref[...])

    xn = pre(0)
    for j in range(n_sub):
        xn_next = pre(j + 1) if j + 1 < n_sub else None
        up(j, xn)
        if j >= 1:
            down(j - 1)
        xn = xn_next
    down(n_sub - 1)


def _ffn(x, gpre, w_in, w_out, gpost, riders=()):
    t = x.shape[0]
    tm = FFN_TILE
    row = pl.BlockSpec((tm, D_MODEL), lambda i: (i, 0))
    rider_specs, rider_shapes = _cast_riders(riders, t // tm)
    out, *cast = pl.pallas_call(
        functools.partial(_ffn_kernel, len(riders)),
        grid=(t // tm,),
        in_specs=[row, _const_spec((1, D_MODEL)), _const_spec((D_MODEL, 2 * D_FF)),
                  _const_spec((D_FF, D_MODEL)), _const_spec((1, D_MODEL))] + rider_specs,
        out_specs=[row] + rider_specs,
        out_shape=[jax.ShapeDtypeStruct((t, D_MODEL), F32)] + rider_shapes,
        scratch_shapes=[pltpu.VMEM((tm, D_FF), BF16)],
        compiler_params=pltpu.CompilerParams(
            dimension_semantics=("arbitrary",), vmem_limit_bytes=VMEM_LIMIT),
        name="ffn",
    )(x, gpre, w_in, w_out, gpost, *riders)
    return out, cast


def _mixin_kernel(n_riders, x_ref, x_next_ref, gain_ref, w_ref, rope_ref, *rest):
    rq_ref, rk_ref, rv_ref, rg_ref, nq_ref, nk_ref, nv_ref, gr_ref, gn_ref = rest[n_riders:n_riders + 9]
    carry_ref = rest[-1]
    _run_cast_riders(rest[:n_riders], rest[n_riders + 9:-1])
    n_sub = x_ref.shape[0] // SUB_TILE
    step = pl.program_id(0)

    @pl.when(step == 0)
    def _first_norm():
        carry_ref[0] = _rms(x_ref[0:SUB_TILE, :], gain_ref[...]).astype(BF16)

    def pre(j):
        rows = slice(j * SUB_TILE, (j + 1) * SUB_TILE)
        return _rms(x_ref[rows, :], gain_ref[...]).astype(BF16)

    def project(j, u):
        rows = slice(j * SUB_TILE, (j + 1) * SUB_TILE)

        def proj(idx):
            lo = MIX_OFFS[idx]
            return jnp.dot(u, w_ref[:, lo:lo + MIX_SPLITS[idx]], preferred_element_type=F32)

        def store_groups(out_ref, y):
            width = out_ref.shape[-1]
            for g in range(out_ref.shape[0]):
                out_ref[g, rows, :] = y[:, g * width:(g + 1) * width].astype(out_ref.dtype)

        def rotary(y, cos, sin, out_ref):
            for h in range(RET_HEADS):
                yh = y[:, h * RET_QK_DIM:(h + 1) * RET_QK_DIM]
                out_ref[h, rows, :] = (yh * cos + pltpu.roll(yh, RET_QK_DIM // 2, 1) * sin).astype(out_ref.dtype)

        rotary(proj(0), rope_ref[0, rows, :], rope_ref[1, rows, :], rq_ref)
        rotary(proj(1), rope_ref[2, rows, :], rope_ref[3, rows, :], rk_ref)
        store_groups(rv_ref, proj(2))
        store_groups(rg_ref, _silu(proj(3)))
        store_groups(nq_ref, proj(4) * (LOG2E * NA_HEAD_DIM ** -0.5))
        store_groups(nk_ref, proj(5))
        store_groups(nv_ref, proj(6))
        gr_ref[rows, :] = proj(7).astype(gr_ref.dtype)
        gn_ref[rows, :] = proj(8).astype(gn_ref.dtype)

    u = carry_ref[step % 2]
    carry_ref[(step + 1) % 2] = _rms(x_next_ref[...], gain_ref[...]).astype(BF16)
    for j in range(n_sub):
        u_next = pre(j + 1) if j + 1 < n_sub else None
        project(j, u)
        u = u_next


def _rope_tables(seq):
    half = RET_QK_DIM // 2
    pos = jnp.arange(seq, dtype=F32)
    inv = 1.0 / (ROPE_BASE ** jnp.linspace(0.0, 1.0, half, dtype=F32))
    ang = pos[:, None] * inv[None, :]
    cos, sin = jnp.cos(ang), jnp.sin(ang)
    cos2 = jnp.concatenate([cos, cos], axis=-1)
    sin2 = jnp.concatenate([-sin, sin], axis=-1)
    ks = RET_QK_DIM ** -0.5
    return jnp.stack([cos2, sin2, cos2 * ks, sin2 * ks])


def _mix_in(x, gain, w, seq, riders=()):
    t = x.shape[0]
    tm = TOKEN_TILE
    tiles_per_seq = seq // tm
    out_groups = (RET_HEADS,) * 4 + (NA_SLABS,) * 3 + (1, 1)
    steps = t // tm
    rider_specs, rider_shapes = _cast_riders(riders, steps)
    outs = pl.pallas_call(
        functools.partial(_mixin_kernel, len(riders)),
        grid=(steps,),
        in_specs=[pl.BlockSpec((tm, D_MODEL), lambda i: (i, 0)),
                  pl.BlockSpec((SUB_TILE, D_MODEL),
                               lambda i: (jnp.minimum(i + 1, steps - 1) * (tm // SUB_TILE), 0)),
                  _const_spec((1, D_MODEL)), _const_spec((D_MODEL, MIX_IN_W)),
                  pl.BlockSpec((4, tm, RET_QK_DIM), lambda i: (0, i % tiles_per_seq, 0))] + rider_specs,
        out_specs=[pl.BlockSpec((tm, wd), lambda i: (i, 0)) if g == 1
                   else pl.BlockSpec((g, tm, wd // g), lambda i: (0, i, 0))
                   for wd, g in zip(MIX_SPLITS, out_groups)] + rider_specs,
        out_shape=[jax.ShapeDtypeStruct((t, wd) if g == 1 else (g, t, wd // g), BF16)
                   for wd, g in zip(MIX_SPLITS, out_groups)] + rider_shapes,
        scratch_shapes=[pltpu.VMEM((2, SUB_TILE, D_MODEL), BF16)],
        compiler_params=pltpu.CompilerParams(
            dimension_semantics=("arbitrary",), vmem_limit_bytes=VMEM_LIMIT),
        name="mix_in",
    )(x, x, gain, w, _rope_tables(seq), *riders)
    return outs[:len(MIX_SPLITS)], outs[len(MIX_SPLITS):]


def _log_sigmoid(z):
    return jnp.minimum(z, 0.0) - jnp.log1p(jnp.exp(-jnp.abs(z)))


def _ret_kernel(dec_ref, q_ref, k_ref, v_ref, g_ref, o_ref, u_ref, st_ref, dm_ref, vec_ref, blk_ref):
    c = RET_BLOCK
    dk = RET_QK_DIM
    dv = RET_V_DIM
    n_heads = q_ref.shape[0]
    n_blocks = q_ref.shape[1] // c
    tn = (((0,), (0,)), ((), ()))
    nt = (((1,), (1,)), ((), ()))

    @pl.when(pl.program_id(0) == 0)
    def _decay_tables():
        def log_gamma(z, shape):
            return _log_sigmoid(jnp.full(shape, z, F32))

        ri = lax.broadcasted_iota(jnp.int32, (c, c), 0).astype(F32)
        ci = lax.broadcasted_iota(jnp.int32, (c, c), 1).astype(F32)
        diff = ri - ci
        r = lax.broadcasted_iota(jnp.int32, (c, dk), 0).astype(F32)
        for h in range(n_heads):
            zf = dec_ref[0, h]
            zb = dec_ref[1, h]
            dm_ref[h] = jnp.where(diff >= 0, jnp.exp(log_gamma(zf, (c, c)) * jnp.maximum(diff, 0.0)),
                                  jnp.exp(log_gamma(zb, (c, c)) * jnp.maximum(-diff, 0.0)))
            lgf = log_gamma(zf, (c, dk))
            lgb = log_gamma(zb, (c, dk))
            vec_ref[h, 0] = jnp.exp(lgf * (r + 1.0)).astype(BF16)
            vec_ref[h, 1] = jnp.exp(lgb * (c - r)).astype(BF16)
            vec_ref[h, 2] = jnp.exp(lgf * (c - 1.0 - r)).astype(BF16)
            vec_ref[h, 3] = jnp.exp(lgb * r).astype(BF16)
            blk_ref[h, 0] = jnp.exp(log_gamma(zf, (dk, dv)) * c)
            blk_ref[h, 1] = jnp.exp(log_gamma(zb, (dk, dv)) * c)

    def rows(n):
        return slice(n * c, (n + 1) * c)

    def updates(h):
        for n in range(n_blocks):
            k = k_ref[h, rows(n), :]
            k_cat = jnp.concatenate([k * vec_ref[h, 2], k * vec_ref[h, 3]], axis=1)
            u_ref[h % 2, n] = lax.dot_general(k_cat, v_ref[h, rows(n), :], tn, preferred_element_type=F32)

    def scans(h):
        sf = jnp.zeros((dk, dv), F32)
        for n in range(n_blocks):
            st_ref[h % 2, n, 0:dk, :] = sf.astype(BF16)
            if n + 1 < n_blocks:
                sf = blk_ref[h, 0] * sf + u_ref[h % 2, n, 0:dk, :]
        sb = jnp.zeros((dk, dv), F32)
        for n in reversed(range(n_blocks)):
            st_ref[h % 2, n, dk:2 * dk, :] = sb.astype(BF16)
            if n > 0:
                sb = blk_ref[h, 1] * sb + u_ref[h % 2, n, dk:2 * dk, :]

    def outputs(h):
        def scores(n):
            return lax.dot_general(q_ref[h, rows(n), :], k_ref[h, rows(n), :], nt,
                                   preferred_element_type=F32)

        s_next = scores(0)
        for n in range(n_blocks):
            s_cur = s_next
            if n + 1 < n_blocks:
                s_next = scores(n + 1)
            p = (s_cur * dm_ref[h]).astype(BF16)
            q = q_ref[h, rows(n), :]
            lhs = jnp.concatenate([p, q * vec_ref[h, 0], q * vec_ref[h, 1]], axis=1)
            rhs = jnp.concatenate([v_ref[h, rows(n), :], st_ref[h % 2, n]], axis=0)
            y = jnp.dot(lhs, rhs, preferred_element_type=F32)
            yn = y * lax.rsqrt(jnp.mean(y * y, axis=-1, keepdims=True) + RMS_EPS)
            o_ref[h, rows(n), :] = (g_ref[h, rows(n), :].astype(F32) * yn).astype(o_ref.dtype)

    updates(0)
    for h in range(n_heads):
        if h + 1 < n_heads:
            updates(h + 1)
        scans(h)
        outputs(h)


def _retention(decay_logits, rq, rk, rv, rg):
    n_heads, b, s, _ = rq.shape
    c = RET_BLOCK
    n_blocks = s // c
    qk_spec = pl.BlockSpec((n_heads, None, s, RET_QK_DIM), lambda i, dec: (0, i, 0, 0))
    v_spec = pl.BlockSpec((n_heads, None, s, RET_V_DIM), lambda i, dec: (0, i, 0, 0))
    return pl.pallas_call(
        _ret_kernel,
        grid_spec=pltpu.PrefetchScalarGridSpec(
            num_scalar_prefetch=1,
            grid=(b,),
            in_specs=[qk_spec, qk_spec, v_spec, v_spec],
            out_specs=v_spec,
            scratch_shapes=[pltpu.VMEM((2, n_blocks, 2 * RET_QK_DIM, RET_V_DIM), F32),
                            pltpu.VMEM((2, n_blocks, 2 * RET_QK_DIM, RET_V_DIM), BF16),
                            pltpu.VMEM((n_heads, c, c), F32),
                            pltpu.VMEM((n_heads, 4, c, RET_QK_DIM), BF16),
                            pltpu.VMEM((n_heads, 2, RET_QK_DIM, RET_V_DIM), F32)],
        ),
        out_shape=jax.ShapeDtypeStruct((n_heads, b, s, RET_V_DIM), BF16),
        compiler_params=pltpu.CompilerParams(
            dimension_semantics=("arbitrary",), vmem_limit_bytes=VMEM_LIMIT),
        name="retention",
    )(decay_logits, rq, rk, rv, rg)


def _na_blocks(extent, q_size, k_size, window):
    n = extent // q_size
    out = []
    for i in range(n):
        q0 = i * q_size
        k0 = min(max(q0 - window // 2, 0), extent - k_size)
        out.append((q0, k0, 0 if i == 0 else (2 if i == n - 1 else 1)))
    return tuple(out)


NA_ROWS = 32
_NA_ROW_BLOCKS = _na_blocks(NA_ROWS, NA_Q_ROWS, NA_K_ROWS, NA_WIN_ROWS)
_NA_COL_BLOCKS = _na_blocks(GRID_W, NA_Q_COLS, NA_K_COLS, NA_WIN_COLS)
_NA_CONFIGS = 9


def _na_bias_kernel(rel_ref, out_ref, col_ref):
    nk = NA_K_ROWS * NA_K_COLS
    base = pl.program_id(0) * (NA_REL_ROWS * NA_REL_COLS)
    lane = lax.broadcasted_iota(jnp.int32, (NA_Q_COLS, nk), 1)
    kr = lane // NA_K_COLS
    col_cfgs = (_NA_COL_BLOCKS[0], _NA_COL_BLOCKS[1], _NA_COL_BLOCKS[3])
    row_cfgs = (_NA_ROW_BLOCKS[0], _NA_ROW_BLOCKS[1], _NA_ROW_BLOCKS[3])
    reps = nk // NA_LANES
    kc_one = lax.broadcasted_iota(jnp.int32, (NA_Q_COLS, NA_LANES), 1) % NA_K_COLS
    qc_one = lax.broadcasted_iota(jnp.int32, (NA_Q_COLS, NA_LANES), 0)
    for b, (c0, kc0, _) in enumerate(col_cfgs):
        kc_abs = kc_one + kc0
        qc_abs = qc_one + c0
        rel_col = jnp.clip(kc_abs - qc_abs + NA_WIN_COLS - 1, 0, NA_REL_COLS - 1)
        win = jnp.clip(qc_abs - NA_WIN_COLS // 2, 0, GRID_W - NA_WIN_COLS)
        in_win = (kc_abs >= win) & (kc_abs < win + NA_WIN_COLS)

        def rel_row_body(dr, carry, rel_col=rel_col, in_win=in_win, b=b):
            vals = jnp.zeros((NA_Q_COLS, NA_LANES), F32)
            for dc in range(NA_REL_COLS):
                vals = jnp.where(rel_col == dc, rel_ref[base + dr * NA_REL_COLS + dc], vals)
            vals = jnp.where(in_win, vals * LOG2E, NEG_INF)
            col_ref[b * NA_REL_ROWS + dr] = jnp.concatenate([vals] * reps, axis=1)
            return carry

        lax.fori_loop(0, NA_REL_ROWS, rel_row_body, 0)

    for a, (r0, ks, _) in enumerate(row_cfgs):
        kr_abs = kr + ks
        for qr in range(NA_Q_ROWS):
            r_abs = r0 + qr
            rs = min(max(r_abs - NA_WIN_ROWS // 2, 0), NA_ROWS - NA_WIN_ROWS)
            hits = [kr_abs == rs + j for j in range(NA_WIN_ROWS)]
            for b in range(3):
                tile = jnp.full((NA_Q_COLS, nk), NEG_INF, F32)
                for j in range(NA_WIN_ROWS):
                    dr = rs + j - r_abs + NA_WIN_ROWS - 1
                    tile = jnp.where(hits[j], col_ref[b * NA_REL_ROWS + dr], tile)
                out_ref[0, a * 3 + b, qr * NA_Q_COLS:(qr + 1) * NA_Q_COLS, :] = tile


def _na_bias_tables(rel_bias):
    nq = NA_Q_ROWS * NA_Q_COLS
    nk = NA_K_ROWS * NA_K_COLS
    return pl.pallas_call(
        _na_bias_kernel,
        grid_spec=pltpu.PrefetchScalarGridSpec(
            num_scalar_prefetch=1,
            grid=(NA_HEADS,),
            in_specs=[],
            out_specs=pl.BlockSpec((1, _NA_CONFIGS, nq, nk), lambda h, rel: (h, 0, 0, 0)),
            scratch_shapes=[pltpu.VMEM((3 * NA_REL_ROWS, NA_Q_COLS, nk), F32)],
        ),
        out_shape=jax.ShapeDtypeStruct((NA_HEADS, _NA_CONFIGS, nq, nk), F32),
        compiler_params=pltpu.CompilerParams(dimension_semantics=("parallel",)),
        name="na_bias",
    )(rel_bias.astype(F32).reshape(-1))


def _na_kernel(q_ref, k_ref, v_ref, bias_ref, o_ref, ks_ref, vs_ref):
    n_slabs = q_ref.shape[0]
    nq = NA_Q_ROWS * NA_Q_COLS
    nk = NA_K_ROWS * NA_K_COLS
    shift = NA_WIN_COLS // 2
    kept = GRID_W - 2 * shift
    for sl in range(n_slabs):
        for src, dst in ((k_ref, ks_ref), (v_ref, vs_ref)):
            dst[sl, :, 0:kept, :] = src[sl, :, shift:shift + kept, :]

    def window(refs, sl, block):
        (_, kr0, _), (_, kc0, _) = block
        aligned = kc0 % (2 * shift) == 0
        ref, c = (refs[0], kc0) if aligned else (refs[1], kc0 - shift)
        assert c % (2 * shift) == 0 and (aligned or c + NA_K_COLS <= kept)
        return ref[sl, kr0:kr0 + NA_K_ROWS, c:c + NA_K_COLS, :].reshape(nk, NA_LANES)

    first = lax.broadcasted_iota(jnp.int32, (nq, NA_LANES), 1) < NA_HEAD_DIM
    ones = jnp.ones((nk, NA_LANES), BF16)
    nt = (((1,), (1,)), ((), ()))
    work = [(sl, (rb, cb)) for sl in range(n_slabs) for rb in _NA_ROW_BLOCKS for cb in _NA_COL_BLOCKS]

    def scores(sl, block):
        (r0, _, rcfg), (c0, _, ccfg) = block
        q = q_ref[sl, r0:r0 + NA_Q_ROWS, c0:c0 + NA_Q_COLS, :].reshape(nq, NA_LANES)
        zero = jnp.zeros_like(q)
        q2 = jnp.concatenate([jnp.where(first, q, zero), jnp.where(first, zero, q)], axis=0)
        cfg = rcfg * 3 + ccfg
        bias = jnp.concatenate([bias_ref[2 * sl, cfg], bias_ref[2 * sl + 1, cfg]], axis=0)
        return lax.dot_general(q2, window((k_ref, ks_ref), sl, block), nt,
                               preferred_element_type=F32) + bias

    def attend(sl, block, s):
        m = jnp.max(s, axis=-1, keepdims=True)
        p = jnp.exp2(s - m).astype(BF16)
        o = jnp.dot(p, jnp.concatenate([window((v_ref, vs_ref), sl, block), ones], axis=1),
                    preferred_element_type=F32)
        o = o[:, :NA_LANES] / o[:, NA_LANES:]
        return jnp.where(first, o[:nq], o[nq:])

    s_next = scores(*work[0])
    for i, (sl, blk) in enumerate(work):
        s_cur = s_next
        if i + 1 < len(work):
            s_next = scores(*work[i + 1])
        (r0, _, _), (c0, _, _) = blk
        o_ref[sl, r0:r0 + NA_Q_ROWS, c0:c0 + NA_Q_COLS, :] = (
            attend(sl, blk, s_cur).reshape(NA_Q_ROWS, NA_Q_COLS, NA_LANES).astype(o_ref.dtype))


def _natten(nq, nk, nv, bias_tabs):
    b, rows = nq.shape[1], nq.shape[2]
    per = NA_SLABS_PER_STEP
    slab = pl.BlockSpec((per, None, rows, GRID_W, NA_LANES), lambda j, i: (j, i, 0, 0, 0))
    nqk = NA_Q_ROWS * NA_Q_COLS
    nkk = NA_K_ROWS * NA_K_COLS
    return pl.pallas_call(
        _na_kernel,
        grid=(NA_SLABS // per, b),
        in_specs=[slab, slab, slab,
                  pl.BlockSpec((2 * per, _NA_CONFIGS, nqk, nkk), lambda j, i: (j, 0, 0, 0),
                               pipeline_mode=pl.Buffered(1))],
        out_specs=slab,
        out_shape=jax.ShapeDtypeStruct((NA_SLABS, b, rows, GRID_W, NA_LANES), BF16),
        scratch_shapes=[pltpu.VMEM((per, rows, GRID_W, NA_LANES), BF16),
                        pltpu.VMEM((per, rows, GRID_W, NA_LANES), BF16)],
        compiler_params=pltpu.CompilerParams(
            dimension_semantics=("parallel", "parallel"), vmem_limit_bytes=VMEM_LIMIT),
        name="natten",
    )(nq, nk, nv, bias_tabs)


def _mixout_kernel(x_ref, yr_ref, na_ref, gr_ref, gn_ref, wr_ref, wn_ref, wo_ref, gain_ref, o_ref):
    n_sub = x_ref.shape[0] // SUB_TILE

    def rows(j):
        return slice(j * SUB_TILE, (j + 1) * SUB_TILE)

    def branches(j):
        def gathered(ref):
            return jnp.concatenate([ref[g, rows(j), :] for g in range(ref.shape[0])], axis=1)

        y_ret = jnp.dot(gathered(yr_ref), wr_ref[...], preferred_element_type=F32)
        y_na = jnp.dot(gathered(na_ref), wn_ref[...], preferred_element_type=F32)
        merged = (jax.nn.sigmoid(gr_ref[rows(j), :].astype(F32)) * y_ret
                  + jax.nn.sigmoid(gn_ref[rows(j), :].astype(F32)) * y_na)
        return merged.astype(BF16)

    def finish(j, merged):
        m = jnp.dot(merged, wo_ref[...], preferred_element_type=F32)
        o_ref[rows(j), :] = x_ref[rows(j), :] + _rms(m, gain_ref[...])

    merged = branches(0)
    for j in range(n_sub):
        merged_next = branches(j + 1) if j + 1 < n_sub else None
        finish(j, merged)
        merged = merged_next


def _mix_out(x, y_ret, y_na, g_ret, g_na, w_ret, w_na, w_out, gain):
    t = x.shape[0]
    tm = FFN_TILE

    def row(width):
        return pl.BlockSpec((tm, width), lambda i: (i, 0))

    return pl.pallas_call(
        _mixout_kernel,
        grid=(t // tm,),
        in_specs=[row(D_MODEL),
                  pl.BlockSpec((RET_HEADS, tm, RET_V_DIM), lambda i: (0, i, 0)),
                  pl.BlockSpec((NA_SLABS, tm, NA_LANES), lambda i: (0, i, 0)),
                  row(D_MODEL), row(D_MODEL),
                  _const_spec((RET_V_W, D_MODEL)), _const_spec((NA_W, D_MODEL)),
                  _const_spec((D_MODEL, D_MODEL)), _const_spec((1, D_MODEL))],
        out_specs=row(D_MODEL),
        out_shape=jax.ShapeDtypeStruct((t, D_MODEL), F32),
        compiler_params=pltpu.CompilerParams(
            dimension_semantics=("parallel",), vmem_limit_bytes=VMEM_LIMIT),
        name="mix_out",
    )(x, y_ret, y_na, g_ret, g_na, w_ret, w_na, w_out, gain)


def kernel(x, ffn1_pre_norm, ffn1_w_in, ffn1_w_out, ffn1_post_norm, mix_pre_norm, w_mix_in,
           ret_decay_fwd, ret_decay_bwd, na_rel_bias, w_ret_out, w_na_out, w_mix_out, mix_post_norm,
           ffn2_pre_norm, ffn2_w_in, ffn2_w_out, ffn2_post_norm):
    b, s, d = x.shape
    assert d == D_MODEL and s % GRID_W == 0 and s % RET_BLOCK == 0 and (b * s) % TOKEN_TILE == 0
    assert s % TOKEN_TILE == 0
    rows = s // GRID_W
    assert rows == NA_ROWS, "the neighbourhood-attention tiling is laid out for a 32 x 64 token grid"
    t = b * s
    xt = x.reshape(t, d)
    for l in range(ffn1_w_in.shape[0]):
        gain = lambda g: g[l].reshape(1, D_MODEL).astype(F32)
        wt = lambda w: w[l].astype(BF16)
        xt, (w_mix, w_ret, w_na, w_out) = _ffn(
            xt, gain(ffn1_pre_norm), wt(ffn1_w_in), wt(ffn1_w_out), gain(ffn1_post_norm),
            riders=(w_mix_in[l], w_ret_out[l], w_na_out[l], w_mix_out[l]))
        (rq, rk, rv, rg, nq, nk, nv, g_ret, g_na), (w_ffn2_in, w_ffn2_out) = _mix_in(
            xt, gain(mix_pre_norm), w_mix, s, riders=(ffn2_w_in[l], ffn2_w_out[l]))
        decay = jnp.stack([ret_decay_fwd[l], ret_decay_bwd[l]]).astype(F32)
        seq = lambda a: a.reshape(a.shape[0], b, s, a.shape[-1])
        y_ret = _retention(decay, seq(rq), seq(rk), seq(rv), seq(rg)).reshape(RET_HEADS, t, RET_V_DIM)
        grid5 = lambda a: a.reshape(NA_SLABS, b, rows, GRID_W, NA_LANES)
        y_na = _natten(grid5(nq), grid5(nk), grid5(nv), _na_bias_tables(na_rel_bias[l]))
        xt = _mix_out(xt, y_ret, y_na.reshape(NA_SLABS, t, NA_LANES), g_ret, g_na,
                      w_ret, w_na, w_out, gain(mix_post_norm))
        xt, _ = _ffn(xt, gain(ffn2_pre_norm), w_ffn2_in, w_ffn2_out, gain(ffn2_post_norm))
    return xt.reshape(b, s, d)
```

```python
import functools

import numpy as np
import jax
import jax.numpy as jnp
from jax import lax
from jax.experimental import pallas as pl
from jax.experimental.pallas import tpu as pltpu

F32 = jnp.float32
BF16 = jnp.bfloat16

D_MODEL = 1024
D_FF = 2816
RMS_EPS = 1e-6
ROPE_BASE = 10000.0
NEG_INF = -1e30
LOG2E = 1.4426950408889634

RET_HEADS = 4
RET_QK_DIM = 128
RET_V_DIM = 256
RET_QK_W = RET_HEADS * RET_QK_DIM
RET_V_W = RET_HEADS * RET_V_DIM
RET_BLOCK = 256

NA_HEADS = 8
NA_HEAD_DIM = 64
NA_W = NA_HEADS * NA_HEAD_DIM
GRID_W = 64
NA_WIN_ROWS = 8
NA_WIN_COLS = 16
NA_REL_ROWS = 2 * NA_WIN_ROWS - 1
NA_REL_COLS = 2 * NA_WIN_COLS - 1
NA_Q_ROWS = 8
NA_Q_COLS = 16
NA_K_ROWS = 16
NA_K_COLS = 32
NA_LANES = 128
NA_SLABS = NA_W // NA_LANES
NA_SLABS_PER_STEP = 4

MIX_SPLITS = (RET_QK_W, RET_QK_W, RET_V_W, RET_V_W, NA_W, NA_W, NA_W, D_MODEL, D_MODEL)
MIX_OFFS = tuple(int(v) for v in np.cumsum((0,) + MIX_SPLITS[:-1]))
MIX_IN_W = sum(MIX_SPLITS)

TOKEN_TILE = 512
FFN_TILE = 1024
SUB_TILE = 256
FF_CHUNK = 256
VMEM_LIMIT = 56 * 1024 * 1024
BF16_SUBLANES = 16


def _rms(x, gain):
    ms = jnp.mean(x * x, axis=-1, keepdims=True)
    return x * lax.rsqrt(ms + RMS_EPS) * gain


def _silu(x):
    return x * jax.nn.sigmoid(x)


def _cast_riders(weights, steps):
    in_specs, out_shapes = [], []
    for w in weights:
        rows, cols = w.shape
        block = min(d for d in range(BF16_SUBLANES, rows + 1, BF16_SUBLANES)
                    if rows % d == 0 and rows // d <= steps)
        last = rows // block - 1
        in_specs.append(pl.BlockSpec((block, cols), lambda i, last=last: (jnp.minimum(i, last), 0)))
        out_shapes.append(jax.ShapeDtypeStruct(w.shape, BF16))
    return in_specs, out_shapes


def _run_cast_riders(src_refs, dst_refs):
    for src, dst in zip(src_refs, dst_refs):
        dst[...] = src[...].astype(dst.dtype)


def _const_spec(shape):
    nd = len(shape)
    return pl.BlockSpec(shape, lambda *_: (0,) * nd, pipeline_mode=pl.Buffered(1))


def _ffn_kernel(n_riders, x_ref, gpre_ref, win_ref, wout_ref, gpost_ref, *rest):
    o_ref, act_ref = rest[n_riders], rest[-1]
    _run_cast_riders(rest[:n_riders], rest[n_riders + 1:-1])
    n_sub = x_ref.shape[0] // SUB_TILE

    def rows(j):
        return slice(j * SUB_TILE, (j + 1) * SUB_TILE)

    def pre(j):
        return _rms(x_ref[rows(j), :], gpre_ref[...]).astype(BF16)

    def up(j, xn):
        for c in range(D_FF // FF_CHUNK):
            lo = c * FF_CHUNK
            g = jnp.dot(xn, win_ref[:, lo:lo + FF_CHUNK], preferred_element_type=F32)
            u = jnp.dot(xn, win_ref[:, D_FF + lo:D_FF + lo + FF_CHUNK], preferred_element_type=F32)
            act_ref[rows(j), lo:lo + FF_CHUNK] = (_silu(g) * u).astype(BF16)

    def down(j):
        h = jnp.dot(act_ref[rows(j), :], wout_ref[...], preferred_element_type=F32)
        o_ref[rows(j), :] = x_ref[rows(j), :] + 0.5 * _rms(h, gpost_ref[...])

    xn = pre(0)
    for j in range(n_sub):
        xn_next = pre(j + 1) if j + 1 < n_sub else None
        up(j, xn)
        if j >= 1:
            down(j - 1)
        xn = xn_next
    down(n_sub - 1)


def _ffn(x, gpre, w_in, w_out, gpost, riders=()):
    t = x.shape[0]
    tm = FFN_TILE
    row = pl.BlockSpec((tm, D_MODEL), lambda i: (i, 0))
    rider_specs, rider_shapes = _cast_riders(riders, t // tm)
    out, *cast = pl.pallas_call(
        functools.partial(_ffn_kernel, len(riders)),
        grid=(t // tm,),
        in_specs=[row, _const_spec((1, D_MODEL)), _const_spec((D_MODEL, 2 * D_FF)),
                  _const_spec((D_FF, D_MODEL)), _const_spec((1, D_MODEL))] + rider_specs,
        out_specs=[row] + rider_specs,
        out_shape=[jax.ShapeDtypeStruct((t, D_MODEL), F32)] + rider_shapes,
        scratch_shapes=[pltpu.VMEM((tm, D_FF), BF16)],
        compiler_params=pltpu.CompilerParams(
            dimension_semantics=("arbitrary",), vmem_limit_bytes=VMEM_LIMIT),
        name="ffn",
    )(x, gpre, w_in, w_out, gpost, *riders)
    return out, cast


def _mixin_kernel(n_riders, x_ref, gain_ref, w_ref, rope_ref, *rest):
    rq_ref, rk_ref, rv_ref, rg_ref, nq_ref, nk_ref, nv_ref, gr_ref, gn_ref = rest[n_riders:n_riders + 9]
    _run_cast_riders(rest[:n_riders], rest[n_riders + 9:])
    n_sub = x_ref.shape[0] // SUB_TILE

    def pre(j):
        rows = slice(j * SUB_TILE, (j + 1) * SUB_TILE)
        return _rms(x_ref[rows, :], gain_ref[...]).astype(BF16)

    def project(j, u):
        rows = slice(j * SUB_TILE, (j + 1) * SUB_TILE)

        def proj(idx):
            lo = MIX_OFFS[idx]
            return jnp.dot(u, w_ref[:, lo:lo + MIX_SPLITS[idx]], preferred_element_type=F32)

        def store_groups(out_ref, y):
            width = out_ref.shape[-1]
            for g in range(out_ref.shape[0]):
                out_ref[g, rows, :] = y[:, g * width:(g + 1) * width].astype(out_ref.dtype)

        def rotary(y, cos, sin, out_ref):
            for h in range(RET_HEADS):
                yh = y[:, h * RET_QK_DIM:(h + 1) * RET_QK_DIM]
                out_ref[h, rows, :] = (yh * cos + pltpu.roll(yh, RET_QK_DIM // 2, 1) * sin).astype(out_ref.dtype)

        rotary(proj(0), rope_ref[0, rows, :], rope_ref[1, rows, :], rq_ref)
        rotary(proj(1), rope_ref[2, rows, :], rope_ref[3, rows, :], rk_ref)
        store_groups(rv_ref, proj(2))
        store_groups(rg_ref, _silu(proj(3)))
        store_groups(nq_ref, proj(4) * (LOG2E * NA_HEAD_DIM ** -0.5))
        store_groups(nk_ref, proj(5))
        store_groups(nv_ref, proj(6))
        gr_ref[rows, :] = proj(7).astype(gr_ref.dtype)
        gn_ref[rows, :] = proj(8).astype(gn_ref.dtype)

    u = pre(0)
    for j in range(n_sub):
        u_next = pre(j + 1) if j + 1 < n_sub else None
        project(j, u)
        u = u_next


def _rope_tables(seq):
    half = RET_QK_DIM // 2
    pos = jnp.arange(seq, dtype=F32)
    inv = 1.0 / (ROPE_BASE ** jnp.linspace(0.0, 1.0, half, dtype=F32))
    ang = pos[:, None] * inv[None, :]
    cos, sin = jnp.cos(ang), jnp.sin(ang)
    cos2 = jnp.concatenate([cos, cos], axis=-1)
    sin2 = jnp.concatenate([-sin, sin], axis=-1)
    ks = RET_QK_DIM ** -0.5
    return jnp.stack([cos2, sin2, cos2 * ks, sin2 * ks])


def _mix_in(x, gain, w, seq, riders=()):
    t = x.shape[0]
    tm = TOKEN_TILE
    tiles_per_seq = seq // tm
    out_groups = (RET_HEADS,) * 4 + (NA_SLABS,) * 3 + (1, 1)
    rider_specs, rider_shapes = _cast_riders(riders, t // tm)
    outs = pl.pallas_call(
        functools.partial(_mixin_kernel, len(riders)),
        grid=(t // tm,),
        in_specs=[pl.BlockSpec((tm, D_MODEL), lambda i: (i, 0)),
                  _const_spec((1, D_MODEL)), _const_spec((D_MODEL, MIX_IN_W)),
                  pl.BlockSpec((4, tm, RET_QK_DIM), lambda i: (0, i % tiles_per_seq, 0))] + rider_specs,
        out_specs=[pl.BlockSpec((tm, wd), lambda i: (i, 0)) if g == 1
                   else pl.BlockSpec((g, tm, wd // g), lambda i: (0, i, 0))
                   for wd, g in zip(MIX_SPLITS, out_groups)] + rider_specs,
        out_shape=[jax.ShapeDtypeStruct((t, wd) if g == 1 else (g, t, wd // g), BF16)
                   for wd, g in zip(MIX_SPLITS, out_groups)] + rider_shapes,
        compiler_params=pltpu.CompilerParams(
            dimension_semantics=("arbitrary",), vmem_limit_bytes=VMEM_LIMIT),
        name="mix_in",
    )(x, gain, w, _rope_tables(seq), *riders)
    return outs[:len(MIX_SPLITS)], outs[len(MIX_SPLITS):]


def _log_sigmoid(z):
    return jnp.minimum(z, 0.0) - jnp.log1p(jnp.exp(-jnp.abs(z)))


def _ret_kernel(dec_ref, q_ref, k_ref, v_ref, g_ref, o_ref, u_ref, st_ref, dm_ref, vec_ref, blk_ref):
    c = RET_BLOCK
    dk = RET_QK_DIM
    dv = RET_V_DIM
    n_heads = q_ref.shape[0]
    n_blocks = q_ref.shape[1] // c
    tn = (((0,), (0,)), ((), ()))
    nt = (((1,), (1,)), ((), ()))

    @pl.when(pl.program_id(0) == 0)
    def _decay_tables():
        def log_gamma(z, shape):
            return _log_sigmoid(jnp.full(shape, z, F32))

        ri = lax.broadcasted_iota(jnp.int32, (c, c), 0).astype(F32)
        ci = lax.broadcasted_iota(jnp.int32, (c, c), 1).astype(F32)
        diff = ri - ci
        r = lax.broadcasted_iota(jnp.int32, (c, dk), 0).astype(F32)
        for h in range(n_heads):
            zf = dec_ref[0, h]
            zb = dec_ref[1, h]
            dm_ref[h] = jnp.where(diff >= 0, jnp.exp(log_gamma(zf, (c, c)) * jnp.maximum(diff, 0.0)),
                                  jnp.exp(log_gamma(zb, (c, c)) * jnp.maximum(-diff, 0.0)))
            lgf = log_gamma(zf, (c, dk))
            lgb = log_gamma(zb, (c, dk))
            vec_ref[h, 0] = jnp.exp(lgf * (r + 1.0)).astype(BF16)
            vec_ref[h, 1] = jnp.exp(lgb * (c - r)).astype(BF16)
            vec_ref[h, 2] = jnp.exp(lgf * (c - 1.0 - r)).astype(BF16)
            vec_ref[h, 3] = jnp.exp(lgb * r).astype(BF16)
            blk_ref[h, 0] = jnp.exp(log_gamma(zf, (dk, dv)) * c)
            blk_ref[h, 1] = jnp.exp(log_gamma(zb, (dk, dv)) * c)

    def rows(n):
        return slice(n * c, (n + 1) * c)

    def updates(h):
        for n in range(n_blocks):
            k = k_ref[h, rows(n), :]
            k_cat = jnp.concatenate([k * vec_ref[h, 2], k * vec_ref[h, 3]], axis=1)
            u_ref[h % 2, n] = lax.dot_general(k_cat, v_ref[h, rows(n), :], tn, preferred_element_type=F32)

    def scans(h):
        sf = jnp.zeros((dk, dv), F32)
        for n in range(n_blocks):
            st_ref[h % 2, n, 0:dk, :] = sf.astype(BF16)
            if n + 1 < n_blocks:
                sf = blk_ref[h, 0] * sf + u_ref[h % 2, n, 0:dk, :]
        sb = jnp.zeros((dk, dv), F32)
        for n in reversed(range(n_blocks)):
            st_ref[h % 2, n, dk:2 * dk, :] = sb.astype(BF16)
            if n > 0:
                sb = blk_ref[h, 1] * sb + u_ref[h % 2, n, dk:2 * dk, :]

    def outputs(h):
        def scores(n):
            return lax.dot_general(q_ref[h, rows(n), :], k_ref[h, rows(n), :], nt,
                                   preferred_element_type=F32)

        s_next = scores(0)
        for n in range(n_blocks):
            s_cur = s_next
            if n + 1 < n_blocks:
                s_next = scores(n + 1)
            p = (s_cur * dm_ref[h]).astype(BF16)
            q = q_ref[h, rows(n), :]
            lhs = jnp.concatenate([p, q * vec_ref[h, 0], q * vec_ref[h, 1]], axis=1)
            rhs = jnp.concatenate([v_ref[h, rows(n), :], st_ref[h % 2, n]], axis=0)
            y = jnp.dot(lhs, rhs, preferred_element_type=F32)
            yn = y * lax.rsqrt(jnp.mean(y * y, axis=-1, keepdims=True) + RMS_EPS)
            o_ref[h, rows(n), :] = g_ref[h, rows(n), :] * yn.astype(o_ref.dtype)

    updates(0)
    for h in range(n_heads):
        if h + 1 < n_heads:
            updates(h + 1)
        scans(h)
        outputs(h)


def _retention(decay_logits, rq, rk, rv, rg):
    n_heads, b, s, _ = rq.shape
    c = RET_BLOCK
    n_blocks = s // c
    qk_spec = pl.BlockSpec((n_heads, None, s, RET_QK_DIM), lambda i, dec: (0, i, 0, 0))
    v_spec = pl.BlockSpec((n_heads, None, s, RET_V_DIM), lambda i, dec: (0, i, 0, 0))
    return pl.pallas_call(
        _ret_kernel,
        grid_spec=pltpu.PrefetchScalarGridSpec(
            num_scalar_prefetch=1,
            grid=(b,),
            in_specs=[qk_spec, qk_spec, v_spec, v_spec],
            out_specs=v_spec,
            scratch_shapes=[pltpu.VMEM((2, n_blocks, 2 * RET_QK_DIM, RET_V_DIM), F32),
                            pltpu.VMEM((2, n_blocks, 2 * RET_QK_DIM, RET_V_DIM), BF16),
                            pltpu.VMEM((n_heads, c, c), F32),
                            pltpu.VMEM((n_heads, 4, c, RET_QK_DIM), BF16),
                            pltpu.VMEM((n_heads, 2, RET_QK_DIM, RET_V_DIM), F32)],
        ),
        out_shape=jax.ShapeDtypeStruct((n_heads, b, s, RET_V_DIM), BF16),
        compiler_params=pltpu.CompilerParams(
            dimension_semantics=("arbitrary",), vmem_limit_bytes=VMEM_LIMIT),
        name="retention",
    )(decay_logits, rq, rk, rv, rg)


def _na_blocks(extent, q_size, k_size, window):
    n = extent // q_size
    out = []
    for i in range(n):
        q0 = i * q_size
        k0 = min(max(q0 - window // 2, 0), extent - k_size)
        out.append((q0, k0, 0 if i == 0 else (2 if i == n - 1 else 1)))
    return tuple(out)


NA_ROWS = 32
_NA_ROW_BLOCKS = _na_blocks(NA_ROWS, NA_Q_ROWS, NA_K_ROWS, NA_WIN_ROWS)
_NA_COL_BLOCKS = _na_blocks(GRID_W, NA_Q_COLS, NA_K_COLS, NA_WIN_COLS)
_NA_CONFIGS = 9


def _na_bias_kernel(rel_ref, out_ref, col_ref):
    nk = NA_K_ROWS * NA_K_COLS
    base = pl.program_id(0) * (NA_REL_ROWS * NA_REL_COLS)
    lane = lax.broadcasted_iota(jnp.int32, (NA_Q_COLS, nk), 1)
    kr = lane // NA_K_COLS
    col_cfgs = (_NA_COL_BLOCKS[0], _NA_COL_BLOCKS[1], _NA_COL_BLOCKS[3])
    row_cfgs = (_NA_ROW_BLOCKS[0], _NA_ROW_BLOCKS[1], _NA_ROW_BLOCKS[3])
    reps = nk // NA_LANES
    kc_one = lax.broadcasted_iota(jnp.int32, (NA_Q_COLS, NA_LANES), 1) % NA_K_COLS
    qc_one = lax.broadcasted_iota(jnp.int32, (NA_Q_COLS, NA_LANES), 0)
    for b, (c0, kc0, _) in enumerate(col_cfgs):
        kc_abs = kc_one + kc0
        qc_abs = qc_one + c0
        rel_col = jnp.clip(kc_abs - qc_abs + NA_WIN_COLS - 1, 0, NA_REL_COLS - 1)
        win = jnp.clip(qc_abs - NA_WIN_COLS // 2, 0, GRID_W - NA_WIN_COLS)
        in_win = (kc_abs >= win) & (kc_abs < win + NA_WIN_COLS)

        def rel_row_body(dr, carry, rel_col=rel_col, in_win=in_win, b=b):
            vals = jnp.zeros((NA_Q_COLS, NA_LANES), F32)
            for dc in range(NA_REL_COLS):
                vals = jnp.where(rel_col == dc, rel_ref[base + dr * NA_REL_COLS + dc], vals)
            vals = jnp.where(in_win, vals * LOG2E, NEG_INF)
            col_ref[b * NA_REL_ROWS + dr] = jnp.concatenate([vals] * reps, axis=1)
            return carry

        lax.fori_loop(0, NA_REL_ROWS, rel_row_body, 0)

    for a, (r0, ks, _) in enumerate(row_cfgs):
        kr_abs = kr + ks
        for qr in range(NA_Q_ROWS):
            r_abs = r0 + qr
            rs = min(max(r_abs - NA_WIN_ROWS // 2, 0), NA_ROWS - NA_WIN_ROWS)
            hits = [kr_abs == rs + j for j in range(NA_WIN_ROWS)]
            for b in range(3):
                tile = jnp.full((NA_Q_COLS, nk), NEG_INF, F32)
                for j in range(NA_WIN_ROWS):
                    dr = rs + j - r_abs + NA_WIN_ROWS - 1
                    tile = jnp.where(hits[j], col_ref[b * NA_REL_ROWS + dr], tile)
                out_ref[0, a * 3 + b, qr * NA_Q_COLS:(qr + 1) * NA_Q_COLS, :] = tile


def _na_bias_tables(rel_bias):
    nq = NA_Q_ROWS * NA_Q_COLS
    nk = NA_K_ROWS * NA_K_COLS
    return pl.pallas_call(
        _na_bias_kernel,
        grid_spec=pltpu.PrefetchScalarGridSpec(
            num_scalar_prefetch=1,
            grid=(NA_HEADS,),
            in_specs=[],
            out_specs=pl.BlockSpec((1, _NA_CONFIGS, nq, nk), lambda h, rel: (h, 0, 0, 0)),
            scratch_shapes=[pltpu.VMEM((3 * NA_REL_ROWS, NA_Q_COLS, nk), F32)],
        ),
        out_shape=jax.ShapeDtypeStruct((NA_HEADS, _NA_CONFIGS, nq, nk), F32),
        compiler_params=pltpu.CompilerParams(dimension_semantics=("parallel",)),
        name="na_bias",
    )(rel_bias.astype(F32).reshape(-1))


def _na_kernel(q_ref, k_ref, v_ref, bias_ref, o_ref, ks_ref, vs_ref):
    n_slabs = q_ref.shape[0]
    nq = NA_Q_ROWS * NA_Q_COLS
    nk = NA_K_ROWS * NA_K_COLS
    shift = NA_WIN_COLS // 2
    kept = GRID_W - 2 * shift
    for sl in range(n_slabs):
        for src, dst in ((k_ref, ks_ref), (v_ref, vs_ref)):
            dst[sl, :, 0:kept, :] = src[sl, :, shift:shift + kept, :]

    def window(refs, sl, block):
        (_, kr0, _), (_, kc0, _) = block
        aligned = kc0 % (2 * shift) == 0
        ref, c = (refs[0], kc0) if aligned else (refs[1], kc0 - shift)
        assert c % (2 * shift) == 0 and (aligned or c + NA_K_COLS <= kept)
        return ref[sl, kr0:kr0 + NA_K_ROWS, c:c + NA_K_COLS, :].reshape(nk, NA_LANES)

    first = lax.broadcasted_iota(jnp.int32, (nq, NA_LANES), 1) < NA_HEAD_DIM
    ones = jnp.ones((nk, NA_LANES), BF16)
    nt = (((1,), (1,)), ((), ()))
    work = [(sl, (rb, cb)) for sl in range(n_slabs) for rb in _NA_ROW_BLOCKS for cb in _NA_COL_BLOCKS]

    def scores(sl, block):
        (r0, _, rcfg), (c0, _, ccfg) = block
        q = q_ref[sl, r0:r0 + NA_Q_ROWS, c0:c0 + NA_Q_COLS, :].reshape(nq, NA_LANES)
        zero = jnp.zeros_like(q)
        q2 = jnp.concatenate([jnp.where(first, q, zero), jnp.where(first, zero, q)], axis=0)
        cfg = rcfg * 3 + ccfg
        bias = jnp.concatenate([bias_ref[2 * sl, cfg], bias_ref[2 * sl + 1, cfg]], axis=0)
        return lax.dot_general(q2, window((k_ref, ks_ref), sl, block), nt,
                               preferred_element_type=F32) + bias

    def attend(sl, block, s):
        m = jnp.max(s, axis=-1, keepdims=True)
        p = jnp.exp2(s - m).astype(BF16)
        o = jnp.dot(p, jnp.concatenate([window((v_ref, vs_ref), sl, block), ones], axis=1),
                    preferred_element_type=F32)
        o = o[:, :NA_LANES] / o[:, NA_LANES:]
        return jnp.where(first, o[:nq], o[nq:])

    s_next = scores(*work[0])
    for i, (sl, blk) in enumerate(work):
        s_cur = s_next
        if i + 1 < len(work):
            s_next = scores(*work[i + 1])
        (r0, _, _), (c0, _, _) = blk
        o_ref[sl, r0:r0 + NA_Q_ROWS, c0:c0 + NA_Q_COLS, :] = (
            attend(sl, blk, s_cur).reshape(NA_Q_ROWS, NA_Q_COLS, NA_LANES).astype(o_ref.dtype))


def _natten(nq, nk, nv, bias_tabs):
    b, rows = nq.shape[1], nq.shape[2]
    per = NA_SLABS_PER_STEP
    slab = pl.BlockSpec((per, None, rows, GRID_W, NA_LANES), lambda j, i: (j, i, 0, 0, 0))
    nqk = NA_Q_ROWS * NA_Q_COLS
    nkk = NA_K_ROWS * NA_K_COLS
    return pl.pallas_call(
        _na_kernel,
        grid=(NA_SLABS // per, b),
        in_specs=[slab, slab, slab,
                  pl.BlockSpec((2 * per, _NA_CONFIGS, nqk, nkk), lambda j, i: (j, 0, 0, 0),
                               pipeline_mode=pl.Buffered(1))],
        out_specs=slab,
        out_shape=jax.ShapeDtypeStruct((NA_SLABS, b, rows, GRID_W, NA_LANES), BF16),
        scratch_shapes=[pltpu.VMEM((per, rows, GRID_W, NA_LANES), BF16),
                        pltpu.VMEM((per, rows, GRID_W, NA_LANES), BF16)],
        compiler_params=pltpu.CompilerParams(
            dimension_semantics=("parallel", "parallel"), vmem_limit_bytes=VMEM_LIMIT),
        name="natten",
    )(nq, nk, nv, bias_tabs)


def _mixout_kernel(x_ref, yr_ref, na_ref, gr_ref, gn_ref, wr_ref, wn_ref, wo_ref, gain_ref, o_ref):
    n_sub = x_ref.shape[0] // SUB_TILE

    def rows(j):
        return slice(j * SUB_TILE, (j + 1) * SUB_TILE)

    def branches(j):
        def gathered(ref):
            return jnp.concatenate([ref[g, rows(j), :] for g in range(ref.shape[0])], axis=1)

        y_ret = jnp.dot(gathered(yr_ref), wr_ref[...], preferred_element_type=F32)
        y_na = jnp.dot(gathered(na_ref), wn_ref[...], preferred_element_type=F32)
        merged = (jax.nn.sigmoid(gr_ref[rows(j), :].astype(F32)) * y_ret
                  + jax.nn.sigmoid(gn_ref[rows(j), :].astype(F32)) * y_na)
        return merged.astype(BF16)

    def finish(j, merged):
        m = jnp.dot(merged, wo_ref[...], preferred_element_type=F32)
        o_ref[rows(j), :] = x_ref[rows(j), :] + _rms(m, gain_ref[...])

    merged = branches(0)
    for j in range(n_sub):
        merged_next = branches(j + 1) if j + 1 < n_sub else None
        finish(j, merged)
        merged = merged_next


def _mix_out(x, y_ret, y_na, g_ret, g_na, w_ret, w_na, w_out, gain):
    t = x.shape[0]
    tm = FFN_TILE

    def row(width):
        return pl.BlockSpec((tm, width), lambda i: (i, 0))

    return pl.pallas_call(
        _mixout_kernel,
        grid=(t // tm,),
        in_specs=[row(D_MODEL),
                  pl.BlockSpec((RET_HEADS, tm, RET_V_DIM), lambda i: (0, i, 0)),
                  pl.BlockSpec((NA_SLABS, tm, NA_LANES), lambda i: (0, i, 0)),
                  row(D_MODEL), row(D_MODEL),
                  _const_spec((RET_V_W, D_MODEL)), _const_spec((NA_W, D_MODEL)),
                  _const_spec((D_MODEL, D_MODEL)), _const_spec((1, D_MODEL))],
        out_specs=row(D_MODEL),
        out_shape=jax.ShapeDtypeStruct((t, D_MODEL), F32),
        compiler_params=pltpu.CompilerParams(
            dimension_semantics=("parallel",), vmem_limit_bytes=VMEM_LIMIT),
        name="mix_out",
    )(x, y_ret, y_na, g_ret, g_na, w_ret, w_na, w_out, gain)


def kernel(x, ffn1_pre_norm, ffn1_w_in, ffn1_w_out, ffn1_post_norm, mix_pre_norm, w_mix_in,
           ret_decay_fwd, ret_decay_bwd, na_rel_bias, w_ret_out, w_na_out, w_mix_out, mix_post_norm,
           ffn2_pre_norm, ffn2_w_in, ffn2_w_out, ffn2_post_norm):
    b, s, d = x.shape
    assert d == D_MODEL and s % GRID_W == 0 and s % RET_BLOCK == 0 and (b * s) % TOKEN_TILE == 0
    assert s % TOKEN_TILE == 0
    rows = s // GRID_W
    assert rows == NA_ROWS, "the neighbourhood-attention tiling is laid out for a 32 x 64 token grid"
    t = b * s
    xt = x.reshape(t, d)
    for l in range(ffn1_w_in.shape[0]):
        gain = lambda g: g[l].reshape(1, D_MODEL).astype(F32)
        wt = lambda w: w[l].astype(BF16)
        xt, (w_mix, w_ret, w_na, w_out) = _ffn(
            xt, gain(ffn1_pre_norm), wt(ffn1_w_in), wt(ffn1_w_out), gain(ffn1_post_norm),
            riders=(w_mix_in[l], w_ret_out[l], w_na_out[l], w_mix_out[l]))
        (rq, rk, rv, rg, nq, nk, nv, g_ret, g_na), (w_ffn2_in, w_ffn2_out) = _mix_in(
            xt, gain(mix_pre_norm), w_mix, s, riders=(ffn2_w_in[l], ffn2_w_out[l]))
        decay = jnp.stack([ret_decay_fwd[l], ret_decay_bwd[l]]).astype(F32)
        seq = lambda a: a.reshape(a.shape[0], b, s, a.shape[-1])
        y_ret = _retention(decay, seq(rq), seq(rk), seq(rv), seq(rg)).reshape(RET_HEADS, t, RET_V_DIM)
        grid5 = lambda a: a.reshape(NA_SLABS, b, rows, GRID_W, NA_LANES)
        y_na = _natten(grid5(nq), grid5(nk), grid5(nv), _na_bias_tables(na_rel_bias[l]))
        xt = _mix_out(xt, y_ret, y_na.reshape(NA_SLABS, t, NA_LANES), g_ret, g_na,
                      w_ret, w_na, w_out, gain(mix_post_norm))
        xt, _ = _ffn(xt, gain(ffn2_pre_norm), w_ffn2_in, w_ffn2_out, gain(ffn2_post_norm))
    return xt.reshape(b, s, d)
```
